```python
import math
import jax
import jax.numpy as jnp
from jax import lax
import numpy as np

D_MODEL = 1024
BATCH = 8
SEQ = 2048
DEPTH = 2

HEAD_DIM = 64
DSA_HEADS = 16
DSA_KV_DIM = 64
IDX_HEADS = 8
IDX_DIM = 32
DSA_TOPK = 256
NSA_HEADS = 16
NSA_KV_HEADS = 2
NSA_GROUP = NSA_HEADS // NSA_KV_HEADS
CMP_BLOCK = 32
CMP_STRIDE = 16
SEL_BLOCK = 64
SEL_TOPN = 4
WINDOW = 512
SSD_INNER = 2 * D_MODEL
SSD_HEAD_DIM = 64
SSD_HEADS = SSD_INNER // SSD_HEAD_DIM
SSD_GROUPS = 4
SSD_HPG = SSD_HEADS // SSD_GROUPS
SSD_STATE = 128
SSD_CONV_DIM = SSD_INNER + 2 * SSD_GROUPS * SSD_STATE
CONV_WIDTH = 4
SSD_CHUNK = 128
D_FF = 4 * D_MODEL
Q_BLOCK = 128
EPS = 1e-6
NEG = -1e30
IN_SPLITS = (DSA_HEADS * HEAD_DIM, DSA_KV_DIM, DSA_KV_DIM, IDX_HEADS * IDX_DIM, IDX_DIM, IDX_HEADS,
             NSA_HEADS * HEAD_DIM, 6 * NSA_KV_HEADS * HEAD_DIM, 3 * NSA_HEADS,
             SSD_INNER, SSD_CONV_DIM, SSD_HEADS, 3 * D_MODEL)
D_IN = sum(IN_SPLITS)

kernel_name = 'hybrid_dsa_nsa_ssd_gated_block'


def rms_norm(x, g):
    xf = x.astype(jnp.float32)
    y = xf * lax.rsqrt(jnp.mean(xf * xf, axis=-1, keepdims=True) + EPS)
    return (y * g).astype(x.dtype)


def _to_blocks(a):
    b, s = a.shape[:2]
    return jnp.moveaxis(a.reshape((b, s // Q_BLOCK, Q_BLOCK) + a.shape[2:]), 1, 0)


def _from_blocks(a):
    nb, b = a.shape[:2]
    return jnp.moveaxis(a, 0, 1).reshape((b, nb * a.shape[2]) + a.shape[3:])


def dsa_mixer(q, k, v, iq, ik, iw):
    seq = q.shape[1]
    n_sel = min(DSA_TOPK, seq // 4)
    key_pos = jnp.arange(seq)
    scale = HEAD_DIM ** -0.5
    gather = jax.vmap(lambda a, i: a[i])

    def block(args):
        qb, iqb, iwb, start = args
        t = start + jnp.arange(Q_BLOCK)
        idx = jnp.einsum('bqhd,bsd->bqhs', iqb, ik)
        idx = jnp.einsum('bqh,bqhs->bqs', iwb, jax.nn.relu(idx)).astype(jnp.float32)
        idx = jnp.where((key_pos[None, :] <= t[:, None])[None], idx, -jnp.inf)
        _, sel = lax.top_k(idx, n_sel)
        k_sel = gather(k, sel)
        v_sel = gather(v, sel)
        s = jnp.einsum('bqhd,bqkd->bqhk', qb, k_sel).astype(jnp.float32) * scale
        s = jnp.where((sel <= t[None, :, None])[:, :, None, :], s, NEG)
        p = jax.nn.softmax(s, axis=-1).astype(v.dtype)
        return jnp.einsum('bqhk,bqkd->bqhd', p, v_sel)

    starts = jnp.arange(seq // Q_BLOCK) * Q_BLOCK
    out = lax.map(block, (_to_blocks(q), _to_blocks(iq), _to_blocks(iw), starts))
    return _from_blocks(out)


def nsa_mixer(q, kc, vc, ks, vs, kw, vw, gates, k_norm_g, cmp_pos, cmp_w):
    bsz, seq = q.shape[:2]
    scale = HEAD_DIM ** -0.5
    n_cmp = (seq - CMP_BLOCK) // CMP_STRIDE + 1
    win_idx = jnp.arange(n_cmp)[:, None] * CMP_STRIDE + jnp.arange(CMP_BLOCK)[None, :]

    def compress(a, pos, w):
        blocks = a[:, win_idx] + pos[None, None, :, None, :]
        return jnp.einsum('bnlgd,lde->bnge', blocks, w)

    k_cmp = rms_norm(compress(kc, cmp_pos[0], cmp_w[0]), k_norm_g[0])
    v_cmp = compress(vc, cmp_pos[1], cmp_w[1])
    cmp_start = jnp.arange(n_cmp) * CMP_STRIDE
    cmp_end = cmp_start + CMP_BLOCK - 1
    n_blk = seq // SEL_BLOCK
    n_sel = min(SEL_TOPN, n_blk)
    blk_start = jnp.arange(n_blk) * SEL_BLOCK
    cover = ((cmp_start[:, None] < blk_start[None, :] + SEL_BLOCK)
             & (cmp_start[:, None] + CMP_BLOCK > blk_start[None, :])).astype(jnp.float32)
    ks = rms_norm(ks, k_norm_g[1])
    ks_blocks = ks.reshape(bsz, n_blk, SEL_BLOCK, NSA_KV_HEADS, HEAD_DIM).transpose(0, 3, 1, 2, 4)
    vs_blocks = vs.reshape(bsz, n_blk, SEL_BLOCK, NSA_KV_HEADS, HEAD_DIM).transpose(0, 3, 1, 2, 4)
    gather_blocks = jax.vmap(jax.vmap(lambda blk, i: blk[i]))
    kw = rms_norm(kw, k_norm_g[2])
    pad = ((0, 0), (WINDOW, 0), (0, 0), (0, 0))
    kw_pad = jnp.pad(kw, pad)
    vw_pad = jnp.pad(vw, pad)
    n_slc_keys = n_sel * SEL_BLOCK

    def block(args):
        qb, gb, start = args
        t = start + jnp.arange(Q_BLOCK)
        s = jnp.einsum('bqghd,bngd->bqghn', qb, k_cmp).astype(jnp.float32) * scale
        vis = (cmp_end[None, :] <= t[:, None])[None, :, None, None, :]
        s = jnp.where(vis, s, NEG)
        p_cmp = jax.nn.softmax(s, axis=-1) * vis
        o_cmp = jnp.einsum('bqghn,bngd->bqghd', p_cmp.astype(v_cmp.dtype), v_cmp)
        imp = jnp.einsum('bqghn,nj->bqgj', p_cmp, cover)
        cur = t // SEL_BLOCK
        j = jnp.arange(n_blk)
        forced = ((j[None, :] == cur[:, None]) | (j[None, :] == 0))[None, :, None, :]
        future = (j[None, :] > cur[:, None])[None, :, None, :]
        imp = jnp.where(forced, jnp.inf, jnp.where(future, -jnp.inf, imp))
        _, sel = lax.top_k(imp, n_sel)
        sel_t = sel.transpose(0, 2, 1, 3)
        bq = qb.shape[0]
        k_sel = gather_blocks(ks_blocks, sel_t).reshape(bq, NSA_KV_HEADS, Q_BLOCK, n_slc_keys, HEAD_DIM)
        v_sel = gather_blocks(vs_blocks, sel_t).reshape(bq, NSA_KV_HEADS, Q_BLOCK, n_slc_keys, HEAD_DIM)
        pos = (sel_t[..., None] * SEL_BLOCK + jnp.arange(SEL_BLOCK)).reshape(bq, NSA_KV_HEADS, Q_BLOCK, n_slc_keys)
        ok = (pos <= t[None, None, :, None]).transpose(0, 2, 1, 3)[:, :, :, None, :]
        s = jnp.einsum('bqghd,bgqkd->bqghk', qb, k_sel).astype(jnp.float32) * scale
        s = jnp.where(ok, s, NEG)
        p = jax.nn.softmax(s, axis=-1).astype(v_sel.dtype)
        o_slc = jnp.einsum('bqghk,bgqkd->bqghd', p, v_sel)
        kwb = lax.dynamic_slice_in_dim(kw_pad, start, WINDOW + Q_BLOCK, axis=1)
        vwb = lax.dynamic_slice_in_dim(vw_pad, start, WINDOW + Q_BLOCK, axis=1)
        pos_w = start - WINDOW + jnp.arange(WINDOW + Q_BLOCK)
        ok_w = ((pos_w[None, :] <= t[:, None]) & (pos_w[None, :] > t[:, None] - WINDOW)
                & (pos_w[None, :] >= 0))[None, :, None, None, :]
        s = jnp.einsum('bqghd,bkgd->bqghk', qb, kwb).astype(jnp.float32) * scale
        s = jnp.where(ok_w, s, NEG)
        p = jax.nn.softmax(s, axis=-1).astype(vwb.dtype)
        o_win = jnp.einsum('bqghk,bkgd->bqghd', p, vwb)
        return gb[..., 0:1] * o_cmp + gb[..., 1:2] * o_slc + gb[..., 2:3] * o_win

    starts = jnp.arange(seq // Q_BLOCK) * Q_BLOCK
    out = lax.map(block, (_to_blocks(q), _to_blocks(gates), starts))
    return _from_blocks(out)


def causal_conv(x, w, b):
    out = lax.conv_general_dilated(x, w[:, None, :], window_strides=(1,), padding=[(CONV_WIDTH - 1, 0)],
                                   dimension_numbers=('NWC', 'WIO', 'NWC'), feature_group_count=x.shape[-1])
    return out + b


def ssd_mixer(z, xbc, dt, conv_w, conv_b, dt_bias, a_log, d_skip, norm_g):
    bsz, seq = z.shape[:2]
    nc = seq // SSD_CHUNK
    xbc = jax.nn.silu(causal_conv(xbc, conv_w, conv_b))
    gn = SSD_GROUPS * SSD_STATE
    xs = xbc[..., :SSD_INNER].reshape(bsz, nc, SSD_CHUNK, SSD_GROUPS, SSD_HPG, SSD_HEAD_DIM)
    bm = xbc[..., SSD_INNER:SSD_INNER + gn].reshape(bsz, nc, SSD_CHUNK, SSD_GROUPS, SSD_STATE)
    cm = xbc[..., SSD_INNER + gn:].reshape(bsz, nc, SSD_CHUNK, SSD_GROUPS, SSD_STATE)
    dtc = jax.nn.softplus(dt.astype(jnp.float32) + dt_bias).reshape(bsz, nc, SSD_CHUNK, SSD_GROUPS, SSD_HPG)
    a = -jnp.exp(a_log.astype(jnp.float32)).reshape(SSD_GROUPS, SSD_HPG)
    a_cum = jnp.cumsum(dtc * a, axis=2)
    seg = a_cum[:, :, :, None] - a_cum[:, :, None, :]
    tril = jnp.tril(jnp.ones((SSD_CHUNK, SSD_CHUNK), dtype=bool))[None, None, :, :, None, None]
    lmat = jnp.exp(jnp.where(tril, seg, -jnp.inf))
    cb = jnp.einsum('bcign,bcjgn->bcijg', cm, bm)
    y_diag = jnp.einsum('bcijg,bcijgh,bcjgh,bcjghp->bcighp', cb, lmat, dtc, xs)
    decay_states = jnp.exp(a_cum[:, :, -1:] - a_cum)
    states = jnp.einsum('bcjgn,bcjgh,bcjghp->bcghpn', bm, decay_states * dtc, xs)
    chunk_decay = jnp.exp(a_cum[:, :, -1])

    def step(h, inp):
        s_c, d_c = inp
        return h * d_c[..., None, None] + s_c, h

    h0 = jnp.zeros((bsz, SSD_GROUPS, SSD_HPG, SSD_HEAD_DIM, SSD_STATE), states.dtype)
    _, h_in = lax.scan(step, h0, (jnp.moveaxis(states, 1, 0), jnp.moveaxis(chunk_decay, 1, 0)))
    h_in = jnp.moveaxis(h_in, 0, 1)
    y_off = jnp.einsum('bcign,bcghpn,bcigh->bcighp', cm, h_in, jnp.exp(a_cum))
    y = y_diag + y_off + xs * d_skip.reshape(SSD_GROUPS, SSD_HPG)[:, :, None]
    y = y.reshape(bsz, seq, SSD_INNER)
    y = rms_norm(y * jax.nn.silu(z.astype(jnp.float32)), norm_g)
    return y.astype(z.dtype)


def setup_inputs(seed: int = 0) -> dict:
    key = jax.random.key(seed)
    ks = jax.random.split(key, 24)
    f32 = jnp.float32

    def nrm(k, shape, scale):
        return jax.random.normal(k, shape, f32) * scale

    def gain(k, shape):
        return 1.0 + 0.02 * jax.random.normal(k, shape, f32)

    dt0 = jnp.exp(jax.random.uniform(ks[12], (DEPTH, SSD_HEADS), f32, math.log(1e-3), math.log(1e-1)))
    return {
        'x': nrm(ks[0], (BATCH, SEQ, D_MODEL), 1.0),
        'norm1_g': gain(ks[1], (DEPTH, D_MODEL)),
        'w_in': nrm(ks[2], (DEPTH, D_MODEL, D_IN), D_MODEL ** -0.5),
        'dsa_q_norm': gain(ks[3], (DEPTH, HEAD_DIM)),
        'dsa_k_norm': gain(ks[4], (DEPTH, DSA_KV_DIM)),
        'nsa_q_norm': gain(ks[5], (DEPTH, HEAD_DIM)),
        'nsa_k_norm': gain(ks[6], (DEPTH, 3, HEAD_DIM)),
        'nsa_cmp_pos': nrm(ks[7], (DEPTH, 2, CMP_BLOCK, HEAD_DIM), 0.02),
        'nsa_cmp_w': nrm(ks[8], (DEPTH, 2, CMP_BLOCK, HEAD_DIM, HEAD_DIM), (CMP_BLOCK * HEAD_DIM) ** -0.5),
        'ssd_conv_w': nrm(ks[9], (DEPTH, CONV_WIDTH, SSD_CONV_DIM), CONV_WIDTH ** -0.5),
        'ssd_conv_b': nrm(ks[10], (DEPTH, SSD_CONV_DIM), 0.02),
        'ssd_dt_bias': dt0 + jnp.log(-jnp.expm1(-dt0)),
        'ssd_a_log': jnp.log(jax.random.uniform(ks[11], (DEPTH, SSD_HEADS), f32, 1.0, 16.0)),
        'ssd_d': gain(ks[13], (DEPTH, SSD_HEADS)),
        'ssd_norm_g': gain(ks[14], (DEPTH, SSD_INNER)),
        'w_br_dsa': nrm(ks[15], (DEPTH, DSA_HEADS * HEAD_DIM, D_MODEL), (DSA_HEADS * HEAD_DIM) ** -0.5),
        'w_br_nsa': nrm(ks[16], (DEPTH, NSA_HEADS * HEAD_DIM, D_MODEL), (NSA_HEADS * HEAD_DIM) ** -0.5),
        'w_br_ssd': nrm(ks[17], (DEPTH, SSD_INNER, D_MODEL), SSD_INNER ** -0.5),
        'w_out': nrm(ks[18], (DEPTH, D_MODEL, D_MODEL), D_MODEL ** -0.5),
        'norm2_g': gain(ks[19], (DEPTH, D_MODEL)),
        'w_ff1': nrm(ks[20], (DEPTH, D_MODEL, D_FF), D_MODEL ** -0.5),
        'w_ff2': nrm(ks[21], (DEPTH, D_FF, D_MODEL), D_FF ** -0.5),
    }


def reference(x, norm1_g, w_in, dsa_q_norm, dsa_k_norm, nsa_q_norm, nsa_k_norm, nsa_cmp_pos, nsa_cmp_w,
              ssd_conv_w, ssd_conv_b, ssd_dt_bias, ssd_a_log, ssd_d, ssd_norm_g,
              w_br_dsa, w_br_nsa, w_br_ssd, w_out, norm2_g, w_ff1, w_ff2):
    bsz, seq = x.shape[:2]
    offsets = [int(o) for o in np.cumsum(IN_SPLITS)[:-1]]
    for l in range(DEPTH):
        h = rms_norm(x, norm1_g[l])
        u = h @ w_in[l]
        (dq, dk, dv, iq, ik, iw, nq, nkv, ng, sz, sxbc, sdt, mg) = jnp.split(u, offsets, axis=-1)
        y_a = dsa_mixer(rms_norm(dq.reshape(bsz, seq, DSA_HEADS, HEAD_DIM), dsa_q_norm[l]),
                        rms_norm(dk, dsa_k_norm[l]), dv,
                        iq.reshape(bsz, seq, IDX_HEADS, IDX_DIM), ik, iw)
        y_a = y_a.reshape(bsz, seq, DSA_HEADS * HEAD_DIM)
        nkv = nkv.reshape(bsz, seq, 6, NSA_KV_HEADS, HEAD_DIM)
        y_b = nsa_mixer(rms_norm(nq.reshape(bsz, seq, NSA_KV_HEADS, NSA_GROUP, HEAD_DIM), nsa_q_norm[l]),
                        nkv[:, :, 0], nkv[:, :, 1], nkv[:, :, 2], nkv[:, :, 3], nkv[:, :, 4], nkv[:, :, 5],
                        jax.nn.sigmoid(ng.reshape(bsz, seq, NSA_KV_HEADS, NSA_GROUP, 3)),
                        nsa_k_norm[l], nsa_cmp_pos[l], nsa_cmp_w[l])
        y_b = y_b.reshape(bsz, seq, NSA_HEADS * HEAD_DIM)
        y_c = ssd_mixer(sz, sxbc, sdt, ssd_conv_w[l], ssd_conv_b[l], ssd_dt_bias[l], ssd_a_log[l],
                        ssd_d[l], ssd_norm_g[l])
        gates = jax.nn.sigmoid(mg.reshape(bsz, seq, 3, D_MODEL))
        mix = (gates[:, :, 0] * (y_a @ w_br_dsa[l]) + gates[:, :, 1] * (y_b @ w_br_nsa[l])
               + gates[:, :, 2] * (y_c @ w_br_ssd[l]))
        x = x + mix @ w_out[l]
        h = rms_norm(x, norm2_g[l])
        x = x + jnp.square(jax.nn.relu(h @ w_ff1[l])) @ w_ff2[l]
    return x
```

```python
import functools
import math

import numpy as np
import jax
import jax.numpy as jnp
from jax import lax
from jax.experimental import pallas as pl
from jax.experimental.pallas import tpu as pltpu

F32 = jnp.float32
BF16 = jnp.bfloat16

D_MODEL = 1024
HEAD_DIM = 64
DSA_HEADS = 16
DSA_KV_DIM = 64
IDX_HEADS = 8
IDX_DIM = 32
DSA_TOPK = 256
NSA_HEADS = 16
NSA_KV_HEADS = 2
NSA_GROUP = NSA_HEADS // NSA_KV_HEADS
CMP_BLOCK = 32
CMP_STRIDE = 16
SEL_BLOCK = 64
SEL_TOPN = 4
WINDOW = 512
SSD_INNER = 2 * D_MODEL
SSD_HEAD_DIM = 64
SSD_HEADS = SSD_INNER // SSD_HEAD_DIM
SSD_GROUPS = 4
SSD_HPG = SSD_HEADS // SSD_GROUPS
SSD_STATE = 128
SSD_CONV_DIM = SSD_INNER + 2 * SSD_GROUPS * SSD_STATE
CONV_WIDTH = 4
SSD_CHUNK = 128
D_FF = 4 * D_MODEL
Q_BLOCK = 128
EPS = 1e-6
NEG = -1e30
IN_SPLITS = (DSA_HEADS * HEAD_DIM, DSA_KV_DIM, DSA_KV_DIM, IDX_HEADS * IDX_DIM, IDX_DIM, IDX_HEADS,
             NSA_HEADS * HEAD_DIM, 6 * NSA_KV_HEADS * HEAD_DIM, 3 * NSA_HEADS,
             SSD_INNER, SSD_CONV_DIM, SSD_HEADS, 3 * D_MODEL)

V7X_VMEM_LIMIT_BYTES = 56 * 1024 * 1024
LANES = 128

U16_WIDTH = 10240
U16_XBC_BLK = 0
U16_MG_BLK = 1
U16_Z_BLK = 3
U16_DQ_BLK = 8
U16_NQ_BLK = 9
U32_WIDTH = 1792
U32_NKV_BLK512 = 0
U32_DS_BLK512 = 2
U32_NG_BLK128 = 12
U32_DT_BLK128 = 13
DS_DK, DS_DV, DS_IQ, DS_IK, DS_IW = 0, 64, 128, 384, 416

KEY_TILE = 256
BISECT_ITERS = 40


def _cparams(sem):
    return pltpu.CompilerParams(dimension_semantics=sem, vmem_limit_bytes=V7X_VMEM_LIMIT_BYTES)


def _rms(x, g):
    return x * lax.rsqrt(jnp.mean(x * x, axis=-1, keepdims=True) + EPS) * g


def _dot_nt(a, b):
    return lax.dot_general(a, b, (((1,), (1,)), ((), ())), preferred_element_type=F32)


def _dot(a, b):
    return jnp.dot(a, b, preferred_element_type=F32)


def _dot_hi(a, b):
    return jnp.dot(a, b, preferred_element_type=F32, precision=lax.Precision.HIGHEST)


def _norm_matmul_kernel(x_ref, g_ref, w_ref, o_ref, h_ref):
    @pl.when(pl.program_id(1) == 0)
    def _():
        h_ref[...] = _rms(x_ref[...], g_ref[...]).astype(BF16)

    o_ref[...] = _dot(h_ref[...], w_ref[...]).astype(o_ref.dtype)


def _norm_matmul(x, g, w, out_dtype, tm, tn):
    m, k = x.shape
    n = w.shape[1]
    return pl.pallas_call(
        _norm_matmul_kernel,
        grid=(m // tm, n // tn),
        in_specs=[pl.BlockSpec((tm, k), lambda i, j: (i, 0)),
                  pl.BlockSpec((1, k), lambda i, j: (0, 0)),
                  pl.BlockSpec((k, tn), lambda i, j: (0, j))],
        out_specs=pl.BlockSpec((tm, tn), lambda i, j: (i, j)),
        out_shape=jax.ShapeDtypeStruct((m, n), out_dtype),
        scratch_shapes=[pltpu.VMEM((tm, k), BF16)],
        compiler_params=_cparams(("parallel", "arbitrary")),
        name="norm_matmul",
    )(x, g, w)


def _flash(q, k_ref, v_ref, bias_ref, nkt):
    nq, d = q.shape

    def body(kt, carry):
        m, l, acc = carry
        off = pl.multiple_of(kt * KEY_TILE, KEY_TILE)
        s = _dot_nt(q, k_ref[pl.ds(off, KEY_TILE), :]) + bias_ref[kt]
        m_new = jnp.maximum(m, jnp.max(s, axis=-1, keepdims=True))
        alpha = jnp.exp(m - m_new)
        p = jnp.exp(s - m_new)
        l = alpha * l + jnp.sum(p, axis=-1, keepdims=True)
        acc = alpha * acc + _dot(p.astype(BF16), v_ref[pl.ds(off, KEY_TILE), :])
        return m_new, l, acc

    m0 = jnp.full((nq, 1), -jnp.inf, F32)
    l0 = jnp.zeros((nq, 1), F32)
    a0 = jnp.zeros((nq, d), F32)
    _, l, acc = lax.fori_loop(0, nkt, body, (m0, l0, a0))
    return acc / l


def _dsa_kernel(q_ref, sq_ref, sk_ref, gq_ref, gk_ref, o_ref,
                kn_ref, v_ref, ik_ref, bias_ref, qn_ref, p_ref, *, seq, n_sel):
    i = pl.program_id(1)
    nq = Q_BLOCK
    kt_shape = (nq, KEY_TILE)

    @pl.when(i == 0)
    def _():
        def rows(r, c):
            off = pl.multiple_of(r * 256, 256)
            sk = sk_ref[0, pl.ds(off, 256), :]
            kn_ref[pl.ds(off, 256), :] = _rms(sk[:, DS_DK:DS_DK + 64], gk_ref[...]).astype(BF16)
            v_ref[pl.ds(off, 256), :] = sk[:, DS_DV:DS_DV + 64].astype(BF16)
            ik_ref[pl.ds(off, 256), :] = sk[:, DS_IK:DS_IK + IDX_DIM].astype(BF16)
            return c

        lax.fori_loop(0, seq // 256, rows, 0)

    start = i * nq
    nkt = (start + nq + KEY_TILE - 1) // KEY_TILE
    t_col = (start + lax.broadcasted_iota(jnp.int32, (nq, 1), 0)).astype(F32)
    lane_pos = lax.broadcasted_iota(jnp.int32, (1, KEY_TILE), 1).astype(F32)

    sq = sq_ref[0]
    iq = sq[:, DS_IQ:DS_IQ + IDX_HEADS * IDX_DIM].astype(BF16)
    iw = sq[:, DS_IW:DS_IW + IDX_HEADS]

    def idx_tile(kt, carry):
        rmin, rmax = carry
        off = pl.multiple_of(kt * KEY_TILE, KEY_TILE)
        ik_t = ik_ref[pl.ds(off, KEY_TILE), :]
        acc = jnp.zeros(kt_shape, F32)
        for h in range(IDX_HEADS):
            s = _dot_nt(iq[:, h * IDX_DIM:(h + 1) * IDX_DIM], ik_t)
            acc = acc + iw[:, h:h + 1] * jnp.maximum(s, 0.0)
        valid = (lane_pos + (kt * KEY_TILE).astype(F32)) <= t_col
        bias_ref[kt] = jnp.where(valid, acc, -jnp.inf)
        rmin = jnp.minimum(rmin, jnp.min(jnp.where(valid, acc, jnp.inf), axis=-1, keepdims=True))
        rmax = jnp.maximum(rmax, jnp.max(jnp.where(valid, acc, -jnp.inf), axis=-1, keepdims=True))
        return rmin, rmax

    rmin, rmax = lax.fori_loop(0, nkt, idx_tile,
                               (jnp.full((nq, 1), jnp.inf, F32), jnp.full((nq, 1), -jnp.inf, F32)))

    kf = float(n_sel)

    def count_gt(thr):
        thr_b = jnp.broadcast_to(thr, kt_shape)

        def body(kt, acc):
            return acc + jnp.where(bias_ref[kt] > thr_b, 1.0, 0.0)

        return jnp.sum(lax.fori_loop(0, nkt, body, jnp.zeros(kt_shape, F32)), axis=-1, keepdims=True)

    n_valid = t_col + 1.0
    lo0 = rmin - (jnp.abs(rmin) * 0.01 + 1.0)

    def bis_cond(c):
        it, _, _, clo, _ = c
        return jnp.logical_and(it < BISECT_ITERS, jnp.max(clo) > kf)

    def bis_body(c):
        it, lo, hi, clo, chi = c
        mid = 0.5 * (lo + hi)
        cnt = count_gt(mid)
        ge = cnt >= kf
        return (it + 1, jnp.where(ge, mid, lo), jnp.where(ge, hi, mid),
                jnp.where(ge, cnt, clo), jnp.where(ge, chi, cnt))

    _, lo, hi, clo, chi = lax.while_loop(
        bis_cond, bis_body, (jnp.int32(0), lo0, rmax, n_valid, jnp.zeros((nq, 1), F32)))

    p_ref[...] = jnp.full((nq, 1), float(seq), F32)

    @pl.when(jnp.max(clo) > kf)
    def _():
        quota = kf - chi
        lo_b = jnp.broadcast_to(lo, kt_shape)
        hi_b = jnp.broadcast_to(hi, kt_shape)

        def count_tie_le(pm):
            pm_b = jnp.broadcast_to(pm, kt_shape)

            def body(kt, acc):
                v = bias_ref[kt]
                pos = lane_pos + (kt * KEY_TILE).astype(F32)
                tie = (v > lo_b) & (v <= hi_b) & (pos <= pm_b)
                return acc + jnp.where(tie, 1.0, 0.0)

            return jnp.sum(lax.fori_loop(0, nkt, body, jnp.zeros(kt_shape, F32)), axis=-1, keepdims=True)

        def pbody(_, c):
            plo, phi = c
            mid = jnp.floor(0.5 * (plo + phi))
            ge = count_tie_le(mid) >= quota
            return jnp.where(ge, plo, mid), jnp.where(ge, mid, phi)

        steps = int(math.ceil(math.log2(seq))) + 1
        _, phi = lax.fori_loop(0, steps, pbody,
                               (jnp.full((nq, 1), -1.0, F32), jnp.full((nq, 1), float(seq - 1), F32)))
        p_ref[...] = jnp.where(clo > kf, phi, float(seq))

    lo_b = jnp.broadcast_to(lo, kt_shape)
    hi_b = jnp.broadcast_to(hi, kt_shape)
    p_b = jnp.broadcast_to(p_ref[...], kt_shape)

    def mask_tile(kt, c):
        v = bias_ref[kt]
        pos = lane_pos + (kt * KEY_TILE).astype(F32)
        sel = (v > hi_b) | ((v > lo_b) & (pos <= p_b))
        bias_ref[kt] = jnp.where(sel, 0.0, NEG)
        return c

    lax.fori_loop(0, nkt, mask_tile, 0)

    scale = HEAD_DIM ** -0.5
    for h in range(DSA_HEADS):
        qh = q_ref[0, :, h * HEAD_DIM:(h + 1) * HEAD_DIM].astype(F32)
        qn_ref[h] = (_rms(qh, gq_ref[...]) * scale).astype(BF16)

    def head(h, c):
        qn_ref[h] = _flash(qn_ref[h], kn_ref, v_ref, bias_ref, nkt).astype(BF16)
        return c

    lax.fori_loop(0, DSA_HEADS, head, 0)
    for h in range(0, DSA_HEADS, 2):
        o_ref[0, :, h * HEAD_DIM:(h + 2) * HEAD_DIM] = jnp.concatenate([qn_ref[h], qn_ref[h + 1]], axis=-1)


def _dsa(u16, u32, gq, gk, bsz, seq):
    n_sel = min(DSA_TOPK, seq // 4)
    nqb = seq // Q_BLOCK
    kern = functools.partial(_dsa_kernel, seq=seq, n_sel=n_sel)
    return pl.pallas_call(
        kern,
        grid=(bsz, nqb),
        in_specs=[pl.BlockSpec((1, Q_BLOCK, 1024), lambda b, i: (b, i, U16_DQ_BLK)),
                  pl.BlockSpec((1, Q_BLOCK, 512), lambda b, i: (b, i, U32_DS_BLK512)),
                  pl.BlockSpec((1, seq, 512), lambda b, i: (b, 0, U32_DS_BLK512)),
                  pl.BlockSpec((1, HEAD_DIM), lambda b, i: (0, 0)),
                  pl.BlockSpec((1, DSA_KV_DIM), lambda b, i: (0, 0))],
        out_specs=pl.BlockSpec((1, Q_BLOCK, DSA_HEADS * HEAD_DIM), lambda b, i: (b, i, 0)),
        out_shape=jax.ShapeDtypeStruct((bsz, seq, DSA_HEADS * HEAD_DIM), BF16),
        scratch_shapes=[pltpu.VMEM((seq, DSA_KV_DIM), BF16),
                        pltpu.VMEM((seq, DSA_KV_DIM), BF16),
                        pltpu.VMEM((seq, IDX_DIM), BF16),
                        pltpu.VMEM((seq // KEY_TILE, Q_BLOCK, KEY_TILE), F32),
                        pltpu.VMEM((DSA_HEADS, Q_BLOCK, HEAD_DIM), BF16),
                        pltpu.VMEM((Q_BLOCK, 1), F32)],
        compiler_params=_cparams(("parallel", "arbitrary")),
        name="dsa",
    )(u16, u32, u32, gq, gk)


def _nsa_compress_kernel(x_ref, pos_ref, w_ref, g_ref, kc_ref, vc_ref):
    for j in range(2):
        for g in range(NSA_KV_HEADS):
            x = x_ref[0, j, g]
            a = _dot((x + pos_ref[j, 0:1, :]).astype(BF16), w_ref[j, 0])
            b = _dot((x + pos_ref[j, 1:2, :]).astype(BF16), w_ref[j, 1])
            out = a + pltpu.roll(b, b.shape[0] - 1, axis=0)
            if j == 0:
                kc_ref[0, g] = _rms(out, g_ref[...]).astype(BF16)
            else:
                vc_ref[0, g] = out.astype(BF16)


def _nsa_compress(x, pos, w, g):
    bsz, _, _, nch, width = x.shape
    return pl.pallas_call(
        _nsa_compress_kernel,
        grid=(bsz,),
        in_specs=[pl.BlockSpec((1, 2, NSA_KV_HEADS, nch, width), lambda b: (b, 0, 0, 0, 0)),
                  pl.BlockSpec((2, 2, width), lambda b: (0, 0, 0)),
                  pl.BlockSpec((2, 2, width, HEAD_DIM), lambda b: (0, 0, 0, 0)),
                  pl.BlockSpec((1, HEAD_DIM), lambda b: (0, 0))],
        out_specs=[pl.BlockSpec((1, NSA_KV_HEADS, nch, HEAD_DIM), lambda b: (b, 0, 0, 0)),
                   pl.BlockSpec((1, NSA_KV_HEADS, nch, HEAD_DIM), lambda b: (b, 0, 0, 0))],
        out_shape=[jax.ShapeDtypeStruct((bsz, NSA_KV_HEADS, nch, HEAD_DIM), BF16),
                   jax.ShapeDtypeStruct((bsz, NSA_KV_HEADS, nch, HEAD_DIM), BF16)],
        compiler_params=_cparams(("parallel",)),
        name="nsa_compress",
    )(x, pos, w, g)


def _nsa_kernel(q_ref, ng_ref, kv_ref, kc_ref, vc_ref, gq_ref, gk_ref, cover_ref, expand_ref, o_ref,
                ks_ref, vs_ref, kw_ref, vw_ref, bias_ref, qn_ref, oh_ref, *, seq):
    i = pl.program_id(1)
    nq = Q_BLOCK
    n_cmp_pad = kc_ref.shape[2]
    n_blk = seq // SEL_BLOCK
    n_sel = min(SEL_TOPN, n_blk)
    win_keys = min(WINDOW + Q_BLOCK, seq)

    @pl.when(i == 0)
    def _():
        def rows(r, c):
            off = pl.multiple_of(r * 256, 256)
            kv = kv_ref[0, pl.ds(off, 256), :]
            for g in range(NSA_KV_HEADS):
                c0 = g * HEAD_DIM
                ks_ref[g, pl.ds(off, 256), :] = _rms(kv[:, c0:c0 + 64], gk_ref[1:2, :]).astype(BF16)
                vs_ref[g, pl.ds(off, 256), :] = kv[:, 128 + c0:128 + c0 + 64].astype(BF16)
                kw_ref[g, pl.ds(off, 256), :] = _rms(kv[:, 256 + c0:256 + c0 + 64], gk_ref[2:3, :]).astype(BF16)
                vw_ref[g, pl.ds(off, 256), :] = kv[:, 384 + c0:384 + c0 + 64].astype(BF16)
            return c

        lax.fori_loop(0, seq // 256, rows, 0)

    start = i * nq
    nkt = (start + nq + KEY_TILE - 1) // KEY_TILE
    t_i = start + lax.broadcasted_iota(jnp.int32, (nq, 1), 0)
    t_col = t_i.astype(F32)
    lane_pos = lax.broadcasted_iota(jnp.int32, (1, KEY_TILE), 1).astype(F32)
    scale = HEAD_DIM ** -0.5

    for h in range(NSA_HEADS):
        qh = q_ref[0, :, h * HEAD_DIM:(h + 1) * HEAD_DIM].astype(F32)
        qn_ref[h] = (_rms(qh, gq_ref[...]) * scale).astype(BF16)

    gates = jax.nn.sigmoid(ng_ref[0])

    cmp_end = (lax.broadcasted_iota(jnp.int32, (1, n_cmp_pad), 1) * CMP_STRIDE + (CMP_BLOCK - 1)).astype(F32)
    vis = cmp_end <= t_col
    wbase = pl.multiple_of(jnp.maximum(start + nq - win_keys, 0), Q_BLOCK)
    wpos = wbase.astype(F32) + lax.broadcasted_iota(jnp.int32, (1, win_keys), 1).astype(F32)
    wok = (wpos <= t_col) & (wpos > t_col - float(WINDOW))
    blk_j = lax.broadcasted_iota(jnp.int32, (1, n_blk), 1)
    cur = jnp.right_shift(t_i, int(math.log2(SEL_BLOCK)))

    for g in range(NSA_KV_HEADS):
        psum = jnp.zeros((nq, n_cmp_pad), F32)
        for hh in range(NSA_GROUP):
            h = g * NSA_GROUP + hh
            s = jnp.where(vis, _dot_nt(qn_ref[h], kc_ref[0, g]), NEG)
            e = jnp.exp(s - jnp.max(s, axis=-1, keepdims=True))
            p = jnp.where(vis, e / jnp.sum(e, axis=-1, keepdims=True), 0.0)
            psum = psum + p
            oh_ref[h] = gates[:, 3 * h:3 * h + 1] * _dot(p.astype(BF16), vc_ref[0, g])
        imp = _dot_hi(psum, cover_ref[...])
        forced = (blk_j == cur) | (blk_j == 0)
        imp = jnp.where(forced, jnp.inf, jnp.where(blk_j > cur, -jnp.inf, imp))
        selb = jnp.zeros((nq, n_blk), jnp.bool_)
        for _ in range(n_sel):
            mx = jnp.max(imp, axis=-1, keepdims=True)
            first = jnp.min(jnp.where(imp == mx, blk_j, n_blk), axis=-1, keepdims=True)
            pick = blk_j == first
            selb = selb | pick
            imp = jnp.where(pick, -jnp.inf, imp)
        selb_bf = jnp.where(selb, 1.0, 0.0).astype(BF16)

        def mask_tile(kt, c):
            off = pl.multiple_of(kt * KEY_TILE, KEY_TILE)
            hit = _dot(selb_bf, expand_ref[:, pl.ds(off, KEY_TILE)])
            pos = lane_pos + (kt * KEY_TILE).astype(F32)
            bias_ref[kt] = jnp.where((hit > 0.5) & (pos <= t_col), 0.0, NEG)
            return c

        lax.fori_loop(0, nkt, mask_tile, 0)

        for hh in range(NSA_GROUP):
            h = g * NSA_GROUP + hh
            q = qn_ref[h]
            o_slc = _flash(q, ks_ref.at[g], vs_ref.at[g], bias_ref, nkt)
            sw = jnp.where(wok, _dot_nt(q, kw_ref[g, pl.ds(wbase, win_keys), :]), NEG)
            ew = jnp.exp(sw - jnp.max(sw, axis=-1, keepdims=True))
            o_win = _dot(ew.astype(BF16), vw_ref[g, pl.ds(wbase, win_keys), :]) / jnp.sum(ew, axis=-1, keepdims=True)
            oh_ref[h] = (oh_ref[h] + gates[:, 3 * h + 1:3 * h + 2] * o_slc
                         + gates[:, 3 * h + 2:3 * h + 3] * o_win)

    for h in range(0, NSA_HEADS, 2):
        o_ref[0, :, h * HEAD_DIM:(h + 2) * HEAD_DIM] = jnp.concatenate(
            [oh_ref[h], oh_ref[h + 1]], axis=-1).astype(BF16)


def _nsa(u16, u32, kc, vc, gq, gk, cover, expand, bsz, seq):
    nqb = seq // Q_BLOCK
    nch = kc.shape[2]
    n_blk = seq // SEL_BLOCK
    kern = functools.partial(_nsa_kernel, seq=seq)
    return pl.pallas_call(
        kern,
        grid=(bsz, nqb),
        in_specs=[pl.BlockSpec((1, Q_BLOCK, 1024), lambda b, i: (b, i, U16_NQ_BLK)),
                  pl.BlockSpec((1, Q_BLOCK, 128), lambda b, i: (b, i, U32_NG_BLK128)),
                  pl.BlockSpec((1, seq, 512), lambda b, i: (b, 0, U32_NKV_BLK512)),
                  pl.BlockSpec((1, NSA_KV_HEADS, nch, HEAD_DIM), lambda b, i: (b, 0, 0, 0)),
                  pl.BlockSpec((1, NSA_KV_HEADS, nch, HEAD_DIM), lambda b, i: (b, 0, 0, 0)),
                  pl.BlockSpec((1, HEAD_DIM), lambda b, i: (0, 0)),
                  pl.BlockSpec((3, HEAD_DIM), lambda b, i: (0, 0)),
                  pl.BlockSpec((nch, n_blk), lambda b, i: (0, 0)),
                  pl.BlockSpec((n_blk, seq), lambda b, i: (0, 0))],
        out_specs=pl.BlockSpec((1, Q_BLOCK, NSA_HEADS * HEAD_DIM), lambda b, i: (b, i, 0)),
        out_shape=jax.ShapeDtypeStruct((bsz, seq, NSA_HEADS * HEAD_DIM), BF16),
        scratch_shapes=[pltpu.VMEM((NSA_KV_HEADS, seq, HEAD_DIM), BF16),
                        pltpu.VMEM((NSA_KV_HEADS, seq, HEAD_DIM), BF16),
                        pltpu.VMEM((NSA_KV_HEADS, seq, HEAD_DIM), BF16),
                        pltpu.VMEM((NSA_KV_HEADS, seq, HEAD_DIM), BF16),
                        pltpu.VMEM((seq // KEY_TILE, Q_BLOCK, KEY_TILE), F32),
                        pltpu.VMEM((NSA_HEADS, Q_BLOCK, HEAD_DIM), BF16),
                        pltpu.VMEM((NSA_HEADS, Q_BLOCK, HEAD_DIM), F32)],
        compiler_params=_cparams(("parallel", "arbitrary")),
        name="nsa",
    )(u16, u32, u32, kc, vc, gq, gk, cover, expand)


def _softplus(x):
    return jnp.maximum(x, 0.0) + jnp.log1p(jnp.exp(-jnp.abs(x)))


def _ssd_kernel(xbc_ref, z_ref, dt_ref, dtt_ref, cw_ref, cb_ref, dtb_ref, dtbt_ref, al_ref, alt_ref,
                d_ref, ng_ref, o_ref, xc_ref, xa_ref, y_ref, st_ref):
    c = pl.program_id(1)
    q = SSD_CHUNK
    gn = SSD_GROUPS * SSD_STATE

    @pl.when(c == 0)
    def _():
        st_ref[...] = jnp.zeros_like(st_ref)
        xc_ref[0:8, :] = jnp.zeros((8, SSD_CONV_DIM), F32)

    @pl.when(c > 0)
    def _():
        xc_ref[0:8, :] = xc_ref[q:q + 8, :]

    xc_ref[8:8 + q, :] = xbc_ref[0].astype(F32)
    for cc in range(SSD_CONV_DIM // 512):
        cs = slice(cc * 512, (cc + 1) * 512)
        acc = cb_ref[:, cs] + cw_ref[CONV_WIDTH - 1:CONV_WIDTH, cs] * xc_ref[8:8 + q, cs]
        for s in range(1, CONV_WIDTH):
            acc = acc + cw_ref[CONV_WIDTH - 1 - s:CONV_WIDTH - s, cs] * xc_ref[8 - s:8 - s + q, cs]
        xa_ref[:, cs] = acc * jax.nn.sigmoid(acc)

    dt = _softplus(dt_ref[0][:, :SSD_HEADS] + dtb_ref[...])
    dtt = _softplus(dtt_ref[0] + dtbt_ref[...])
    a = -jnp.exp(al_ref[...])
    at = -jnp.exp(alt_ref[...])
    ri = lax.broadcasted_iota(jnp.int32, (q, q), 0)
    ci = lax.broadcasted_iota(jnp.int32, (q, q), 1)
    tril = ri >= ci
    acum = _dot_hi(jnp.where(tril, 1.0, 0.0), dt * a)
    acumt = _dot_hi(dtt * at, jnp.where(ri <= ci, 1.0, 0.0))
    alast = acum[q - 1:q, :]
    decay_in = jnp.exp(acum)
    wst = jnp.exp(alast - acum) * dt
    cdecay = jnp.exp(alast)

    for g in range(SSD_GROUPS):
        bmat = xa_ref[:, SSD_INNER + g * SSD_STATE:SSD_INNER + (g + 1) * SSD_STATE]
        cmat = xa_ref[:, SSD_INNER + gn + g * SSD_STATE:SSD_INNER + gn + (g + 1) * SSD_STATE].astype(BF16)
        cb = _dot_nt(cmat, bmat.astype(BF16))
        bt = bmat.T.astype(BF16)
        for hh in range(SSD_HPG):
            h = g * SSD_HPG + hh
            xs = xa_ref[:, h * SSD_HEAD_DIM:(h + 1) * SSD_HEAD_DIM]
            seg = acum[:, h:h + 1] - acumt[h:h + 1, :]
            lmat = jnp.exp(jnp.where(tril, seg, -jnp.inf))
            m = cb * lmat * dtt[h:h + 1, :]
            y = _dot(m.astype(BF16), xs.astype(BF16))
            y = y + _dot(cmat, st_ref[h].astype(BF16)) * decay_in[:, h:h + 1]
            y = y + xs * d_ref[:, h:h + 1]
            y_ref[:, h * SSD_HEAD_DIM:(h + 1) * SSD_HEAD_DIM] = y
            st_new = _dot(bt, (xs * wst[:, h:h + 1]).astype(BF16))
            st_ref[h] = st_ref[h] * cdecay[:, h:h + 1] + st_new

    z = z_ref[0].astype(F32)
    yz = y_ref[...] * (z * jax.nn.sigmoid(z))
    o_ref[0] = _rms(yz, ng_ref[...]).astype(o_ref.dtype)


def _ssd(u16, u32, dtt, cw, cb, dtb, al, d, ng, bsz, seq):
    nc = seq // SSD_CHUNK
    q = SSD_CHUNK
    full = lambda shape: pl.BlockSpec(shape, lambda b, c: (0,) * len(shape))
    return pl.pallas_call(
        _ssd_kernel,
        grid=(bsz, nc),
        in_specs=[pl.BlockSpec((1, q, SSD_CONV_DIM), lambda b, c: (b, c, U16_XBC_BLK)),
                  pl.BlockSpec((1, q, SSD_INNER), lambda b, c: (b, c, U16_Z_BLK)),
                  pl.BlockSpec((1, q, 128), lambda b, c: (b, c, U32_DT_BLK128)),
                  pl.BlockSpec((1, SSD_HEADS, q), lambda b, c: (b, 0, c)),
                  full((CONV_WIDTH, SSD_CONV_DIM)), full((1, SSD_CONV_DIM)),
                  full((1, SSD_HEADS)), full((SSD_HEADS, 1)),
                  full((1, SSD_HEADS)), full((SSD_HEADS, 1)),
                  full((1, SSD_HEADS)), full((1, SSD_INNER))],
        out_specs=pl.BlockSpec((1, q, SSD_INNER), lambda b, c: (b, c, 0)),
        out_shape=jax.ShapeDtypeStruct((bsz, seq, SSD_INNER), BF16),
        scratch_shapes=[pltpu.VMEM((q + 8, SSD_CONV_DIM), F32),
                        pltpu.VMEM((q, SSD_CONV_DIM), F32),
                        pltpu.VMEM((q, SSD_INNER), F32),
                        pltpu.VMEM((SSD_HEADS, SSD_STATE, SSD_HEAD_DIM), F32)],
        compiler_params=_cparams(("parallel", "arbitrary")),
        name="ssd",
    )(u16, u16, u32, dtt, cw, cb, dtb.reshape(1, -1), dtb.reshape(-1, 1), al.reshape(1, -1), al.reshape(-1, 1),
      d.reshape(1, -1), ng.reshape(1, -1))


def _merge_kernel(x_ref, ya_ref, yb_ref, yc_ref, mg_ref, wa_ref, wb_ref, wc_ref, wo_ref, o_ref):
    d = D_MODEL
    mg = mg_ref[...].astype(F32)
    mix = jax.nn.sigmoid(mg[:, 0:d]) * _dot(ya_ref[...], wa_ref[...])
    mix = mix + jax.nn.sigmoid(mg[:, d:2 * d]) * _dot(yb_ref[...], wb_ref[...])
    mix = mix + jax.nn.sigmoid(mg[:, 2 * d:3 * d]) * _dot(yc_ref[...], wc_ref[...])
    o_ref[...] = x_ref[...] + _dot(mix.astype(BF16), wo_ref[...])


def _merge(x, ya, yb, yc, u16, wa, wb, wc, wo, tm):
    m, d = x.shape
    full = lambda a: pl.BlockSpec(a.shape, lambda i: (0, 0))
    return pl.pallas_call(
        _merge_kernel,
        grid=(m // tm,),
        in_specs=[pl.BlockSpec((tm, d), lambda i: (i, 0)),
                  pl.BlockSpec((tm, ya.shape[1]), lambda i: (i, 0)),
                  pl.BlockSpec((tm, yb.shape[1]), lambda i: (i, 0)),
                  pl.BlockSpec((tm, yc.shape[1]), lambda i: (i, 0)),
                  pl.BlockSpec((tm, 3 * d), lambda i: (i, U16_MG_BLK)),
                  full(wa), full(wb), full(wc), full(wo)],
        out_specs=pl.BlockSpec((tm, d), lambda i: (i, 0)),
        out_shape=jax.ShapeDtypeStruct((m, d), F32),
        compiler_params=_cparams(("parallel",)),
        name="merge",
    )(x, ya, yb, yc, u16, wa, wb, wc, wo)


def _ffn_kernel(x_ref, g_ref, w1_ref, w2_ref, o_ref, h_ref, acc_ref):
    j = pl.program_id(1)

    @pl.when(j == 0)
    def _():
        h_ref[...] = _rms(x_ref[...], g_ref[...]).astype(BF16)
        acc_ref[...] = x_ref[...]

    a = jnp.maximum(_dot(h_ref[...], w1_ref[...]), 0.0)
    acc_ref[...] += _dot((a * a).astype(BF16), w2_ref[...])

    @pl.when(j == pl.num_programs(1) - 1)
    def _():
        o_ref[...] = acc_ref[...]


def _ffn(x, g, w1, w2, tm, tf):
    m, d = x.shape
    f = w1.shape[1]
    return pl.pallas_call(
        _ffn_kernel,
        grid=(m // tm, f // tf),
        in_specs=[pl.BlockSpec((tm, d), lambda i, j: (i, 0)),
                  pl.BlockSpec((1, d), lambda i, j: (0, 0)),
                  pl.BlockSpec((d, tf), lambda i, j: (0, j)),
                  pl.BlockSpec((tf, d), lambda i, j: (j, 0))],
        out_specs=pl.BlockSpec((tm, d), lambda i, j: (i, 0)),
        out_shape=jax.ShapeDtypeStruct((m, d), F32),
        scratch_shapes=[pltpu.VMEM((tm, d), BF16), pltpu.VMEM((tm, d), F32)],
        compiler_params=_cparams(("parallel", "arbitrary")),
        name="ffn",
    )(x, g, w1, w2)


def _prep_w_in(w):
    offs = [0] + [int(o) for o in np.cumsum(IN_SPLITS)]
    seg = lambda k: w[:, offs[k]:offs[k + 1]]
    dq, dk, dv, iq, ik, iw, nq, nkv, ng, sz, sxbc, sdt, mg = [seg(k) for k in range(13)]
    zeros = lambda n: jnp.zeros((w.shape[0], n), w.dtype)
    w16 = jnp.concatenate([sxbc, mg, sz, dq, nq], axis=1).astype(BF16)
    w32 = jnp.concatenate([nkv[:, 256:768], nkv[:, 0:256], zeros(256),
                           dk, dv, iq, ik, iw, zeros(512 - 424),
                           ng, zeros(128 - ng.shape[1]),
                           sdt, zeros(128 - sdt.shape[1])], axis=1).astype(BF16)
    return w16, w32


def _pick_tile(n, pref):
    t = min(n, pref)
    while n % t:
        t //= 2
    return t


def kernel(x, norm1_g, w_in, dsa_q_norm, dsa_k_norm, nsa_q_norm, nsa_k_norm, nsa_cmp_pos, nsa_cmp_w,
           ssd_conv_w, ssd_conv_b, ssd_dt_bias, ssd_a_log, ssd_d, ssd_norm_g,
           w_br_dsa, w_br_nsa, w_br_ssd, w_out, norm2_g, w_ff1, w_ff2):
    bsz, seq, d = x.shape
    m = bsz * seq
    depth = w_in.shape[0]
    nch = seq // CMP_STRIDE
    n_blk = seq // SEL_BLOCK
    half = CMP_BLOCK // 2

    cmp_start = np.arange(nch) * CMP_STRIDE
    blk_start = np.arange(n_blk) * SEL_BLOCK
    cover = ((cmp_start[:, None] < blk_start[None, :] + SEL_BLOCK)
             & (cmp_start[:, None] + CMP_BLOCK > blk_start[None, :])
             & (np.arange(nch)[:, None] < (seq - CMP_BLOCK) // CMP_STRIDE + 1)).astype(np.float32)
    expand = (np.arange(seq)[None, :] // SEL_BLOCK == np.arange(n_blk)[:, None]).astype(np.float32)
    cover = jnp.asarray(cover)
    expand = jnp.asarray(expand, dtype=BF16)

    tm = _pick_tile(m, 1024)
    xf = x.reshape(m, d)
    for l in range(depth):
        w16, w32 = _prep_w_in(w_in[l])
        g1 = norm1_g[l].reshape(1, d)
        u16 = _norm_matmul(xf, g1, w16, BF16, tm, 2048).reshape(bsz, seq, U16_WIDTH)
        u32 = _norm_matmul(xf, g1, w32, F32, tm, U32_WIDTH).reshape(bsz, seq, U32_WIDTH)

        ya = _dsa(u16, u32, dsa_q_norm[l].reshape(1, -1), dsa_k_norm[l].reshape(1, -1), bsz, seq)

        kcvc = u32[:, :, 512:768].reshape(bsz, nch, CMP_STRIDE, 2, NSA_KV_HEADS, HEAD_DIM)
        kcvc = kcvc.transpose(0, 3, 4, 1, 2, 5).reshape(bsz, 2, NSA_KV_HEADS, nch, CMP_STRIDE * HEAD_DIM)
        cpos = nsa_cmp_pos[l].reshape(2, 2, half * HEAD_DIM)
        cw = nsa_cmp_w[l].reshape(2, 2, half * HEAD_DIM, HEAD_DIM).astype(BF16)
        kc, vc = _nsa_compress(kcvc, cpos, cw, nsa_k_norm[l][0:1])
        yb = _nsa(u16, u32, kc, vc, nsa_q_norm[l].reshape(1, -1), nsa_k_norm[l], cover, expand, bsz, seq)

        dtt = u32[:, :, 1664:1664 + SSD_HEADS].transpose(0, 2, 1)
        yc = _ssd(u16, u32, dtt, ssd_conv_w[l], ssd_conv_b[l].reshape(1, -1), ssd_dt_bias[l], ssd_a_log[l],
                  ssd_d[l], ssd_norm_g[l], bsz, seq)

        xf = _merge(xf, ya.reshape(m, -1), yb.reshape(m, -1), yc.reshape(m, -1), u16.reshape(m, U16_WIDTH),
                    w_br_dsa[l].astype(BF16), w_br_nsa[l].astype(BF16), w_br_ssd[l].astype(BF16),
                    w_out[l].astype(BF16), _pick_tile(m, 256))
        xf = _ffn(xf, norm2_g[l].reshape(1, d), w_ff1[l].astype(BF16), w_ff2[l].astype(BF16),
                  _pick_tile(m, 1024), 1024)
    return xf.reshape(bsz, seq, d)
```

```python
import functools
import math

import numpy as np
import jax
import jax.numpy as jnp
from jax import lax
from jax.experimental import pallas as pl
from jax.experimental.pallas import tpu as pltpu

F32 = jnp.float32
BF16 = jnp.bfloat16

D_MODEL = 1024
HEAD_DIM = 64
DSA_HEADS = 16
DSA_KV_DIM = 64
IDX_HEADS = 8
IDX_DIM = 32
DSA_TOPK = 256
NSA_HEADS = 16
NSA_KV_HEADS = 2
NSA_GROUP = NSA_HEADS // NSA_KV_HEADS
CMP_BLOCK = 32
CMP_STRIDE = 16
SEL_BLOCK = 64
SEL_TOPN = 4
WINDOW = 512
SSD_INNER = 2 * D_MODEL
SSD_HEAD_DIM = 64
SSD_HEADS = SSD_INNER // SSD_HEAD_DIM
SSD_GROUPS = 4
SSD_HPG = SSD_HEADS // SSD_GROUPS
SSD_STATE = 128
SSD_CONV_DIM = SSD_INNER + 2 * SSD_GROUPS * SSD_STATE
CONV_WIDTH = 4
SSD_CHUNK = 128
D_FF = 4 * D_MODEL
Q_BLOCK = 128
EPS = 1e-6
NEG = -1e30
IN_SPLITS = (DSA_HEADS * HEAD_DIM, DSA_KV_DIM, DSA_KV_DIM, IDX_HEADS * IDX_DIM, IDX_DIM, IDX_HEADS,
             NSA_HEADS * HEAD_DIM, 6 * NSA_KV_HEADS * HEAD_DIM, 3 * NSA_HEADS,
             SSD_INNER, SSD_CONV_DIM, SSD_HEADS, 3 * D_MODEL)

V7X_VMEM_LIMIT_BYTES = 56 * 1024 * 1024
LANES = 128

U16_WIDTH = 10240
U16_XBC_BLK = 0
U16_MG_BLK = 1
U16_Z_BLK = 3
U16_DQ_BLK = 8
U16_NQ_BLK = 9
U32_WIDTH = 1792
U32_NKV_BLK512 = 0
U32_DS_BLK512 = 2
U32_NG_BLK128 = 12
U32_DT_BLK128 = 13
DS_DK, DS_DV, DS_IQ, DS_IK, DS_IW = 0, 64, 128, 384, 416

KEY_TILE = 256
BISECT_ITERS = 40
HEAD_UNROLL = 4


def _cparams(sem):
    return pltpu.CompilerParams(dimension_semantics=sem, vmem_limit_bytes=V7X_VMEM_LIMIT_BYTES)


def _rms(x, g):
    return x * lax.rsqrt(jnp.mean(x * x, axis=-1, keepdims=True) + EPS) * g


def _dot_nt(a, b):
    return lax.dot_general(a, b, (((1,), (1,)), ((), ())), preferred_element_type=F32)


def _dot(a, b):
    return jnp.dot(a, b, preferred_element_type=F32)


def _dot_hi(a, b):
    return jnp.dot(a, b, preferred_element_type=F32, precision=lax.Precision.HIGHEST)


def _norm_matmul_kernel(x_ref, g_ref, w_ref, o_ref, h_ref):
    @pl.when(pl.program_id(1) == 0)
    def _():
        h_ref[...] = _rms(x_ref[...], g_ref[...]).astype(BF16)

    o_ref[...] = _dot(h_ref[...], w_ref[...]).astype(o_ref.dtype)


def _norm_matmul(x, g, w, out_dtype, tm, tn):
    m, k = x.shape
    n = w.shape[1]
    return pl.pallas_call(
        _norm_matmul_kernel,
        grid=(m // tm, n // tn),
        in_specs=[pl.BlockSpec((tm, k), lambda i, j: (i, 0)),
                  pl.BlockSpec((1, k), lambda i, j: (0, 0)),
                  pl.BlockSpec((k, tn), lambda i, j: (0, j))],
        out_specs=pl.BlockSpec((tm, tn), lambda i, j: (i, j)),
        out_shape=jax.ShapeDtypeStruct((m, n), out_dtype),
        scratch_shapes=[pltpu.VMEM((tm, k), BF16)],
        compiler_params=_cparams(("parallel", "arbitrary")),
        name="norm_matmul",
    )(x, g, w)


def _flash_heads(qn_ref, groups, bias_of, nkt, n_heads, s_ref, p_ref, m_ref, a_ref, acc_ref):
    m_ref[...] = jnp.full(m_ref.shape, -jnp.inf, F32)
    acc_ref[...] = jnp.zeros(acc_ref.shape, F32)

    def tile(kt, c):
        off = pl.multiple_of(kt * KEY_TILE, KEY_TILE)
        for r0, nr, k_ref, _ in groups:
            s_ref[r0:r0 + nr, 0:KEY_TILE] = _dot_nt(qn_ref[r0:r0 + nr, :], k_ref[pl.ds(off, KEY_TILE), :])

        def head(h, c2):
            r = pl.multiple_of(h * Q_BLOCK, Q_BLOCK)
            s = s_ref[pl.ds(r, Q_BLOCK), 0:KEY_TILE] + bias_of(h, kt)
            m_old = m_ref[pl.ds(r, Q_BLOCK), :]
            m_new = jnp.maximum(m_old, jnp.max(s, axis=-1, keepdims=True))
            a_ref[pl.ds(r, Q_BLOCK), :] = jnp.exp(m_old - m_new)
            m_ref[pl.ds(r, Q_BLOCK), :] = m_new
            p_ref[pl.ds(r, Q_BLOCK), 0:KEY_TILE] = jnp.exp(
                s - pltpu.repeat(m_new, KEY_TILE // LANES, axis=1)).astype(BF16)
            return c2

        lax.fori_loop(0, n_heads, head, 0, unroll=HEAD_UNROLL)
        for r0, nr, _, vx_ref in groups:
            pv = _dot(p_ref[r0:r0 + nr, 0:KEY_TILE], vx_ref[pl.ds(off, KEY_TILE), :])
            acc_ref[r0:r0 + nr, :] = a_ref[r0:r0 + nr, :] * acc_ref[r0:r0 + nr, :] + pv
        return c

    lax.fori_loop(0, nkt, tile, 0)


def _norm_heads(q_ref, g_ref, qn_ref, n_heads):
    scale = HEAD_DIM ** -0.5
    for h in range(n_heads):
        qh = q_ref[0, :, h * HEAD_DIM:(h + 1) * HEAD_DIM].astype(F32)
        qn_ref[h * Q_BLOCK:(h + 1) * Q_BLOCK, :] = (_rms(qh, g_ref[...]) * scale).astype(BF16)


def _dsa_kernel(q_ref, sq_ref, iwt_ref, sk_ref, gq_ref, gk_ref, o_ref,
                kn_ref, vx_ref, ik_ref, idx_ref, bias_ref, iqs_ref, pq_ref,
                qn_ref, s_ref, p_ref, m_ref, a_ref, acc_ref, *, seq, n_sel):
    i = pl.program_id(1)
    nq = Q_BLOCK
    kt_shape = (KEY_TILE, nq)

    @pl.when(i == 0)
    def _():
        def rows(r, c):
            off = pl.multiple_of(r * 256, 256)
            sk = sk_ref[0, pl.ds(off, 256), :]
            kn_ref[pl.ds(off, 256), :] = _rms(sk[:, DS_DK:DS_DK + 64], gk_ref[...]).astype(BF16)
            vx_ref[pl.ds(off, 256), :] = jnp.concatenate(
                [sk[:, DS_DV:DS_DV + 64], jnp.ones((256, 64), F32)], axis=-1).astype(BF16)
            ik_ref[pl.ds(off, 256), :] = sk[:, DS_IK:DS_IK + IDX_DIM].astype(BF16)
            return c

        lax.fori_loop(0, seq // 256, rows, 0)

    start = i * nq
    nkt = (start + nq + KEY_TILE - 1) // KEY_TILE
    t_row = (start + lax.broadcasted_iota(jnp.int32, (1, nq), 1)).astype(F32)
    sub_pos = lax.broadcasted_iota(jnp.int32, (KEY_TILE, 1), 0).astype(F32)

    sq = sq_ref[0]
    for h in range(IDX_HEADS):
        iqs_ref[h * nq:(h + 1) * nq, :] = sq[:, DS_IQ + h * IDX_DIM:DS_IQ + (h + 1) * IDX_DIM].astype(BF16)
    iwt = iwt_ref[0]

    def idx_tile(kt, carry):
        rmin, rmax = carry
        off = pl.multiple_of(kt * KEY_TILE, KEY_TILE)
        r = _dot_nt(ik_ref[pl.ds(off, KEY_TILE), :], iqs_ref[...])
        acc = jnp.zeros(kt_shape, F32)
        for h in range(IDX_HEADS):
            acc = acc + iwt[h:h + 1, :] * jnp.maximum(r[:, h * nq:(h + 1) * nq], 0.0)
        valid = (sub_pos + (kt * KEY_TILE).astype(F32)) <= t_row
        idx_ref[kt] = jnp.where(valid, acc, -jnp.inf)
        rmin = jnp.minimum(rmin, jnp.min(jnp.where(valid, acc, jnp.inf), axis=0, keepdims=True))
        rmax = jnp.maximum(rmax, jnp.max(jnp.where(valid, acc, -jnp.inf), axis=0, keepdims=True))
        return rmin, rmax

    rmin, rmax = lax.fori_loop(0, nkt, idx_tile,
                               (jnp.full((1, nq), jnp.inf, F32), jnp.full((1, nq), -jnp.inf, F32)))

    kf = float(n_sel)

    def count_gt(thr):
        def body(kt, acc):
            return acc + jnp.where(idx_ref[kt] > thr, 1.0, 0.0)

        return jnp.sum(lax.fori_loop(0, nkt, body, jnp.zeros(kt_shape, F32)), axis=0, keepdims=True)

    n_valid = t_row + 1.0
    lo0 = rmin - (jnp.abs(rmin) * 0.01 + 1.0)

    def bis_cond(c):
        it, _, _, clo, _ = c
        return jnp.logical_and(it < BISECT_ITERS, jnp.max(clo) > kf)

    def bis_body(c):
        it, lo, hi, clo, chi = c
        mid = 0.5 * (lo + hi)
        cnt = count_gt(mid)
        ge = cnt >= kf
        return (it + 1, jnp.where(ge, mid, lo), jnp.where(ge, hi, mid),
                jnp.where(ge, cnt, clo), jnp.where(ge, chi, cnt))

    _, lo, hi, clo, chi = lax.while_loop(
        bis_cond, bis_body, (jnp.int32(0), lo0, rmax, n_valid, jnp.zeros((1, nq), F32)))

    pq_ref[...] = jnp.full((1, nq), float(seq), F32)

    @pl.when(jnp.max(clo) > kf)
    def _():
        quota = kf - chi

        def count_tie_le(pm):
            def body(kt, acc):
                v = idx_ref[kt]
                pos = sub_pos + (kt * KEY_TILE).astype(F32)
                tie = (v > lo) & (v <= hi) & (pos <= pm)
                return acc + jnp.where(tie, 1.0, 0.0)

            return jnp.sum(lax.fori_loop(0, nkt, body, jnp.zeros(kt_shape, F32)), axis=0, keepdims=True)

        def pbody(_, c):
            plo, phi = c
            mid = jnp.floor(0.5 * (plo + phi))
            ge = count_tie_le(mid) >= quota
            return jnp.where(ge, plo, mid), jnp.where(ge, mid, phi)

        steps = int(math.ceil(math.log2(seq))) + 1
        _, phi = lax.fori_loop(0, steps, pbody,
                               (jnp.full((1, nq), -1.0, F32), jnp.full((1, nq), float(seq - 1), F32)))
        pq_ref[...] = jnp.where(clo > kf, phi, float(seq))

    p_lim = pq_ref[...]
    eye = jnp.where(lax.broadcasted_iota(jnp.int32, (nq, nq), 0) == lax.broadcasted_iota(jnp.int32, (nq, nq), 1),
                    1.0, 0.0).astype(BF16)

    def mask_tile(kt, c):
        v = idx_ref[kt]
        pos = sub_pos + (kt * KEY_TILE).astype(F32)
        sel = (v > hi) | ((v > lo) & (pos <= p_lim))
        hit = _dot_nt(eye, jnp.where(sel, 1.0, 0.0).astype(BF16))
        bias_ref[kt] = jnp.where(hit > 0.5, 0.0, NEG)
        return c

    lax.fori_loop(0, nkt, mask_tile, 0)

    _norm_heads(q_ref, gq_ref, qn_ref, DSA_HEADS)
    _flash_heads(qn_ref, [(0, DSA_HEADS * nq, kn_ref, vx_ref)], lambda h, kt: bias_ref[kt], nkt, DSA_HEADS,
                 s_ref, p_ref, m_ref, a_ref, acc_ref)
    for h in range(0, DSA_HEADS, 2):
        pair = []
        for hh in (h, h + 1):
            acc = acc_ref[hh * nq:(hh + 1) * nq, :]
            pair.append(acc[:, :HEAD_DIM] / acc[:, HEAD_DIM:])
        o_ref[0, :, h * HEAD_DIM:(h + 2) * HEAD_DIM] = jnp.concatenate(pair, axis=-1).astype(o_ref.dtype)


def _dsa(u16, u32, iwt, gq, gk, bsz, seq):
    n_sel = min(DSA_TOPK, seq // 4)
    nqb = seq // Q_BLOCK
    rows = DSA_HEADS * Q_BLOCK
    kern = functools.partial(_dsa_kernel, seq=seq, n_sel=n_sel)
    return pl.pallas_call(
        kern,
        grid=(bsz, nqb),
        in_specs=[pl.BlockSpec((1, Q_BLOCK, 1024), lambda b, i: (b, i, U16_DQ_BLK)),
                  pl.BlockSpec((1, Q_BLOCK, 512), lambda b, i: (b, i, U32_DS_BLK512)),
                  pl.BlockSpec((1, IDX_HEADS, Q_BLOCK), lambda b, i: (b, 0, i)),
                  pl.BlockSpec((1, seq, 512), lambda b, i: (b, 0, U32_DS_BLK512)),
                  pl.BlockSpec((1, HEAD_DIM), lambda b, i: (0, 0)),
                  pl.BlockSpec((1, DSA_KV_DIM), lambda b, i: (0, 0))],
        out_specs=pl.BlockSpec((1, Q_BLOCK, DSA_HEADS * HEAD_DIM), lambda b, i: (b, i, 0)),
        out_shape=jax.ShapeDtypeStruct((bsz, seq, DSA_HEADS * HEAD_DIM), BF16),
        scratch_shapes=[pltpu.VMEM((seq, DSA_KV_DIM), BF16),
                        pltpu.VMEM((seq, 2 * DSA_KV_DIM), BF16),
                        pltpu.VMEM((seq, IDX_DIM), BF16),
                        pltpu.VMEM((seq // KEY_TILE, KEY_TILE, Q_BLOCK), F32),
                        pltpu.VMEM((seq // KEY_TILE, Q_BLOCK, KEY_TILE), F32),
                        pltpu.VMEM((IDX_HEADS * Q_BLOCK, IDX_DIM), BF16),
                        pltpu.VMEM((1, Q_BLOCK), F32),
                        pltpu.VMEM((rows, HEAD_DIM), BF16),
                        pltpu.VMEM((rows, KEY_TILE), F32),
                        pltpu.VMEM((rows, KEY_TILE), BF16),
                        pltpu.VMEM((rows, LANES), F32),
                        pltpu.VMEM((rows, LANES), F32),
                        pltpu.VMEM((rows, 2 * HEAD_DIM), F32)],
        compiler_params=_cparams(("parallel", "arbitrary")),
        name="dsa",
    )(u16, u32, iwt, u32, gq, gk)


def _nsa_compress_kernel(x_ref, pos_ref, w_ref, g_ref, kc_ref, vc_ref):
    for j in range(2):
        for g in range(NSA_KV_HEADS):
            x = x_ref[0, j, g]
            a = _dot((x + pos_ref[j, 0:1, :]).astype(BF16), w_ref[j, 0])
            b = _dot((x + pos_ref[j, 1:2, :]).astype(BF16), w_ref[j, 1])
            out = a + pltpu.roll(b, b.shape[0] - 1, axis=0)
            if j == 0:
                kc_ref[0, g] = _rms(out, g_ref[...]).astype(BF16)
            else:
                vc_ref[0, g] = out.astype(BF16)


def _nsa_compress(x, pos, w, g):
    bsz, _, _, nch, width = x.shape
    return pl.pallas_call(
        _nsa_compress_kernel,
        grid=(bsz,),
        in_specs=[pl.BlockSpec((1, 2, NSA_KV_HEADS, nch, width), lambda b: (b, 0, 0, 0, 0)),
                  pl.BlockSpec((2, 2, width), lambda b: (0, 0, 0)),
                  pl.BlockSpec((2, 2, width, HEAD_DIM), lambda b: (0, 0, 0, 0)),
                  pl.BlockSpec((1, HEAD_DIM), lambda b: (0, 0))],
        out_specs=[pl.BlockSpec((1, NSA_KV_HEADS, nch, HEAD_DIM), lambda b: (b, 0, 0, 0)),
                   pl.BlockSpec((1, NSA_KV_HEADS, nch, HEAD_DIM), lambda b: (b, 0, 0, 0))],
        out_shape=[jax.ShapeDtypeStruct((bsz, NSA_KV_HEADS, nch, HEAD_DIM), BF16),
                   jax.ShapeDtypeStruct((bsz, NSA_KV_HEADS, nch, HEAD_DIM), BF16)],
        compiler_params=_cparams(("parallel",)),
        name="nsa_compress",
    )(x, pos, w, g)


def _nsa_kernel(q_ref, ng_ref, kv_ref, kc_ref, vc_ref, gq_ref, gk_ref, cover_ref, expand_ref, o_ref,
                ks_ref, vsx_ref, kw_ref, vwx_ref, bias_ref, wbias_ref, psum_ref, ocmp_ref, accw_ref,
                qn_ref, s_ref, p_ref, m_ref, a_ref, acc_ref, *, seq):
    i = pl.program_id(1)
    nq = Q_BLOCK
    grows = NSA_GROUP * nq
    n_cmp_pad = kc_ref.shape[2]
    n_blk = seq // SEL_BLOCK
    n_sel = min(SEL_TOPN, n_blk)
    win_keys = min(WINDOW + Q_BLOCK, seq)

    @pl.when(i == 0)
    def _():
        def rows(r, c):
            off = pl.multiple_of(r * 256, 256)
            kv = kv_ref[0, pl.ds(off, 256), :]
            ones = jnp.ones((256, HEAD_DIM), F32)
            for g in range(NSA_KV_HEADS):
                c0 = g * HEAD_DIM
                ks_ref[g, pl.ds(off, 256), :] = _rms(kv[:, c0:c0 + 64], gk_ref[1:2, :]).astype(BF16)
                vsx_ref[g, pl.ds(off, 256), :] = jnp.concatenate(
                    [kv[:, 128 + c0:128 + c0 + 64], ones], axis=-1).astype(BF16)
                kw_ref[g, pl.ds(off, 256), :] = _rms(kv[:, 256 + c0:256 + c0 + 64], gk_ref[2:3, :]).astype(BF16)
                vwx_ref[g, pl.ds(off, 256), :] = jnp.concatenate(
                    [kv[:, 384 + c0:384 + c0 + 64], ones], axis=-1).astype(BF16)
            return c

        lax.fori_loop(0, seq // 256, rows, 0)

    start = i * nq
    nkt = (start + nq + KEY_TILE - 1) // KEY_TILE
    t_i = start + lax.broadcasted_iota(jnp.int32, (nq, 1), 0)
    t_col = t_i.astype(F32)
    lane_pos = lax.broadcasted_iota(jnp.int32, (1, KEY_TILE), 1).astype(F32)

    _norm_heads(q_ref, gq_ref, qn_ref, NSA_HEADS)

    cmp_end = (lax.broadcasted_iota(jnp.int32, (1, n_cmp_pad), 1) * CMP_STRIDE + (CMP_BLOCK - 1)).astype(F32)
    vis = cmp_end <= t_col
    for g in range(NSA_KV_HEADS):
        s_ref[g * grows:(g + 1) * grows, 0:n_cmp_pad] = _dot_nt(qn_ref[g * grows:(g + 1) * grows, :], kc_ref[0, g])
    psum_ref[...] = jnp.zeros(psum_ref.shape, F32)

    def cmp_head(h, c):
        r = pl.multiple_of(h * nq, nq)
        s = jnp.where(vis, s_ref[pl.ds(r, nq), 0:n_cmp_pad], NEG)
        e = jnp.exp(s - jnp.max(s, axis=-1, keepdims=True))
        p = jnp.where(vis, e / jnp.sum(e, axis=-1, keepdims=True), 0.0)
        p_ref[pl.ds(r, nq), 0:n_cmp_pad] = p.astype(BF16)
        g = h // NSA_GROUP
        psum_ref[g] = psum_ref[g] + p
        return c

    lax.fori_loop(0, NSA_HEADS, cmp_head, 0, unroll=HEAD_UNROLL)
    for g in range(NSA_KV_HEADS):
        ocmp_ref[g * grows:(g + 1) * grows, :] = _dot(p_ref[g * grows:(g + 1) * grows, 0:n_cmp_pad], vc_ref[0, g])

    blk_j = lax.broadcasted_iota(jnp.int32, (1, n_blk), 1)
    cur = jnp.right_shift(t_i, int(math.log2(SEL_BLOCK)))
    for g in range(NSA_KV_HEADS):
        imp = _dot_hi(psum_ref[g], cover_ref[...])
        forced = (blk_j == cur) | (blk_j == 0)
        imp = jnp.where(forced, jnp.inf, jnp.where(blk_j > cur, -jnp.inf, imp))
        selb = jnp.zeros((nq, n_blk), jnp.bool_)
        for _ in range(n_sel):
            mx = jnp.max(imp, axis=-1, keepdims=True)
            first = jnp.min(jnp.where(imp == mx, blk_j, n_blk), axis=-1, keepdims=True)
            pick = blk_j == first
            selb = selb | pick
            imp = jnp.where(pick, -jnp.inf, imp)
        selb_bf = jnp.where(selb, 1.0, 0.0).astype(BF16)

        def mask_tile(kt, c, g=g, selb_bf=selb_bf):
            off = pl.multiple_of(kt * KEY_TILE, KEY_TILE)
            hit = _dot(selb_bf, expand_ref[:, pl.ds(off, KEY_TILE)])
            pos = lane_pos + (kt * KEY_TILE).astype(F32)
            bias_ref[g, kt] = jnp.where((hit > 0.5) & (pos <= t_col), 0.0, NEG)
            return c

        lax.fori_loop(0, nkt, mask_tile, 0)

    groups = [(g * grows, grows, ks_ref.at[g], vsx_ref.at[g]) for g in range(NSA_KV_HEADS)]
    _flash_heads(qn_ref, groups, lambda h, kt: bias_ref[h // NSA_GROUP, kt], nkt, NSA_HEADS,
                 s_ref, p_ref, m_ref, a_ref, acc_ref)

    wbase = pl.multiple_of(jnp.maximum(start + nq - win_keys, 0), Q_BLOCK)
    wpos = wbase.astype(F32) + lax.broadcasted_iota(jnp.int32, (1, win_keys), 1).astype(F32)
    wbias_ref[...] = jnp.where((wpos <= t_col) & (wpos > t_col - float(WINDOW)), 0.0, NEG)
    for g in range(NSA_KV_HEADS):
        s_ref[g * grows:(g + 1) * grows, 0:win_keys] = _dot_nt(
            qn_ref[g * grows:(g + 1) * grows, :], kw_ref[g, pl.ds(wbase, win_keys), :])

    def win_head(h, c):
        r = pl.multiple_of(h * nq, nq)
        s = s_ref[pl.ds(r, nq), 0:win_keys] + wbias_ref[...]
        p_ref[pl.ds(r, nq), 0:win_keys] = jnp.exp(s - jnp.max(s, axis=-1, keepdims=True)).astype(BF16)
        return c

    lax.fori_loop(0, NSA_HEADS, win_head, 0, unroll=2)
    for g in range(NSA_KV_HEADS):
        accw_ref[g * grows:(g + 1) * grows, :] = _dot(
            p_ref[g * grows:(g + 1) * grows, 0:win_keys], vwx_ref[g, pl.ds(wbase, win_keys), :])

    gates = jax.nn.sigmoid(ng_ref[0])
    for h in range(0, NSA_HEADS, 2):
        pair = []
        for hh in (h, h + 1):
            rs = slice(hh * nq, (hh + 1) * nq)
            acc = acc_ref[rs, :]
            accw = accw_ref[rs, :]
            pair.append(gates[:, 3 * hh:3 * hh + 1] * ocmp_ref[rs, :]
                        + gates[:, 3 * hh + 1:3 * hh + 2] * (acc[:, :HEAD_DIM] / acc[:, HEAD_DIM:])
                        + gates[:, 3 * hh + 2:3 * hh + 3] * (accw[:, :HEAD_DIM] / accw[:, HEAD_DIM:]))
        o_ref[0, :, h * HEAD_DIM:(h + 2) * HEAD_DIM] = jnp.concatenate(pair, axis=-1).astype(o_ref.dtype)


def _nsa(u16, u32, kc, vc, gq, gk, cover, expand, bsz, seq):
    nqb = seq // Q_BLOCK
    nch = kc.shape[2]
    n_blk = seq // SEL_BLOCK
    rows = NSA_HEADS * Q_BLOCK
    win_keys = min(WINDOW + Q_BLOCK, seq)
    s_cols = max(KEY_TILE, win_keys, nch)
    kern = functools.partial(_nsa_kernel, seq=seq)
    return pl.pallas_call(
        kern,
        grid=(bsz, nqb),
        in_specs=[pl.BlockSpec((1, Q_BLOCK, 1024), lambda b, i: (b, i, U16_NQ_BLK)),
                  pl.BlockSpec((1, Q_BLOCK, 128), lambda b, i: (b, i, U32_NG_BLK128)),
                  pl.BlockSpec((1, seq, 512), lambda b, i: (b, 0, U32_NKV_BLK512)),
                  pl.BlockSpec((1, NSA_KV_HEADS, nch, HEAD_DIM), lambda b, i: (b, 0, 0, 0)),
                  pl.BlockSpec((1, NSA_KV_HEADS, nch, HEAD_DIM), lambda b, i: (b, 0, 0, 0)),
                  pl.BlockSpec((1, HEAD_DIM), lambda b, i: (0, 0)),
                  pl.BlockSpec((3, HEAD_DIM), lambda b, i: (0, 0)),
                  pl.BlockSpec((nch, n_blk), lambda b, i: (0, 0)),
                  pl.BlockSpec((n_blk, seq), lambda b, i: (0, 0))],
        out_specs=pl.BlockSpec((1, Q_BLOCK, NSA_HEADS * HEAD_DIM), lambda b, i: (b, i, 0)),
        out_shape=jax.ShapeDtypeStruct((bsz, seq, NSA_HEADS * HEAD_DIM), BF16),
        scratch_shapes=[pltpu.VMEM((NSA_KV_HEADS, seq, HEAD_DIM), BF16),
                        pltpu.VMEM((NSA_KV_HEADS, seq, 2 * HEAD_DIM), BF16),
                        pltpu.VMEM((NSA_KV_HEADS, seq, HEAD_DIM), BF16),
                        pltpu.VMEM((NSA_KV_HEADS, seq, 2 * HEAD_DIM), BF16),
                        pltpu.VMEM((NSA_KV_HEADS, seq // KEY_TILE, Q_BLOCK, KEY_TILE), F32),
                        pltpu.VMEM((Q_BLOCK, win_keys), F32),
                        pltpu.VMEM((NSA_KV_HEADS, Q_BLOCK, nch), F32),
                        pltpu.VMEM((rows, HEAD_DIM), F32),
                        pltpu.VMEM((rows, 2 * HEAD_DIM), F32),
                        pltpu.VMEM((rows, HEAD_DIM), BF16),
                        pltpu.VMEM((rows, s_cols), F32),
                        pltpu.VMEM((rows, s_cols), BF16),
                        pltpu.VMEM((rows, LANES), F32),
                        pltpu.VMEM((rows, LANES), F32),
                        pltpu.VMEM((rows, 2 * HEAD_DIM), F32)],
        compiler_params=_cparams(("parallel", "arbitrary")),
        name="nsa",
    )(u16, u32, u32, kc, vc, gq, gk, cover, expand)


def _softplus(x):
    return jnp.maximum(x, 0.0) + jnp.log1p(jnp.exp(-jnp.abs(x)))


def _ssd_kernel(xbc_ref, z_ref, dt_ref, dtt_ref, cw_ref, cb_ref, dtb_ref, dtbt_ref, al_ref, alt_ref,
                d_ref, ng_ref, o_ref, xc_ref, xa_ref, y_ref, st_ref):
    c = pl.program_id(1)
    q = SSD_CHUNK
    gn = SSD_GROUPS * SSD_STATE

    @pl.when(c == 0)
    def _():
        st_ref[...] = jnp.zeros_like(st_ref)
        xc_ref[0:8, :] = jnp.zeros((8, SSD_CONV_DIM), F32)

    @pl.when(c > 0)
    def _():
        xc_ref[0:8, :] = xc_ref[q:q + 8, :]

    xc_ref[8:8 + q, :] = xbc_ref[0].astype(F32)
    for cc in range(SSD_CONV_DIM // 512):
        cs = slice(cc * 512, (cc + 1) * 512)
        acc = cb_ref[:, cs] + cw_ref[CONV_WIDTH - 1:CONV_WIDTH, cs] * xc_ref[8:8 + q, cs]
        for s in range(1, CONV_WIDTH):
            acc = acc + cw_ref[CONV_WIDTH - 1 - s:CONV_WIDTH - s, cs] * xc_ref[8 - s:8 - s + q, cs]
        xa_ref[:, cs] = acc * jax.nn.sigmoid(acc)

    dt = _softplus(dt_ref[0][:, :SSD_HEADS] + dtb_ref[...])
    dtt = _softplus(dtt_ref[0] + dtbt_ref[...])
    a = -jnp.exp(al_ref[...])
    at = -jnp.exp(alt_ref[...])
    ri = lax.broadcasted_iota(jnp.int32, (q, q), 0)
    ci = lax.broadcasted_iota(jnp.int32, (q, q), 1)
    tril = ri >= ci
    acum = _dot_hi(jnp.where(tril, 1.0, 0.0), dt * a)
    acumt = _dot_hi(dtt * at, jnp.where(ri <= ci, 1.0, 0.0))
    alast = acum[q - 1:q, :]
    decay_in = jnp.exp(acum)
    wst = jnp.exp(alast - acum) * dt
    cdecay = jnp.exp(alast)

    for g in range(SSD_GROUPS):
        bmat = xa_ref[:, SSD_INNER + g * SSD_STATE:SSD_INNER + (g + 1) * SSD_STATE]
        cmat = xa_ref[:, SSD_INNER + gn + g * SSD_STATE:SSD_INNER + gn + (g + 1) * SSD_STATE].astype(BF16)
        cb = _dot_nt(cmat, bmat.astype(BF16))
        bt = bmat.T.astype(BF16)
        for hh in range(SSD_HPG):
            h = g * SSD_HPG + hh
            xs = xa_ref[:, h * SSD_HEAD_DIM:(h + 1) * SSD_HEAD_DIM]
            seg = acum[:, h:h + 1] - acumt[h:h + 1, :]
            lmat = jnp.exp(jnp.where(tril, seg, -jnp.inf))
            m = cb * lmat * dtt[h:h + 1, :]
            y = _dot(m.astype(BF16), xs.astype(BF16))
            y = y + _dot(cmat, st_ref[h].astype(BF16)) * decay_in[:, h:h + 1]
            y = y + xs * d_ref[:, h:h + 1]
            y_ref[:, h * SSD_HEAD_DIM:(h + 1) * SSD_HEAD_DIM] = y
            st_new = _dot(bt, (xs * wst[:, h:h + 1]).astype(BF16))
            st_ref[h] = st_ref[h] * cdecay[:, h:h + 1] + st_new

    z = z_ref[0].astype(F32)
    yz = y_ref[...] * (z * jax.nn.sigmoid(z))
    o_ref[0] = _rms(yz, ng_ref[...]).astype(o_ref.dtype)


def _ssd(u16, u32, dtt, cw, cb, dtb, al, d, ng, bsz, seq):
    nc = seq // SSD_CHUNK
    q = SSD_CHUNK
    full = lambda shape: pl.BlockSpec(shape, lambda b, c: (0,) * len(shape))
    return pl.pallas_call(
        _ssd_kernel,
        grid=(bsz, nc),
        in_specs=[pl.BlockSpec((1, q, SSD_CONV_DIM), lambda b, c: (b, c, U16_XBC_BLK)),
                  pl.BlockSpec((1, q, SSD_INNER), lambda b, c: (b, c, U16_Z_BLK)),
                  pl.BlockSpec((1, q, 128), lambda b, c: (b, c, U32_DT_BLK128)),
                  pl.BlockSpec((1, SSD_HEADS, q), lambda b, c: (b, 0, c)),
                  full((CONV_WIDTH, SSD_CONV_DIM)), full((1, SSD_CONV_DIM)),
                  full((1, SSD_HEADS)), full((SSD_HEADS, 1)),
                  full((1, SSD_HEADS)), full((SSD_HEADS, 1)),
                  full((1, SSD_HEADS)), full((1, SSD_INNER))],
        out_specs=pl.BlockSpec((1, q, SSD_INNER), lambda b, c: (b, c, 0)),
        out_shape=jax.ShapeDtypeStruct((bsz, seq, SSD_INNER), BF16),
        scratch_shapes=[pltpu.VMEM((q + 8, SSD_CONV_DIM), F32),
                        pltpu.VMEM((q, SSD_CONV_DIM), F32),
                        pltpu.VMEM((q, SSD_INNER), F32),
                        pltpu.VMEM((SSD_HEADS, SSD_STATE, SSD_HEAD_DIM), F32)],
        compiler_params=_cparams(("parallel", "arbitrary")),
        name="ssd",
    )(u16, u16, u32, dtt, cw, cb, dtb.reshape(1, -1), dtb.reshape(-1, 1), al.reshape(1, -1), al.reshape(-1, 1),
      d.reshape(1, -1), ng.reshape(1, -1))


def _merge_kernel(x_ref, ya_ref, yb_ref, yc_ref, mg_ref, wa_ref, wb_ref, wc_ref, wo_ref, o_ref):
    d = D_MODEL
    mg = mg_ref[...].astype(F32)
    mix = jax.nn.sigmoid(mg[:, 0:d]) * _dot(ya_ref[...], wa_ref[...])
    mix = mix + jax.nn.sigmoid(mg[:, d:2 * d]) * _dot(yb_ref[...], wb_ref[...])
    mix = mix + jax.nn.sigmoid(mg[:, 2 * d:3 * d]) * _dot(yc_ref[...], wc_ref[...])
    o_ref[...] = x_ref[...] + _dot(mix.astype(BF16), wo_ref[...])


def _merge(x, ya, yb, yc, u16, wa, wb, wc, wo, tm):
    m, d = x.shape
    full = lambda a: pl.BlockSpec(a.shape, lambda i: (0, 0))
    return pl.pallas_call(
        _merge_kernel,
        grid=(m // tm,),
        in_specs=[pl.BlockSpec((tm, d), lambda i: (i, 0)),
                  pl.BlockSpec((tm, ya.shape[1]), lambda i: (i, 0)),
                  pl.BlockSpec((tm, yb.shape[1]), lambda i: (i, 0)),
                  pl.BlockSpec((tm, yc.shape[1]), lambda i: (i, 0)),
                  pl.BlockSpec((tm, 3 * d), lambda i: (i, U16_MG_BLK)),
                  full(wa), full(wb), full(wc), full(wo)],
        out_specs=pl.BlockSpec((tm, d), lambda i: (i, 0)),
        out_shape=jax.ShapeDtypeStruct((m, d), F32),
        compiler_params=_cparams(("parallel",)),
        name="merge",
    )(x, ya, yb, yc, u16, wa, wb, wc, wo)


def _ffn_kernel(x_ref, g_ref, w1_ref, w2_ref, o_ref, h_ref, acc_ref):
    j = pl.program_id(1)

    @pl.when(j == 0)
    def _():
        h_ref[...] = _rms(x_ref[...], g_ref[...]).astype(BF16)
        acc_ref[...] = x_ref[...]

    a = jnp.maximum(_dot(h_ref[...], w1_ref[...]), 0.0)
    acc_ref[...] += _dot((a * a).astype(BF16), w2_ref[...])

    @pl.when(j == pl.num_programs(1) - 1)
    def _():
        o_ref[...] = acc_ref[...]


def _ffn(x, g, w1, w2, tm, tf):
    m, d = x.shape
    f = w1.shape[1]
    return pl.pallas_call(
        _ffn_kernel,
        grid=(m // tm, f // tf),
        in_specs=[pl.BlockSpec((tm, d), lambda i, j: (i, 0)),
                  pl.BlockSpec((1, d), lambda i, j: (0, 0)),
                  pl.BlockSpec((d, tf), lambda i, j: (0, j)),
                  pl.BlockSpec((tf, d), lambda i, j: (j, 0))],
        out_specs=pl.BlockSpec((tm, d), lambda i, j: (i, 0)),
        out_shape=jax.ShapeDtypeStruct((m, d), F32),
        scratch_shapes=[pltpu.VMEM((tm, d), BF16), pltpu.VMEM((tm, d), F32)],
        compiler_params=_cparams(("parallel", "arbitrary")),
        name="ffn",
    )(x, g, w1, w2)


def _prep_w_in(w):
    offs = [0] + [int(o) for o in np.cumsum(IN_SPLITS)]
    seg = lambda k: w[:, offs[k]:offs[k + 1]]
    dq, dk, dv, iq, ik, iw, nq, nkv, ng, sz, sxbc, sdt, mg = [seg(k) for k in range(13)]
    zeros = lambda n: jnp.zeros((w.shape[0], n), w.dtype)
    w16 = jnp.concatenate([sxbc, mg, sz, dq, nq], axis=1).astype(BF16)
    w32 = jnp.concatenate([nkv[:, 256:768], nkv[:, 0:256], zeros(256),
                           dk, dv, iq, ik, iw, zeros(512 - 424),
                           ng, zeros(128 - ng.shape[1]),
                           sdt, zeros(128 - sdt.shape[1])], axis=1).astype(BF16)
    return w16, w32


def _pick_tile(n, pref):
    t = min(n, pref)
    while n % t:
        t //= 2
    return t


def kernel(x, norm1_g, w_in, dsa_q_norm, dsa_k_norm, nsa_q_norm, nsa_k_norm, nsa_cmp_pos, nsa_cmp_w,
           ssd_conv_w, ssd_conv_b, ssd_dt_bias, ssd_a_log, ssd_d, ssd_norm_g,
           w_br_dsa, w_br_nsa, w_br_ssd, w_out, norm2_g, w_ff1, w_ff2):
    bsz, seq, d = x.shape
    m = bsz * seq
    depth = w_in.shape[0]
    nch = seq // CMP_STRIDE
    n_blk = seq // SEL_BLOCK
    half = CMP_BLOCK // 2

    cmp_start = np.arange(nch) * CMP_STRIDE
    blk_start = np.arange(n_blk) * SEL_BLOCK
    cover = ((cmp_start[:, None] < blk_start[None, :] + SEL_BLOCK)
             & (cmp_start[:, None] + CMP_BLOCK > blk_start[None, :])
             & (np.arange(nch)[:, None] < (seq - CMP_BLOCK) // CMP_STRIDE + 1)).astype(np.float32)
    expand = (np.arange(seq)[None, :] // SEL_BLOCK == np.arange(n_blk)[:, None]).astype(np.float32)
    cover = jnp.asarray(cover)
    expand = jnp.asarray(expand, dtype=BF16)

    tm = _pick_tile(m, 1024)
    xf = x.reshape(m, d)
    for l in range(depth):
        w16, w32 = _prep_w_in(w_in[l])
        g1 = norm1_g[l].reshape(1, d)
        u16 = _norm_matmul(xf, g1, w16, BF16, tm, 2048).reshape(bsz, seq, U16_WIDTH)
        u32 = _norm_matmul(xf, g1, w32, F32, tm, U32_WIDTH).reshape(bsz, seq, U32_WIDTH)

        iw0 = U32_DS_BLK512 * 512 + DS_IW
        iwt = u32[:, :, iw0:iw0 + IDX_HEADS].transpose(0, 2, 1)
        ya = _dsa(u16, u32, iwt, dsa_q_norm[l].reshape(1, -1), dsa_k_norm[l].reshape(1, -1), bsz, seq)

        kcvc = u32[:, :, 512:768].reshape(bsz, nch, CMP_STRIDE, 2, NSA_KV_HEADS, HEAD_DIM)
        kcvc = kcvc.transpose(0, 3, 4, 1, 2, 5).reshape(bsz, 2, NSA_KV_HEADS, nch, CMP_STRIDE * HEAD_DIM)
        cpos = nsa_cmp_pos[l].reshape(2, 2, half * HEAD_DIM)
        cw = nsa_cmp_w[l].reshape(2, 2, half * HEAD_DIM, HEAD_DIM).astype(BF16)
        kc, vc = _nsa_compress(kcvc, cpos, cw, nsa_k_norm[l][0:1])
        yb = _nsa(u16, u32, kc, vc, nsa_q_norm[l].reshape(1, -1), nsa_k_norm[l], cover, expand, bsz, seq)

        dt0 = U32_DT_BLK128 * 128
        dtt = u32[:, :, dt0:dt0 + SSD_HEADS].transpose(0, 2, 1)
        yc = _ssd(u16, u32, dtt, ssd_conv_w[l], ssd_conv_b[l].reshape(1, -1), ssd_dt_bias[l], ssd_a_log[l],
                  ssd_d[l], ssd_norm_g[l], bsz, seq)

        xf = _merge(xf, ya.reshape(m, -1), yb.reshape(m, -1), yc.reshape(m, -1), u16.reshape(m, U16_WIDTH),
                    w_br_dsa[l].astype(BF16), w_br_nsa[l].astype(BF16), w_br_ssd[l].astype(BF16),
                    w_out[l].astype(BF16), _pick_tile(m, 256))
        xf = _ffn(xf, norm2_g[l].reshape(1, d), w_ff1[l].astype(BF16), w_ff2[l].astype(BF16),
                  _pick_tile(m, 1024), 1024)
    return xf.reshape(bsz, seq, d)
```

```python
import functools
import math

import numpy as np
import jax
import jax.numpy as jnp
from jax import lax
from jax.experimental import pallas as pl
from jax.experimental.pallas import tpu as pltpu

F32 = jnp.float32
BF16 = jnp.bfloat16

D_MODEL = 1024
HEAD_DIM = 64
DSA_HEADS = 16
DSA_KV_DIM = 64
IDX_HEADS = 8
IDX_DIM = 32
DSA_TOPK = 256
NSA_HEADS = 16
NSA_KV_HEADS = 2
NSA_GROUP = NSA_HEADS // NSA_KV_HEADS
CMP_BLOCK = 32
CMP_STRIDE = 16
SEL_BLOCK = 64
SEL_TOPN = 4
WINDOW = 512
SSD_INNER = 2 * D_MODEL
SSD_HEAD_DIM = 64
SSD_HEADS = SSD_INNER // SSD_HEAD_DIM
SSD_GROUPS = 4
SSD_HPG = SSD_HEADS // SSD_GROUPS
SSD_STATE = 128
SSD_CONV_DIM = SSD_INNER + 2 * SSD_GROUPS * SSD_STATE
CONV_WIDTH = 4
SSD_CHUNK = 128
D_FF = 4 * D_MODEL
Q_BLOCK = 128
EPS = 1e-6
NEG = -1e30
IN_SPLITS = (DSA_HEADS * HEAD_DIM, DSA_KV_DIM, DSA_KV_DIM, IDX_HEADS * IDX_DIM, IDX_DIM, IDX_HEADS,
             NSA_HEADS * HEAD_DIM, 6 * NSA_KV_HEADS * HEAD_DIM, 3 * NSA_HEADS,
             SSD_INNER, SSD_CONV_DIM, SSD_HEADS, 3 * D_MODEL)

V7X_VMEM_LIMIT_BYTES = 56 * 1024 * 1024
LANES = 128

U16_WIDTH = 10240
U16_XBC_BLK = 0
U16_MG_BLK = 1
U16_Z_BLK = 3
U16_DQ_BLK = 8
U16_NQ_BLK = 9
U32_WIDTH = 1792
U32_NKV_BLK512 = 0
U32_DS_BLK512 = 2
U32_NG_BLK128 = 12
U32_DT_BLK128 = 13
DS_DK, DS_DV, DS_IQ, DS_IK, DS_IW = 0, 64, 128, 384, 416

KEY_TILE = 256
BISECT_ITERS = 40
HEAD_UNROLL = 4
HEAD_STEP = 4
BISECT_UNROLL = 4


def _cparams(sem):
    return pltpu.CompilerParams(dimension_semantics=sem, vmem_limit_bytes=V7X_VMEM_LIMIT_BYTES)


def _rms(x, g):
    return x * lax.rsqrt(jnp.mean(x * x, axis=-1, keepdims=True) + EPS) * g


def _dot_nt(a, b):
    return lax.dot_general(a, b, (((1,), (1,)), ((), ())), preferred_element_type=F32)


def _dot(a, b):
    return jnp.dot(a, b, preferred_element_type=F32)


def _dot_hi(a, b):
    return jnp.dot(a, b, preferred_element_type=F32, precision=lax.Precision.HIGHEST)


def _norm_matmul_kernel(x_ref, g_ref, w_ref, o_ref, h_ref):
    @pl.when(pl.program_id(1) == 0)
    def _():
        h_ref[...] = _rms(x_ref[...], g_ref[...]).astype(BF16)

    o_ref[...] = _dot(h_ref[...], w_ref[...]).astype(o_ref.dtype)


def _norm_matmul(x, g, w, out_dtype, tm, tn):
    m, k = x.shape
    n = w.shape[1]
    return pl.pallas_call(
        _norm_matmul_kernel,
        grid=(m // tm, n // tn),
        in_specs=[pl.BlockSpec((tm, k), lambda i, j: (i, 0)),
                  pl.BlockSpec((1, k), lambda i, j: (0, 0)),
                  pl.BlockSpec((k, tn), lambda i, j: (0, j))],
        out_specs=pl.BlockSpec((tm, tn), lambda i, j: (i, j)),
        out_shape=jax.ShapeDtypeStruct((m, n), out_dtype),
        scratch_shapes=[pltpu.VMEM((tm, k), BF16)],
        compiler_params=_cparams(("parallel", "arbitrary")),
        name="norm_matmul",
    )(x, g, w)


def _flash_heads(qn_ref, k_ref, vx_ref, bias_ref, nkt, n_heads, heads_per_kv, bufs, m_ref, acc_ref):
    (s0, s1), (p0, p1), (a0, a1) = bufs
    n_hg = n_heads // HEAD_STEP
    rows = HEAD_STEP * Q_BLOCK
    n = nkt * n_hg
    m_ref[...] = jnp.full(m_ref.shape, -jnp.inf, F32)
    acc_ref[...] = jnp.zeros(acc_ref.shape, F32)
    p1[...] = jnp.zeros(p1.shape, BF16)
    a1[...] = jnp.ones(a1.shape, F32)

    def where(j):
        kt = j // n_hg
        hg = j % n_hg
        g = (hg * HEAD_STEP) // heads_per_kv
        return kt, g, pl.multiple_of(kt * KEY_TILE, KEY_TILE), pl.multiple_of(hg * rows, rows)

    def qk(j, s_ref):
        _, g, off, r = where(j)
        s_ref[...] = _dot_nt(qn_ref[pl.ds(r, rows), :], k_ref[g, pl.ds(off, KEY_TILE), :])

    def softmax(j, s_ref, p_ref, a_ref):
        kt, g, _, r = where(j)
        for u in range(HEAD_STEP):
            us = slice(u * Q_BLOCK, (u + 1) * Q_BLOCK)
            ru = pl.multiple_of(r + u * Q_BLOCK, Q_BLOCK)
            s = s_ref[us, :] + bias_ref[g, kt]
            m_old = m_ref[pl.ds(ru, Q_BLOCK), :]
            m_new = jnp.maximum(m_old, jnp.max(s, axis=-1, keepdims=True))
            a_ref[us, :] = jnp.exp(m_old - m_new)
            m_ref[pl.ds(ru, Q_BLOCK), :] = m_new
            p_ref[us, :] = jnp.exp(s - jnp.concatenate([m_new] * (KEY_TILE // LANES), axis=1)).astype(BF16)

    def pv(j, p_ref, a_ref):
        _, g, off, r = where(j)
        acc_ref[pl.ds(r, rows), :] = (a_ref[...] * acc_ref[pl.ds(r, rows), :]
                                      + _dot(p_ref[...], vx_ref[g, pl.ds(off, KEY_TILE), :]))

    qk(0, s0)

    def body(jj, c):
        j = 2 * jj
        qk(j + 1, s1)
        softmax(j, s0, p0, a0)
        pv(jnp.maximum(j - 1, 0), p1, a1)
        qk(jnp.minimum(j + 2, n - 1), s0)
        softmax(j + 1, s1, p1, a1)
        pv(j, p0, a0)
        return c

    lax.fori_loop(0, n // 2, body, 0)
    pv(n - 1, p1, a1)


def _flash_scratch(rows):
    step = HEAD_STEP * Q_BLOCK
    return ([pltpu.VMEM((step, KEY_TILE), F32)] * 2 + [pltpu.VMEM((step, KEY_TILE), BF16)] * 2
            + [pltpu.VMEM((step, LANES), F32)] * 2
            + [pltpu.VMEM((rows, LANES), F32), pltpu.VMEM((rows, 2 * HEAD_DIM), F32)])


def _norm_heads(q_ref, g_ref, qn_ref, n_heads):
    scale = HEAD_DIM ** -0.5
    for h in range(n_heads):
        qh = q_ref[0, :, h * HEAD_DIM:(h + 1) * HEAD_DIM].astype(F32)
        qn_ref[h * Q_BLOCK:(h + 1) * Q_BLOCK, :] = (_rms(qh, g_ref[...]) * scale).astype(BF16)


def _dsa_kernel(q_ref, sq_ref, iwt_ref, sk_ref, gq_ref, gk_ref, o_ref,
                kn_ref, vx_ref, ik_ref, idx_ref, bias_ref, iqs_ref, pq_ref,
                qn_ref, s0_ref, s1_ref, p0_ref, p1_ref, a0_ref, a1_ref, m_ref, acc_ref, *, seq, n_sel):
    i = pl.program_id(1)
    nq = Q_BLOCK
    kt_shape = (KEY_TILE, nq)

    @pl.when(i == 0)
    def _():
        def rows(r, c):
            off = pl.multiple_of(r * 256, 256)
            sk = sk_ref[0, pl.ds(off, 256), :]
            kn_ref[0, pl.ds(off, 256), :] = _rms(sk[:, DS_DK:DS_DK + 64], gk_ref[...]).astype(BF16)
            vx_ref[0, pl.ds(off, 256), :] = jnp.concatenate(
                [sk[:, DS_DV:DS_DV + 64], jnp.ones((256, 64), F32)], axis=-1).astype(BF16)
            ik_ref[pl.ds(off, 256), :] = sk[:, DS_IK:DS_IK + IDX_DIM].astype(BF16)
            return c

        lax.fori_loop(0, seq // 256, rows, 0)

    start = i * nq
    nkt = (start + nq + KEY_TILE - 1) // KEY_TILE
    t_row = (start + lax.broadcasted_iota(jnp.int32, (1, nq), 1)).astype(F32)
    sub_pos = lax.broadcasted_iota(jnp.int32, (KEY_TILE, 1), 0).astype(F32)

    sq = sq_ref[0]
    for h in range(IDX_HEADS):
        iqs_ref[h * nq:(h + 1) * nq, :] = sq[:, DS_IQ + h * IDX_DIM:DS_IQ + (h + 1) * IDX_DIM].astype(BF16)
    iwt = iwt_ref[0]

    def idx_tile(kt, carry):
        rmin, rmax = carry
        off = pl.multiple_of(kt * KEY_TILE, KEY_TILE)
        r = _dot_nt(ik_ref[pl.ds(off, KEY_TILE), :], iqs_ref[...])
        acc = jnp.zeros(kt_shape, F32)
        for h in range(IDX_HEADS):
            acc = acc + iwt[h:h + 1, :] * jnp.maximum(r[:, h * nq:(h + 1) * nq], 0.0)
        valid = (sub_pos + (kt * KEY_TILE).astype(F32)) <= t_row
        idx_ref[kt] = jnp.where(valid, acc, -jnp.inf)
        rmin = jnp.minimum(rmin, jnp.min(jnp.where(valid, acc, jnp.inf), axis=0, keepdims=True))
        rmax = jnp.maximum(rmax, jnp.max(jnp.where(valid, acc, -jnp.inf), axis=0, keepdims=True))
        return rmin, rmax

    rmin, rmax = lax.fori_loop(0, nkt, idx_tile,
                               (jnp.full((1, nq), jnp.inf, F32), jnp.full((1, nq), -jnp.inf, F32)))

    kf = float(n_sel)

    def count_gt(thr):
        def body(kt, acc):
            return acc + jnp.where(idx_ref[kt] > thr, 1.0, 0.0)

        return jnp.sum(lax.fori_loop(0, nkt, body, jnp.zeros(kt_shape, F32)), axis=0, keepdims=True)

    n_valid = t_row + 1.0
    lo0 = rmin - (jnp.abs(rmin) * 0.01 + 1.0)

    def bis_cond(c):
        it, _, _, clo, _ = c
        return jnp.logical_and(it < BISECT_ITERS, jnp.max(clo) > kf)

    def bis_body(c):
        it, lo, hi, clo, chi = c
        for _ in range(BISECT_UNROLL):
            mid = 0.5 * (lo + hi)
            cnt = count_gt(mid)
            ge = cnt >= kf
            lo, hi = jnp.where(ge, mid, lo), jnp.where(ge, hi, mid)
            clo, chi = jnp.where(ge, cnt, clo), jnp.where(ge, chi, cnt)
        return it + BISECT_UNROLL, lo, hi, clo, chi

    _, lo, hi, clo, chi = lax.while_loop(
        bis_cond, bis_body, (jnp.int32(0), lo0, rmax, n_valid, jnp.zeros((1, nq), F32)))

    pq_ref[...] = jnp.full((1, nq), float(seq), F32)

    @pl.when(jnp.max(clo) > kf)
    def _():
        quota = kf - chi

        def count_tie_le(pm):
            def body(kt, acc):
                v = idx_ref[kt]
                pos = sub_pos + (kt * KEY_TILE).astype(F32)
                tie = (v > lo) & (v <= hi) & (pos <= pm)
                return acc + jnp.where(tie, 1.0, 0.0)

            return jnp.sum(lax.fori_loop(0, nkt, body, jnp.zeros(kt_shape, F32)), axis=0, keepdims=True)

        def pbody(_, c):
            plo, phi = c
            mid = jnp.floor(0.5 * (plo + phi))
            ge = count_tie_le(mid) >= quota
            return jnp.where(ge, plo, mid), jnp.where(ge, mid, phi)

        steps = int(math.ceil(math.log2(seq))) + 1
        _, phi = lax.fori_loop(0, steps, pbody,
                               (jnp.full((1, nq), -1.0, F32), jnp.full((1, nq), float(seq - 1), F32)))
        pq_ref[...] = jnp.where(clo > kf, phi, float(seq))

    p_lim = pq_ref[...]
    eye = jnp.where(lax.broadcasted_iota(jnp.int32, (nq, nq), 0) == lax.broadcasted_iota(jnp.int32, (nq, nq), 1),
                    1.0, 0.0).astype(BF16)

    def mask_tile(kt, c):
        v = idx_ref[kt]
        pos = sub_pos + (kt * KEY_TILE).astype(F32)
        sel = (v > hi) | ((v > lo) & (pos <= p_lim))
        hit = _dot_nt(eye, jnp.where(sel, 1.0, 0.0).astype(BF16))
        bias_ref[0, kt] = jnp.where(hit > 0.5, 0.0, NEG)
        return c

    lax.fori_loop(0, nkt, mask_tile, 0)

    _norm_heads(q_ref, gq_ref, qn_ref, DSA_HEADS)
    _flash_heads(qn_ref, kn_ref, vx_ref, bias_ref, nkt, DSA_HEADS, DSA_HEADS,
                 ((s0_ref, s1_ref), (p0_ref, p1_ref), (a0_ref, a1_ref)), m_ref, acc_ref)
    for h in range(0, DSA_HEADS, 2):
        pair = []
        for hh in (h, h + 1):
            acc = acc_ref[hh * nq:(hh + 1) * nq, :]
            pair.append(acc[:, :HEAD_DIM] / acc[:, HEAD_DIM:])
        o_ref[0, :, h * HEAD_DIM:(h + 2) * HEAD_DIM] = jnp.concatenate(pair, axis=-1).astype(o_ref.dtype)


def _dsa(u16, u32, iwt, gq, gk, bsz, seq):
    n_sel = min(DSA_TOPK, seq // 4)
    nqb = seq // Q_BLOCK
    rows = DSA_HEADS * Q_BLOCK
    kern = functools.partial(_dsa_kernel, seq=seq, n_sel=n_sel)
    return pl.pallas_call(
        kern,
        grid=(bsz, nqb),
        in_specs=[pl.BlockSpec((1, Q_BLOCK, 1024), lambda b, i: (b, i, U16_DQ_BLK)),
                  pl.BlockSpec((1, Q_BLOCK, 512), lambda b, i: (b, i, U32_DS_BLK512)),
                  pl.BlockSpec((1, IDX_HEADS, Q_BLOCK), lambda b, i: (b, 0, i)),
                  pl.BlockSpec((1, seq, 512), lambda b, i: (b, 0, U32_DS_BLK512)),
                  pl.BlockSpec((1, HEAD_DIM), lambda b, i: (0, 0)),
                  pl.BlockSpec((1, DSA_KV_DIM), lambda b, i: (0, 0))],
        out_specs=pl.BlockSpec((1, Q_BLOCK, DSA_HEADS * HEAD_DIM), lambda b, i: (b, i, 0)),
        out_shape=jax.ShapeDtypeStruct((bsz, seq, DSA_HEADS * HEAD_DIM), BF16),
        scratch_shapes=[pltpu.VMEM((1, seq, DSA_KV_DIM), BF16),
                        pltpu.VMEM((1, seq, 2 * DSA_KV_DIM), BF16),
                        pltpu.VMEM((seq, IDX_DIM), BF16),
                        pltpu.VMEM((seq // KEY_TILE, KEY_TILE, Q_BLOCK), F32),
                        pltpu.VMEM((1, seq // KEY_TILE, Q_BLOCK, KEY_TILE), F32),
                        pltpu.VMEM((IDX_HEADS * Q_BLOCK, IDX_DIM), BF16),
                        pltpu.VMEM((1, Q_BLOCK), F32),
                        pltpu.VMEM((rows, HEAD_DIM), BF16)]
                       + _flash_scratch(rows),
        compiler_params=_cparams(("parallel", "arbitrary")),
        name="dsa",
    )(u16, u32, iwt, u32, gq, gk)


def _nsa_compress_kernel(x_ref, pos_ref, w_ref, g_ref, kc_ref, vc_ref):
    for j in range(2):
        for g in range(NSA_KV_HEADS):
            x = x_ref[0, j, g]
            a = _dot((x + pos_ref[j, 0:1, :]).astype(BF16), w_ref[j, 0])
            b = _dot((x + pos_ref[j, 1:2, :]).astype(BF16), w_ref[j, 1])
            out = a + pltpu.roll(b, b.shape[0] - 1, axis=0)
            if j == 0:
                kc_ref[0, g] = _rms(out, g_ref[...]).astype(BF16)
            else:
                vc_ref[0, g] = out.astype(BF16)


def _nsa_compress(x, pos, w, g):
    bsz, _, _, nch, width = x.shape
    return pl.pallas_call(
        _nsa_compress_kernel,
        grid=(bsz,),
        in_specs=[pl.BlockSpec((1, 2, NSA_KV_HEADS, nch, width), lambda b: (b, 0, 0, 0, 0)),
                  pl.BlockSpec((2, 2, width), lambda b: (0, 0, 0)),
                  pl.BlockSpec((2, 2, width, HEAD_DIM), lambda b: (0, 0, 0, 0)),
                  pl.BlockSpec((1, HEAD_DIM), lambda b: (0, 0))],
        out_specs=[pl.BlockSpec((1, NSA_KV_HEADS, nch, HEAD_DIM), lambda b: (b, 0, 0, 0)),
                   pl.BlockSpec((1, NSA_KV_HEADS, nch, HEAD_DIM), lambda b: (b, 0, 0, 0))],
        out_shape=[jax.ShapeDtypeStruct((bsz, NSA_KV_HEADS, nch, HEAD_DIM), BF16),
                   jax.ShapeDtypeStruct((bsz, NSA_KV_HEADS, nch, HEAD_DIM), BF16)],
        compiler_params=_cparams(("parallel",)),
        name="nsa_compress",
    )(x, pos, w, g)


def _nsa_kernel(q_ref, ng_ref, kv_ref, kc_ref, vc_ref, gq_ref, gk_ref, cover_ref, expand_ref, o_ref,
                ks_ref, vsx_ref, kw_ref, vwx_ref, bias_ref, wbias_ref, psum_ref, ocmp_ref, accw_ref,
                qn_ref, s_ref, p_ref, s0_ref, s1_ref, p0_ref, p1_ref, a0_ref, a1_ref, m_ref, acc_ref, *, seq):
    i = pl.program_id(1)
    nq = Q_BLOCK
    grows = NSA_GROUP * nq
    n_cmp_pad = kc_ref.shape[2]
    n_blk = seq // SEL_BLOCK
    n_sel = min(SEL_TOPN, n_blk)
    win_keys = min(WINDOW + Q_BLOCK, seq)

    @pl.when(i == 0)
    def _():
        def rows(r, c):
            off = pl.multiple_of(r * 256, 256)
            kv = kv_ref[0, pl.ds(off, 256), :]
            ones = jnp.ones((256, HEAD_DIM), F32)
            for g in range(NSA_KV_HEADS):
                c0 = g * HEAD_DIM
                ks_ref[g, pl.ds(off, 256), :] = _rms(kv[:, c0:c0 + 64], gk_ref[1:2, :]).astype(BF16)
                vsx_ref[g, pl.ds(off, 256), :] = jnp.concatenate(
                    [kv[:, 128 + c0:128 + c0 + 64], ones], axis=-1).astype(BF16)
                kw_ref[g, pl.ds(off, 256), :] = _rms(kv[:, 256 + c0:256 + c0 + 64], gk_ref[2:3, :]).astype(BF16)
                vwx_ref[g, pl.ds(off, 256), :] = jnp.concatenate(
                    [kv[:, 384 + c0:384 + c0 + 64], ones], axis=-1).astype(BF16)
            return c

        lax.fori_loop(0, seq // 256, rows, 0)

    start = i * nq
    nkt = (start + nq + KEY_TILE - 1) // KEY_TILE
    t_i = start + lax.broadcasted_iota(jnp.int32, (nq, 1), 0)
    t_col = t_i.astype(F32)
    lane_pos = lax.broadcasted_iota(jnp.int32, (1, KEY_TILE), 1).astype(F32)

    _norm_heads(q_ref, gq_ref, qn_ref, NSA_HEADS)

    cmp_end = (lax.broadcasted_iota(jnp.int32, (1, n_cmp_pad), 1) * CMP_STRIDE + (CMP_BLOCK - 1)).astype(F32)
    vis = cmp_end <= t_col
    for g in range(NSA_KV_HEADS):
        s_ref[g * grows:(g + 1) * grows, 0:n_cmp_pad] = _dot_nt(qn_ref[g * grows:(g + 1) * grows, :], kc_ref[0, g])
    psum_ref[...] = jnp.zeros(psum_ref.shape, F32)

    def cmp_head(h, c):
        r = pl.multiple_of(h * nq, nq)
        s = jnp.where(vis, s_ref[pl.ds(r, nq), 0:n_cmp_pad], NEG)
        e = jnp.exp(s - jnp.max(s, axis=-1, keepdims=True))
        p = jnp.where(vis, e / jnp.sum(e, axis=-1, keepdims=True), 0.0)
        p_ref[pl.ds(r, nq), 0:n_cmp_pad] = p.astype(BF16)
        g = h // NSA_GROUP
        psum_ref[g] = psum_ref[g] + p
        return c

    lax.fori_loop(0, NSA_HEADS, cmp_head, 0, unroll=HEAD_UNROLL)
    for g in range(NSA_KV_HEADS):
        ocmp_ref[g * grows:(g + 1) * grows, :] = _dot(p_ref[g * grows:(g + 1) * grows, 0:n_cmp_pad], vc_ref[0, g])

    blk_j = lax.broadcasted_iota(jnp.int32, (1, n_blk), 1)
    cur = jnp.right_shift(t_i, int(math.log2(SEL_BLOCK)))
    for g in range(NSA_KV_HEADS):
        imp = _dot_hi(psum_ref[g], cover_ref[...])
        forced = (blk_j == cur) | (blk_j == 0)
        imp = jnp.where(forced, jnp.inf, jnp.where(blk_j > cur, -jnp.inf, imp))
        selb = jnp.zeros((nq, n_blk), jnp.bool_)
        for _ in range(n_sel):
            mx = jnp.max(imp, axis=-1, keepdims=True)
            first = jnp.min(jnp.where(imp == mx, blk_j, n_blk), axis=-1, keepdims=True)
            pick = blk_j == first
            selb = selb | pick
            imp = jnp.where(pick, -jnp.inf, imp)
        selb_bf = jnp.where(selb, 1.0, 0.0).astype(BF16)

        def mask_tile(kt, c, g=g, selb_bf=selb_bf):
            off = pl.multiple_of(kt * KEY_TILE, KEY_TILE)
            hit = _dot(selb_bf, expand_ref[:, pl.ds(off, KEY_TILE)])
            pos = lane_pos + (kt * KEY_TILE).astype(F32)
            bias_ref[g, kt] = jnp.where((hit > 0.5) & (pos <= t_col), 0.0, NEG)
            return c

        lax.fori_loop(0, nkt, mask_tile, 0)

    _flash_heads(qn_ref, ks_ref, vsx_ref, bias_ref, nkt, NSA_HEADS, NSA_GROUP,
                 ((s0_ref, s1_ref), (p0_ref, p1_ref), (a0_ref, a1_ref)), m_ref, acc_ref)

    wbase = pl.multiple_of(jnp.maximum(start + nq - win_keys, 0), Q_BLOCK)
    wpos = wbase.astype(F32) + lax.broadcasted_iota(jnp.int32, (1, win_keys), 1).astype(F32)
    wbias_ref[...] = jnp.where((wpos <= t_col) & (wpos > t_col - float(WINDOW)), 0.0, NEG)
    for g in range(NSA_KV_HEADS):
        s_ref[g * grows:(g + 1) * grows, 0:win_keys] = _dot_nt(
            qn_ref[g * grows:(g + 1) * grows, :], kw_ref[g, pl.ds(wbase, win_keys), :])

    def win_head(h, c):
        r = pl.multiple_of(h * nq, nq)
        s = s_ref[pl.ds(r, nq), 0:win_keys] + wbias_ref[...]
        p_ref[pl.ds(r, nq), 0:win_keys] = jnp.exp(s - jnp.max(s, axis=-1, keepdims=True)).astype(BF16)
        return c

    lax.fori_loop(0, NSA_HEADS, win_head, 0, unroll=2)
    for g in range(NSA_KV_HEADS):
        accw_ref[g * grows:(g + 1) * grows, :] = _dot(
            p_ref[g * grows:(g + 1) * grows, 0:win_keys], vwx_ref[g, pl.ds(wbase, win_keys), :])

    gates = jax.nn.sigmoid(ng_ref[0])
    for h in range(0, NSA_HEADS, 2):
        pair = []
        for hh in (h, h + 1):
            rs = slice(hh * nq, (hh + 1) * nq)
            acc = acc_ref[rs, :]
            accw = accw_ref[rs, :]
            pair.append(gates[:, 3 * hh:3 * hh + 1] * ocmp_ref[rs, :]
                        + gates[:, 3 * hh + 1:3 * hh + 2] * (acc[:, :HEAD_DIM] / acc[:, HEAD_DIM:])
                        + gates[:, 3 * hh + 2:3 * hh + 3] * (accw[:, :HEAD_DIM] / accw[:, HEAD_DIM:]))
        o_ref[0, :, h * HEAD_DIM:(h + 2) * HEAD_DIM] = jnp.concatenate(pair, axis=-1).astype(o_ref.dtype)


def _nsa(u16, u32, kc, vc, gq, gk, cover, expand, bsz, seq):
    nqb = seq // Q_BLOCK
    nch = kc.shape[2]
    n_blk = seq // SEL_BLOCK
    rows = NSA_HEADS * Q_BLOCK
    win_keys = min(WINDOW + Q_BLOCK, seq)
    s_cols = max(win_keys, nch)
    kern = functools.partial(_nsa_kernel, seq=seq)
    return pl.pallas_call(
        kern,
        grid=(bsz, nqb),
        in_specs=[pl.BlockSpec((1, Q_BLOCK, 1024), lambda b, i: (b, i, U16_NQ_BLK)),
                  pl.BlockSpec((1, Q_BLOCK, 128), lambda b, i: (b, i, U32_NG_BLK128)),
                  pl.BlockSpec((1, seq, 512), lambda b, i: (b, 0, U32_NKV_BLK512)),
                  pl.BlockSpec((1, NSA_KV_HEADS, nch, HEAD_DIM), lambda b, i: (b, 0, 0, 0)),
                  pl.BlockSpec((1, NSA_KV_HEADS, nch, HEAD_DIM), lambda b, i: (b, 0, 0, 0)),
                  pl.BlockSpec((1, HEAD_DIM), lambda b, i: (0, 0)),
                  pl.BlockSpec((3, HEAD_DIM), lambda b, i: (0, 0)),
                  pl.BlockSpec((nch, n_blk), lambda b, i: (0, 0)),
                  pl.BlockSpec((n_blk, seq), lambda b, i: (0, 0))],
        out_specs=pl.BlockSpec((1, Q_BLOCK, NSA_HEADS * HEAD_DIM), lambda b, i: (b, i, 0)),
        out_shape=jax.ShapeDtypeStruct((bsz, seq, NSA_HEADS * HEAD_DIM), BF16),
        scratch_shapes=[pltpu.VMEM((NSA_KV_HEADS, seq, HEAD_DIM), BF16),
                        pltpu.VMEM((NSA_KV_HEADS, seq, 2 * HEAD_DIM), BF16),
                        pltpu.VMEM((NSA_KV_HEADS, seq, HEAD_DIM), BF16),
                        pltpu.VMEM((NSA_KV_HEADS, seq, 2 * HEAD_DIM), BF16),
                        pltpu.VMEM((NSA_KV_HEADS, seq // KEY_TILE, Q_BLOCK, KEY_TILE), F32),
                        pltpu.VMEM((Q_BLOCK, win_keys), F32),
                        pltpu.VMEM((NSA_KV_HEADS, Q_BLOCK, nch), F32),
                        pltpu.VMEM((rows, HEAD_DIM), F32),
                        pltpu.VMEM((rows, 2 * HEAD_DIM), F32),
                        pltpu.VMEM((rows, HEAD_DIM), BF16),
                        pltpu.VMEM((rows, s_cols), F32),
                        pltpu.VMEM((rows, s_cols), BF16)]
                       + _flash_scratch(rows),
        compiler_params=_cparams(("parallel", "arbitrary")),
        name="nsa",
    )(u16, u32, u32, kc, vc, gq, gk, cover, expand)


def _softplus(x):
    return jnp.maximum(x, 0.0) + jnp.log1p(jnp.exp(-jnp.abs(x)))


def _ssd_kernel(xbc_ref, z_ref, dt_ref, dtt_ref, cw_ref, cb_ref, dtb_ref, dtbt_ref, al_ref, alt_ref,
                d_ref, ng_ref, o_ref, xc_ref, xa_ref, y_ref, st_ref):
    c = pl.program_id(1)
    q = SSD_CHUNK
    gn = SSD_GROUPS * SSD_STATE

    @pl.when(c == 0)
    def _():
        st_ref[...] = jnp.zeros_like(st_ref)
        xc_ref[0:8, :] = jnp.zeros((8, SSD_CONV_DIM), F32)

    @pl.when(c > 0)
    def _():
        xc_ref[0:8, :] = xc_ref[q:q + 8, :]

    xc_ref[8:8 + q, :] = xbc_ref[0].astype(F32)
    for cc in range(SSD_CONV_DIM // 512):
        cs = slice(cc * 512, (cc + 1) * 512)
        acc = cb_ref[:, cs] + cw_ref[CONV_WIDTH - 1:CONV_WIDTH, cs] * xc_ref[8:8 + q, cs]
        for s in range(1, CONV_WIDTH):
            acc = acc + cw_ref[CONV_WIDTH - 1 - s:CONV_WIDTH - s, cs] * xc_ref[8 - s:8 - s + q, cs]
        xa_ref[:, cs] = acc * jax.nn.sigmoid(acc)

    dt = _softplus(dt_ref[0][:, :SSD_HEADS] + dtb_ref[...])
    dtt = _softplus(dtt_ref[0] + dtbt_ref[...])
    a = -jnp.exp(al_ref[...])
    at = -jnp.exp(alt_ref[...])
    ri = lax.broadcasted_iota(jnp.int32, (q, q), 0)
    ci = lax.broadcasted_iota(jnp.int32, (q, q), 1)
    tril = ri >= ci
    acum = _dot_hi(jnp.where(tril, 1.0, 0.0), dt * a)
    acumt = _dot_hi(dtt * at, jnp.where(ri <= ci, 1.0, 0.0))
    alast = acum[q - 1:q, :]
    decay_in = jnp.exp(acum)
    wst = jnp.exp(alast - acum) * dt
    cdecay = jnp.exp(alast)

    for g in range(SSD_GROUPS):
        bmat = xa_ref[:, SSD_INNER + g * SSD_STATE:SSD_INNER + (g + 1) * SSD_STATE]
        cmat = xa_ref[:, SSD_INNER + gn + g * SSD_STATE:SSD_INNER + gn + (g + 1) * SSD_STATE].astype(BF16)
        cb = _dot_nt(cmat, bmat.astype(BF16))
        bt = bmat.T.astype(BF16)
        for hh in range(SSD_HPG):
            h = g * SSD_HPG + hh
            xs = xa_ref[:, h * SSD_HEAD_DIM:(h + 1) * SSD_HEAD_DIM]
            seg = acum[:, h:h + 1] - acumt[h:h + 1, :]
            lmat = jnp.exp(jnp.where(tril, seg, -jnp.inf))
            m = cb * lmat * dtt[h:h + 1, :]
            y = _dot(m.astype(BF16), xs.astype(BF16))
            y = y + _dot(cmat, st_ref[h].astype(BF16)) * decay_in[:, h:h + 1]
            y = y + xs * d_ref[:, h:h + 1]
            y_ref[:, h * SSD_HEAD_DIM:(h + 1) * SSD_HEAD_DIM] = y
            st_new = _dot(bt, (xs * wst[:, h:h + 1]).astype(BF16))
            st_ref[h] = st_ref[h] * cdecay[:, h:h + 1] + st_new

    z = z_ref[0].astype(F32)
    yz = y_ref[...] * (z * jax.nn.sigmoid(z))
    o_ref[0] = _rms(yz, ng_ref[...]).astype(o_ref.dtype)


def _ssd(u16, u32, dtt, cw, cb, dtb, al, d, ng, bsz, seq):
    nc = seq // SSD_CHUNK
    q = SSD_CHUNK
    full = lambda shape: pl.BlockSpec(shape, lambda b, c: (0,) * len(shape))
    return pl.pallas_call(
        _ssd_kernel,
        grid=(bsz, nc),
        in_specs=[pl.BlockSpec((1, q, SSD_CONV_DIM), lambda b, c: (b, c, U16_XBC_BLK)),
                  pl.BlockSpec((1, q, SSD_INNER), lambda b, c: (b, c, U16_Z_BLK)),
                  pl.BlockSpec((1, q, 128), lambda b, c: (b, c, U32_DT_BLK128)),
                  pl.BlockSpec((1, SSD_HEADS, q), lambda b, c: (b, 0, c)),
                  full((CONV_WIDTH, SSD_CONV_DIM)), full((1, SSD_CONV_DIM)),
                  full((1, SSD_HEADS)), full((SSD_HEADS, 1)),
                  full((1, SSD_HEADS)), full((SSD_HEADS, 1)),
                  full((1, SSD_HEADS)), full((1, SSD_INNER))],
        out_specs=pl.BlockSpec((1, q, SSD_INNER), lambda b, c: (b, c, 0)),
        out_shape=jax.ShapeDtypeStruct((bsz, seq, SSD_INNER), BF16),
        scratch_shapes=[pltpu.VMEM((q + 8, SSD_CONV_DIM), F32),
                        pltpu.VMEM((q, SSD_CONV_DIM), F32),
                        pltpu.VMEM((q, SSD_INNER), F32),
                        pltpu.VMEM((SSD_HEADS, SSD_STATE, SSD_HEAD_DIM), F32)],
        compiler_params=_cparams(("parallel", "arbitrary")),
        name="ssd",
    )(u16, u16, u32, dtt, cw, cb, dtb.reshape(1, -1), dtb.reshape(-1, 1), al.reshape(1, -1), al.reshape(-1, 1),
      d.reshape(1, -1), ng.reshape(1, -1))


def _merge_kernel(x_ref, ya_ref, yb_ref, yc_ref, mg_ref, wa_ref, wb_ref, wc_ref, wo_ref, o_ref):
    d = D_MODEL
    mg = mg_ref[...].astype(F32)
    mix = jax.nn.sigmoid(mg[:, 0:d]) * _dot(ya_ref[...], wa_ref[...])
    mix = mix + jax.nn.sigmoid(mg[:, d:2 * d]) * _dot(yb_ref[...], wb_ref[...])
    mix = mix + jax.nn.sigmoid(mg[:, 2 * d:3 * d]) * _dot(yc_ref[...], wc_ref[...])
    o_ref[...] = x_ref[...] + _dot(mix.astype(BF16), wo_ref[...])


def _merge(x, ya, yb, yc, u16, wa, wb, wc, wo, tm):
    m, d = x.shape
    full = lambda a: pl.BlockSpec(a.shape, lambda i: (0, 0))
    return pl.pallas_call(
        _merge_kernel,
        grid=(m // tm,),
        in_specs=[pl.BlockSpec((tm, d), lambda i: (i, 0)),
                  pl.BlockSpec((tm, ya.shape[1]), lambda i: (i, 0)),
                  pl.BlockSpec((tm, yb.shape[1]), lambda i: (i, 0)),
                  pl.BlockSpec((tm, yc.shape[1]), lambda i: (i, 0)),
                  pl.BlockSpec((tm, 3 * d), lambda i: (i, U16_MG_BLK)),
                  full(wa), full(wb), full(wc), full(wo)],
        out_specs=pl.BlockSpec((tm, d), lambda i: (i, 0)),
        out_shape=jax.ShapeDtypeStruct((m, d), F32),
        compiler_params=_cparams(("parallel",)),
        name="merge",
    )(x, ya, yb, yc, u16, wa, wb, wc, wo)


def _ffn_kernel(x_ref, g_ref, w1_ref, w2_ref, o_ref, h_ref, acc_ref):
    j = pl.program_id(1)

    @pl.when(j == 0)
    def _():
        h_ref[...] = _rms(x_ref[...], g_ref[...]).astype(BF16)
        acc_ref[...] = x_ref[...]

    a = jnp.maximum(_dot(h_ref[...], w1_ref[...]), 0.0)
    acc_ref[...] += _dot((a * a).astype(BF16), w2_ref[...])

    @pl.when(j == pl.num_programs(1) - 1)
    def _():
        o_ref[...] = acc_ref[...]


def _ffn(x, g, w1, w2, tm, tf):
    m, d = x.shape
    f = w1.shape[1]
    return pl.pallas_call(
        _ffn_kernel,
        grid=(m // tm, f // tf),
        in_specs=[pl.BlockSpec((tm, d), lambda i, j: (i, 0)),
                  pl.BlockSpec((1, d), lambda i, j: (0, 0)),
                  pl.BlockSpec((d, tf), lambda i, j: (0, j)),
                  pl.BlockSpec((tf, d), lambda i, j: (j, 0))],
        out_specs=pl.BlockSpec((tm, d), lambda i, j: (i, 0)),
        out_shape=jax.ShapeDtypeStruct((m, d), F32),
        scratch_shapes=[pltpu.VMEM((tm, d), BF16), pltpu.VMEM((tm, d), F32)],
        compiler_params=_cparams(("parallel", "arbitrary")),
        name="ffn",
    )(x, g, w1, w2)


def _prep_w_in(w):
    offs = [0] + [int(o) for o in np.cumsum(IN_SPLITS)]
    seg = lambda k: w[:, offs[k]:offs[k + 1]]
    dq, dk, dv, iq, ik, iw, nq, nkv, ng, sz, sxbc, sdt, mg = [seg(k) for k in range(13)]
    zeros = lambda n: jnp.zeros((w.shape[0], n), w.dtype)
    w16 = jnp.concatenate([sxbc, mg, sz, dq, nq], axis=1).astype(BF16)
    w32 = jnp.concatenate([nkv[:, 256:768], nkv[:, 0:256], zeros(256),
                           dk, dv, iq, ik, iw, zeros(512 - 424),
                           ng, zeros(128 - ng.shape[1]),
                           sdt, zeros(128 - sdt.shape[1])], axis=1).astype(BF16)
    return w16, w32


def _pick_tile(n, pref):
    t = min(n, pref)
    while n % t:
        t //= 2
    return t


def kernel(x, norm1_g, w_in, dsa_q_norm, dsa_k_norm, nsa_q_norm, nsa_k_norm, nsa_cmp_pos, nsa_cmp_w,
           ssd_conv_w, ssd_conv_b, ssd_dt_bias, ssd_a_log, ssd_d, ssd_norm_g,
           w_br_dsa, w_br_nsa, w_br_ssd, w_out, norm2_g, w_ff1, w_ff2):
    bsz, seq, d = x.shape
    m = bsz * seq
    depth = w_in.shape[0]
    nch = seq // CMP_STRIDE
    n_blk = seq // SEL_BLOCK
    half = CMP_BLOCK // 2

    cmp_start = np.arange(nch) * CMP_STRIDE
    blk_start = np.arange(n_blk) * SEL_BLOCK
    cover = ((cmp_start[:, None] < blk_start[None, :] + SEL_BLOCK)
             & (cmp_start[:, None] + CMP_BLOCK > blk_start[None, :])
             & (np.arange(nch)[:, None] < (seq - CMP_BLOCK) // CMP_STRIDE + 1)).astype(np.float32)
    expand = (np.arange(seq)[None, :] // SEL_BLOCK == np.arange(n_blk)[:, None]).astype(np.float32)
    cover = jnp.asarray(cover)
    expand = jnp.asarray(expand, dtype=BF16)

    tm = _pick_tile(m, 1024)
    xf = x.reshape(m, d)
    for l in range(depth):
        w16, w32 = _prep_w_in(w_in[l])
        g1 = norm1_g[l].reshape(1, d)
        u16 = _norm_matmul(xf, g1, w16, BF16, tm, 2048).reshape(bsz, seq, U16_WIDTH)
        u32 = _norm_matmul(xf, g1, w32, F32, tm, U32_WIDTH).reshape(bsz, seq, U32_WIDTH)

        iw0 = U32_DS_BLK512 * 512 + DS_IW
        iwt = u32[:, :, iw0:iw0 + IDX_HEADS].transpose(0, 2, 1)
        ya = _dsa(u16, u32, iwt, dsa_q_norm[l].reshape(1, -1), dsa_k_norm[l].reshape(1, -1), bsz, seq)

        kcvc = u32[:, :, 512:768].reshape(bsz, nch, CMP_STRIDE, 2, NSA_KV_HEADS, HEAD_DIM)
        kcvc = kcvc.transpose(0, 3, 4, 1, 2, 5).reshape(bsz, 2, NSA_KV_HEADS, nch, CMP_STRIDE * HEAD_DIM)
        cpos = nsa_cmp_pos[l].reshape(2, 2, half * HEAD_DIM)
        cw = nsa_cmp_w[l].reshape(2, 2, half * HEAD_DIM, HEAD_DIM).astype(BF16)
        kc, vc = _nsa_compress(kcvc, cpos, cw, nsa_k_norm[l][0:1])
        yb = _nsa(u16, u32, kc, vc, nsa_q_norm[l].reshape(1, -1), nsa_k_norm[l], cover, expand, bsz, seq)

        dt0 = U32_DT_BLK128 * 128
        dtt = u32[:, :, dt0:dt0 + SSD_HEADS].transpose(0, 2, 1)
        yc = _ssd(u16, u32, dtt, ssd_conv_w[l], ssd_conv_b[l].reshape(1, -1), ssd_dt_bias[l], ssd_a_log[l],
                  ssd_d[l], ssd_norm_g[l], bsz, seq)

        xf = _merge(xf, ya.reshape(m, -1), yb.reshape(m, -1), yc.reshape(m, -1), u16.reshape(m, U16_WIDTH),
                    w_br_dsa[l].astype(BF16), w_br_nsa[l].astype(BF16), w_br_ssd[l].astype(BF16),
                    w_out[l].astype(BF16), _pick_tile(m, 256))
        xf = _ffn(xf, norm2_g[l].reshape(1, d), w_ff1[l].astype(BF16), w_ff2[l].astype(BF16),
                  _pick_tile(m, 1024), 1024)
    return xf.reshape(bsz, seq, d)
```

```python
import functools
import math

import numpy as np
import jax
import jax.numpy as jnp
from jax import lax
from jax.experimental import pallas as pl
from jax.experimental.pallas import tpu as pltpu

F32 = jnp.float32
BF16 = jnp.bfloat16

D_MODEL = 1024
HEAD_DIM = 64
DSA_HEADS = 16
DSA_KV_DIM = 64
IDX_HEADS = 8
IDX_DIM = 32
DSA_TOPK = 256
NSA_HEADS = 16
NSA_KV_HEADS = 2
NSA_GROUP = NSA_HEADS // NSA_KV_HEADS
CMP_BLOCK = 32
CMP_STRIDE = 16
SEL_BLOCK = 64
SEL_TOPN = 4
WINDOW = 512
SSD_INNER = 2 * D_MODEL
SSD_HEAD_DIM = 64
SSD_HEADS = SSD_INNER // SSD_HEAD_DIM
SSD_GROUPS = 4
SSD_HPG = SSD_HEADS // SSD_GROUPS
SSD_STATE = 128
SSD_CONV_DIM = SSD_INNER + 2 * SSD_GROUPS * SSD_STATE
CONV_WIDTH = 4
SSD_CHUNK = 128
D_FF = 4 * D_MODEL
Q_BLOCK = 128
EPS = 1e-6
NEG = -1e30
IN_SPLITS = (DSA_HEADS * HEAD_DIM, DSA_KV_DIM, DSA_KV_DIM, IDX_HEADS * IDX_DIM, IDX_DIM, IDX_HEADS,
             NSA_HEADS * HEAD_DIM, 6 * NSA_KV_HEADS * HEAD_DIM, 3 * NSA_HEADS,
             SSD_INNER, SSD_CONV_DIM, SSD_HEADS, 3 * D_MODEL)

V7X_VMEM_LIMIT_BYTES = 56 * 1024 * 1024
LANES = 128

U16_WIDTH = 10240
U16_XBC_BLK = 0
U16_MG_BLK = 1
U16_Z_BLK = 3
U16_DQ_BLK = 8
U16_NQ_BLK = 9
U32_WIDTH = 1792
U32_NKV_BLK512 = 0
U32_DS_BLK512 = 2
U32_NG_BLK128 = 12
U32_DT_BLK128 = 13
DS_DK, DS_DV, DS_IQ, DS_IK, DS_IW = 0, 64, 128, 384, 416

KEY_TILE = 256
BISECT_ITERS = 28
PAIR_STEP = 2
LOG2E = 1.4426950408889634
BISECT_UNROLL = 4


def _cparams(sem):
    return pltpu.CompilerParams(dimension_semantics=sem, vmem_limit_bytes=V7X_VMEM_LIMIT_BYTES)


def _rms(x, g):
    return x * lax.rsqrt(jnp.mean(x * x, axis=-1, keepdims=True) + EPS) * g


def _dot_nt(a, b):
    return lax.dot_general(a, b, (((1,), (1,)), ((), ())), preferred_element_type=F32)


def _dot(a, b):
    return jnp.dot(a, b, preferred_element_type=F32)


def _dot_hi(a, b):
    return jnp.dot(a, b, preferred_element_type=F32, precision=lax.Precision.HIGHEST)


def _norm_matmul_kernel(x_ref, g_ref, w_ref, o_ref, h_ref):
    @pl.when(pl.program_id(1) == 0)
    def _():
        h_ref[...] = _rms(x_ref[...], g_ref[...]).astype(BF16)

    o_ref[...] = _dot(h_ref[...], w_ref[...]).astype(o_ref.dtype)


def _norm_matmul(x, g, w, out_dtype, tm, tn):
    m, k = x.shape
    n = w.shape[1]
    return pl.pallas_call(
        _norm_matmul_kernel,
        grid=(m // tm, n // tn),
        in_specs=[pl.BlockSpec((tm, k), lambda i, j: (i, 0)),
                  pl.BlockSpec((1, k), lambda i, j: (0, 0)),
                  pl.BlockSpec((k, tn), lambda i, j: (0, j))],
        out_specs=pl.BlockSpec((tm, tn), lambda i, j: (i, j)),
        out_shape=jax.ShapeDtypeStruct((m, n), out_dtype),
        scratch_shapes=[pltpu.VMEM((tm, k), BF16)],
        compiler_params=_cparams(("parallel", "arbitrary")),
        name="norm_matmul",
    )(x, g, w)


def _flash_pairs(qp_ref, kbd_ref, wext_ref, bias_ref, nkt, n_pairs, pairs_per_kv, bufs, m_ref, acc_ref):
    (s0, s1), (p0, p1), (a0, a1) = bufs
    n_pg = n_pairs // PAIR_STEP
    rows = PAIR_STEP * Q_BLOCK
    n = nkt * n_pg
    lo_half = lax.broadcasted_iota(jnp.int32, (1, LANES), 1) < HEAD_DIM
    m_ref[...] = jnp.full(m_ref.shape, -jnp.inf, F32)
    acc_ref[...] = jnp.zeros(acc_ref.shape, F32)
    p1[...] = jnp.zeros(p1.shape, BF16)
    a1[...] = jnp.ones(a1.shape, F32)

    def where(j):
        kt = j // n_pg
        pg = j % n_pg
        g = (pg * PAIR_STEP) // pairs_per_kv
        return kt, pg, g, pl.multiple_of(kt * 2 * KEY_TILE, 2 * KEY_TILE), pl.multiple_of(pg * rows, rows)

    def qk(j, s_ref):
        _, _, g, koff, r = where(j)
        s_ref[...] = _dot_nt(qp_ref[pl.ds(r, rows), :], kbd_ref[g, pl.ds(koff, 2 * KEY_TILE), :])

    def softmax(j, s_ref, p_ref, a_ref):
        kt, pg, g, _, _ = where(j)
        for u in range(PAIR_STEP):
            us = slice(u * Q_BLOCK, (u + 1) * Q_BLOCK)
            alpha = []
            for par in range(2):
                cs = slice(par * KEY_TILE, (par + 1) * KEY_TILE)
                hrow = pl.multiple_of(((pg * PAIR_STEP + u) * 2 + par) * Q_BLOCK, Q_BLOCK)
                s = s_ref[us, cs] + bias_ref[g, kt]
                m_old = m_ref[pl.ds(hrow, Q_BLOCK), :]
                m_new = jnp.maximum(m_old, jnp.max(s, axis=-1, keepdims=True))
                alpha.append(jnp.exp2(m_old - m_new))
                m_ref[pl.ds(hrow, Q_BLOCK), :] = m_new
                p_ref[us, cs] = jnp.exp2(s - jnp.concatenate([m_new] * (KEY_TILE // LANES), axis=1)).astype(BF16)
            a_ref[us, :] = jnp.where(lo_half, alpha[0], alpha[1])

    def pv(j, p_ref, a_ref):
        _, _, g, koff, r = where(j)
        a = a_ref[...]
        acc_ref[pl.ds(r, rows), :] = (jnp.concatenate([a, a], axis=1) * acc_ref[pl.ds(r, rows), :]
                                      + _dot(p_ref[...], wext_ref[g, pl.ds(koff, 2 * KEY_TILE), :]))

    qk(0, s0)

    def body(jj, c):
        j = 2 * jj
        qk(j + 1, s1)
        softmax(j, s0, p0, a0)
        pv(jnp.maximum(j - 1, 0), p1, a1)
        qk(jnp.minimum(j + 2, n - 1), s0)
        softmax(j + 1, s1, p1, a1)
        pv(j, p0, a0)
        return c

    lax.fori_loop(0, n // 2, body, 0)
    pv(n - 1, p1, a1)


def _flash_scratch(n_pairs):
    step = PAIR_STEP * Q_BLOCK
    return ([pltpu.VMEM((step, 2 * KEY_TILE), F32)] * 2 + [pltpu.VMEM((step, 2 * KEY_TILE), BF16)] * 2
            + [pltpu.VMEM((step, LANES), F32)] * 2
            + [pltpu.VMEM((2 * n_pairs * Q_BLOCK, LANES), F32), pltpu.VMEM((n_pairs * Q_BLOCK, 2 * LANES), F32)])


def _norm_pairs(q_ref, g_ref, qp_ref, n_pairs):
    mult = HEAD_DIM ** -0.5 * LOG2E
    lo_half = lax.broadcasted_iota(jnp.int32, (1, LANES), 1) < HEAD_DIM
    g2 = jnp.concatenate([g_ref[...], g_ref[...]], axis=-1) * mult
    for j in range(n_pairs):
        x = q_ref[0, :, j * LANES:(j + 1) * LANES].astype(F32)
        x2 = x * x
        s_lo = jnp.sum(jnp.where(lo_half, x2, 0.0), axis=-1, keepdims=True)
        s_hi = jnp.sum(jnp.where(lo_half, 0.0, x2), axis=-1, keepdims=True)
        r = jnp.where(lo_half, lax.rsqrt(s_lo * (1.0 / HEAD_DIM) + EPS), lax.rsqrt(s_hi * (1.0 / HEAD_DIM) + EPS))
        qp_ref[j * Q_BLOCK:(j + 1) * Q_BLOCK, :] = (x * r * g2).astype(BF16)


def _store_pair_kv(kbd_ref, wext_ref, g, row0, n, k, v):
    z = jnp.zeros((n, HEAD_DIM), F32)
    one = jnp.ones((n, HEAD_DIM), F32)
    kbd_ref[g, pl.ds(row0, n), :] = jnp.concatenate([k, z], axis=-1).astype(BF16)
    kbd_ref[g, pl.ds(row0 + n, n), :] = jnp.concatenate([z, k], axis=-1).astype(BF16)
    wext_ref[g, pl.ds(row0, n), :] = jnp.concatenate([v, z, one, z], axis=-1).astype(BF16)
    wext_ref[g, pl.ds(row0 + n, n), :] = jnp.concatenate([z, v, z, one], axis=-1).astype(BF16)


def _dsa_kernel(q_ref, sq_ref, iwt_ref, sk_ref, gq_ref, gk_ref, o_ref,
                kbd_ref, wext_ref, ik_ref, idx_ref, bias_ref, iqs_ref, pq_ref,
                qp_ref, s0_ref, s1_ref, p0_ref, p1_ref, a0_ref, a1_ref, m_ref, acc_ref, *, seq, n_sel):
    i = pl.program_id(1)
    nq = Q_BLOCK
    n_pairs = DSA_HEADS // 2
    kt_shape = (KEY_TILE, nq)

    @pl.when(i == 0)
    def _():
        def rows(r, c):
            off = pl.multiple_of(r * KEY_TILE, KEY_TILE)
            sk = sk_ref[0, pl.ds(off, KEY_TILE), :]
            _store_pair_kv(kbd_ref, wext_ref, 0, pl.multiple_of(2 * off, 2 * KEY_TILE), KEY_TILE,
                           _rms(sk[:, DS_DK:DS_DK + HEAD_DIM], gk_ref[...]), sk[:, DS_DV:DS_DV + HEAD_DIM])
            ik_ref[pl.ds(off, KEY_TILE), :] = sk[:, DS_IK:DS_IK + IDX_DIM].astype(BF16)
            return c

        lax.fori_loop(0, seq // KEY_TILE, rows, 0)

    start = i * nq
    nkt = (start + nq + KEY_TILE - 1) // KEY_TILE
    t_row = (start + lax.broadcasted_iota(jnp.int32, (1, nq), 1)).astype(F32)
    sub_pos = lax.broadcasted_iota(jnp.int32, (KEY_TILE, 1), 0).astype(F32)

    sq = sq_ref[0]
    for h in range(IDX_HEADS):
        iqs_ref[h * nq:(h + 1) * nq, :] = sq[:, DS_IQ + h * IDX_DIM:DS_IQ + (h + 1) * IDX_DIM].astype(BF16)
    iwt = iwt_ref[0]

    def idx_tile(kt, carry):
        rmin, rmax = carry
        off = pl.multiple_of(kt * KEY_TILE, KEY_TILE)
        r = _dot_nt(ik_ref[pl.ds(off, KEY_TILE), :], iqs_ref[...])
        acc = jnp.zeros(kt_shape, F32)
        for h in range(IDX_HEADS):
            acc = acc + iwt[h:h + 1, :] * jnp.maximum(r[:, h * nq:(h + 1) * nq], 0.0)
        valid = (sub_pos + (kt * KEY_TILE).astype(F32)) <= t_row
        idx_ref[kt] = jnp.where(valid, acc, -jnp.inf)
        rmin = jnp.minimum(rmin, jnp.min(jnp.where(valid, acc, jnp.inf), axis=0, keepdims=True))
        rmax = jnp.maximum(rmax, jnp.max(jnp.where(valid, acc, -jnp.inf), axis=0, keepdims=True))
        return rmin, rmax

    rmin, rmax = lax.fori_loop(0, nkt, idx_tile,
                               (jnp.full((1, nq), jnp.inf, F32), jnp.full((1, nq), -jnp.inf, F32)))

    kf = float(n_sel)

    def col_sum(x):
        parts = [x[r * 8:(r + 1) * 8] for r in range(KEY_TILE // 8)]
        while len(parts) > 1:
            parts = [parts[k] + parts[k + 1] for k in range(0, len(parts), 2)]
        return jnp.sum(parts[0], axis=0, keepdims=True)

    def count_gt(thr):
        def body(kt, acc):
            return acc + jnp.where(idx_ref[kt] > thr, 1.0, 0.0)

        return col_sum(lax.fori_loop(0, nkt, body, jnp.zeros(kt_shape, F32)))

    n_valid = t_row + 1.0
    lo0 = rmin - (jnp.abs(rmin) * 0.01 + 1.0)

    def bis_cond(c):
        it, _, _, clo, _ = c
        return jnp.logical_and(it < BISECT_ITERS, jnp.max(clo) > kf)

    def bis_body(c):
        it, lo, hi, clo, chi = c
        for _ in range(BISECT_UNROLL):
            mid = 0.5 * (lo + hi)
            cnt = count_gt(mid)
            ge = cnt >= kf
            lo, hi = jnp.where(ge, mid, lo), jnp.where(ge, hi, mid)
            clo, chi = jnp.where(ge, cnt, clo), jnp.where(ge, chi, cnt)
        return it + BISECT_UNROLL, lo, hi, clo, chi

    _, lo, hi, clo, chi = lax.while_loop(
        bis_cond, bis_body, (jnp.int32(0), lo0, rmax, n_valid, jnp.zeros((1, nq), F32)))

    pq_ref[...] = jnp.full((1, nq), float(seq), F32)

    @pl.when(jnp.max(clo) > kf)
    def _():
        quota = kf - chi

        def count_tie_le(pm):
            def body(kt, acc):
                v = idx_ref[kt]
                pos = sub_pos + (kt * KEY_TILE).astype(F32)
                tie = (v > lo) & (v <= hi) & (pos <= pm)
                return acc + jnp.where(tie, 1.0, 0.0)

            return col_sum(lax.fori_loop(0, nkt, body, jnp.zeros(kt_shape, F32)))

        def pbody(_, c):
            plo, phi = c
            mid = jnp.floor(0.5 * (plo + phi))
            ge = count_tie_le(mid) >= quota
            return jnp.where(ge, plo, mid), jnp.where(ge, mid, phi)

        steps = int(math.ceil(math.log2(seq))) + 1
        _, phi = lax.fori_loop(0, steps, pbody,
                               (jnp.full((1, nq), -1.0, F32), jnp.full((1, nq), float(seq - 1), F32)))
        pq_ref[...] = jnp.where(clo > kf, phi, float(seq))

    p_lim = pq_ref[...]
    eye = jnp.where(lax.broadcasted_iota(jnp.int32, (nq, nq), 0) == lax.broadcasted_iota(jnp.int32, (nq, nq), 1),
                    1.0, 0.0).astype(BF16)

    def mask_tile(kt, c):
        v = idx_ref[kt]
        pos = sub_pos + (kt * KEY_TILE).astype(F32)
        sel = (v > hi) | ((v > lo) & (pos <= p_lim))
        hit = _dot_nt(eye, jnp.where(sel, 1.0, 0.0).astype(BF16))
        bias_ref[0, kt] = jnp.where(hit > 0.5, 0.0, NEG)
        return c

    lax.fori_loop(0, nkt, mask_tile, 0)

    _norm_pairs(q_ref, gq_ref, qp_ref, n_pairs)
    _flash_pairs(qp_ref, kbd_ref, wext_ref, bias_ref, nkt, n_pairs, n_pairs,
                 ((s0_ref, s1_ref), (p0_ref, p1_ref), (a0_ref, a1_ref)), m_ref, acc_ref)
    for j in range(n_pairs):
        acc = acc_ref[j * nq:(j + 1) * nq, :]
        o_ref[0, :, j * LANES:(j + 1) * LANES] = (acc[:, :LANES] / acc[:, LANES:]).astype(o_ref.dtype)


def _dsa(u16, u32, iwt, gq, gk, bsz, seq):
    n_sel = min(DSA_TOPK, seq // 4)
    nqb = seq // Q_BLOCK
    n_pairs = DSA_HEADS // 2
    kern = functools.partial(_dsa_kernel, seq=seq, n_sel=n_sel)
    return pl.pallas_call(
        kern,
        grid=(bsz, nqb),
        in_specs=[pl.BlockSpec((1, Q_BLOCK, 1024), lambda b, i: (b, i, U16_DQ_BLK)),
                  pl.BlockSpec((1, Q_BLOCK, 512), lambda b, i: (b, i, U32_DS_BLK512)),
                  pl.BlockSpec((1, IDX_HEADS, Q_BLOCK), lambda b, i: (b, 0, i)),
                  pl.BlockSpec((1, seq, 512), lambda b, i: (b, 0, U32_DS_BLK512)),
                  pl.BlockSpec((1, HEAD_DIM), lambda b, i: (0, 0)),
                  pl.BlockSpec((1, DSA_KV_DIM), lambda b, i: (0, 0))],
        out_specs=pl.BlockSpec((1, Q_BLOCK, DSA_HEADS * HEAD_DIM), lambda b, i: (b, i, 0)),
        out_shape=jax.ShapeDtypeStruct((bsz, seq, DSA_HEADS * HEAD_DIM), BF16),
        scratch_shapes=[pltpu.VMEM((1, 2 * seq, LANES), BF16),
                        pltpu.VMEM((1, 2 * seq, 2 * LANES), BF16),
                        pltpu.VMEM((seq, IDX_DIM), BF16),
                        pltpu.VMEM((seq // KEY_TILE, KEY_TILE, Q_BLOCK), F32),
                        pltpu.VMEM((1, seq // KEY_TILE, Q_BLOCK, KEY_TILE), F32),
                        pltpu.VMEM((IDX_HEADS * Q_BLOCK, IDX_DIM), BF16),
                        pltpu.VMEM((1, Q_BLOCK), F32),
                        pltpu.VMEM((n_pairs * Q_BLOCK, LANES), BF16)]
                       + _flash_scratch(n_pairs),
        compiler_params=_cparams(("parallel", "arbitrary")),
        name="dsa",
    )(u16, u32, iwt, u32, gq, gk)


def _nsa_compress_kernel(x_ref, pos_ref, w_ref, g_ref, kc_ref, vc_ref):
    for j in range(2):
        for g in range(NSA_KV_HEADS):
            x = x_ref[0, j, g]
            a = _dot((x + pos_ref[j, 0:1, :]).astype(BF16), w_ref[j, 0])
            b = _dot((x + pos_ref[j, 1:2, :]).astype(BF16), w_ref[j, 1])
            out = a + pltpu.roll(b, b.shape[0] - 1, axis=0)
            if j == 0:
                kc_ref[0, g] = _rms(out, g_ref[...]).astype(BF16)
            else:
                vc_ref[0, g] = out.astype(BF16)


def _nsa_compress(x, pos, w, g):
    bsz, _, _, nch, width = x.shape
    return pl.pallas_call(
        _nsa_compress_kernel,
        grid=(bsz,),
        in_specs=[pl.BlockSpec((1, 2, NSA_KV_HEADS, nch, width), lambda b: (b, 0, 0, 0, 0)),
                  pl.BlockSpec((2, 2, width), lambda b: (0, 0, 0)),
                  pl.BlockSpec((2, 2, width, HEAD_DIM), lambda b: (0, 0, 0, 0)),
                  pl.BlockSpec((1, HEAD_DIM), lambda b: (0, 0))],
        out_specs=[pl.BlockSpec((1, NSA_KV_HEADS, nch, HEAD_DIM), lambda b: (b, 0, 0, 0)),
                   pl.BlockSpec((1, NSA_KV_HEADS, nch, HEAD_DIM), lambda b: (b, 0, 0, 0))],
        out_shape=[jax.ShapeDtypeStruct((bsz, NSA_KV_HEADS, nch, HEAD_DIM), BF16),
                   jax.ShapeDtypeStruct((bsz, NSA_KV_HEADS, nch, HEAD_DIM), BF16)],
        compiler_params=_cparams(("parallel",)),
        name="nsa_compress",
    )(x, pos, w, g)


def _nsa_kernel(q_ref, ng_ref, kv_ref, kc_ref, vc_ref, gq_ref, gk_ref, cover_ref, expand_ref, gexp_ref, o_ref,
                ksbd_ref, wsext_ref, kwbd_ref, wwext_ref, kcbd_ref, vcext_ref, bias_ref, wbias_ref, psum_ref,
                ocmp_ref, accw_ref, qp_ref, sw_ref, pw_ref,
                s0_ref, s1_ref, p0_ref, p1_ref, a0_ref, a1_ref, m_ref, acc_ref, *, seq):
    i = pl.program_id(1)
    nq = Q_BLOCK
    n_pairs = NSA_HEADS // 2
    gpairs = NSA_GROUP // 2
    grows = gpairs * nq
    n_cmp_pad = kc_ref.shape[2]
    n_blk = seq // SEL_BLOCK
    n_sel = min(SEL_TOPN, n_blk)
    win_keys = min(WINDOW + Q_BLOCK, seq)
    win_tiles = win_keys // Q_BLOCK
    wt = Q_BLOCK

    @pl.when(i == 0)
    def _():
        def rows(r, c):
            off = pl.multiple_of(r * KEY_TILE, KEY_TILE)
            kv = kv_ref[0, pl.ds(off, KEY_TILE), :]
            for g in range(NSA_KV_HEADS):
                c0 = g * HEAD_DIM
                _store_pair_kv(ksbd_ref, wsext_ref, g, pl.multiple_of(2 * off, 2 * KEY_TILE), KEY_TILE,
                               _rms(kv[:, c0:c0 + 64], gk_ref[1:2, :]), kv[:, 128 + c0:128 + c0 + 64])
                kw = _rms(kv[:, 256 + c0:256 + c0 + 64], gk_ref[2:3, :])
                vw = kv[:, 384 + c0:384 + c0 + 64]
                for hf in range(KEY_TILE // wt):
                    _store_pair_kv(kwbd_ref, wwext_ref, g, pl.multiple_of(2 * off + hf * 2 * wt, 2 * wt), wt,
                                   kw[hf * wt:(hf + 1) * wt], vw[hf * wt:(hf + 1) * wt])
            return c

        lax.fori_loop(0, seq // KEY_TILE, rows, 0)
        zc = jnp.zeros((n_cmp_pad, HEAD_DIM), BF16)
        for g in range(NSA_KV_HEADS):
            kcbd_ref[g, 0:n_cmp_pad, :] = jnp.concatenate([kc_ref[0, g], zc], axis=-1)
            kcbd_ref[g, n_cmp_pad:2 * n_cmp_pad, :] = jnp.concatenate([zc, kc_ref[0, g]], axis=-1)
            vcext_ref[g, 0:n_cmp_pad, :] = jnp.concatenate([vc_ref[0, g], zc], axis=-1)
            vcext_ref[g, n_cmp_pad:2 * n_cmp_pad, :] = jnp.concatenate([zc, vc_ref[0, g]], axis=-1)

    start = i * nq
    nkt = (start + nq + KEY_TILE - 1) // KEY_TILE
    t_i = start + lax.broadcasted_iota(jnp.int32, (nq, 1), 0)
    t_col = t_i.astype(F32)
    lane_pos = lax.broadcasted_iota(jnp.int32, (1, KEY_TILE), 1).astype(F32)

    _norm_pairs(q_ref, gq_ref, qp_ref, n_pairs)

    cmp_end = (lax.broadcasted_iota(jnp.int32, (1, n_cmp_pad), 1) * CMP_STRIDE + (CMP_BLOCK - 1)).astype(F32)
    vis = cmp_end <= t_col
    for g in range(NSA_KV_HEADS):
        sw_ref[g * grows:(g + 1) * grows, 0:2 * n_cmp_pad] = _dot_nt(qp_ref[g * grows:(g + 1) * grows, :], kcbd_ref[g])
    psum_ref[...] = jnp.zeros(psum_ref.shape, F32)

    def cmp_pair(j, c):
        r = pl.multiple_of(j * nq, nq)
        g = j // gpairs
        tot = jnp.zeros((nq, n_cmp_pad), F32)
        for par in range(2):
            cs = slice(par * n_cmp_pad, (par + 1) * n_cmp_pad)
            s = jnp.where(vis, sw_ref[pl.ds(r, nq), cs], NEG)
            e = jnp.exp2(s - jnp.max(s, axis=-1, keepdims=True))
            p = jnp.where(vis, e / jnp.sum(e, axis=-1, keepdims=True), 0.0)
            pw_ref[pl.ds(r, nq), cs] = p.astype(BF16)
            tot = tot + p
        psum_ref[g] = psum_ref[g] + tot
        return c

    lax.fori_loop(0, n_pairs, cmp_pair, 0, unroll=2)
    for g in range(NSA_KV_HEADS):
        ocmp_ref[g * grows:(g + 1) * grows, :] = _dot(pw_ref[g * grows:(g + 1) * grows, 0:2 * n_cmp_pad], vcext_ref[g])

    blk_j = lax.broadcasted_iota(jnp.int32, (1, n_blk), 1)
    cur1 = jnp.right_shift(t_i, int(math.log2(SEL_BLOCK)))
    cur = jnp.concatenate([cur1] * NSA_KV_HEADS, axis=0)
    imp = jnp.concatenate([_dot_hi(psum_ref[g], cover_ref[...]) for g in range(NSA_KV_HEADS)], axis=0)
    forced = (blk_j == cur) | (blk_j == 0)
    imp = jnp.where(forced, jnp.inf, jnp.where(blk_j > cur, -jnp.inf, imp))
    selb = jnp.zeros(imp.shape, jnp.bool_)
    for _ in range(n_sel):
        mx = jnp.max(imp, axis=-1, keepdims=True)
        first = jnp.min(jnp.where(imp == mx, blk_j, n_blk), axis=-1, keepdims=True)
        pick = blk_j == first
        selb = selb | pick
        imp = jnp.where(pick, -jnp.inf, imp)
    selb_bf = jnp.where(selb, 1.0, 0.0).astype(BF16)

    def mask_tile(kt, c):
        off = pl.multiple_of(kt * KEY_TILE, KEY_TILE)
        hit = _dot(selb_bf, expand_ref[:, pl.ds(off, KEY_TILE)])
        ok = (lane_pos + (kt * KEY_TILE).astype(F32)) <= t_col
        for g in range(NSA_KV_HEADS):
            bias_ref[g, kt] = jnp.where((hit[g * nq:(g + 1) * nq] > 0.5) & ok, 0.0, NEG)
        return c

    lax.fori_loop(0, nkt, mask_tile, 0)

    _flash_pairs(qp_ref, ksbd_ref, wsext_ref, bias_ref, nkt, n_pairs, gpairs,
                 ((s0_ref, s1_ref), (p0_ref, p1_ref), (a0_ref, a1_ref)), m_ref, acc_ref)

    wbase = pl.multiple_of(jnp.maximum(start + nq - win_keys, 0), Q_BLOCK)
    wb2 = pl.multiple_of(2 * wbase, 2 * Q_BLOCK)
    col = lax.broadcasted_iota(jnp.int32, (1, 2 * win_keys), 1)
    wpos = (wbase + jnp.right_shift(col, 8) * wt + jnp.bitwise_and(col, wt - 1)).astype(F32)
    wbias_ref[...] = jnp.where((wpos <= t_col) & (wpos > t_col - float(WINDOW)), 0.0, NEG)
    for g in range(NSA_KV_HEADS):
        sw_ref[g * grows:(g + 1) * grows, :] = _dot_nt(
            qp_ref[g * grows:(g + 1) * grows, :], kwbd_ref[g, pl.ds(wb2, 2 * win_keys), :])

    def win_pair(j, c):
        r = pl.multiple_of(j * nq, nq)
        for par in range(2):
            cols = [slice(t * 2 * wt + par * wt, t * 2 * wt + (par + 1) * wt) for t in range(win_tiles)]
            s = [sw_ref[pl.ds(r, nq), cs] + wbias_ref[:, cs] for cs in cols]
            mx = s[0]
            for st in s[1:]:
                mx = jnp.maximum(mx, st)
            m = jnp.max(mx, axis=-1, keepdims=True)
            for cs, st in zip(cols, s):
                pw_ref[pl.ds(r, nq), cs] = jnp.exp2(st - m).astype(BF16)
        return c

    lax.fori_loop(0, n_pairs, win_pair, 0)
    for g in range(NSA_KV_HEADS):
        accw_ref[g * grows:(g + 1) * grows, :] = _dot(
            pw_ref[g * grows:(g + 1) * grows, :], wwext_ref[g, pl.ds(wb2, 2 * win_keys), :])

    gates = jax.nn.sigmoid(ng_ref[0])
    g_hi = gates.astype(BF16)
    g_lo = (gates - g_hi.astype(F32)).astype(BF16)
    for j in range(n_pairs):
        rs = slice(j * nq, (j + 1) * nq)
        ls = slice(j * LANES, (j + 1) * LANES)
        gb = [_dot(g_hi, gexp_ref[b, :, ls]) + _dot(g_lo, gexp_ref[b, :, ls]) for b in range(3)]
        acc = acc_ref[rs, :]
        accw = accw_ref[rs, :]
        out = (gb[0] * ocmp_ref[rs, :] + gb[1] * (acc[:, :LANES] / acc[:, LANES:])
               + gb[2] * (accw[:, :LANES] / accw[:, LANES:]))
        o_ref[0, :, ls] = out.astype(o_ref.dtype)


def _nsa(u16, u32, kc, vc, gq, gk, cover, expand, gexp, bsz, seq):
    nqb = seq // Q_BLOCK
    nch = kc.shape[2]
    n_blk = seq // SEL_BLOCK
    n_pairs = NSA_HEADS // 2
    prow = n_pairs * Q_BLOCK
    win_keys = min(WINDOW + Q_BLOCK, seq)
    kern = functools.partial(_nsa_kernel, seq=seq)
    return pl.pallas_call(
        kern,
        grid=(bsz, nqb),
        in_specs=[pl.BlockSpec((1, Q_BLOCK, 1024), lambda b, i: (b, i, U16_NQ_BLK)),
                  pl.BlockSpec((1, Q_BLOCK, 128), lambda b, i: (b, i, U32_NG_BLK128)),
                  pl.BlockSpec((1, seq, 512), lambda b, i: (b, 0, U32_NKV_BLK512)),
                  pl.BlockSpec((1, NSA_KV_HEADS, nch, HEAD_DIM), lambda b, i: (b, 0, 0, 0)),
                  pl.BlockSpec((1, NSA_KV_HEADS, nch, HEAD_DIM), lambda b, i: (b, 0, 0, 0)),
                  pl.BlockSpec((1, HEAD_DIM), lambda b, i: (0, 0)),
                  pl.BlockSpec((3, HEAD_DIM), lambda b, i: (0, 0)),
                  pl.BlockSpec((nch, n_blk), lambda b, i: (0, 0)),
                  pl.BlockSpec((n_blk, seq), lambda b, i: (0, 0)),
                  pl.BlockSpec((3, LANES, NSA_HEADS * HEAD_DIM), lambda b, i: (0, 0, 0))],
        out_specs=pl.BlockSpec((1, Q_BLOCK, NSA_HEADS * HEAD_DIM), lambda b, i: (b, i, 0)),
        out_shape=jax.ShapeDtypeStruct((bsz, seq, NSA_HEADS * HEAD_DIM), BF16),
        scratch_shapes=[pltpu.VMEM((NSA_KV_HEADS, 2 * seq, LANES), BF16),
                        pltpu.VMEM((NSA_KV_HEADS, 2 * seq, 2 * LANES), BF16),
                        pltpu.VMEM((NSA_KV_HEADS, 2 * seq, LANES), BF16),
                        pltpu.VMEM((NSA_KV_HEADS, 2 * seq, 2 * LANES), BF16),
                        pltpu.VMEM((NSA_KV_HEADS, 2 * nch, LANES), BF16),
                        pltpu.VMEM((NSA_KV_HEADS, 2 * nch, LANES), BF16),
                        pltpu.VMEM((NSA_KV_HEADS, seq // KEY_TILE, Q_BLOCK, KEY_TILE), F32),
                        pltpu.VMEM((Q_BLOCK, 2 * win_keys), F32),
                        pltpu.VMEM((NSA_KV_HEADS, Q_BLOCK, nch), F32),
                        pltpu.VMEM((prow, LANES), F32),
                        pltpu.VMEM((prow, 2 * LANES), F32),
                        pltpu.VMEM((prow, LANES), BF16),
                        pltpu.VMEM((prow, 2 * win_keys), F32),
                        pltpu.VMEM((prow, 2 * win_keys), BF16)]
                       + _flash_scratch(n_pairs),
        compiler_params=_cparams(("parallel", "arbitrary")),
        name="nsa",
    )(u16, u32, u32, kc, vc, gq, gk, cover, expand, gexp)


def _softplus(x):
    return jnp.maximum(x, 0.0) + jnp.log1p(jnp.exp(-jnp.abs(x)))


def _ssd_kernel(xbc_ref, z_ref, dt_ref, dtt_ref, cw_ref, cb_ref, dtb_ref, dtbt_ref, al_ref, alt_ref,
                d_ref, ng_ref, o_ref, xc_ref, xa_ref, y_ref, st_ref):
    c = pl.program_id(1)
    q = SSD_CHUNK
    gn = SSD_GROUPS * SSD_STATE

    @pl.when(c == 0)
    def _():
        st_ref[...] = jnp.zeros_like(st_ref)
        xc_ref[0:8, :] = jnp.zeros((8, SSD_CONV_DIM), F32)

    @pl.when(c > 0)
    def _():
        xc_ref[0:8, :] = xc_ref[q:q + 8, :]

    xc_ref[8:8 + q, :] = xbc_ref[0].astype(F32)
    for cc in range(SSD_CONV_DIM // 512):
        cs = slice(cc * 512, (cc + 1) * 512)
        acc = cb_ref[:, cs] + cw_ref[CONV_WIDTH - 1:CONV_WIDTH, cs] * xc_ref[8:8 + q, cs]
        for s in range(1, CONV_WIDTH):
            acc = acc + cw_ref[CONV_WIDTH - 1 - s:CONV_WIDTH - s, cs] * xc_ref[8 - s:8 - s + q, cs]
        xa_ref[:, cs] = acc * jax.nn.sigmoid(acc)

    dt = _softplus(dt_ref[0][:, :SSD_HEADS] + dtb_ref[...])
    dtt = _softplus(dtt_ref[0] + dtbt_ref[...])
    a = -jnp.exp(al_ref[...])
    at = -jnp.exp(alt_ref[...])
    ri = lax.broadcasted_iota(jnp.int32, (q, q), 0)
    ci = lax.broadcasted_iota(jnp.int32, (q, q), 1)
    tril = ri >= ci
    acum = _dot_hi(jnp.where(tril, 1.0, 0.0), dt * a)
    acumt = _dot_hi(dtt * at, jnp.where(ri <= ci, 1.0, 0.0))
    alast = acum[q - 1:q, :]
    decay_in = jnp.exp(acum)
    wst = jnp.exp(alast - acum) * dt
    cdecay = jnp.exp(alast)

    for g in range(SSD_GROUPS):
        bmat = xa_ref[:, SSD_INNER + g * SSD_STATE:SSD_INNER + (g + 1) * SSD_STATE]
        cmat = xa_ref[:, SSD_INNER + gn + g * SSD_STATE:SSD_INNER + gn + (g + 1) * SSD_STATE].astype(BF16)
        cb = _dot_nt(cmat, bmat.astype(BF16))
        bt = bmat.T.astype(BF16)
        for hh in range(SSD_HPG):
            h = g * SSD_HPG + hh
            xs = xa_ref[:, h * SSD_HEAD_DIM:(h + 1) * SSD_HEAD_DIM]
            seg = acum[:, h:h + 1] - acumt[h:h + 1, :]
            lmat = jnp.exp(jnp.where(tril, seg, -jnp.inf))
            m = cb * lmat * dtt[h:h + 1, :]
            y = _dot(m.astype(BF16), xs.astype(BF16))
            y = y + _dot(cmat, st_ref[h].astype(BF16)) * decay_in[:, h:h + 1]
            y = y + xs * d_ref[:, h:h + 1]
            y_ref[:, h * SSD_HEAD_DIM:(h + 1) * SSD_HEAD_DIM] = y
            st_new = _dot(bt, (xs * wst[:, h:h + 1]).astype(BF16))
            st_ref[h] = st_ref[h] * cdecay[:, h:h + 1] + st_new

    z = z_ref[0].astype(F32)
    yz = y_ref[...] * (z * jax.nn.sigmoid(z))
    o_ref[0] = _rms(yz, ng_ref[...]).astype(o_ref.dtype)


def _ssd(u16, u32, dtt, cw, cb, dtb, al, d, ng, bsz, seq):
    nc = seq // SSD_CHUNK
    q = SSD_CHUNK
    full = lambda shape: pl.BlockSpec(shape, lambda b, c: (0,) * len(shape))
    return pl.pallas_call(
        _ssd_kernel,
        grid=(bsz, nc),
        in_specs=[pl.BlockSpec((1, q, SSD_CONV_DIM), lambda b, c: (b, c, U16_XBC_BLK)),
                  pl.BlockSpec((1, q, SSD_INNER), lambda b, c: (b, c, U16_Z_BLK)),
                  pl.BlockSpec((1, q, 128), lambda b, c: (b, c, U32_DT_BLK128)),
                  pl.BlockSpec((1, SSD_HEADS, q), lambda b, c: (b, 0, c)),
                  full((CONV_WIDTH, SSD_CONV_DIM)), full((1, SSD_CONV_DIM)),
                  full((1, SSD_HEADS)), full((SSD_HEADS, 1)),
                  full((1, SSD_HEADS)), full((SSD_HEADS, 1)),
                  full((1, SSD_HEADS)), full((1, SSD_INNER))],
        out_specs=pl.BlockSpec((1, q, SSD_INNER), lambda b, c: (b, c, 0)),
        out_shape=jax.ShapeDtypeStruct((bsz, seq, SSD_INNER), BF16),
        scratch_shapes=[pltpu.VMEM((q + 8, SSD_CONV_DIM), F32),
                        pltpu.VMEM((q, SSD_CONV_DIM), F32),
                        pltpu.VMEM((q, SSD_INNER), F32),
                        pltpu.VMEM((SSD_HEADS, SSD_STATE, SSD_HEAD_DIM), F32)],
        compiler_params=_cparams(("parallel", "arbitrary")),
        name="ssd",
    )(u16, u16, u32, dtt, cw, cb, dtb.reshape(1, -1), dtb.reshape(-1, 1), al.reshape(1, -1), al.reshape(-1, 1),
      d.reshape(1, -1), ng.reshape(1, -1))


def _merge_kernel(x_ref, ya_ref, yb_ref, yc_ref, mg_ref, wa_ref, wb_ref, wc_ref, wo_ref, o_ref):
    d = D_MODEL
    mg = mg_ref[...].astype(F32)
    mix = jax.nn.sigmoid(mg[:, 0:d]) * _dot(ya_ref[...], wa_ref[...])
    mix = mix + jax.nn.sigmoid(mg[:, d:2 * d]) * _dot(yb_ref[...], wb_ref[...])
    mix = mix + jax.nn.sigmoid(mg[:, 2 * d:3 * d]) * _dot(yc_ref[...], wc_ref[...])
    o_ref[...] = x_ref[...] + _dot(mix.astype(BF16), wo_ref[...])


def _merge(x, ya, yb, yc, u16, wa, wb, wc, wo, tm):
    m, d = x.shape
    full = lambda a: pl.BlockSpec(a.shape, lambda i: (0, 0))
    return pl.pallas_call(
        _merge_kernel,
        grid=(m // tm,),
        in_specs=[pl.BlockSpec((tm, d), lambda i: (i, 0)),
                  pl.BlockSpec((tm, ya.shape[1]), lambda i: (i, 0)),
                  pl.BlockSpec((tm, yb.shape[1]), lambda i: (i, 0)),
                  pl.BlockSpec((tm, yc.shape[1]), lambda i: (i, 0)),
                  pl.BlockSpec((tm, 3 * d), lambda i: (i, U16_MG_BLK)),
                  full(wa), full(wb), full(wc), full(wo)],
        out_specs=pl.BlockSpec((tm, d), lambda i: (i, 0)),
        out_shape=jax.ShapeDtypeStruct((m, d), F32),
        compiler_params=_cparams(("parallel",)),
        name="merge",
    )(x, ya, yb, yc, u16, wa, wb, wc, wo)


def _ffn_kernel(x_ref, g_ref, w1_ref, w2_ref, o_ref, h_ref, acc_ref):
    j = pl.program_id(1)

    @pl.when(j == 0)
    def _():
        h_ref[...] = _rms(x_ref[...], g_ref[...]).astype(BF16)
        acc_ref[...] = x_ref[...]

    a = jnp.maximum(_dot(h_ref[...], w1_ref[...]), 0.0)
    acc_ref[...] += _dot((a * a).astype(BF16), w2_ref[...])

    @pl.when(j == pl.num_programs(1) - 1)
    def _():
        o_ref[...] = acc_ref[...]


def _ffn(x, g, w1, w2, tm, tf):
    m, d = x.shape
    f = w1.shape[1]
    return pl.pallas_call(
        _ffn_kernel,
        grid=(m // tm, f // tf),
        in_specs=[pl.BlockSpec((tm, d), lambda i, j: (i, 0)),
                  pl.BlockSpec((1, d), lambda i, j: (0, 0)),
                  pl.BlockSpec((d, tf), lambda i, j: (0, j)),
                  pl.BlockSpec((tf, d), lambda i, j: (j, 0))],
        out_specs=pl.BlockSpec((tm, d), lambda i, j: (i, 0)),
        out_shape=jax.ShapeDtypeStruct((m, d), F32),
        scratch_shapes=[pltpu.VMEM((tm, d), BF16), pltpu.VMEM((tm, d), F32)],
        compiler_params=_cparams(("parallel", "arbitrary")),
        name="ffn",
    )(x, g, w1, w2)


def _prep_w_in(w):
    offs = [0] + [int(o) for o in np.cumsum(IN_SPLITS)]
    seg = lambda k: w[:, offs[k]:offs[k + 1]]
    dq, dk, dv, iq, ik, iw, nq, nkv, ng, sz, sxbc, sdt, mg = [seg(k) for k in range(13)]
    zeros = lambda n: jnp.zeros((w.shape[0], n), w.dtype)
    w16 = jnp.concatenate([sxbc, mg, sz, dq, nq], axis=1).astype(BF16)
    w32 = jnp.concatenate([nkv[:, 256:768], nkv[:, 0:256], zeros(256),
                           dk, dv, iq, ik, iw, zeros(512 - 424),
                           ng, zeros(128 - ng.shape[1]),
                           sdt, zeros(128 - sdt.shape[1])], axis=1).astype(BF16)
    return w16, w32


def _pick_tile(n, pref):
    t = min(n, pref)
    while n % t:
        t //= 2
    return t


def kernel(x, norm1_g, w_in, dsa_q_norm, dsa_k_norm, nsa_q_norm, nsa_k_norm, nsa_cmp_pos, nsa_cmp_w,
           ssd_conv_w, ssd_conv_b, ssd_dt_bias, ssd_a_log, ssd_d, ssd_norm_g,
           w_br_dsa, w_br_nsa, w_br_ssd, w_out, norm2_g, w_ff1, w_ff2):
    bsz, seq, d = x.shape
    m = bsz * seq
    depth = w_in.shape[0]
    nch = seq // CMP_STRIDE
    n_blk = seq // SEL_BLOCK
    half = CMP_BLOCK // 2

    cmp_start = np.arange(nch) * CMP_STRIDE
    blk_start = np.arange(n_blk) * SEL_BLOCK
    cover = ((cmp_start[:, None] < blk_start[None, :] + SEL_BLOCK)
             & (cmp_start[:, None] + CMP_BLOCK > blk_start[None, :])
             & (np.arange(nch)[:, None] < (seq - CMP_BLOCK) // CMP_STRIDE + 1)).astype(np.float32)
    expand = (np.arange(seq)[None, :] // SEL_BLOCK == np.arange(n_blk)[:, None]).astype(np.float32)
    gexp = np.zeros((3, LANES, NSA_HEADS * HEAD_DIM), np.float32)
    for hh in range(NSA_HEADS):
        for br in range(3):
            gexp[br, 3 * hh + br, hh * HEAD_DIM:(hh + 1) * HEAD_DIM] = 1.0
    gexp = jnp.asarray(gexp, dtype=BF16)
    cover = jnp.asarray(cover)
    expand = jnp.asarray(expand, dtype=BF16)

    tm = _pick_tile(m, 1024)
    xf = x.reshape(m, d)
    for l in range(depth):
        w16, w32 = _prep_w_in(w_in[l])
        g1 = norm1_g[l].reshape(1, d)
        u16 = _norm_matmul(xf, g1, w16, BF16, tm, 2048).reshape(bsz, seq, U16_WIDTH)
        u32 = _norm_matmul(xf, g1, w32, F32, tm, U32_WIDTH).reshape(bsz, seq, U32_WIDTH)

        iw0 = U32_DS_BLK512 * 512 + DS_IW
        iwt = u32[:, :, iw0:iw0 + IDX_HEADS].transpose(0, 2, 1)
        ya = _dsa(u16, u32, iwt, dsa_q_norm[l].reshape(1, -1), dsa_k_norm[l].reshape(1, -1), bsz, seq)

        kcvc = u32[:, :, 512:768].reshape(bsz, nch, CMP_STRIDE, 2, NSA_KV_HEADS, HEAD_DIM)
        kcvc = kcvc.transpose(0, 3, 4, 1, 2, 5).reshape(bsz, 2, NSA_KV_HEADS, nch, CMP_STRIDE * HEAD_DIM)
        cpos = nsa_cmp_pos[l].reshape(2, 2, half * HEAD_DIM)
        cw = nsa_cmp_w[l].reshape(2, 2, half * HEAD_DIM, HEAD_DIM).astype(BF16)
        kc, vc = _nsa_compress(kcvc, cpos, cw, nsa_k_norm[l][0:1])
        yb = _nsa(u16, u32, kc, vc, nsa_q_norm[l].reshape(1, -1), nsa_k_norm[l], cover, expand, gexp, bsz, seq)

        dt0 = U32_DT_BLK128 * 128
        dtt = u32[:, :, dt0:dt0 + SSD_HEADS].transpose(0, 2, 1)
        yc = _ssd(u16, u32, dtt, ssd_conv_w[l], ssd_conv_b[l].reshape(1, -1), ssd_dt_bias[l], ssd_a_log[l],
                  ssd_d[l], ssd_norm_g[l], bsz, seq)

        xf = _merge(xf, ya.reshape(m, -1), yb.reshape(m, -1), yc.reshape(m, -1), u16.reshape(m, U16_WIDTH),
                    w_br_dsa[l].astype(BF16), w_br_nsa[l].astype(BF16), w_br_ssd[l].astype(BF16),
                    w_out[l].astype(BF16), _pick_tile(m, 256))
        xf = _ffn(xf, norm2_g[l].reshape(1, d), w_ff1[l].astype(BF16), w_ff2[l].astype(BF16),
                  _pick_tile(m, 1024), 1024)
    return xf.reshape(bsz, seq, d)
```

```python
import functools
import math

import numpy as np
import jax
import jax.numpy as jnp
from jax import lax
from jax.experimental import pallas as pl
from jax.experimental.pallas import tpu as pltpu

F32 = jnp.float32
BF16 = jnp.bfloat16

D_MODEL = 1024
HEAD_DIM = 64
DSA_HEADS = 16
DSA_KV_DIM = 64
IDX_HEADS = 8
IDX_DIM = 32
DSA_TOPK = 256
NSA_HEADS = 16
NSA_KV_HEADS = 2
NSA_GROUP = NSA_HEADS // NSA_KV_HEADS
CMP_BLOCK = 32
CMP_STRIDE = 16
SEL_BLOCK = 64
SEL_TOPN = 4
WINDOW = 512
SSD_INNER = 2 * D_MODEL
SSD_HEAD_DIM = 64
SSD_HEADS = SSD_INNER // SSD_HEAD_DIM
SSD_GROUPS = 4
SSD_HPG = SSD_HEADS // SSD_GROUPS
SSD_STATE = 128
SSD_CONV_DIM = SSD_INNER + 2 * SSD_GROUPS * SSD_STATE
CONV_WIDTH = 4
SSD_CHUNK = 128
D_FF = 4 * D_MODEL
Q_BLOCK = 128
EPS = 1e-6
NEG = -1e30
IN_SPLITS = (DSA_HEADS * HEAD_DIM, DSA_KV_DIM, DSA_KV_DIM, IDX_HEADS * IDX_DIM, IDX_DIM, IDX_HEADS,
             NSA_HEADS * HEAD_DIM, 6 * NSA_KV_HEADS * HEAD_DIM, 3 * NSA_HEADS,
             SSD_INNER, SSD_CONV_DIM, SSD_HEADS, 3 * D_MODEL)

V7X_VMEM_LIMIT_BYTES = 56 * 1024 * 1024
LANES = 128

U16_WIDTH = 10240
U16_XBC_BLK = 0
U16_MG_BLK = 1
U16_Z_BLK = 3
U16_DQ_BLK = 8
U16_NQ_BLK = 9
U32_WIDTH = 1792
U32_NKV_BLK512 = 0
U32_DS_BLK512 = 2
U32_NG_BLK128 = 12
U32_DT_BLK128 = 13
DS_DK, DS_DV, DS_IQ, DS_IK, DS_IW = 0, 64, 128, 384, 416

KEY_TILE = 256
BISECT_ITERS = 28
PAIR_STEP = 2
LOG2E = 1.4426950408889634
BISECT_UNROLL = 4


def _cparams(sem):
    return pltpu.CompilerParams(dimension_semantics=sem, vmem_limit_bytes=V7X_VMEM_LIMIT_BYTES)


def _rms(x, g):
    return x * lax.rsqrt(jnp.mean(x * x, axis=-1, keepdims=True) + EPS) * g


def _dot_nt(a, b):
    return lax.dot_general(a, b, (((1,), (1,)), ((), ())), preferred_element_type=F32)


def _dot(a, b):
    return jnp.dot(a, b, preferred_element_type=F32)


def _dot_hi(a, b):
    return jnp.dot(a, b, preferred_element_type=F32, precision=lax.Precision.HIGHEST)


def _norm_matmul_kernel(x_ref, g_ref, w_ref, o_ref, h_ref):
    @pl.when(pl.program_id(1) == 0)
    def _():
        h_ref[...] = _rms(x_ref[...], g_ref[...]).astype(BF16)

    o_ref[...] = _dot(h_ref[...], w_ref[...]).astype(o_ref.dtype)


def _norm_matmul(x, g, w, out_dtype, tm, tn):
    m, k = x.shape
    n = w.shape[1]
    return pl.pallas_call(
        _norm_matmul_kernel,
        grid=(m // tm, n // tn),
        in_specs=[pl.BlockSpec((tm, k), lambda i, j: (i, 0)),
                  pl.BlockSpec((1, k), lambda i, j: (0, 0)),
                  pl.BlockSpec((k, tn), lambda i, j: (0, j))],
        out_specs=pl.BlockSpec((tm, tn), lambda i, j: (i, j)),
        out_shape=jax.ShapeDtypeStruct((m, n), out_dtype),
        scratch_shapes=[pltpu.VMEM((tm, k), BF16)],
        compiler_params=_cparams(("parallel", "arbitrary")),
        name="norm_matmul",
    )(x, g, w)


def _flash_pairs(qp_ref, kbd_ref, wext_ref, bias_ref, nkt, n_pairs, pairs_per_kv, bufs, m_ref, acc_ref):
    (s0, s1), (p0, p1), (a0, a1) = bufs
    n_pg = n_pairs // PAIR_STEP
    rows = PAIR_STEP * Q_BLOCK
    n = nkt * n_pg
    lo_half = lax.broadcasted_iota(jnp.int32, (1, LANES), 1) < HEAD_DIM
    m_ref[...] = jnp.full(m_ref.shape, -jnp.inf, F32)
    acc_ref[...] = jnp.zeros(acc_ref.shape, F32)
    p1[...] = jnp.zeros(p1.shape, BF16)
    a1[...] = jnp.ones(a1.shape, F32)

    def where(j):
        kt = j // n_pg
        pg = j % n_pg
        g = (pg * PAIR_STEP) // pairs_per_kv
        return kt, pg, g, pl.multiple_of(kt * 2 * KEY_TILE, 2 * KEY_TILE), pl.multiple_of(pg * rows, rows)

    def qk(j, s_ref):
        _, _, g, koff, r = where(j)
        s_ref[...] = _dot_nt(qp_ref[pl.ds(r, rows), :], kbd_ref[g, pl.ds(koff, 2 * KEY_TILE), :])

    def softmax(j, s_ref, p_ref, a_ref):
        kt, pg, g, _, _ = where(j)
        for u in range(PAIR_STEP):
            us = slice(u * Q_BLOCK, (u + 1) * Q_BLOCK)
            alpha = []
            for par in range(2):
                cs = slice(par * KEY_TILE, (par + 1) * KEY_TILE)
                hrow = pl.multiple_of(((pg * PAIR_STEP + u) * 2 + par) * Q_BLOCK, Q_BLOCK)
                s = s_ref[us, cs] + bias_ref[g, kt]
                m_old = m_ref[pl.ds(hrow, Q_BLOCK), :]
                m_new = jnp.maximum(m_old, jnp.max(s, axis=-1, keepdims=True))
                alpha.append(jnp.exp2(m_old - m_new))
                m_ref[pl.ds(hrow, Q_BLOCK), :] = m_new
                p_ref[us, cs] = jnp.exp2(s - jnp.concatenate([m_new] * (KEY_TILE // LANES), axis=1)).astype(BF16)
            a_ref[us, :] = jnp.where(lo_half, alpha[0], alpha[1])

    def pv(j, p_ref, a_ref):
        _, _, g, koff, r = where(j)
        a = a_ref[...]
        acc_ref[pl.ds(r, rows), :] = (jnp.concatenate([a, a], axis=1) * acc_ref[pl.ds(r, rows), :]
                                      + _dot(p_ref[...], wext_ref[g, pl.ds(koff, 2 * KEY_TILE), :]))

    qk(0, s0)

    def body(jj, c):
        j = 2 * jj
        qk(j + 1, s1)
        softmax(j, s0, p0, a0)
        pv(jnp.maximum(j - 1, 0), p1, a1)
        qk(jnp.minimum(j + 2, n - 1), s0)
        softmax(j + 1, s1, p1, a1)
        pv(j, p0, a0)
        return c

    lax.fori_loop(0, n // 2, body, 0)
    pv(n - 1, p1, a1)


def _flash_scratch(n_pairs):
    step = PAIR_STEP * Q_BLOCK
    return ([pltpu.VMEM((step, 2 * KEY_TILE), F32)] * 2 + [pltpu.VMEM((step, 2 * KEY_TILE), BF16)] * 2
            + [pltpu.VMEM((step, LANES), F32)] * 2
            + [pltpu.VMEM((2 * n_pairs * Q_BLOCK, LANES), F32), pltpu.VMEM((n_pairs * Q_BLOCK, 2 * LANES), F32)])


def _norm_pairs(q_ref, g_ref, qp_ref, n_pairs):
    mult = HEAD_DIM ** -0.5 * LOG2E
    lo_half = lax.broadcasted_iota(jnp.int32, (1, LANES), 1) < HEAD_DIM
    g2 = jnp.concatenate([g_ref[...], g_ref[...]], axis=-1) * mult
    for j in range(n_pairs):
        x = q_ref[:, j * LANES:(j + 1) * LANES].astype(F32)
        x2 = x * x
        s_lo = jnp.sum(jnp.where(lo_half, x2, 0.0), axis=-1, keepdims=True)
        s_hi = jnp.sum(jnp.where(lo_half, 0.0, x2), axis=-1, keepdims=True)
        r = jnp.where(lo_half, lax.rsqrt(s_lo * (1.0 / HEAD_DIM) + EPS), lax.rsqrt(s_hi * (1.0 / HEAD_DIM) + EPS))
        qp_ref[j * Q_BLOCK:(j + 1) * Q_BLOCK, :] = (x * r * g2).astype(BF16)


def _store_pair_kv(kbd_ref, wext_ref, g, row0, n, k, v):
    z = jnp.zeros((n, HEAD_DIM), F32)
    one = jnp.ones((n, HEAD_DIM), F32)
    kbd_ref[g, pl.ds(row0, n), :] = jnp.concatenate([k, z], axis=-1).astype(BF16)
    kbd_ref[g, pl.ds(row0 + n, n), :] = jnp.concatenate([z, k], axis=-1).astype(BF16)
    wext_ref[g, pl.ds(row0, n), :] = jnp.concatenate([v, z, one, z], axis=-1).astype(BF16)
    wext_ref[g, pl.ds(row0 + n, n), :] = jnp.concatenate([z, v, z, one], axis=-1).astype(BF16)


def _dsa_kernel(q_ref, sq_ref, iwt_ref, sk_ref, gq_ref, gk_ref, o_ref,
                kbd_ref, wext_ref, ik_ref, idx_ref, bias_ref, iqs_ref, pq_ref,
                qp_ref, s0_ref, s1_ref, p0_ref, p1_ref, a0_ref, a1_ref, m_ref, acc_ref, *, seq, n_sel):
    i = pl.program_id(1)
    nq = Q_BLOCK
    n_pairs = DSA_HEADS // 2
    kt_shape = (KEY_TILE, nq)

    @pl.when(i == 0)
    def _():
        def rows(r, c):
            off = pl.multiple_of(r * KEY_TILE, KEY_TILE)
            sk = sk_ref[pl.ds(off, KEY_TILE), :]
            _store_pair_kv(kbd_ref, wext_ref, 0, pl.multiple_of(2 * off, 2 * KEY_TILE), KEY_TILE,
                           _rms(sk[:, DS_DK:DS_DK + HEAD_DIM], gk_ref[...]), sk[:, DS_DV:DS_DV + HEAD_DIM])
            ik_ref[pl.ds(off, KEY_TILE), :] = sk[:, DS_IK:DS_IK + IDX_DIM].astype(BF16)
            return c

        lax.fori_loop(0, seq // KEY_TILE, rows, 0)

    start = i * nq
    nkt = (start + nq + KEY_TILE - 1) // KEY_TILE
    t_row = (start + lax.broadcasted_iota(jnp.int32, (1, nq), 1)).astype(F32)
    sub_pos = lax.broadcasted_iota(jnp.int32, (KEY_TILE, 1), 0).astype(F32)

    sq = sq_ref[...]
    for h in range(IDX_HEADS):
        iqs_ref[h * nq:(h + 1) * nq, :] = sq[:, DS_IQ + h * IDX_DIM:DS_IQ + (h + 1) * IDX_DIM].astype(BF16)
    iwt = iwt_ref[0]

    def idx_tile(kt, carry):
        rmin, rmax = carry
        off = pl.multiple_of(kt * KEY_TILE, KEY_TILE)
        r = _dot_nt(ik_ref[pl.ds(off, KEY_TILE), :], iqs_ref[...])
        acc = jnp.zeros(kt_shape, F32)
        for h in range(IDX_HEADS):
            acc = acc + iwt[h:h + 1, :] * jnp.maximum(r[:, h * nq:(h + 1) * nq], 0.0)
        valid = (sub_pos + (kt * KEY_TILE).astype(F32)) <= t_row
        idx_ref[kt] = jnp.where(valid, acc, -jnp.inf)
        rmin = jnp.minimum(rmin, jnp.min(jnp.where(valid, acc, jnp.inf), axis=0, keepdims=True))
        rmax = jnp.maximum(rmax, jnp.max(jnp.where(valid, acc, -jnp.inf), axis=0, keepdims=True))
        return rmin, rmax

    rmin, rmax = lax.fori_loop(0, nkt, idx_tile,
                               (jnp.full((1, nq), jnp.inf, F32), jnp.full((1, nq), -jnp.inf, F32)))

    kf = float(n_sel)

    def col_sum(x):
        parts = [x[r * 8:(r + 1) * 8] for r in range(KEY_TILE // 8)]
        while len(parts) > 1:
            parts = [parts[k] + parts[k + 1] for k in range(0, len(parts), 2)]
        return jnp.sum(parts[0], axis=0, keepdims=True)

    def count_gt(thr):
        def body(kt, acc):
            return acc + jnp.where(idx_ref[kt] > thr, 1.0, 0.0)

        return col_sum(lax.fori_loop(0, nkt, body, jnp.zeros(kt_shape, F32)))

    n_valid = t_row + 1.0
    lo0 = rmin - (jnp.abs(rmin) * 0.01 + 1.0)

    def bis_cond(c):
        it, _, _, clo, _ = c
        return jnp.logical_and(it < BISECT_ITERS, jnp.max(clo) > kf)

    def bis_body(c):
        it, lo, hi, clo, chi = c
        for _ in range(BISECT_UNROLL):
            mid = 0.5 * (lo + hi)
            cnt = count_gt(mid)
            ge = cnt >= kf
            lo, hi = jnp.where(ge, mid, lo), jnp.where(ge, hi, mid)
            clo, chi = jnp.where(ge, cnt, clo), jnp.where(ge, chi, cnt)
        return it + BISECT_UNROLL, lo, hi, clo, chi

    _, lo, hi, clo, chi = lax.while_loop(
        bis_cond, bis_body, (jnp.int32(0), lo0, rmax, n_valid, jnp.zeros((1, nq), F32)))

    pq_ref[...] = jnp.full((1, nq), float(seq), F32)

    @pl.when(jnp.max(clo) > kf)
    def _():
        quota = kf - chi

        def count_tie_le(pm):
            def body(kt, acc):
                v = idx_ref[kt]
                pos = sub_pos + (kt * KEY_TILE).astype(F32)
                tie = (v > lo) & (v <= hi) & (pos <= pm)
                return acc + jnp.where(tie, 1.0, 0.0)

            return col_sum(lax.fori_loop(0, nkt, body, jnp.zeros(kt_shape, F32)))

        def pbody(_, c):
            plo, phi = c
            mid = jnp.floor(0.5 * (plo + phi))
            ge = count_tie_le(mid) >= quota
            return jnp.where(ge, plo, mid), jnp.where(ge, mid, phi)

        steps = int(math.ceil(math.log2(seq))) + 1
        _, phi = lax.fori_loop(0, steps, pbody,
                               (jnp.full((1, nq), -1.0, F32), jnp.full((1, nq), float(seq - 1), F32)))
        pq_ref[...] = jnp.where(clo > kf, phi, float(seq))

    p_lim = pq_ref[...]
    eye = jnp.where(lax.broadcasted_iota(jnp.int32, (nq, nq), 0) == lax.broadcasted_iota(jnp.int32, (nq, nq), 1),
                    1.0, 0.0).astype(BF16)

    def mask_tile(kt, c):
        v = idx_ref[kt]
        pos = sub_pos + (kt * KEY_TILE).astype(F32)
        sel = (v > hi) | ((v > lo) & (pos <= p_lim))
        hit = _dot_nt(eye, jnp.where(sel, 1.0, 0.0).astype(BF16))
        bias_ref[0, kt] = jnp.where(hit > 0.5, 0.0, NEG)
        return c

    lax.fori_loop(0, nkt, mask_tile, 0)

    _norm_pairs(q_ref, gq_ref, qp_ref, n_pairs)
    _flash_pairs(qp_ref, kbd_ref, wext_ref, bias_ref, nkt, n_pairs, n_pairs,
                 ((s0_ref, s1_ref), (p0_ref, p1_ref), (a0_ref, a1_ref)), m_ref, acc_ref)
    for j in range(n_pairs):
        acc = acc_ref[j * nq:(j + 1) * nq, :]
        o_ref[:, j * LANES:(j + 1) * LANES] = (acc[:, :LANES] / acc[:, LANES:]).astype(o_ref.dtype)


def _dsa(u16, u32, iwt, gq, gk, bsz, seq):
    n_sel = min(DSA_TOPK, seq // 4)
    nqb = seq // Q_BLOCK
    n_pairs = DSA_HEADS // 2
    kern = functools.partial(_dsa_kernel, seq=seq, n_sel=n_sel)
    return pl.pallas_call(
        kern,
        grid=(bsz, nqb),
        in_specs=[pl.BlockSpec((Q_BLOCK, 1024), lambda b, i: (b * nqb + i, U16_DQ_BLK)),
                  pl.BlockSpec((Q_BLOCK, 512), lambda b, i: (b * nqb + i, U32_DS_BLK512)),
                  pl.BlockSpec((1, IDX_HEADS, Q_BLOCK), lambda b, i: (b, 0, i)),
                  pl.BlockSpec((seq, 512), lambda b, i: (b, U32_DS_BLK512)),
                  pl.BlockSpec((1, HEAD_DIM), lambda b, i: (0, 0)),
                  pl.BlockSpec((1, DSA_KV_DIM), lambda b, i: (0, 0))],
        out_specs=pl.BlockSpec((Q_BLOCK, DSA_HEADS * HEAD_DIM), lambda b, i: (b * nqb + i, 0)),
        out_shape=jax.ShapeDtypeStruct((bsz * seq, DSA_HEADS * HEAD_DIM), BF16),
        scratch_shapes=[pltpu.VMEM((1, 2 * seq, LANES), BF16),
                        pltpu.VMEM((1, 2 * seq, 2 * LANES), BF16),
                        pltpu.VMEM((seq, IDX_DIM), BF16),
                        pltpu.VMEM((seq // KEY_TILE, KEY_TILE, Q_BLOCK), F32),
                        pltpu.VMEM((1, seq // KEY_TILE, Q_BLOCK, KEY_TILE), F32),
                        pltpu.VMEM((IDX_HEADS * Q_BLOCK, IDX_DIM), BF16),
                        pltpu.VMEM((1, Q_BLOCK), F32),
                        pltpu.VMEM((n_pairs * Q_BLOCK, LANES), BF16)]
                       + _flash_scratch(n_pairs),
        compiler_params=_cparams(("parallel", "arbitrary")),
        name="dsa",
    )(u16, u32, iwt, u32, gq, gk)


def _nsa_compress_kernel(x_ref, pos_ref, w_ref, g_ref, kc_ref, vc_ref):
    for j in range(2):
        for g in range(NSA_KV_HEADS):
            x = x_ref[0, j, g]
            a = _dot((x + pos_ref[j, 0:1, :]).astype(BF16), w_ref[j, 0])
            b = _dot((x + pos_ref[j, 1:2, :]).astype(BF16), w_ref[j, 1])
            out = a + pltpu.roll(b, b.shape[0] - 1, axis=0)
            if j == 0:
                kc_ref[0, g] = _rms(out, g_ref[...]).astype(BF16)
            else:
                vc_ref[0, g] = out.astype(BF16)


def _nsa_compress(x, pos, w, g):
    bsz, _, _, nch, width = x.shape
    return pl.pallas_call(
        _nsa_compress_kernel,
        grid=(bsz,),
        in_specs=[pl.BlockSpec((1, 2, NSA_KV_HEADS, nch, width), lambda b: (b, 0, 0, 0, 0)),
                  pl.BlockSpec((2, 2, width), lambda b: (0, 0, 0)),
                  pl.BlockSpec((2, 2, width, HEAD_DIM), lambda b: (0, 0, 0, 0)),
                  pl.BlockSpec((1, HEAD_DIM), lambda b: (0, 0))],
        out_specs=[pl.BlockSpec((1, NSA_KV_HEADS, nch, HEAD_DIM), lambda b: (b, 0, 0, 0)),
                   pl.BlockSpec((1, NSA_KV_HEADS, nch, HEAD_DIM), lambda b: (b, 0, 0, 0))],
        out_shape=[jax.ShapeDtypeStruct((bsz, NSA_KV_HEADS, nch, HEAD_DIM), BF16),
                   jax.ShapeDtypeStruct((bsz, NSA_KV_HEADS, nch, HEAD_DIM), BF16)],
        compiler_params=_cparams(("parallel",)),
        name="nsa_compress",
    )(x, pos, w, g)


def _nsa_kernel(q_ref, ng_ref, kv_ref, kc_ref, vc_ref, gq_ref, gk_ref, cover_ref, expand_ref, gexp_ref, o_ref,
                ksbd_ref, wsext_ref, kwbd_ref, wwext_ref, kcbd_ref, vcext_ref, bias_ref, wbias_ref, psum_ref,
                ocmp_ref, accw_ref, qp_ref, sw_ref, pw_ref,
                s0_ref, s1_ref, p0_ref, p1_ref, a0_ref, a1_ref, m_ref, acc_ref, *, seq):
    i = pl.program_id(1)
    nq = Q_BLOCK
    n_pairs = NSA_HEADS // 2
    gpairs = NSA_GROUP // 2
    grows = gpairs * nq
    n_cmp_pad = kc_ref.shape[2]
    n_blk = seq // SEL_BLOCK
    n_sel = min(SEL_TOPN, n_blk)
    win_keys = min(WINDOW + Q_BLOCK, seq)
    win_tiles = win_keys // Q_BLOCK
    wt = Q_BLOCK

    @pl.when(i == 0)
    def _():
        def rows(r, c):
            off = pl.multiple_of(r * KEY_TILE, KEY_TILE)
            kv = kv_ref[pl.ds(off, KEY_TILE), :]
            for g in range(NSA_KV_HEADS):
                c0 = g * HEAD_DIM
                _store_pair_kv(ksbd_ref, wsext_ref, g, pl.multiple_of(2 * off, 2 * KEY_TILE), KEY_TILE,
                               _rms(kv[:, c0:c0 + 64], gk_ref[1:2, :]), kv[:, 128 + c0:128 + c0 + 64])
                kw = _rms(kv[:, 256 + c0:256 + c0 + 64], gk_ref[2:3, :])
                vw = kv[:, 384 + c0:384 + c0 + 64]
                for hf in range(KEY_TILE // wt):
                    _store_pair_kv(kwbd_ref, wwext_ref, g, pl.multiple_of(2 * off + hf * 2 * wt, 2 * wt), wt,
                                   kw[hf * wt:(hf + 1) * wt], vw[hf * wt:(hf + 1) * wt])
            return c

        lax.fori_loop(0, seq // KEY_TILE, rows, 0)
        zc = jnp.zeros((n_cmp_pad, HEAD_DIM), BF16)
        for g in range(NSA_KV_HEADS):
            kcbd_ref[g, 0:n_cmp_pad, :] = jnp.concatenate([kc_ref[0, g], zc], axis=-1)
            kcbd_ref[g, n_cmp_pad:2 * n_cmp_pad, :] = jnp.concatenate([zc, kc_ref[0, g]], axis=-1)
            vcext_ref[g, 0:n_cmp_pad, :] = jnp.concatenate([vc_ref[0, g], zc], axis=-1)
            vcext_ref[g, n_cmp_pad:2 * n_cmp_pad, :] = jnp.concatenate([zc, vc_ref[0, g]], axis=-1)

    start = i * nq
    nkt = (start + nq + KEY_TILE - 1) // KEY_TILE
    t_i = start + lax.broadcasted_iota(jnp.int32, (nq, 1), 0)
    t_col = t_i.astype(F32)
    lane_pos = lax.broadcasted_iota(jnp.int32, (1, KEY_TILE), 1).astype(F32)

    _norm_pairs(q_ref, gq_ref, qp_ref, n_pairs)

    cmp_end = (lax.broadcasted_iota(jnp.int32, (1, n_cmp_pad), 1) * CMP_STRIDE + (CMP_BLOCK - 1)).astype(F32)
    vis = cmp_end <= t_col
    for g in range(NSA_KV_HEADS):
        sw_ref[g * grows:(g + 1) * grows, 0:2 * n_cmp_pad] = _dot_nt(qp_ref[g * grows:(g + 1) * grows, :], kcbd_ref[g])
    psum_ref[...] = jnp.zeros(psum_ref.shape, F32)

    def cmp_pair(j, c):
        r = pl.multiple_of(j * nq, nq)
        g = j // gpairs
        tot = jnp.zeros((nq, n_cmp_pad), F32)
        for par in range(2):
            cs = slice(par * n_cmp_pad, (par + 1) * n_cmp_pad)
            s = jnp.where(vis, sw_ref[pl.ds(r, nq), cs], NEG)
            e = jnp.exp2(s - jnp.max(s, axis=-1, keepdims=True))
            p = jnp.where(vis, e / jnp.sum(e, axis=-1, keepdims=True), 0.0)
            pw_ref[pl.ds(r, nq), cs] = p.astype(BF16)
            tot = tot + p
        psum_ref[g] = psum_ref[g] + tot
        return c

    lax.fori_loop(0, n_pairs, cmp_pair, 0, unroll=2)
    for g in range(NSA_KV_HEADS):
        ocmp_ref[g * grows:(g + 1) * grows, :] = _dot(pw_ref[g * grows:(g + 1) * grows, 0:2 * n_cmp_pad], vcext_ref[g])

    blk_j = lax.broadcasted_iota(jnp.int32, (1, n_blk), 1)
    cur1 = jnp.right_shift(t_i, int(math.log2(SEL_BLOCK)))
    cur = jnp.concatenate([cur1] * NSA_KV_HEADS, axis=0)
    imp = jnp.concatenate([_dot_hi(psum_ref[g], cover_ref[...]) for g in range(NSA_KV_HEADS)], axis=0)
    forced = (blk_j == cur) | (blk_j == 0)
    imp = jnp.where(forced, jnp.inf, jnp.where(blk_j > cur, -jnp.inf, imp))
    selb = jnp.zeros(imp.shape, jnp.bool_)
    for _ in range(n_sel):
        mx = jnp.max(imp, axis=-1, keepdims=True)
        first = jnp.min(jnp.where(imp == mx, blk_j, n_blk), axis=-1, keepdims=True)
        pick = blk_j == first
        selb = selb | pick
        imp = jnp.where(pick, -jnp.inf, imp)
    selb_bf = jnp.where(selb, 1.0, 0.0).astype(BF16)

    def mask_tile(kt, c):
        off = pl.multiple_of(kt * KEY_TILE, KEY_TILE)
        hit = _dot(selb_bf, expand_ref[:, pl.ds(off, KEY_TILE)])
        ok = (lane_pos + (kt * KEY_TILE).astype(F32)) <= t_col
        for g in range(NSA_KV_HEADS):
            bias_ref[g, kt] = jnp.where((hit[g * nq:(g + 1) * nq] > 0.5) & ok, 0.0, NEG)
        return c

    lax.fori_loop(0, nkt, mask_tile, 0)

    _flash_pairs(qp_ref, ksbd_ref, wsext_ref, bias_ref, nkt, n_pairs, gpairs,
                 ((s0_ref, s1_ref), (p0_ref, p1_ref), (a0_ref, a1_ref)), m_ref, acc_ref)

    wbase = pl.multiple_of(jnp.maximum(start + nq - win_keys, 0), Q_BLOCK)
    wb2 = pl.multiple_of(2 * wbase, 2 * Q_BLOCK)
    col = lax.broadcasted_iota(jnp.int32, (1, 2 * win_keys), 1)
    wpos = (wbase + jnp.right_shift(col, int(math.log2(2 * wt))) * wt + jnp.bitwise_and(col, wt - 1)).astype(F32)
    wbias_ref[...] = jnp.where((wpos <= t_col) & (wpos > t_col - float(WINDOW)), 0.0, NEG)
    for g in range(NSA_KV_HEADS):
        sw_ref[g * grows:(g + 1) * grows, :] = _dot_nt(
            qp_ref[g * grows:(g + 1) * grows, :], kwbd_ref[g, pl.ds(wb2, 2 * win_keys), :])

    def win_pair(j, c):
        r = pl.multiple_of(j * nq, nq)
        for par in range(2):
            cols = [slice(t * 2 * wt + par * wt, t * 2 * wt + (par + 1) * wt) for t in range(win_tiles)]
            s = [sw_ref[pl.ds(r, nq), cs] + wbias_ref[:, cs] for cs in cols]
            mx = s[0]
            for st in s[1:]:
                mx = jnp.maximum(mx, st)
            m = jnp.max(mx, axis=-1, keepdims=True)
            for cs, st in zip(cols, s):
                pw_ref[pl.ds(r, nq), cs] = jnp.exp2(st - m).astype(BF16)
        return c

    lax.fori_loop(0, n_pairs, win_pair, 0)
    for g in range(NSA_KV_HEADS):
        accw_ref[g * grows:(g + 1) * grows, :] = _dot(
            pw_ref[g * grows:(g + 1) * grows, :], wwext_ref[g, pl.ds(wb2, 2 * win_keys), :])

    gates = jax.nn.sigmoid(ng_ref[...])
    g_hi = gates.astype(BF16)
    g_lo = (gates - g_hi.astype(F32)).astype(BF16)
    for j in range(n_pairs):
        rs = slice(j * nq, (j + 1) * nq)
        ls = slice(j * LANES, (j + 1) * LANES)
        gb = [_dot(g_hi, gexp_ref[b, :, ls]) + _dot(g_lo, gexp_ref[b, :, ls]) for b in range(3)]
        acc = acc_ref[rs, :]
        accw = accw_ref[rs, :]
        out = (gb[0] * ocmp_ref[rs, :] + gb[1] * (acc[:, :LANES] / acc[:, LANES:])
               + gb[2] * (accw[:, :LANES] / accw[:, LANES:]))
        o_ref[:, ls] = out.astype(o_ref.dtype)


def _nsa(u16, u32, kc, vc, gq, gk, cover, expand, gexp, bsz, seq):
    nqb = seq // Q_BLOCK
    nch = kc.shape[2]
    n_blk = seq // SEL_BLOCK
    n_pairs = NSA_HEADS // 2
    prow = n_pairs * Q_BLOCK
    win_keys = min(WINDOW + Q_BLOCK, seq)
    kern = functools.partial(_nsa_kernel, seq=seq)
    return pl.pallas_call(
        kern,
        grid=(bsz, nqb),
        in_specs=[pl.BlockSpec((Q_BLOCK, 1024), lambda b, i: (b * nqb + i, U16_NQ_BLK)),
                  pl.BlockSpec((Q_BLOCK, 128), lambda b, i: (b * nqb + i, U32_NG_BLK128)),
                  pl.BlockSpec((seq, 512), lambda b, i: (b, U32_NKV_BLK512)),
                  pl.BlockSpec((1, NSA_KV_HEADS, nch, HEAD_DIM), lambda b, i: (b, 0, 0, 0)),
                  pl.BlockSpec((1, NSA_KV_HEADS, nch, HEAD_DIM), lambda b, i: (b, 0, 0, 0)),
                  pl.BlockSpec((1, HEAD_DIM), lambda b, i: (0, 0)),
                  pl.BlockSpec((3, HEAD_DIM), lambda b, i: (0, 0)),
                  pl.BlockSpec((nch, n_blk), lambda b, i: (0, 0)),
                  pl.BlockSpec((n_blk, seq), lambda b, i: (0, 0)),
                  pl.BlockSpec((3, LANES, NSA_HEADS * HEAD_DIM), lambda b, i: (0, 0, 0))],
        out_specs=pl.BlockSpec((Q_BLOCK, NSA_HEADS * HEAD_DIM), lambda b, i: (b * nqb + i, 0)),
        out_shape=jax.ShapeDtypeStruct((bsz * seq, NSA_HEADS * HEAD_DIM), BF16),
        scratch_shapes=[pltpu.VMEM((NSA_KV_HEADS, 2 * seq, LANES), BF16),
                        pltpu.VMEM((NSA_KV_HEADS, 2 * seq, 2 * LANES), BF16),
                        pltpu.VMEM((NSA_KV_HEADS, 2 * seq, LANES), BF16),
                        pltpu.VMEM((NSA_KV_HEADS, 2 * seq, 2 * LANES), BF16),
                        pltpu.VMEM((NSA_KV_HEADS, 2 * nch, LANES), BF16),
                        pltpu.VMEM((NSA_KV_HEADS, 2 * nch, LANES), BF16),
                        pltpu.VMEM((NSA_KV_HEADS, seq // KEY_TILE, Q_BLOCK, KEY_TILE), F32),
                        pltpu.VMEM((Q_BLOCK, 2 * win_keys), F32),
                        pltpu.VMEM((NSA_KV_HEADS, Q_BLOCK, nch), F32),
                        pltpu.VMEM((prow, LANES), F32),
                        pltpu.VMEM((prow, 2 * LANES), F32),
                        pltpu.VMEM((prow, LANES), BF16),
                        pltpu.VMEM((prow, 2 * win_keys), F32),
                        pltpu.VMEM((prow, 2 * win_keys), BF16)]
                       + _flash_scratch(n_pairs),
        compiler_params=_cparams(("parallel", "arbitrary")),
        name="nsa",
    )(u16, u32, u32, kc, vc, gq, gk, cover, expand, gexp)


def _softplus(x):
    return jnp.maximum(x, 0.0) + jnp.log1p(jnp.exp(-jnp.abs(x)))


def _ssd_kernel(xbc_ref, z_ref, dt_ref, dtt_ref, cw_ref, cb_ref, dtb_ref, dtbt_ref, al_ref, alt_ref,
                d_ref, ng_ref, o_ref, xcat_ref, xa_ref, y_ref, st_ref):
    c = pl.program_id(1)
    q = SSD_CHUNK
    gn = SSD_GROUPS * SSD_STATE
    tail = 16
    lo_half = lax.broadcasted_iota(jnp.int32, (1, LANES), 1) < SSD_HEAD_DIM

    @pl.when(c == 0)
    def _():
        st_ref[...] = jnp.zeros_like(st_ref)
        xcat_ref[0:q, :] = jnp.zeros((q, SSD_CONV_DIM), BF16)

    @pl.when(c > 0)
    def _():
        xcat_ref[q - tail:q, :] = xcat_ref[2 * q - tail:2 * q, :]

    xcat_ref[q:2 * q, :] = xbc_ref[...]
    ri3 = lax.broadcasted_iota(jnp.int32, ((CONV_WIDTH - 1) * q, 2 * q), 0)
    ci3 = lax.broadcasted_iota(jnp.int32, ((CONV_WIDTH - 1) * q, 2 * q), 1)
    qbits = int(math.log2(q))
    src = q + jnp.bitwise_and(ri3, q - 1) - (jnp.right_shift(ri3, qbits) + 1)
    shifts = jnp.where(ci3 == src, 1.0, 0.0).astype(BF16)
    for cc in range(SSD_CONV_DIM // 512):
        cs = slice(cc * 512, (cc + 1) * 512)
        sh = _dot(shifts, xcat_ref[:, cs])
        acc = cb_ref[:, cs] + cw_ref[CONV_WIDTH - 1:CONV_WIDTH, cs] * xcat_ref[q:2 * q, cs].astype(F32)
        for s in range(1, CONV_WIDTH):
            acc = acc + cw_ref[CONV_WIDTH - 1 - s:CONV_WIDTH - s, cs] * sh[(s - 1) * q:s * q]
        xa_ref[:, cs] = acc * jax.nn.sigmoid(acc)

    dt = _softplus(dt_ref[:, :SSD_HEADS] + dtb_ref[...])
    dtt = _softplus(dtt_ref[0] + dtbt_ref[...])
    a = -jnp.exp(al_ref[...]) * LOG2E
    at = -jnp.exp(alt_ref[...]) * LOG2E
    ri = lax.broadcasted_iota(jnp.int32, (q, q), 0)
    ci = lax.broadcasted_iota(jnp.int32, (q, q), 1)
    tril = ri >= ci
    acum = _dot_hi(jnp.where(tril, 1.0, 0.0), dt * a)
    acumt = _dot_hi(dtt * at, jnp.where(ri <= ci, 1.0, 0.0))
    wrow = jnp.exp2(acumt[:, q - 1:q] - acumt) * dtt

    for g in range(SSD_GROUPS):
        bmat = xa_ref[:, SSD_INNER + g * SSD_STATE:SSD_INNER + (g + 1) * SSD_STATE]
        cmat = xa_ref[:, SSD_INNER + gn + g * SSD_STATE:SSD_INNER + gn + (g + 1) * SSD_STATE].astype(BF16)
        cb = _dot_nt(cmat, bmat.astype(BF16))
        bt = bmat.T
        for pp in range(SSD_HPG // 2):
            j = g * (SSD_HPG // 2) + pp
            ls = slice(j * LANES, (j + 1) * LANES)
            xs = xa_ref[:, ls]
            xbd = jnp.concatenate([jnp.where(lo_half, xs, 0.0), jnp.where(lo_half, 0.0, xs)], axis=0).astype(BF16)
            mm, dec, btw = [], [], []
            for par in range(2):
                h = 2 * j + par
                abc = jnp.broadcast_to(acum[:, h:h + 1], (q, q))
                lmat = jnp.exp2(jnp.where(tril, abc - acumt[h:h + 1, :], -jnp.inf))
                mm.append((cb * lmat * dtt[h:h + 1, :]).astype(BF16))
                dec.append(jnp.exp2(abc))
                btw.append((bt * wrow[h:h + 1, :]).astype(BF16))
            y = _dot(jnp.concatenate(mm, axis=1), xbd)
            y = y + _dot(cmat, st_ref[j].astype(BF16)) * jnp.where(lo_half, dec[0], dec[1])
            y_ref[:, ls] = y + xs * d_ref[:, ls]
            cdec = jnp.where(lo_half, dec[0][q - 1:q, :], dec[1][q - 1:q, :])
            st_ref[j] = st_ref[j] * cdec + _dot(jnp.concatenate(btw, axis=1), xbd)

    z = z_ref[...].astype(F32)
    yz = y_ref[...] * (z * jax.nn.sigmoid(z))
    o_ref[...] = _rms(yz, ng_ref[...]).astype(o_ref.dtype)


def _ssd(u16, u32, dtt, cw, cb, dtb, al, d, ng, bsz, seq):
    nc = seq // SSD_CHUNK
    q = SSD_CHUNK
    full = lambda shape: pl.BlockSpec(shape, lambda b, c: (0,) * len(shape))
    d_full = jnp.repeat(d, SSD_HEAD_DIM).reshape(1, SSD_INNER)
    return pl.pallas_call(
        _ssd_kernel,
        grid=(bsz, nc),
        in_specs=[pl.BlockSpec((q, SSD_CONV_DIM), lambda b, c: (b * nc + c, U16_XBC_BLK)),
                  pl.BlockSpec((q, SSD_INNER), lambda b, c: (b * nc + c, U16_Z_BLK)),
                  pl.BlockSpec((q, 128), lambda b, c: (b * nc + c, U32_DT_BLK128)),
                  pl.BlockSpec((1, SSD_HEADS, q), lambda b, c: (b, 0, c)),
                  full((CONV_WIDTH, SSD_CONV_DIM)), full((1, SSD_CONV_DIM)),
                  full((1, SSD_HEADS)), full((SSD_HEADS, 1)),
                  full((1, SSD_HEADS)), full((SSD_HEADS, 1)),
                  full((1, SSD_INNER)), full((1, SSD_INNER))],
        out_specs=pl.BlockSpec((q, SSD_INNER), lambda b, c: (b * nc + c, 0)),
        out_shape=jax.ShapeDtypeStruct((bsz * seq, SSD_INNER), BF16),
        scratch_shapes=[pltpu.VMEM((2 * q, SSD_CONV_DIM), BF16),
                        pltpu.VMEM((q, SSD_CONV_DIM), F32),
                        pltpu.VMEM((q, SSD_INNER), F32),
                        pltpu.VMEM((SSD_HEADS // 2, SSD_STATE, LANES), F32)],
        compiler_params=_cparams(("parallel", "arbitrary")),
        name="ssd",
    )(u16, u16, u32, dtt, cw, cb, dtb.reshape(1, -1), dtb.reshape(-1, 1), al.reshape(1, -1), al.reshape(-1, 1),
      d_full, ng.reshape(1, -1))


def _merge_kernel(x_ref, ya_ref, yb_ref, yc_ref, mg_ref, wa_ref, wb_ref, wc_ref, wo_ref, o_ref):
    d = D_MODEL
    mg = mg_ref[...].astype(F32)
    mix = jax.nn.sigmoid(mg[:, 0:d]) * _dot(ya_ref[...], wa_ref[...])
    mix = mix + jax.nn.sigmoid(mg[:, d:2 * d]) * _dot(yb_ref[...], wb_ref[...])
    mix = mix + jax.nn.sigmoid(mg[:, 2 * d:3 * d]) * _dot(yc_ref[...], wc_ref[...])
    o_ref[...] = x_ref[...] + _dot(mix.astype(BF16), wo_ref[...])


def _merge(x, ya, yb, yc, u16, wa, wb, wc, wo, tm):
    m, d = x.shape
    full = lambda a: pl.BlockSpec(a.shape, lambda i: (0, 0))
    return pl.pallas_call(
        _merge_kernel,
        grid=(m // tm,),
        in_specs=[pl.BlockSpec((tm, d), lambda i: (i, 0)),
                  pl.BlockSpec((tm, ya.shape[1]), lambda i: (i, 0)),
                  pl.BlockSpec((tm, yb.shape[1]), lambda i: (i, 0)),
                  pl.BlockSpec((tm, yc.shape[1]), lambda i: (i, 0)),
                  pl.BlockSpec((tm, 3 * d), lambda i: (i, U16_MG_BLK)),
                  full(wa), full(wb), full(wc), full(wo)],
        out_specs=pl.BlockSpec((tm, d), lambda i: (i, 0)),
        out_shape=jax.ShapeDtypeStruct((m, d), F32),
        compiler_params=_cparams(("parallel",)),
        name="merge",
    )(x, ya, yb, yc, u16, wa, wb, wc, wo)


def _ffn_kernel(x_ref, g_ref, w1_ref, w2_ref, o_ref, h_ref, acc_ref):
    j = pl.program_id(1)

    @pl.when(j == 0)
    def _():
        h_ref[...] = _rms(x_ref[...], g_ref[...]).astype(BF16)
        acc_ref[...] = x_ref[...]

    a = jnp.maximum(_dot(h_ref[...], w1_ref[...]), 0.0)
    acc_ref[...] += _dot((a * a).astype(BF16), w2_ref[...])

    @pl.when(j == pl.num_programs(1) - 1)
    def _():
        o_ref[...] = acc_ref[...]


def _ffn(x, g, w1, w2, tm, tf):
    m, d = x.shape
    f = w1.shape[1]
    return pl.pallas_call(
        _ffn_kernel,
        grid=(m // tm, f // tf),
        in_specs=[pl.BlockSpec((tm, d), lambda i, j: (i, 0)),
                  pl.BlockSpec((1, d), lambda i, j: (0, 0)),
                  pl.BlockSpec((d, tf), lambda i, j: (0, j)),
                  pl.BlockSpec((tf, d), lambda i, j: (j, 0))],
        out_specs=pl.BlockSpec((tm, d), lambda i, j: (i, 0)),
        out_shape=jax.ShapeDtypeStruct((m, d), F32),
        scratch_shapes=[pltpu.VMEM((tm, d), BF16), pltpu.VMEM((tm, d), F32)],
        compiler_params=_cparams(("parallel", "arbitrary")),
        name="ffn",
    )(x, g, w1, w2)


def _prep_w_in(w):
    offs = [0] + [int(o) for o in np.cumsum(IN_SPLITS)]
    seg = lambda k: w[:, offs[k]:offs[k + 1]]
    dq, dk, dv, iq, ik, iw, nq, nkv, ng, sz, sxbc, sdt, mg = [seg(k) for k in range(13)]
    zeros = lambda n: jnp.zeros((w.shape[0], n), w.dtype)
    w16 = jnp.concatenate([sxbc, mg, sz, dq, nq], axis=1).astype(BF16)
    w32 = jnp.concatenate([nkv[:, 256:768], nkv[:, 0:256], zeros(256),
                           dk, dv, iq, ik, iw, zeros(512 - 424),
                           ng, zeros(128 - ng.shape[1]),
                           sdt, zeros(128 - sdt.shape[1])], axis=1).astype(BF16)
    return w16, w32


def _pick_tile(n, pref):
    t = min(n, pref)
    while n % t:
        t //= 2
    return t


def kernel(x, norm1_g, w_in, dsa_q_norm, dsa_k_norm, nsa_q_norm, nsa_k_norm, nsa_cmp_pos, nsa_cmp_w,
           ssd_conv_w, ssd_conv_b, ssd_dt_bias, ssd_a_log, ssd_d, ssd_norm_g,
           w_br_dsa, w_br_nsa, w_br_ssd, w_out, norm2_g, w_ff1, w_ff2):
    bsz, seq, d = x.shape
    m = bsz * seq
    depth = w_in.shape[0]
    nch = seq // CMP_STRIDE
    n_blk = seq // SEL_BLOCK
    half = CMP_BLOCK // 2

    cmp_start = np.arange(nch) * CMP_STRIDE
    blk_start = np.arange(n_blk) * SEL_BLOCK
    cover = ((cmp_start[:, None] < blk_start[None, :] + SEL_BLOCK)
             & (cmp_start[:, None] + CMP_BLOCK > blk_start[None, :])
             & (np.arange(nch)[:, None] < (seq - CMP_BLOCK) // CMP_STRIDE + 1)).astype(np.float32)
    expand = (np.arange(seq)[None, :] // SEL_BLOCK == np.arange(n_blk)[:, None]).astype(np.float32)
    gexp = np.zeros((3, LANES, NSA_HEADS * HEAD_DIM), np.float32)
    for hh in range(NSA_HEADS):
        for br in range(3):
            gexp[br, 3 * hh + br, hh * HEAD_DIM:(hh + 1) * HEAD_DIM] = 1.0
    gexp = jnp.asarray(gexp, dtype=BF16)
    cover = jnp.asarray(cover)
    expand = jnp.asarray(expand, dtype=BF16)

    tm = _pick_tile(m, 1024)
    xf = x.reshape(m, d)
    for l in range(depth):
        w16, w32 = _prep_w_in(w_in[l])
        g1 = norm1_g[l].reshape(1, d)
        u16 = _norm_matmul(xf, g1, w16, BF16, tm, 2048)
        u32 = _norm_matmul(xf, g1, w32, F32, tm, U32_WIDTH)

        iw0 = U32_DS_BLK512 * 512 + DS_IW
        iwt = u32[:, iw0:iw0 + IDX_HEADS].reshape(bsz, seq, IDX_HEADS).transpose(0, 2, 1)
        ya = _dsa(u16, u32, iwt, dsa_q_norm[l].reshape(1, -1), dsa_k_norm[l].reshape(1, -1), bsz, seq)

        kcvc = u32[:, 512:768].reshape(bsz, nch, CMP_STRIDE, 2, NSA_KV_HEADS, HEAD_DIM)
        kcvc = kcvc.transpose(0, 3, 4, 1, 2, 5).reshape(bsz, 2, NSA_KV_HEADS, nch, CMP_STRIDE * HEAD_DIM)
        cpos = nsa_cmp_pos[l].reshape(2, 2, half * HEAD_DIM)
        cw = nsa_cmp_w[l].reshape(2, 2, half * HEAD_DIM, HEAD_DIM).astype(BF16)
        kc, vc = _nsa_compress(kcvc, cpos, cw, nsa_k_norm[l][0:1])
        yb = _nsa(u16, u32, kc, vc, nsa_q_norm[l].reshape(1, -1), nsa_k_norm[l], cover, expand, gexp, bsz, seq)

        dt0 = U32_DT_BLK128 * 128
        dtt = u32[:, dt0:dt0 + SSD_HEADS].reshape(bsz, seq, SSD_HEADS).transpose(0, 2, 1)
        yc = _ssd(u16, u32, dtt, ssd_conv_w[l], ssd_conv_b[l].reshape(1, -1), ssd_dt_bias[l], ssd_a_log[l],
                  ssd_d[l], ssd_norm_g[l], bsz, seq)

        xf = _merge(xf, ya, yb, yc, u16,
                    w_br_dsa[l].astype(BF16), w_br_nsa[l].astype(BF16), w_br_ssd[l].astype(BF16),
                    w_out[l].astype(BF16), _pick_tile(m, 256))
        xf = _ffn(xf, norm2_g[l].reshape(1, d), w_ff1[l].astype(BF16), w_ff2[l].astype(BF16),
                  _pick_tile(m, 1024), 1024)
    return xf.reshape(bsz, seq, d)
```

```python
import functools
import math

import numpy as np
import jax
import jax.numpy as jnp
from jax import lax
from jax.experimental import pallas as pl
from jax.experimental.pallas import tpu as pltpu

F32 = jnp.float32
BF16 = jnp.bfloat16

D_MODEL = 1024
HEAD_DIM = 64
DSA_HEADS = 16
DSA_KV_DIM = 64
IDX_HEADS = 8
IDX_DIM = 32
DSA_TOPK = 256
NSA_HEADS = 16
NSA_KV_HEADS = 2
NSA_GROUP = NSA_HEADS // NSA_KV_HEADS
CMP_BLOCK = 32
CMP_STRIDE = 16
SEL_BLOCK = 64
SEL_TOPN = 4
WINDOW = 512
SSD_INNER = 2 * D_MODEL
SSD_HEAD_DIM = 64
SSD_HEADS = SSD_INNER // SSD_HEAD_DIM
SSD_GROUPS = 4
SSD_HPG = SSD_HEADS // SSD_GROUPS
SSD_STATE = 128
SSD_CONV_DIM = SSD_INNER + 2 * SSD_GROUPS * SSD_STATE
CONV_WIDTH = 4
SSD_CHUNK = 128
D_FF = 4 * D_MODEL
Q_BLOCK = 128
EPS = 1e-6
NEG = -1e30
IN_SPLITS = (DSA_HEADS * HEAD_DIM, DSA_KV_DIM, DSA_KV_DIM, IDX_HEADS * IDX_DIM, IDX_DIM, IDX_HEADS,
             NSA_HEADS * HEAD_DIM, 6 * NSA_KV_HEADS * HEAD_DIM, 3 * NSA_HEADS,
             SSD_INNER, SSD_CONV_DIM, SSD_HEADS, 3 * D_MODEL)

V7X_VMEM_LIMIT_BYTES = 56 * 1024 * 1024
LANES = 128

U16_WIDTH = 10240
U16_XBC_BLK = 0
U16_MG_BLK = 1
U16_Z_BLK = 3
U16_DQ_BLK = 8
U16_NQ_BLK = 9
U32_WIDTH = 1792
U32_NKV_BLK512 = 0
U32_KC_BLK128 = 4
U32_DS_BLK512 = 2
U32_NG_BLK128 = 12
U32_DT_BLK128 = 13
DS_DK, DS_DV, DS_IQ, DS_IK, DS_IW = 0, 64, 128, 384, 416

KEY_TILE = 256
BISECT_ITERS = 28
PAIR_STEP = 2
LOG2E = 1.4426950408889634
BISECT_UNROLL = 4


def _cparams(sem):
    return pltpu.CompilerParams(dimension_semantics=sem, vmem_limit_bytes=V7X_VMEM_LIMIT_BYTES)


def _rms(x, g):
    return x * lax.rsqrt(jnp.mean(x * x, axis=-1, keepdims=True) + EPS) * g


def _dot_nt(a, b):
    return lax.dot_general(a, b, (((1,), (1,)), ((), ())), preferred_element_type=F32)


def _dot(a, b):
    return jnp.dot(a, b, preferred_element_type=F32)


def _dot_hi(a, b):
    return jnp.dot(a, b, preferred_element_type=F32, precision=lax.Precision.HIGHEST)


def _norm_matmul_kernel(x_ref, g_ref, w_ref, o_ref, h_ref):
    @pl.when(pl.program_id(1) == 0)
    def _():
        h_ref[...] = _rms(x_ref[...], g_ref[...]).astype(BF16)

    o_ref[...] = _dot(h_ref[...], w_ref[...]).astype(o_ref.dtype)


def _norm_matmul(x, g, w, out_dtype, tm, tn):
    m, k = x.shape
    n = w.shape[1]
    return pl.pallas_call(
        _norm_matmul_kernel,
        grid=(m // tm, n // tn),
        in_specs=[pl.BlockSpec((tm, k), lambda i, j: (i, 0)),
                  pl.BlockSpec((1, k), lambda i, j: (0, 0)),
                  pl.BlockSpec((k, tn), lambda i, j: (0, j))],
        out_specs=pl.BlockSpec((tm, tn), lambda i, j: (i, j)),
        out_shape=jax.ShapeDtypeStruct((m, n), out_dtype),
        scratch_shapes=[pltpu.VMEM((tm, k), BF16)],
        compiler_params=_cparams(("parallel", "arbitrary")),
        name="norm_matmul",
    )(x, g, w)


def _flash_pairs(qp_ref, kbd_ref, wext_ref, bias_ref, nkt, n_pairs, pairs_per_kv, bufs, m_ref, acc_ref):
    (s0, s1), (p0, p1), (a0, a1) = bufs
    n_pg = n_pairs // PAIR_STEP
    rows = PAIR_STEP * Q_BLOCK
    n = nkt * n_pg
    lo_half = lax.broadcasted_iota(jnp.int32, (1, LANES), 1) < HEAD_DIM
    m_ref[...] = jnp.full(m_ref.shape, -jnp.inf, F32)
    acc_ref[...] = jnp.zeros(acc_ref.shape, F32)
    p1[...] = jnp.zeros(p1.shape, BF16)
    a1[...] = jnp.ones(a1.shape, F32)

    def where(j):
        kt = j // n_pg
        pg = j % n_pg
        g = (pg * PAIR_STEP) // pairs_per_kv
        return kt, pg, g, pl.multiple_of(kt * 2 * KEY_TILE, 2 * KEY_TILE), pl.multiple_of(pg * rows, rows)

    def qk(j, s_ref):
        kt, _, g, koff, r = where(j)
        s = _dot_nt(qp_ref[pl.ds(r, rows), :], kbd_ref[g, pl.ds(koff, 2 * KEY_TILE), :])
        s_ref[...] = s + jnp.tile(bias_ref[g, kt], (PAIR_STEP, 2))

    def softmax(j, s_ref, p_ref, a_ref):
        _, pg, _, _, _ = where(j)
        for u in range(PAIR_STEP):
            us = slice(u * Q_BLOCK, (u + 1) * Q_BLOCK)
            alpha = []
            for par in range(2):
                cs = slice(par * KEY_TILE, (par + 1) * KEY_TILE)
                hrow = pl.multiple_of(((pg * PAIR_STEP + u) * 2 + par) * Q_BLOCK, Q_BLOCK)
                m_old = m_ref[pl.ds(hrow, Q_BLOCK), :]
                m_new = jnp.maximum(m_old, jnp.max(s_ref[us, cs], axis=-1, keepdims=True))
                alpha.append(jnp.exp2(m_old - m_new))
                m_ref[pl.ds(hrow, Q_BLOCK), :] = m_new
                p_ref[us, cs] = jnp.exp2(
                    s_ref[us, cs] - jnp.concatenate([m_new] * (KEY_TILE // LANES), axis=1)).astype(BF16)
            a_ref[us, :] = jnp.where(lo_half, alpha[0], alpha[1])

    def pv(j, p_ref, a_ref):
        _, _, g, koff, r = where(j)
        a = a_ref[...]
        acc_ref[pl.ds(r, rows), :] = (jnp.concatenate([a, a], axis=1) * acc_ref[pl.ds(r, rows), :]
                                      + _dot(p_ref[...], wext_ref[g, pl.ds(koff, 2 * KEY_TILE), :]))

    qk(0, s0)

    def body(jj, c):
        j = 2 * jj
        qk(j + 1, s1)
        softmax(j, s0, p0, a0)
        pv(jnp.maximum(j - 1, 0), p1, a1)
        qk(jnp.minimum(j + 2, n - 1), s0)
        softmax(j + 1, s1, p1, a1)
        pv(j, p0, a0)
        return c

    lax.fori_loop(0, n // 2, body, 0)
    pv(n - 1, p1, a1)


def _flash_scratch(n_pairs):
    step = PAIR_STEP * Q_BLOCK
    return ([pltpu.VMEM((step, 2 * KEY_TILE), F32)] * 2 + [pltpu.VMEM((step, 2 * KEY_TILE), BF16)] * 2
            + [pltpu.VMEM((step, LANES), F32)] * 2
            + [pltpu.VMEM((2 * n_pairs * Q_BLOCK, LANES), F32), pltpu.VMEM((n_pairs * Q_BLOCK, 2 * LANES), F32)])


def _norm_pairs(q_ref, g_ref, qp_ref, n_pairs):
    mult = HEAD_DIM ** -0.5 * LOG2E
    lo_half = lax.broadcasted_iota(jnp.int32, (1, LANES), 1) < HEAD_DIM
    g2 = jnp.concatenate([g_ref[...], g_ref[...]], axis=-1) * mult
    for j in range(n_pairs):
        x = q_ref[:, j * LANES:(j + 1) * LANES].astype(F32)
        x2 = x * x
        s_lo = jnp.sum(jnp.where(lo_half, x2, 0.0), axis=-1, keepdims=True)
        s_hi = jnp.sum(jnp.where(lo_half, 0.0, x2), axis=-1, keepdims=True)
        r = jnp.where(lo_half, lax.rsqrt(s_lo * (1.0 / HEAD_DIM) + EPS), lax.rsqrt(s_hi * (1.0 / HEAD_DIM) + EPS))
        qp_ref[j * Q_BLOCK:(j + 1) * Q_BLOCK, :] = (x * r * g2).astype(BF16)


def _store_pair_kv(kbd_ref, wext_ref, g, row0, n, k, v):
    z = jnp.zeros((n, HEAD_DIM), F32)
    one = jnp.ones((n, HEAD_DIM), F32)
    kbd_ref[g, pl.ds(row0, n), :] = jnp.concatenate([k, z], axis=-1).astype(BF16)
    kbd_ref[g, pl.ds(row0 + n, n), :] = jnp.concatenate([z, k], axis=-1).astype(BF16)
    wext_ref[g, pl.ds(row0, n), :] = jnp.concatenate([v, z, one, z], axis=-1).astype(BF16)
    wext_ref[g, pl.ds(row0 + n, n), :] = jnp.concatenate([z, v, z, one], axis=-1).astype(BF16)


def _dsa_kernel(q_ref, sq_ref, iwt_ref, sk_ref, gq_ref, gk_ref, o_ref,
                kbd_ref, wext_ref, ik_ref, idx_ref, bias_ref, iqs_ref, pq_ref,
                qp_ref, s0_ref, s1_ref, p0_ref, p1_ref, a0_ref, a1_ref, m_ref, acc_ref, *, seq, n_sel):
    i = pl.program_id(1)
    nq = Q_BLOCK
    n_pairs = DSA_HEADS // 2
    kt_shape = (KEY_TILE, nq)

    @pl.when(i == 0)
    def _():
        def rows(r, c):
            off = pl.multiple_of(r * KEY_TILE, KEY_TILE)
            sk = sk_ref[pl.ds(off, KEY_TILE), :]
            _store_pair_kv(kbd_ref, wext_ref, 0, pl.multiple_of(2 * off, 2 * KEY_TILE), KEY_TILE,
                           _rms(sk[:, DS_DK:DS_DK + HEAD_DIM], gk_ref[...]), sk[:, DS_DV:DS_DV + HEAD_DIM])
            ik_ref[pl.ds(off, KEY_TILE), :] = sk[:, DS_IK:DS_IK + IDX_DIM].astype(BF16)
            return c

        lax.fori_loop(0, seq // KEY_TILE, rows, 0)

    start = i * nq
    nkt = (start + nq + KEY_TILE - 1) // KEY_TILE
    t_row = (start + lax.broadcasted_iota(jnp.int32, (1, nq), 1)).astype(F32)
    sub_pos = lax.broadcasted_iota(jnp.int32, (KEY_TILE, 1), 0).astype(F32)

    sq = sq_ref[...]
    for h in range(IDX_HEADS):
        iqs_ref[h * nq:(h + 1) * nq, :] = sq[:, DS_IQ + h * IDX_DIM:DS_IQ + (h + 1) * IDX_DIM].astype(BF16)
    iwt = iwt_ref[0]

    def idx_tile(kt, carry):
        rmin, rmax = carry
        off = pl.multiple_of(kt * KEY_TILE, KEY_TILE)
        r = _dot_nt(ik_ref[pl.ds(off, KEY_TILE), :], iqs_ref[...])
        acc = jnp.zeros(kt_shape, F32)
        for h in range(IDX_HEADS):
            acc = acc + iwt[h:h + 1, :] * jnp.maximum(r[:, h * nq:(h + 1) * nq], 0.0)
        valid = (sub_pos + (kt * KEY_TILE).astype(F32)) <= t_row
        idx_ref[kt] = jnp.where(valid, acc, -jnp.inf)
        rmin = jnp.minimum(rmin, jnp.min(jnp.where(valid, acc, jnp.inf), axis=0, keepdims=True))
        rmax = jnp.maximum(rmax, jnp.max(jnp.where(valid, acc, -jnp.inf), axis=0, keepdims=True))
        return rmin, rmax

    rmin, rmax = lax.fori_loop(0, nkt, idx_tile,
                               (jnp.full((1, nq), jnp.inf, F32), jnp.full((1, nq), -jnp.inf, F32)))

    kf = float(n_sel)

    def col_sum(x):
        parts = [x[r * 8:(r + 1) * 8] for r in range(KEY_TILE // 8)]
        while len(parts) > 1:
            parts = [parts[k] + parts[k + 1] for k in range(0, len(parts), 2)]
        return jnp.sum(parts[0], axis=0, keepdims=True)

    def count_gt(thr):
        def body(kt, acc):
            return acc + jnp.where(idx_ref[kt] > thr, 1.0, 0.0)

        return col_sum(lax.fori_loop(0, nkt, body, jnp.zeros(kt_shape, F32)))

    n_valid = t_row + 1.0
    lo0 = rmin - (jnp.abs(rmin) * 0.01 + 1.0)

    def bis_cond(c):
        it, _, _, clo, _ = c
        return jnp.logical_and(it < BISECT_ITERS, jnp.max(clo) > kf)

    def bis_body(c):
        it, lo, hi, clo, chi = c
        for _ in range(BISECT_UNROLL):
            mid = 0.5 * (lo + hi)
            cnt = count_gt(mid)
            ge = cnt >= kf
            lo, hi = jnp.where(ge, mid, lo), jnp.where(ge, hi, mid)
            clo, chi = jnp.where(ge, cnt, clo), jnp.where(ge, chi, cnt)
        return it + BISECT_UNROLL, lo, hi, clo, chi

    _, lo, hi, clo, chi = lax.while_loop(
        bis_cond, bis_body, (jnp.int32(0), lo0, rmax, n_valid, jnp.zeros((1, nq), F32)))

    pq_ref[...] = jnp.full((1, nq), float(seq), F32)

    @pl.when(jnp.max(clo) > kf)
    def _():
        quota = kf - chi

        def count_tie_le(pm):
            def body(kt, acc):
                v = idx_ref[kt]
                pos = sub_pos + (kt * KEY_TILE).astype(F32)
                tie = (v > lo) & (v <= hi) & (pos <= pm)
                return acc + jnp.where(tie, 1.0, 0.0)

            return col_sum(lax.fori_loop(0, nkt, body, jnp.zeros(kt_shape, F32)))

        def pbody(_, c):
            plo, phi = c
            mid = jnp.floor(0.5 * (plo + phi))
            ge = count_tie_le(mid) >= quota
            return jnp.where(ge, plo, mid), jnp.where(ge, mid, phi)

        steps = int(math.ceil(math.log2(seq))) + 1
        _, phi = lax.fori_loop(0, steps, pbody,
                               (jnp.full((1, nq), -1.0, F32), jnp.full((1, nq), float(seq - 1), F32)))
        pq_ref[...] = jnp.where(clo > kf, phi, float(seq))

    p_lim = pq_ref[...]
    eye = jnp.where(lax.broadcasted_iota(jnp.int32, (nq, nq), 0) == lax.broadcasted_iota(jnp.int32, (nq, nq), 1),
                    1.0, 0.0).astype(BF16)

    def mask_tile(kt, c):
        v = idx_ref[kt]
        pos = sub_pos + (kt * KEY_TILE).astype(F32)
        sel = (v > hi) | ((v > lo) & (pos <= p_lim))
        hit = _dot_nt(eye, jnp.where(sel, 1.0, 0.0).astype(BF16))
        bias_ref[0, kt] = jnp.where(hit > 0.5, 0.0, NEG)
        return c

    lax.fori_loop(0, nkt, mask_tile, 0)

    _norm_pairs(q_ref, gq_ref, qp_ref, n_pairs)
    _flash_pairs(qp_ref, kbd_ref, wext_ref, bias_ref, nkt, n_pairs, n_pairs,
                 ((s0_ref, s1_ref), (p0_ref, p1_ref), (a0_ref, a1_ref)), m_ref, acc_ref)
    for j in range(n_pairs):
        acc = acc_ref[j * nq:(j + 1) * nq, :]
        o_ref[:, j * LANES:(j + 1) * LANES] = (acc[:, :LANES] / acc[:, LANES:]).astype(o_ref.dtype)


def _dsa(u16, u32, iwt, gq, gk, bsz, seq):
    n_sel = min(DSA_TOPK, seq // 4)
    nqb = seq // Q_BLOCK
    n_pairs = DSA_HEADS // 2
    kern = functools.partial(_dsa_kernel, seq=seq, n_sel=n_sel)
    return pl.pallas_call(
        kern,
        grid=(bsz, nqb),
        in_specs=[pl.BlockSpec((Q_BLOCK, 1024), lambda b, i: (b * nqb + i, U16_DQ_BLK)),
                  pl.BlockSpec((Q_BLOCK, 512), lambda b, i: (b * nqb + i, U32_DS_BLK512)),
                  pl.BlockSpec((1, IDX_HEADS, Q_BLOCK), lambda b, i: (b, 0, i)),
                  pl.BlockSpec((seq, 512), lambda b, i: (b, U32_DS_BLK512)),
                  pl.BlockSpec((1, HEAD_DIM), lambda b, i: (0, 0)),
                  pl.BlockSpec((1, DSA_KV_DIM), lambda b, i: (0, 0))],
        out_specs=pl.BlockSpec((Q_BLOCK, DSA_HEADS * HEAD_DIM), lambda b, i: (b * nqb + i, 0)),
        out_shape=jax.ShapeDtypeStruct((bsz * seq, DSA_HEADS * HEAD_DIM), BF16),
        scratch_shapes=[pltpu.VMEM((1, 2 * seq, LANES), BF16),
                        pltpu.VMEM((1, 2 * seq, 2 * LANES), BF16),
                        pltpu.VMEM((seq, IDX_DIM), BF16),
                        pltpu.VMEM((seq // KEY_TILE, KEY_TILE, Q_BLOCK), F32),
                        pltpu.VMEM((1, seq // KEY_TILE, Q_BLOCK, KEY_TILE), F32),
                        pltpu.VMEM((IDX_HEADS * Q_BLOCK, IDX_DIM), BF16),
                        pltpu.VMEM((1, Q_BLOCK), F32),
                        pltpu.VMEM((n_pairs * Q_BLOCK, LANES), BF16)]
                       + _flash_scratch(n_pairs),
        compiler_params=_cparams(("parallel", "arbitrary")),
        name="dsa",
    )(u16, u32, iwt, u32, gq, gk)


def _nsa_compress_kernel(xk_ref, xv_ref, pos_ref, w_ref, g_ref, kc_ref, vc_ref):
    nch = xk_ref.shape[0] // CMP_STRIDE
    for jj, (x_ref, o_ref) in enumerate(((xk_ref, kc_ref), (xv_ref, vc_ref))):
        acc_a = jnp.zeros((nch, LANES), F32)
        acc_b = jnp.zeros((nch, LANES), F32)
        for l in range(CMP_STRIDE):
            x = x_ref[pl.ds(l, nch, stride=CMP_STRIDE), :]
            acc_a = acc_a + _dot((x + pos_ref[jj, l:l + 1, :]).astype(BF16), w_ref[jj, l])
            hi = CMP_STRIDE + l
            acc_b = acc_b + _dot((x + pos_ref[jj, hi:hi + 1, :]).astype(BF16), w_ref[jj, hi])
        out = acc_a + pltpu.roll(acc_b, nch - 1, axis=0)
        for g in range(NSA_KV_HEADS):
            og = out[:, g * HEAD_DIM:(g + 1) * HEAD_DIM]
            o_ref[0, g] = (_rms(og, g_ref[...]) if jj == 0 else og).astype(BF16)


def _nsa_compress(u32, pos, w, g, bsz, seq):
    nch = seq // CMP_STRIDE
    return pl.pallas_call(
        _nsa_compress_kernel,
        grid=(bsz,),
        in_specs=[pl.BlockSpec((seq, LANES), lambda b: (b, U32_KC_BLK128)),
                  pl.BlockSpec((seq, LANES), lambda b: (b, U32_KC_BLK128 + 1)),
                  pl.BlockSpec((2, CMP_BLOCK, LANES), lambda b: (0, 0, 0)),
                  pl.BlockSpec((2, CMP_BLOCK, LANES, LANES), lambda b: (0, 0, 0, 0)),
                  pl.BlockSpec((1, HEAD_DIM), lambda b: (0, 0))],
        out_specs=[pl.BlockSpec((1, NSA_KV_HEADS, nch, HEAD_DIM), lambda b: (b, 0, 0, 0)),
                   pl.BlockSpec((1, NSA_KV_HEADS, nch, HEAD_DIM), lambda b: (b, 0, 0, 0))],
        out_shape=[jax.ShapeDtypeStruct((bsz, NSA_KV_HEADS, nch, HEAD_DIM), BF16),
                   jax.ShapeDtypeStruct((bsz, NSA_KV_HEADS, nch, HEAD_DIM), BF16)],
        compiler_params=_cparams(("parallel",)),
        name="nsa_compress",
    )(u32, u32, pos, w, g)


def _nsa_kernel(q_ref, ng_ref, kv_ref, kc_ref, vc_ref, gq_ref, gk_ref, cover_ref, expand_ref, gexp_ref, o_ref,
                ksbd_ref, wsext_ref, kwbd_ref, wwext_ref, kcbd_ref, vcext_ref, bias_ref, wbias_ref, psum_ref,
                ocmp_ref, accw_ref, qp_ref, sw_ref, pw_ref,
                s0_ref, s1_ref, p0_ref, p1_ref, a0_ref, a1_ref, m_ref, acc_ref, *, seq):
    i = pl.program_id(1)
    nq = Q_BLOCK
    n_pairs = NSA_HEADS // 2
    gpairs = NSA_GROUP // 2
    grows = gpairs * nq
    n_cmp_pad = kc_ref.shape[2]
    n_blk = seq // SEL_BLOCK
    n_sel = min(SEL_TOPN, n_blk)
    win_keys = min(WINDOW + Q_BLOCK, seq)
    win_tiles = win_keys // Q_BLOCK
    wt = Q_BLOCK

    @pl.when(i == 0)
    def _():
        def rows(r, c):
            off = pl.multiple_of(r * KEY_TILE, KEY_TILE)
            kv = kv_ref[pl.ds(off, KEY_TILE), :]
            for g in range(NSA_KV_HEADS):
                c0 = g * HEAD_DIM
                _store_pair_kv(ksbd_ref, wsext_ref, g, pl.multiple_of(2 * off, 2 * KEY_TILE), KEY_TILE,
                               _rms(kv[:, c0:c0 + 64], gk_ref[1:2, :]), kv[:, 128 + c0:128 + c0 + 64])
                kw = _rms(kv[:, 256 + c0:256 + c0 + 64], gk_ref[2:3, :])
                vw = kv[:, 384 + c0:384 + c0 + 64]
                for hf in range(KEY_TILE // wt):
                    _store_pair_kv(kwbd_ref, wwext_ref, g, pl.multiple_of(2 * off + hf * 2 * wt, 2 * wt), wt,
                                   kw[hf * wt:(hf + 1) * wt], vw[hf * wt:(hf + 1) * wt])
            return c

        lax.fori_loop(0, seq // KEY_TILE, rows, 0)
        zc = jnp.zeros((n_cmp_pad, HEAD_DIM), BF16)
        for g in range(NSA_KV_HEADS):
            kcbd_ref[g, 0:n_cmp_pad, :] = jnp.concatenate([kc_ref[0, g], zc], axis=-1)
            kcbd_ref[g, n_cmp_pad:2 * n_cmp_pad, :] = jnp.concatenate([zc, kc_ref[0, g]], axis=-1)
            vcext_ref[g, 0:n_cmp_pad, :] = jnp.concatenate([vc_ref[0, g], zc], axis=-1)
            vcext_ref[g, n_cmp_pad:2 * n_cmp_pad, :] = jnp.concatenate([zc, vc_ref[0, g]], axis=-1)

    start = i * nq
    nkt = (start + nq + KEY_TILE - 1) // KEY_TILE
    t_i = start + lax.broadcasted_iota(jnp.int32, (nq, 1), 0)
    t_col = t_i.astype(F32)
    lane_pos = lax.broadcasted_iota(jnp.int32, (1, KEY_TILE), 1).astype(F32)

    _norm_pairs(q_ref, gq_ref, qp_ref, n_pairs)

    cmp_end = (lax.broadcasted_iota(jnp.int32, (1, n_cmp_pad), 1) * CMP_STRIDE + (CMP_BLOCK - 1)).astype(F32)
    vis = cmp_end <= t_col
    for g in range(NSA_KV_HEADS):
        sw_ref[g * grows:(g + 1) * grows, 0:2 * n_cmp_pad] = _dot_nt(qp_ref[g * grows:(g + 1) * grows, :], kcbd_ref[g])
    for g in range(NSA_KV_HEADS):
        tot = jnp.zeros((nq, n_cmp_pad), F32)
        for j in range(g * gpairs, (g + 1) * gpairs):
            rs = slice(j * nq, (j + 1) * nq)
            for par in range(2):
                cs = slice(par * n_cmp_pad, (par + 1) * n_cmp_pad)
                s = jnp.where(vis, sw_ref[rs, cs], NEG)
                e = jnp.exp2(s - jnp.max(s, axis=-1, keepdims=True))
                p = jnp.where(vis, e / jnp.sum(e, axis=-1, keepdims=True), 0.0)
                pw_ref[rs, cs] = p.astype(BF16)
                tot = tot + p
        psum_ref[g] = tot
        ocmp_ref[g * grows:(g + 1) * grows, :] = _dot(pw_ref[g * grows:(g + 1) * grows, 0:2 * n_cmp_pad], vcext_ref[g])

    blk_j = lax.broadcasted_iota(jnp.int32, (1, n_blk), 1)
    cur1 = jnp.right_shift(t_i, int(math.log2(SEL_BLOCK)))
    cur = jnp.concatenate([cur1] * NSA_KV_HEADS, axis=0)
    imp = jnp.concatenate([_dot_hi(psum_ref[g], cover_ref[...]) for g in range(NSA_KV_HEADS)], axis=0)
    forced = (blk_j == cur) | (blk_j == 0)
    imp = jnp.where(forced, jnp.inf, jnp.where(blk_j > cur, -jnp.inf, imp))
    selb = jnp.zeros(imp.shape, jnp.bool_)
    for _ in range(n_sel):
        mx = jnp.max(imp, axis=-1, keepdims=True)
        first = jnp.min(jnp.where(imp == mx, blk_j, n_blk), axis=-1, keepdims=True)
        pick = blk_j == first
        selb = selb | pick
        imp = jnp.where(pick, -jnp.inf, imp)
    selb_bf = jnp.where(selb, 1.0, 0.0).astype(BF16)

    def mask_tile(kt, c):
        off = pl.multiple_of(kt * KEY_TILE, KEY_TILE)
        hit = _dot(selb_bf, expand_ref[:, pl.ds(off, KEY_TILE)])
        ok = (lane_pos + (kt * KEY_TILE).astype(F32)) <= t_col
        for g in range(NSA_KV_HEADS):
            bias_ref[g, kt] = jnp.where((hit[g * nq:(g + 1) * nq] > 0.5) & ok, 0.0, NEG)
        return c

    lax.fori_loop(0, nkt, mask_tile, 0)

    _flash_pairs(qp_ref, ksbd_ref, wsext_ref, bias_ref, nkt, n_pairs, gpairs,
                 ((s0_ref, s1_ref), (p0_ref, p1_ref), (a0_ref, a1_ref)), m_ref, acc_ref)

    wbase = pl.multiple_of(jnp.maximum(start + nq - win_keys, 0), Q_BLOCK)
    wb2 = pl.multiple_of(2 * wbase, 2 * Q_BLOCK)
    col = lax.broadcasted_iota(jnp.int32, (1, 2 * win_keys), 1)
    wpos = (wbase + jnp.right_shift(col, int(math.log2(2 * wt))) * wt + jnp.bitwise_and(col, wt - 1)).astype(F32)
    wbias_ref[...] = jnp.where((wpos <= t_col) & (wpos > t_col - float(WINDOW)), 0.0, NEG)
    for g in range(NSA_KV_HEADS):
        sw_ref[g * grows:(g + 1) * grows, :] = _dot_nt(
            qp_ref[g * grows:(g + 1) * grows, :], kwbd_ref[g, pl.ds(wb2, 2 * win_keys), :])

    for g in range(NSA_KV_HEADS):
        for j in range(g * gpairs, (g + 1) * gpairs):
            rs = slice(j * nq, (j + 1) * nq)
            for par in range(2):
                cols = [slice(t * 2 * wt + par * wt, t * 2 * wt + (par + 1) * wt) for t in range(win_tiles)]
                s = [sw_ref[rs, cs] + wbias_ref[:, cs] for cs in cols]
                mx = s[0]
                for st in s[1:]:
                    mx = jnp.maximum(mx, st)
                m = jnp.max(mx, axis=-1, keepdims=True)
                for cs, st in zip(cols, s):
                    pw_ref[rs, cs] = jnp.exp2(st - m).astype(BF16)
        accw_ref[g * grows:(g + 1) * grows, :] = _dot(
            pw_ref[g * grows:(g + 1) * grows, :], wwext_ref[g, pl.ds(wb2, 2 * win_keys), :])

    gates = jax.nn.sigmoid(ng_ref[...])
    g_hi = gates.astype(BF16)
    g_lo = (gates - g_hi.astype(F32)).astype(BF16)
    gb = [_dot(g_hi, gexp_ref[b]) + _dot(g_lo, gexp_ref[b]) for b in range(3)]
    for j in range(n_pairs):
        rs = slice(j * nq, (j + 1) * nq)
        ls = slice(j * LANES, (j + 1) * LANES)
        acc = acc_ref[rs, :]
        accw = accw_ref[rs, :]
        out = (gb[0][:, ls] * ocmp_ref[rs, :] + gb[1][:, ls] * (acc[:, :LANES] / acc[:, LANES:])
               + gb[2][:, ls] * (accw[:, :LANES] / accw[:, LANES:]))
        o_ref[:, ls] = out.astype(o_ref.dtype)


def _nsa(u16, u32, kc, vc, gq, gk, cover, expand, gexp, bsz, seq):
    nqb = seq // Q_BLOCK
    nch = kc.shape[2]
    n_blk = seq // SEL_BLOCK
    n_pairs = NSA_HEADS // 2
    prow = n_pairs * Q_BLOCK
    win_keys = min(WINDOW + Q_BLOCK, seq)
    kern = functools.partial(_nsa_kernel, seq=seq)
    return pl.pallas_call(
        kern,
        grid=(bsz, nqb),
        in_specs=[pl.BlockSpec((Q_BLOCK, 1024), lambda b, i: (b * nqb + i, U16_NQ_BLK)),
                  pl.BlockSpec((Q_BLOCK, 128), lambda b, i: (b * nqb + i, U32_NG_BLK128)),
                  pl.BlockSpec((seq, 512), lambda b, i: (b, U32_NKV_BLK512)),
                  pl.BlockSpec((1, NSA_KV_HEADS, nch, HEAD_DIM), lambda b, i: (b, 0, 0, 0)),
                  pl.BlockSpec((1, NSA_KV_HEADS, nch, HEAD_DIM), lambda b, i: (b, 0, 0, 0)),
                  pl.BlockSpec((1, HEAD_DIM), lambda b, i: (0, 0)),
                  pl.BlockSpec((3, HEAD_DIM), lambda b, i: (0, 0)),
                  pl.BlockSpec((nch, n_blk), lambda b, i: (0, 0)),
                  pl.BlockSpec((n_blk, seq), lambda b, i: (0, 0)),
                  pl.BlockSpec((3, LANES, NSA_HEADS * HEAD_DIM), lambda b, i: (0, 0, 0))],
        out_specs=pl.BlockSpec((Q_BLOCK, NSA_HEADS * HEAD_DIM), lambda b, i: (b * nqb + i, 0)),
        out_shape=jax.ShapeDtypeStruct((bsz * seq, NSA_HEADS * HEAD_DIM), BF16),
        scratch_shapes=[pltpu.VMEM((NSA_KV_HEADS, 2 * seq, LANES), BF16),
                        pltpu.VMEM((NSA_KV_HEADS, 2 * seq, 2 * LANES), BF16),
                        pltpu.VMEM((NSA_KV_HEADS, 2 * seq, LANES), BF16),
                        pltpu.VMEM((NSA_KV_HEADS, 2 * seq, 2 * LANES), BF16),
                        pltpu.VMEM((NSA_KV_HEADS, 2 * nch, LANES), BF16),
                        pltpu.VMEM((NSA_KV_HEADS, 2 * nch, LANES), BF16),
                        pltpu.VMEM((NSA_KV_HEADS, seq // KEY_TILE, Q_BLOCK, KEY_TILE), F32),
                        pltpu.VMEM((Q_BLOCK, 2 * win_keys), F32),
                        pltpu.VMEM((NSA_KV_HEADS, Q_BLOCK, nch), F32),
                        pltpu.VMEM((prow, LANES), F32),
                        pltpu.VMEM((prow, 2 * LANES), F32),
                        pltpu.VMEM((prow, LANES), BF16),
                        pltpu.VMEM((prow, 2 * win_keys), F32),
                        pltpu.VMEM((prow, 2 * win_keys), BF16)]
                       + _flash_scratch(n_pairs),
        compiler_params=_cparams(("parallel", "arbitrary")),
        name="nsa",
    )(u16, u32, u32, kc, vc, gq, gk, cover, expand, gexp)


def _softplus(x):
    return jnp.maximum(x, 0.0) + jnp.log1p(jnp.exp(-jnp.abs(x)))


def _ssd_kernel(xbc_ref, z_ref, dt_ref, dtt_ref, cw_ref, cb_ref, dtb_ref, dtbt_ref, al_ref, alt_ref,
                d_ref, ng_ref, o_ref, xcat_ref, xa_ref, y_ref, st_ref):
    c = pl.program_id(1)
    q = SSD_CHUNK
    gn = SSD_GROUPS * SSD_STATE
    tail = 16
    lo_half = lax.broadcasted_iota(jnp.int32, (1, LANES), 1) < SSD_HEAD_DIM

    @pl.when(c == 0)
    def _():
        st_ref[...] = jnp.zeros_like(st_ref)
        xcat_ref[0:q, :] = jnp.zeros((q, SSD_CONV_DIM), BF16)

    @pl.when(c > 0)
    def _():
        xcat_ref[q - tail:q, :] = xcat_ref[2 * q - tail:2 * q, :]

    xcat_ref[q:2 * q, :] = xbc_ref[...]
    ri3 = lax.broadcasted_iota(jnp.int32, ((CONV_WIDTH - 1) * q, 2 * q), 0)
    ci3 = lax.broadcasted_iota(jnp.int32, ((CONV_WIDTH - 1) * q, 2 * q), 1)
    qbits = int(math.log2(q))
    src = q + jnp.bitwise_and(ri3, q - 1) - (jnp.right_shift(ri3, qbits) + 1)
    shifts = jnp.where(ci3 == src, 1.0, 0.0).astype(BF16)
    for cc in range(SSD_CONV_DIM // 512):
        cs = slice(cc * 512, (cc + 1) * 512)
        sh = _dot(shifts, xcat_ref[:, cs])
        acc = cb_ref[:, cs] + cw_ref[CONV_WIDTH - 1:CONV_WIDTH, cs] * xcat_ref[q:2 * q, cs].astype(F32)
        for s in range(1, CONV_WIDTH):
            acc = acc + cw_ref[CONV_WIDTH - 1 - s:CONV_WIDTH - s, cs] * sh[(s - 1) * q:s * q]
        xa_ref[:, cs] = acc * jax.nn.sigmoid(acc)

    dt = _softplus(dt_ref[:, :SSD_HEADS] + dtb_ref[...])
    dtt = _softplus(dtt_ref[0] + dtbt_ref[...])
    a = -jnp.exp(al_ref[...]) * LOG2E
    at = -jnp.exp(alt_ref[...]) * LOG2E
    ri = lax.broadcasted_iota(jnp.int32, (q, q), 0)
    ci = lax.broadcasted_iota(jnp.int32, (q, q), 1)
    tril = ri >= ci
    acum = _dot_hi(jnp.where(tril, 1.0, 0.0), dt * a)
    acumt = _dot_hi(dtt * at, jnp.where(ri <= ci, 1.0, 0.0))
    wrow = jnp.exp2(acumt[:, q - 1:q] - acumt) * dtt

    for g in range(SSD_GROUPS):
        bmat = xa_ref[:, SSD_INNER + g * SSD_STATE:SSD_INNER + (g + 1) * SSD_STATE]
        cmat = xa_ref[:, SSD_INNER + gn + g * SSD_STATE:SSD_INNER + gn + (g + 1) * SSD_STATE].astype(BF16)
        cb = _dot_nt(cmat, bmat.astype(BF16))
        bt = bmat.T
        for pp in range(SSD_HPG // 2):
            j = g * (SSD_HPG // 2) + pp
            ls = slice(j * LANES, (j + 1) * LANES)
            xs = xa_ref[:, ls]
            xbd = jnp.concatenate([jnp.where(lo_half, xs, 0.0), jnp.where(lo_half, 0.0, xs)], axis=0).astype(BF16)
            mm, dec, btw = [], [], []
            for par in range(2):
                h = 2 * j + par
                abc = jnp.broadcast_to(acum[:, h:h + 1], (q, q))
                lmat = jnp.exp2(jnp.where(tril, abc - acumt[h:h + 1, :], -jnp.inf))
                mm.append((cb * lmat * dtt[h:h + 1, :]).astype(BF16))
                dec.append(jnp.exp2(abc))
                btw.append((bt * wrow[h:h + 1, :]).astype(BF16))
            y = _dot(jnp.concatenate(mm, axis=1), xbd)
            y = y + _dot(cmat, st_ref[j].astype(BF16)) * jnp.where(lo_half, dec[0], dec[1])
            y_ref[:, ls] = y + xs * d_ref[:, ls]
            cdec = jnp.where(lo_half, dec[0][q - 1:q, :], dec[1][q - 1:q, :])
            st_ref[j] = st_ref[j] * cdec + _dot(jnp.concatenate(btw, axis=1), xbd)

    z = z_ref[...].astype(F32)
    yz = y_ref[...] * (z * jax.nn.sigmoid(z))
    o_ref[...] = _rms(yz, ng_ref[...]).astype(o_ref.dtype)


def _ssd(u16, u32, dtt, cw, cb, dtb, al, d, ng, bsz, seq):
    nc = seq // SSD_CHUNK
    q = SSD_CHUNK
    full = lambda shape: pl.BlockSpec(shape, lambda b, c: (0,) * len(shape))
    d_full = jnp.repeat(d, SSD_HEAD_DIM).reshape(1, SSD_INNER)
    return pl.pallas_call(
        _ssd_kernel,
        grid=(bsz, nc),
        in_specs=[pl.BlockSpec((q, SSD_CONV_DIM), lambda b, c: (b * nc + c, U16_XBC_BLK)),
                  pl.BlockSpec((q, SSD_INNER), lambda b, c: (b * nc + c, U16_Z_BLK)),
                  pl.BlockSpec((q, 128), lambda b, c: (b * nc + c, U32_DT_BLK128)),
                  pl.BlockSpec((1, SSD_HEADS, q), lambda b, c: (b, 0, c)),
                  full((CONV_WIDTH, SSD_CONV_DIM)), full((1, SSD_CONV_DIM)),
                  full((1, SSD_HEADS)), full((SSD_HEADS, 1)),
                  full((1, SSD_HEADS)), full((SSD_HEADS, 1)),
                  full((1, SSD_INNER)), full((1, SSD_INNER))],
        out_specs=pl.BlockSpec((q, SSD_INNER), lambda b, c: (b * nc + c, 0)),
        out_shape=jax.ShapeDtypeStruct((bsz * seq, SSD_INNER), BF16),
        scratch_shapes=[pltpu.VMEM((2 * q, SSD_CONV_DIM), BF16),
                        pltpu.VMEM((q, SSD_CONV_DIM), F32),
                        pltpu.VMEM((q, SSD_INNER), F32),
                        pltpu.VMEM((SSD_HEADS // 2, SSD_STATE, LANES), F32)],
        compiler_params=_cparams(("parallel", "arbitrary")),
        name="ssd",
    )(u16, u16, u32, dtt, cw, cb, dtb.reshape(1, -1), dtb.reshape(-1, 1), al.reshape(1, -1), al.reshape(-1, 1),
      d_full, ng.reshape(1, -1))


def _merge_kernel(x_ref, ya_ref, yb_ref, yc_ref, mg_ref, wa_ref, wb_ref, wc_ref, wo_ref, o_ref):
    d = D_MODEL
    mg = mg_ref[...].astype(F32)
    mix = jax.nn.sigmoid(mg[:, 0:d]) * _dot(ya_ref[...], wa_ref[...])
    mix = mix + jax.nn.sigmoid(mg[:, d:2 * d]) * _dot(yb_ref[...], wb_ref[...])
    mix = mix + jax.nn.sigmoid(mg[:, 2 * d:3 * d]) * _dot(yc_ref[...], wc_ref[...])
    o_ref[...] = x_ref[...] + _dot(mix.astype(BF16), wo_ref[...])


def _merge(x, ya, yb, yc, u16, wa, wb, wc, wo, tm):
    m, d = x.shape
    full = lambda a: pl.BlockSpec(a.shape, lambda i: (0, 0))
    return pl.pallas_call(
        _merge_kernel,
        grid=(m // tm,),
        in_specs=[pl.BlockSpec((tm, d), lambda i: (i, 0)),
                  pl.BlockSpec((tm, ya.shape[1]), lambda i: (i, 0)),
                  pl.BlockSpec((tm, yb.shape[1]), lambda i: (i, 0)),
                  pl.BlockSpec((tm, yc.shape[1]), lambda i: (i, 0)),
                  pl.BlockSpec((tm, 3 * d), lambda i: (i, U16_MG_BLK)),
                  full(wa), full(wb), full(wc), full(wo)],
        out_specs=pl.BlockSpec((tm, d), lambda i: (i, 0)),
        out_shape=jax.ShapeDtypeStruct((m, d), F32),
        compiler_params=_cparams(("parallel",)),
        name="merge",
    )(x, ya, yb, yc, u16, wa, wb, wc, wo)


def _ffn_kernel(x_ref, g_ref, w1_ref, w2_ref, o_ref, h_ref, acc_ref):
    j = pl.program_id(1)

    @pl.when(j == 0)
    def _():
        h_ref[...] = _rms(x_ref[...], g_ref[...]).astype(BF16)
        acc_ref[...] = x_ref[...]

    a = jnp.maximum(_dot(h_ref[...], w1_ref[...]), 0.0)
    acc_ref[...] += _dot((a * a).astype(BF16), w2_ref[...])

    @pl.when(j == pl.num_programs(1) - 1)
    def _():
        o_ref[...] = acc_ref[...]


def _ffn(x, g, w1, w2, tm, tf):
    m, d = x.shape
    f = w1.shape[1]
    return pl.pallas_call(
        _ffn_kernel,
        grid=(m // tm, f // tf),
        in_specs=[pl.BlockSpec((tm, d), lambda i, j: (i, 0)),
                  pl.BlockSpec((1, d), lambda i, j: (0, 0)),
                  pl.BlockSpec((d, tf), lambda i, j: (0, j)),
                  pl.BlockSpec((tf, d), lambda i, j: (j, 0))],
        out_specs=pl.BlockSpec((tm, d), lambda i, j: (i, 0)),
        out_shape=jax.ShapeDtypeStruct((m, d), F32),
        scratch_shapes=[pltpu.VMEM((tm, d), BF16), pltpu.VMEM((tm, d), F32)],
        compiler_params=_cparams(("parallel", "arbitrary")),
        name="ffn",
    )(x, g, w1, w2)


def _prep_w_in(w):
    offs = [0] + [int(o) for o in np.cumsum(IN_SPLITS)]
    seg = lambda k: w[:, offs[k]:offs[k + 1]]
    dq, dk, dv, iq, ik, iw, nq, nkv, ng, sz, sxbc, sdt, mg = [seg(k) for k in range(13)]
    zeros = lambda n: jnp.zeros((w.shape[0], n), w.dtype)
    w16 = jnp.concatenate([sxbc, mg, sz, dq, nq], axis=1).astype(BF16)
    w32 = jnp.concatenate([nkv[:, 256:768], nkv[:, 0:256], zeros(256),
                           dk, dv, iq, ik, iw, zeros(512 - 424),
                           ng, zeros(128 - ng.shape[1]),
                           sdt, zeros(128 - sdt.shape[1])], axis=1).astype(BF16)
    return w16, w32


def _pick_tile(n, pref):
    t = min(n, pref)
    while n % t:
        t //= 2
    return t


def kernel(x, norm1_g, w_in, dsa_q_norm, dsa_k_norm, nsa_q_norm, nsa_k_norm, nsa_cmp_pos, nsa_cmp_w,
           ssd_conv_w, ssd_conv_b, ssd_dt_bias, ssd_a_log, ssd_d, ssd_norm_g,
           w_br_dsa, w_br_nsa, w_br_ssd, w_out, norm2_g, w_ff1, w_ff2):
    bsz, seq, d = x.shape
    m = bsz * seq
    depth = w_in.shape[0]
    nch = seq // CMP_STRIDE
    n_blk = seq // SEL_BLOCK
    half = CMP_BLOCK // 2

    cmp_start = np.arange(nch) * CMP_STRIDE
    blk_start = np.arange(n_blk) * SEL_BLOCK
    cover = ((cmp_start[:, None] < blk_start[None, :] + SEL_BLOCK)
             & (cmp_start[:, None] + CMP_BLOCK > blk_start[None, :])
             & (np.arange(nch)[:, None] < (seq - CMP_BLOCK) // CMP_STRIDE + 1)).astype(np.float32)
    expand = (np.arange(seq)[None, :] // SEL_BLOCK == np.arange(n_blk)[:, None]).astype(np.float32)
    gexp = np.zeros((3, LANES, NSA_HEADS * HEAD_DIM), np.float32)
    for hh in range(NSA_HEADS):
        for br in range(3):
            gexp[br, 3 * hh + br, hh * HEAD_DIM:(hh + 1) * HEAD_DIM] = 1.0
    gexp = jnp.asarray(gexp, dtype=BF16)
    cover = jnp.asarray(cover)
    expand = jnp.asarray(expand, dtype=BF16)

    tm = _pick_tile(m, 1024)
    xf = x.reshape(m, d)
    for l in range(depth):
        w16, w32 = _prep_w_in(w_in[l])
        g1 = norm1_g[l].reshape(1, d)
        u16 = _norm_matmul(xf, g1, w16, BF16, tm, 2048)
        u32 = _norm_matmul(xf, g1, w32, F32, tm, U32_WIDTH)

        iw0 = U32_DS_BLK512 * 512 + DS_IW
        iwt = u32[:, iw0:iw0 + IDX_HEADS].reshape(bsz, seq, IDX_HEADS).transpose(0, 2, 1)
        ya = _dsa(u16, u32, iwt, dsa_q_norm[l].reshape(1, -1), dsa_k_norm[l].reshape(1, -1), bsz, seq)

        wl = nsa_cmp_w[l]
        zblk = jnp.zeros_like(wl)
        cw = jnp.concatenate([jnp.concatenate([wl, zblk], axis=3),
                              jnp.concatenate([zblk, wl], axis=3)], axis=2).astype(BF16)
        cpos = jnp.concatenate([nsa_cmp_pos[l]] * NSA_KV_HEADS, axis=-1)
        kc, vc = _nsa_compress(u32, cpos, cw, nsa_k_norm[l][0:1], bsz, seq)
        yb = _nsa(u16, u32, kc, vc, nsa_q_norm[l].reshape(1, -1), nsa_k_norm[l], cover, expand, gexp, bsz, seq)

        dt0 = U32_DT_BLK128 * 128
        dtt = u32[:, dt0:dt0 + SSD_HEADS].reshape(bsz, seq, SSD_HEADS).transpose(0, 2, 1)
        yc = _ssd(u16, u32, dtt, ssd_conv_w[l], ssd_conv_b[l].reshape(1, -1), ssd_dt_bias[l], ssd_a_log[l],
                  ssd_d[l], ssd_norm_g[l], bsz, seq)

        xf = _merge(xf, ya, yb, yc, u16,
                    w_br_dsa[l].astype(BF16), w_br_nsa[l].astype(BF16), w_br_ssd[l].astype(BF16),
                    w_out[l].astype(BF16), _pick_tile(m, 256))
        xf = _ffn(xf, norm2_g[l].reshape(1, d), w_ff1[l].astype(BF16), w_ff2[l].astype(BF16),
                  _pick_tile(m, 1024), 1024)
    return xf.reshape(bsz, seq, d)
```

```python
import functools
import math

import numpy as np
import jax
import jax.numpy as jnp
from jax import lax
from jax.experimental import pallas as pl
from jax.experimental.pallas import tpu as pltpu

F32 = jnp.float32
BF16 = jnp.bfloat16

D_MODEL = 1024
HEAD_DIM = 64
DSA_HEADS = 16
DSA_KV_DIM = 64
IDX_HEADS = 8
IDX_DIM = 32
DSA_TOPK = 256
NSA_HEADS = 16
NSA_KV_HEADS = 2
NSA_GROUP = NSA_HEADS // NSA_KV_HEADS
CMP_BLOCK = 32
CMP_STRIDE = 16
SEL_BLOCK = 64
SEL_TOPN = 4
WINDOW = 512
SSD_INNER = 2 * D_MODEL
SSD_HEAD_DIM = 64
SSD_HEADS = SSD_INNER // SSD_HEAD_DIM
SSD_GROUPS = 4
SSD_HPG = SSD_HEADS // SSD_GROUPS
SSD_STATE = 128
SSD_CONV_DIM = SSD_INNER + 2 * SSD_GROUPS * SSD_STATE
CONV_WIDTH = 4
SSD_CHUNK = 128
D_FF = 4 * D_MODEL
Q_BLOCK = 128
EPS = 1e-6
NEG = -1e30
IN_SPLITS = (DSA_HEADS * HEAD_DIM, DSA_KV_DIM, DSA_KV_DIM, IDX_HEADS * IDX_DIM, IDX_DIM, IDX_HEADS,
             NSA_HEADS * HEAD_DIM, 6 * NSA_KV_HEADS * HEAD_DIM, 3 * NSA_HEADS,
             SSD_INNER, SSD_CONV_DIM, SSD_HEADS, 3 * D_MODEL)

V7X_VMEM_LIMIT_BYTES = 56 * 1024 * 1024
LANES = 128

U16_WIDTH = 10240
U16_XBC_BLK = 0
U16_MG_BLK = 1
U16_Z_BLK = 3
U16_DQ_BLK = 8
U16_NQ_BLK = 9
U32_WIDTH = 1792
U32_NKV_BLK512 = 0
U32_KC_BLK128 = 4
U32_DS_BLK512 = 2
U32_NG_BLK128 = 12
U32_DT_BLK128 = 13
DS_DK, DS_DV, DS_IQ, DS_IK, DS_IW = 0, 64, 128, 384, 416

KEY_TILE = 256
BISECT_ITERS = 24
PAIR_STEP = 2
LOG2E = 1.4426950408889634
BISECT_UNROLL = 4


def _cparams(sem):
    return pltpu.CompilerParams(dimension_semantics=sem, vmem_limit_bytes=V7X_VMEM_LIMIT_BYTES)


def _rms(x, g):
    return x * lax.rsqrt(jnp.mean(x * x, axis=-1, keepdims=True) + EPS) * g


def _dot_nt(a, b):
    return lax.dot_general(a, b, (((1,), (1,)), ((), ())), preferred_element_type=F32)


def _dot(a, b):
    return jnp.dot(a, b, preferred_element_type=F32)


def _dot_hi(a, b):
    return jnp.dot(a, b, preferred_element_type=F32, precision=lax.Precision.HIGHEST)


def _norm_matmul_kernel(x_ref, g_ref, w_ref, o_ref, h_ref):
    @pl.when(pl.program_id(1) == 0)
    def _():
        h_ref[...] = _rms(x_ref[...], g_ref[...]).astype(BF16)

    o_ref[...] = _dot(h_ref[...], w_ref[...]).astype(o_ref.dtype)


def _norm_matmul(x, g, w, out_dtype, tm, tn):
    m, k = x.shape
    n = w.shape[1]
    return pl.pallas_call(
        _norm_matmul_kernel,
        grid=(m // tm, n // tn),
        in_specs=[pl.BlockSpec((tm, k), lambda i, j: (i, 0)),
                  pl.BlockSpec((1, k), lambda i, j: (0, 0)),
                  pl.BlockSpec((k, tn), lambda i, j: (0, j))],
        out_specs=pl.BlockSpec((tm, tn), lambda i, j: (i, j)),
        out_shape=jax.ShapeDtypeStruct((m, n), out_dtype),
        scratch_shapes=[pltpu.VMEM((tm, k), BF16)],
        compiler_params=_cparams(("parallel", "arbitrary")),
        name="norm_matmul",
    )(x, g, w)


def _flash_pairs(qp_ref, kbd_ref, wext_ref, bias_ref, nkt, n_pairs, pairs_per_kv, bufs, m_ref, acc_ref):
    (s0, s1), (p0, p1), (a0, a1) = bufs
    n_pg = n_pairs // PAIR_STEP
    rows = PAIR_STEP * Q_BLOCK
    n = nkt * n_pg
    lo_half = lax.broadcasted_iota(jnp.int32, (1, LANES), 1) < HEAD_DIM
    m_ref[...] = jnp.full(m_ref.shape, -jnp.inf, F32)
    acc_ref[...] = jnp.zeros(acc_ref.shape, F32)
    p1[...] = jnp.zeros(p1.shape, BF16)
    a1[...] = jnp.ones(a1.shape, F32)

    def where(j):
        kt = j // n_pg
        pg = j % n_pg
        g = (pg * PAIR_STEP) // pairs_per_kv
        return kt, pg, g, pl.multiple_of(kt * 2 * KEY_TILE, 2 * KEY_TILE), pl.multiple_of(pg * rows, rows)

    def qk(j, s_ref):
        kt, _, g, koff, r = where(j)
        s = _dot_nt(qp_ref[pl.ds(r, rows), :], kbd_ref[g, pl.ds(koff, 2 * KEY_TILE), :])
        s_ref[...] = s + jnp.tile(bias_ref[g, kt], (PAIR_STEP, 2))

    def softmax(j, s_ref, p_ref, a_ref):
        _, pg, _, _, _ = where(j)
        for u in range(PAIR_STEP):
            us = slice(u * Q_BLOCK, (u + 1) * Q_BLOCK)
            alpha = []
            for par in range(2):
                cs = slice(par * KEY_TILE, (par + 1) * KEY_TILE)
                hrow = pl.multiple_of(((pg * PAIR_STEP + u) * 2 + par) * Q_BLOCK, Q_BLOCK)
                m_old = m_ref[pl.ds(hrow, Q_BLOCK), :]
                m_new = jnp.maximum(m_old, jnp.max(s_ref[us, cs], axis=-1, keepdims=True))
                alpha.append(jnp.exp2(m_old - m_new))
                m_ref[pl.ds(hrow, Q_BLOCK), :] = m_new
                p_ref[us, cs] = jnp.exp2(
                    s_ref[us, cs] - jnp.concatenate([m_new] * (KEY_TILE // LANES), axis=1)).astype(BF16)
            a_ref[us, :] = jnp.where(lo_half, alpha[0], alpha[1])

    def pv(j, p_ref, a_ref):
        _, _, g, koff, r = where(j)
        a = a_ref[...]
        acc_ref[pl.ds(r, rows), :] = (jnp.concatenate([a, a], axis=1) * acc_ref[pl.ds(r, rows), :]
                                      + _dot(p_ref[...], wext_ref[g, pl.ds(koff, 2 * KEY_TILE), :]))

    qk(0, s0)

    def body(jj, c):
        j = 2 * jj
        qk(j + 1, s1)
        softmax(j, s0, p0, a0)
        pv(jnp.maximum(j - 1, 0), p1, a1)
        qk(jnp.minimum(j + 2, n - 1), s0)
        softmax(j + 1, s1, p1, a1)
        pv(j, p0, a0)
        return c

    lax.fori_loop(0, n // 2, body, 0)
    pv(n - 1, p1, a1)


def _flash_scratch(n_pairs):
    step = PAIR_STEP * Q_BLOCK
    return ([pltpu.VMEM((step, 2 * KEY_TILE), F32)] * 2 + [pltpu.VMEM((step, 2 * KEY_TILE), BF16)] * 2
            + [pltpu.VMEM((step, LANES), F32)] * 2
            + [pltpu.VMEM((2 * n_pairs * Q_BLOCK, LANES), F32), pltpu.VMEM((n_pairs * Q_BLOCK, 2 * LANES), F32)])


def _norm_pairs(q_ref, g_ref, qp_ref, n_pairs):
    mult = HEAD_DIM ** -0.5 * LOG2E
    lo_half = lax.broadcasted_iota(jnp.int32, (1, LANES), 1) < HEAD_DIM
    g2 = jnp.concatenate([g_ref[...], g_ref[...]], axis=-1) * mult
    for j in range(n_pairs):
        x = q_ref[:, j * LANES:(j + 1) * LANES].astype(F32)
        x2 = x * x
        s_lo = jnp.sum(jnp.where(lo_half, x2, 0.0), axis=-1, keepdims=True)
        s_hi = jnp.sum(jnp.where(lo_half, 0.0, x2), axis=-1, keepdims=True)
        r = jnp.where(lo_half, lax.rsqrt(s_lo * (1.0 / HEAD_DIM) + EPS), lax.rsqrt(s_hi * (1.0 / HEAD_DIM) + EPS))
        qp_ref[j * Q_BLOCK:(j + 1) * Q_BLOCK, :] = (x * r * g2).astype(BF16)


def _store_pair_kv(kbd_ref, wext_ref, g, row0, n, k, v):
    z = jnp.zeros((n, HEAD_DIM), F32)
    one = jnp.ones((n, HEAD_DIM), F32)
    kbd_ref[g, pl.ds(row0, n), :] = jnp.concatenate([k, z], axis=-1).astype(BF16)
    kbd_ref[g, pl.ds(row0 + n, n), :] = jnp.concatenate([z, k], axis=-1).astype(BF16)
    wext_ref[g, pl.ds(row0, n), :] = jnp.concatenate([v, z, one, z], axis=-1).astype(BF16)
    wext_ref[g, pl.ds(row0 + n, n), :] = jnp.concatenate([z, v, z, one], axis=-1).astype(BF16)


def _dsa_kernel(q_ref, sq_ref, iwt_ref, sk_ref, gq_ref, gk_ref, o_ref,
                kbd_ref, wext_ref, ik_ref, idx_ref, bias_ref, iqs_ref,
                qp_ref, s0_ref, s1_ref, p0_ref, p1_ref, a0_ref, a1_ref, m_ref, acc_ref, *, seq, n_sel):
    i = pl.program_id(1)
    nq = Q_BLOCK
    n_pairs = DSA_HEADS // 2
    kt_shape = (KEY_TILE, nq)

    @pl.when(i == 0)
    def _():
        def rows(r, c):
            off = pl.multiple_of(r * KEY_TILE, KEY_TILE)
            sk = sk_ref[pl.ds(off, KEY_TILE), :]
            _store_pair_kv(kbd_ref, wext_ref, 0, pl.multiple_of(2 * off, 2 * KEY_TILE), KEY_TILE,
                           _rms(sk[:, DS_DK:DS_DK + HEAD_DIM], gk_ref[...]), sk[:, DS_DV:DS_DV + HEAD_DIM])
            ik_ref[pl.ds(off, KEY_TILE), :] = sk[:, DS_IK:DS_IK + IDX_DIM].astype(BF16)
            return c

        lax.fori_loop(0, seq // KEY_TILE, rows, 0)

    start = i * nq
    nkt = (start + nq + KEY_TILE - 1) // KEY_TILE
    t_row = (start + lax.broadcasted_iota(jnp.int32, (1, nq), 1)).astype(F32)
    sub_pos = lax.broadcasted_iota(jnp.int32, (KEY_TILE, 1), 0).astype(F32)

    sq = sq_ref[...]
    for h in range(IDX_HEADS):
        iqs_ref[h * nq:(h + 1) * nq, :] = sq[:, DS_IQ + h * IDX_DIM:DS_IQ + (h + 1) * IDX_DIM].astype(BF16)
    iwt = iwt_ref[0]

    def idx_tile(kt, carry):
        rmin, rmax = carry
        off = pl.multiple_of(kt * KEY_TILE, KEY_TILE)
        r = _dot_nt(ik_ref[pl.ds(off, KEY_TILE), :], iqs_ref[...])
        acc = jnp.zeros(kt_shape, F32)
        for h in range(IDX_HEADS):
            acc = acc + iwt[h:h + 1, :] * jnp.maximum(r[:, h * nq:(h + 1) * nq], 0.0)
        valid = (sub_pos + (kt * KEY_TILE).astype(F32)) <= t_row
        idx_ref[kt] = jnp.where(valid, acc, -jnp.inf)
        rmin = jnp.minimum(rmin, jnp.min(jnp.where(valid, acc, jnp.inf), axis=0, keepdims=True))
        rmax = jnp.maximum(rmax, jnp.max(jnp.where(valid, acc, -jnp.inf), axis=0, keepdims=True))
        return rmin, rmax

    rmin, rmax = lax.fori_loop(0, nkt, idx_tile,
                               (jnp.full((1, nq), jnp.inf, F32), jnp.full((1, nq), -jnp.inf, F32)))

    kf = float(n_sel)

    def col_sum(x):
        parts = [x[r * 8:(r + 1) * 8] for r in range(KEY_TILE // 8)]
        while len(parts) > 1:
            parts = [parts[k] + parts[k + 1] for k in range(0, len(parts), 2)]
        return jnp.sum(parts[0], axis=0, keepdims=True)

    def count_gt(thr):
        def body(kt, acc):
            return acc + jnp.where(idx_ref[kt] > thr, 1.0, 0.0)

        return col_sum(lax.fori_loop(0, nkt, body, jnp.zeros(kt_shape, F32)))

    n_valid = t_row + 1.0
    lo0 = rmin - (jnp.abs(rmin) * 0.01 + 1.0)

    def bis_cond(c):
        it, _, _, clo, _ = c
        return jnp.logical_and(it < BISECT_ITERS, jnp.max(clo) > kf)

    def bis_body(c):
        it, lo, hi, clo, chi = c
        for _ in range(BISECT_UNROLL):
            mid = 0.5 * (lo + hi)
            cnt = count_gt(mid)
            ge = cnt >= kf
            lo, hi = jnp.where(ge, mid, lo), jnp.where(ge, hi, mid)
            clo, chi = jnp.where(ge, cnt, clo), jnp.where(ge, chi, cnt)
        return it + BISECT_UNROLL, lo, hi, clo, chi

    _, lo, hi, clo, chi = lax.while_loop(
        bis_cond, bis_body, (jnp.int32(0), lo0, rmax, n_valid, jnp.zeros((1, nq), F32)))

    eye_f = jnp.where(lax.broadcasted_iota(jnp.int32, (nq, nq), 0) == lax.broadcasted_iota(jnp.int32, (nq, nq), 1),
                      1.0, 0.0)
    eye = eye_f.astype(BF16)
    tri = jnp.where(lax.broadcasted_iota(jnp.int32, (KEY_TILE, KEY_TILE), 0)
                    <= lax.broadcasted_iota(jnp.int32, (KEY_TILE, KEY_TILE), 1), 1.0, 0.0).astype(BF16)
    quota_row = jnp.where(clo > kf, kf - chi, float(2 * seq))
    quota = jnp.sum(eye_f * quota_row, axis=1, keepdims=True)

    def mask_tile(kt, carry):
        v = idx_ref[kt]
        code = jnp.where(v > hi, 2.0, jnp.where(v > lo, 1.0, 0.0)).astype(BF16)
        code_t = _dot_nt(eye, code)
        tie_t = jnp.where((code_t > 0.5) & (code_t < 1.5), 1.0, 0.0)
        rank = _dot(tie_t.astype(BF16), tri) + carry
        keep = (code_t > 1.5) | ((tie_t > 0.5) & (rank <= quota))
        bias_ref[0, kt] = jnp.where(keep, 0.0, NEG)
        return carry + jnp.sum(tie_t, axis=1, keepdims=True)

    lax.fori_loop(0, nkt, mask_tile, jnp.zeros((nq, 1), F32))

    _norm_pairs(q_ref, gq_ref, qp_ref, n_pairs)
    _flash_pairs(qp_ref, kbd_ref, wext_ref, bias_ref, nkt, n_pairs, n_pairs,
                 ((s0_ref, s1_ref), (p0_ref, p1_ref), (a0_ref, a1_ref)), m_ref, acc_ref)
    for j in range(n_pairs):
        acc = acc_ref[j * nq:(j + 1) * nq, :]
        o_ref[:, j * LANES:(j + 1) * LANES] = (acc[:, :LANES] / acc[:, LANES:]).astype(o_ref.dtype)


def _dsa(u16, u32, iwt, gq, gk, bsz, seq):
    n_sel = min(DSA_TOPK, seq // 4)
    nqb = seq // Q_BLOCK
    n_pairs = DSA_HEADS // 2
    kern = functools.partial(_dsa_kernel, seq=seq, n_sel=n_sel)
    return pl.pallas_call(
        kern,
        grid=(bsz, nqb),
        in_specs=[pl.BlockSpec((Q_BLOCK, 1024), lambda b, i: (b * nqb + i, U16_DQ_BLK)),
                  pl.BlockSpec((Q_BLOCK, 512), lambda b, i: (b * nqb + i, U32_DS_BLK512)),
                  pl.BlockSpec((1, IDX_HEADS, Q_BLOCK), lambda b, i: (b, 0, i)),
                  pl.BlockSpec((seq, 512), lambda b, i: (b, U32_DS_BLK512)),
                  pl.BlockSpec((1, HEAD_DIM), lambda b, i: (0, 0)),
                  pl.BlockSpec((1, DSA_KV_DIM), lambda b, i: (0, 0))],
        out_specs=pl.BlockSpec((Q_BLOCK, DSA_HEADS * HEAD_DIM), lambda b, i: (b * nqb + i, 0)),
        out_shape=jax.ShapeDtypeStruct((bsz * seq, DSA_HEADS * HEAD_DIM), BF16),
        scratch_shapes=[pltpu.VMEM((1, 2 * seq, LANES), BF16),
                        pltpu.VMEM((1, 2 * seq, 2 * LANES), BF16),
                        pltpu.VMEM((seq, IDX_DIM), BF16),
                        pltpu.VMEM((seq // KEY_TILE, KEY_TILE, Q_BLOCK), F32),
                        pltpu.VMEM((1, seq // KEY_TILE, Q_BLOCK, KEY_TILE), F32),
                        pltpu.VMEM((IDX_HEADS * Q_BLOCK, IDX_DIM), BF16),
                        pltpu.VMEM((n_pairs * Q_BLOCK, LANES), BF16)]
                       + _flash_scratch(n_pairs),
        compiler_params=_cparams(("parallel", "arbitrary")),
        name="dsa",
    )(u16, u32, iwt, u32, gq, gk)


def _nsa_compress_kernel(xk_ref, xv_ref, pos_ref, w_ref, g_ref, kc_ref, vc_ref):
    nch = xk_ref.shape[0] // CMP_STRIDE
    for jj, (x_ref, o_ref) in enumerate(((xk_ref, kc_ref), (xv_ref, vc_ref))):
        acc_a = jnp.zeros((nch, LANES), F32)
        acc_b = jnp.zeros((nch, LANES), F32)
        for l in range(CMP_STRIDE):
            x = x_ref[pl.ds(l, nch, stride=CMP_STRIDE), :]
            acc_a = acc_a + _dot((x + pos_ref[jj, l:l + 1, :]).astype(BF16), w_ref[jj, l])
            hi = CMP_STRIDE + l
            acc_b = acc_b + _dot((x + pos_ref[jj, hi:hi + 1, :]).astype(BF16), w_ref[jj, hi])
        out = acc_a + pltpu.roll(acc_b, nch - 1, axis=0)
        for g in range(NSA_KV_HEADS):
            og = out[:, g * HEAD_DIM:(g + 1) * HEAD_DIM]
            o_ref[0, g] = (_rms(og, g_ref[...]) if jj == 0 else og).astype(BF16)


def _nsa_compress(u32, pos, w, g, bsz, seq):
    nch = seq // CMP_STRIDE
    return pl.pallas_call(
        _nsa_compress_kernel,
        grid=(bsz,),
        in_specs=[pl.BlockSpec((seq, LANES), lambda b: (b, U32_KC_BLK128)),
                  pl.BlockSpec((seq, LANES), lambda b: (b, U32_KC_BLK128 + 1)),
                  pl.BlockSpec((2, CMP_BLOCK, LANES), lambda b: (0, 0, 0)),
                  pl.BlockSpec((2, CMP_BLOCK, LANES, LANES), lambda b: (0, 0, 0, 0)),
                  pl.BlockSpec((1, HEAD_DIM), lambda b: (0, 0))],
        out_specs=[pl.BlockSpec((1, NSA_KV_HEADS, nch, HEAD_DIM), lambda b: (b, 0, 0, 0)),
                   pl.BlockSpec((1, NSA_KV_HEADS, nch, HEAD_DIM), lambda b: (b, 0, 0, 0))],
        out_shape=[jax.ShapeDtypeStruct((bsz, NSA_KV_HEADS, nch, HEAD_DIM), BF16),
                   jax.ShapeDtypeStruct((bsz, NSA_KV_HEADS, nch, HEAD_DIM), BF16)],
        compiler_params=_cparams(("parallel",)),
        name="nsa_compress",
    )(u32, u32, pos, w, g)


def _nsa_kernel(q_ref, ng_ref, kv_ref, kc_ref, vc_ref, gq_ref, gk_ref, cover_ref, expand_ref, gexp_ref, o_ref,
                ksbd_ref, wsext_ref, kwbd_ref, wwext_ref, kcbd_ref, vcext_ref, bias_ref, wbias_ref, psum_ref,
                ocmp_ref, accw_ref, qp_ref, sw_ref, pw_ref, sc_ref, pc_ref,
                s0_ref, s1_ref, p0_ref, p1_ref, a0_ref, a1_ref, m_ref, acc_ref, *, seq):
    i = pl.program_id(1)
    nq = Q_BLOCK
    n_pairs = NSA_HEADS // 2
    gpairs = NSA_GROUP // 2
    grows = gpairs * nq
    n_cmp_pad = kc_ref.shape[2]
    n_blk = seq // SEL_BLOCK
    n_sel = min(SEL_TOPN, n_blk)
    win_keys = min(WINDOW + Q_BLOCK, seq)
    win_tiles = win_keys // Q_BLOCK
    wt = Q_BLOCK

    @pl.when(i == 0)
    def _():
        def rows(r, c):
            off = pl.multiple_of(r * KEY_TILE, KEY_TILE)
            kv = kv_ref[pl.ds(off, KEY_TILE), :]
            for g in range(NSA_KV_HEADS):
                c0 = g * HEAD_DIM
                _store_pair_kv(ksbd_ref, wsext_ref, g, pl.multiple_of(2 * off, 2 * KEY_TILE), KEY_TILE,
                               _rms(kv[:, c0:c0 + 64], gk_ref[1:2, :]), kv[:, 128 + c0:128 + c0 + 64])
                kw = _rms(kv[:, 256 + c0:256 + c0 + 64], gk_ref[2:3, :])
                vw = kv[:, 384 + c0:384 + c0 + 64]
                for hf in range(KEY_TILE // wt):
                    _store_pair_kv(kwbd_ref, wwext_ref, g, pl.multiple_of(2 * off + hf * 2 * wt, 2 * wt), wt,
                                   kw[hf * wt:(hf + 1) * wt], vw[hf * wt:(hf + 1) * wt])
            return c

        lax.fori_loop(0, seq // KEY_TILE, rows, 0)
        zc = jnp.zeros((n_cmp_pad, HEAD_DIM), BF16)
        for g in range(NSA_KV_HEADS):
            kcbd_ref[g, 0:n_cmp_pad, :] = jnp.concatenate([kc_ref[0, g], zc], axis=-1)
            kcbd_ref[g, n_cmp_pad:2 * n_cmp_pad, :] = jnp.concatenate([zc, kc_ref[0, g]], axis=-1)
            vcext_ref[g, 0:n_cmp_pad, :] = jnp.concatenate([vc_ref[0, g], zc], axis=-1)
            vcext_ref[g, n_cmp_pad:2 * n_cmp_pad, :] = jnp.concatenate([zc, vc_ref[0, g]], axis=-1)

    start = i * nq
    nkt = (start + nq + KEY_TILE - 1) // KEY_TILE
    t_i = start + lax.broadcasted_iota(jnp.int32, (nq, 1), 0)
    t_col = t_i.astype(F32)
    lane_pos = lax.broadcasted_iota(jnp.int32, (1, KEY_TILE), 1).astype(F32)

    _norm_pairs(q_ref, gq_ref, qp_ref, n_pairs)

    cmp_end = (lax.broadcasted_iota(jnp.int32, (1, n_cmp_pad), 1) * CMP_STRIDE + (CMP_BLOCK - 1)).astype(F32)
    vis = cmp_end <= t_col
    wbase = pl.multiple_of(jnp.maximum(start + nq - win_keys, 0), Q_BLOCK)
    wb2 = pl.multiple_of(2 * wbase, 2 * Q_BLOCK)
    col = lax.broadcasted_iota(jnp.int32, (1, 2 * win_keys), 1)
    wpos = (wbase + jnp.right_shift(col, int(math.log2(2 * wt))) * wt + jnp.bitwise_and(col, wt - 1)).astype(F32)
    wbias_ref[...] = jnp.where((wpos <= t_col) & (wpos > t_col - float(WINDOW)), 0.0, NEG)
    for g in range(NSA_KV_HEADS):
        gs = slice(g * grows, (g + 1) * grows)
        sc_ref[gs, :] = _dot_nt(qp_ref[gs, :], kcbd_ref[g])
        sw_ref[gs, :] = _dot_nt(qp_ref[gs, :], kwbd_ref[g, pl.ds(wb2, 2 * win_keys), :])

    for g in range(NSA_KV_HEADS):
        gs = slice(g * grows, (g + 1) * grows)
        tot = jnp.zeros((nq, n_cmp_pad), F32)
        for j in range(g * gpairs, (g + 1) * gpairs):
            rs = slice(j * nq, (j + 1) * nq)
            for par in range(2):
                cs = slice(par * n_cmp_pad, (par + 1) * n_cmp_pad)
                s = jnp.where(vis, sc_ref[rs, cs], NEG)
                e = jnp.exp2(s - jnp.max(s, axis=-1, keepdims=True))
                p = jnp.where(vis, e / jnp.sum(e, axis=-1, keepdims=True), 0.0)
                pc_ref[rs, cs] = p.astype(BF16)
                tot = tot + p
        psum_ref[g] = tot
        ocmp_ref[gs, :] = _dot(pc_ref[gs, :], vcext_ref[g])

    blk_j = lax.broadcasted_iota(jnp.int32, (1, n_blk), 1)
    cur1 = jnp.right_shift(t_i, int(math.log2(SEL_BLOCK)))
    cur = jnp.concatenate([cur1] * NSA_KV_HEADS, axis=0)
    imp = jnp.concatenate([_dot_hi(psum_ref[g], cover_ref[...]) for g in range(NSA_KV_HEADS)], axis=0)
    forced = (blk_j == cur) | (blk_j == 0)
    imp = jnp.where(forced, jnp.inf, jnp.where(blk_j > cur, -jnp.inf, imp))
    selb = jnp.zeros(imp.shape, jnp.bool_)
    for _ in range(n_sel):
        mx = jnp.max(imp, axis=-1, keepdims=True)
        first = jnp.min(jnp.where(imp == mx, blk_j, n_blk), axis=-1, keepdims=True)
        pick = blk_j == first
        selb = selb | pick
        imp = jnp.where(pick, -jnp.inf, imp)
    selb_bf = jnp.where(selb, 1.0, 0.0).astype(BF16)

    for g in range(NSA_KV_HEADS):
        for j in range(g * gpairs, (g + 1) * gpairs):
            rs = slice(j * nq, (j + 1) * nq)
            for par in range(2):
                cols = [slice(t * 2 * wt + par * wt, t * 2 * wt + (par + 1) * wt) for t in range(win_tiles)]
                s = [sw_ref[rs, cs] + wbias_ref[:, cs] for cs in cols]
                mx = s[0]
                for st in s[1:]:
                    mx = jnp.maximum(mx, st)
                m = jnp.max(mx, axis=-1, keepdims=True)
                for cs, st in zip(cols, s):
                    pw_ref[rs, cs] = jnp.exp2(st - m).astype(BF16)
        accw_ref[g * grows:(g + 1) * grows, :] = _dot(
            pw_ref[g * grows:(g + 1) * grows, :], wwext_ref[g, pl.ds(wb2, 2 * win_keys), :])

    def mask_tile(kt, c):
        off = pl.multiple_of(kt * KEY_TILE, KEY_TILE)
        hit = _dot(selb_bf, expand_ref[:, pl.ds(off, KEY_TILE)])
        ok = (lane_pos + (kt * KEY_TILE).astype(F32)) <= t_col
        for g in range(NSA_KV_HEADS):
            bias_ref[g, kt] = jnp.where((hit[g * nq:(g + 1) * nq] > 0.5) & ok, 0.0, NEG)
        return c

    lax.fori_loop(0, nkt, mask_tile, 0)
    _flash_pairs(qp_ref, ksbd_ref, wsext_ref, bias_ref, nkt, n_pairs, gpairs,
                 ((s0_ref, s1_ref), (p0_ref, p1_ref), (a0_ref, a1_ref)), m_ref, acc_ref)

    gates = jax.nn.sigmoid(ng_ref[...])
    g_hi = gates.astype(BF16)
    g_lo = (gates - g_hi.astype(F32)).astype(BF16)
    gb = [_dot(g_hi, gexp_ref[b]) + _dot(g_lo, gexp_ref[b]) for b in range(3)]
    for j in range(n_pairs):
        rs = slice(j * nq, (j + 1) * nq)
        ls = slice(j * LANES, (j + 1) * LANES)
        acc = acc_ref[rs, :]
        accw = accw_ref[rs, :]
        out = (gb[0][:, ls] * ocmp_ref[rs, :] + gb[1][:, ls] * (acc[:, :LANES] / acc[:, LANES:])
               + gb[2][:, ls] * (accw[:, :LANES] / accw[:, LANES:]))
        o_ref[:, ls] = out.astype(o_ref.dtype)


def _nsa(u16, u32, kc, vc, gq, gk, cover, expand, gexp, bsz, seq):
    nqb = seq // Q_BLOCK
    nch = kc.shape[2]
    n_blk = seq // SEL_BLOCK
    n_pairs = NSA_HEADS // 2
    prow = n_pairs * Q_BLOCK
    win_keys = min(WINDOW + Q_BLOCK, seq)
    kern = functools.partial(_nsa_kernel, seq=seq)
    return pl.pallas_call(
        kern,
        grid=(bsz, nqb),
        in_specs=[pl.BlockSpec((Q_BLOCK, 1024), lambda b, i: (b * nqb + i, U16_NQ_BLK)),
                  pl.BlockSpec((Q_BLOCK, 128), lambda b, i: (b * nqb + i, U32_NG_BLK128)),
                  pl.BlockSpec((seq, 512), lambda b, i: (b, U32_NKV_BLK512)),
                  pl.BlockSpec((1, NSA_KV_HEADS, nch, HEAD_DIM), lambda b, i: (b, 0, 0, 0)),
                  pl.BlockSpec((1, NSA_KV_HEADS, nch, HEAD_DIM), lambda b, i: (b, 0, 0, 0)),
                  pl.BlockSpec((1, HEAD_DIM), lambda b, i: (0, 0)),
                  pl.BlockSpec((3, HEAD_DIM), lambda b, i: (0, 0)),
                  pl.BlockSpec((nch, n_blk), lambda b, i: (0, 0)),
                  pl.BlockSpec((n_blk, seq), lambda b, i: (0, 0)),
                  pl.BlockSpec((3, LANES, NSA_HEADS * HEAD_DIM), lambda b, i: (0, 0, 0))],
        out_specs=pl.BlockSpec((Q_BLOCK, NSA_HEADS * HEAD_DIM), lambda b, i: (b * nqb + i, 0)),
        out_shape=jax.ShapeDtypeStruct((bsz * seq, NSA_HEADS * HEAD_DIM), BF16),
        scratch_shapes=[pltpu.VMEM((NSA_KV_HEADS, 2 * seq, LANES), BF16),
                        pltpu.VMEM((NSA_KV_HEADS, 2 * seq, 2 * LANES), BF16),
                        pltpu.VMEM((NSA_KV_HEADS, 2 * seq, LANES), BF16),
                        pltpu.VMEM((NSA_KV_HEADS, 2 * seq, 2 * LANES), BF16),
                        pltpu.VMEM((NSA_KV_HEADS, 2 * nch, LANES), BF16),
                        pltpu.VMEM((NSA_KV_HEADS, 2 * nch, LANES), BF16),
                        pltpu.VMEM((NSA_KV_HEADS, seq // KEY_TILE, Q_BLOCK, KEY_TILE), F32),
                        pltpu.VMEM((Q_BLOCK, 2 * win_keys), F32),
                        pltpu.VMEM((NSA_KV_HEADS, Q_BLOCK, nch), F32),
                        pltpu.VMEM((prow, LANES), F32),
                        pltpu.VMEM((prow, 2 * LANES), F32),
                        pltpu.VMEM((prow, LANES), BF16),
                        pltpu.VMEM((prow, 2 * win_keys), F32),
                        pltpu.VMEM((prow, 2 * win_keys), BF16),
                        pltpu.VMEM((prow, 2 * nch), F32),
                        pltpu.VMEM((prow, 2 * nch), BF16)]
                       + _flash_scratch(n_pairs),
        compiler_params=_cparams(("parallel", "arbitrary")),
        name="nsa",
    )(u16, u32, u32, kc, vc, gq, gk, cover, expand, gexp)


def _softplus(x):
    return jnp.maximum(x, 0.0) + jnp.log1p(jnp.exp(-jnp.abs(x)))


def _ssd_kernel(xbc_ref, z_ref, dt_ref, dtt_ref, cw_ref, cb_ref, dtb_ref, dtbt_ref, al_ref, alt_ref,
                d_ref, ng_ref, o_ref, xcat_ref, xa_ref, y_ref, st_ref):
    c = pl.program_id(1)
    q = SSD_CHUNK
    gn = SSD_GROUPS * SSD_STATE
    tail = 16
    lo_half = lax.broadcasted_iota(jnp.int32, (1, LANES), 1) < SSD_HEAD_DIM

    @pl.when(c == 0)
    def _():
        st_ref[...] = jnp.zeros_like(st_ref)
        xcat_ref[0:q, :] = jnp.zeros((q, SSD_CONV_DIM), BF16)

    @pl.when(c > 0)
    def _():
        xcat_ref[q - tail:q, :] = xcat_ref[2 * q - tail:2 * q, :]

    xcat_ref[q:2 * q, :] = xbc_ref[...]
    ri3 = lax.broadcasted_iota(jnp.int32, ((CONV_WIDTH - 1) * q, 2 * q), 0)
    ci3 = lax.broadcasted_iota(jnp.int32, ((CONV_WIDTH - 1) * q, 2 * q), 1)
    qbits = int(math.log2(q))
    src = q + jnp.bitwise_and(ri3, q - 1) - (jnp.right_shift(ri3, qbits) + 1)
    shifts = jnp.where(ci3 == src, 1.0, 0.0).astype(BF16)
    for cc in range(SSD_CONV_DIM // 512):
        cs = slice(cc * 512, (cc + 1) * 512)
        sh = _dot(shifts, xcat_ref[:, cs])
        acc = cb_ref[:, cs] + cw_ref[CONV_WIDTH - 1:CONV_WIDTH, cs] * xcat_ref[q:2 * q, cs].astype(F32)
        for s in range(1, CONV_WIDTH):
            acc = acc + cw_ref[CONV_WIDTH - 1 - s:CONV_WIDTH - s, cs] * sh[(s - 1) * q:s * q]
        xa_ref[:, cs] = acc * jax.nn.sigmoid(acc)

    dt = _softplus(dt_ref[:, :SSD_HEADS] + dtb_ref[...])
    dtt = _softplus(dtt_ref[0] + dtbt_ref[...])
    a = -jnp.exp(al_ref[...]) * LOG2E
    at = -jnp.exp(alt_ref[...]) * LOG2E
    ri = lax.broadcasted_iota(jnp.int32, (q, q), 0)
    ci = lax.broadcasted_iota(jnp.int32, (q, q), 1)
    tril = ri >= ci
    acum = _dot_hi(jnp.where(tril, 1.0, 0.0), dt * a)
    acumt = _dot_hi(dtt * at, jnp.where(ri <= ci, 1.0, 0.0))
    wrow = jnp.exp2(acumt[:, q - 1:q] - acumt) * dtt

    for g in range(SSD_GROUPS):
        bmat = xa_ref[:, SSD_INNER + g * SSD_STATE:SSD_INNER + (g + 1) * SSD_STATE]
        cmat = xa_ref[:, SSD_INNER + gn + g * SSD_STATE:SSD_INNER + gn + (g + 1) * SSD_STATE].astype(BF16)
        cb = _dot_nt(cmat, bmat.astype(BF16))
        bt = bmat.T
        for pp in range(SSD_HPG // 2):
            j = g * (SSD_HPG // 2) + pp
            ls = slice(j * LANES, (j + 1) * LANES)
            xs = xa_ref[:, ls]
            xbd = jnp.concatenate([jnp.where(lo_half, xs, 0.0), jnp.where(lo_half, 0.0, xs)], axis=0).astype(BF16)
            mm, dec, btw = [], [], []
            for par in range(2):
                h = 2 * j + par
                abc = jnp.broadcast_to(acum[:, h:h + 1], (q, q))
                lmat = jnp.exp2(jnp.where(tril, abc - acumt[h:h + 1, :], -jnp.inf))
                mm.append((cb * lmat * dtt[h:h + 1, :]).astype(BF16))
                dec.append(jnp.exp2(abc))
                btw.append((bt * wrow[h:h + 1, :]).astype(BF16))
            y = _dot(jnp.concatenate(mm, axis=1), xbd)
            y = y + _dot(cmat, st_ref[j].astype(BF16)) * jnp.where(lo_half, dec[0], dec[1])
            y_ref[:, ls] = y + xs * d_ref[:, ls]
            cdec = jnp.where(lo_half, dec[0][q - 1:q, :], dec[1][q - 1:q, :])
            st_ref[j] = st_ref[j] * cdec + _dot(jnp.concatenate(btw, axis=1), xbd)

    z = z_ref[...].astype(F32)
    yz = y_ref[...] * (z * jax.nn.sigmoid(z))
    o_ref[...] = _rms(yz, ng_ref[...]).astype(o_ref.dtype)


def _ssd(u16, u32, dtt, cw, cb, dtb, al, d, ng, bsz, seq):
    nc = seq // SSD_CHUNK
    q = SSD_CHUNK
    full = lambda shape: pl.BlockSpec(shape, lambda b, c: (0,) * len(shape))
    d_full = jnp.repeat(d, SSD_HEAD_DIM).reshape(1, SSD_INNER)
    return pl.pallas_call(
        _ssd_kernel,
        grid=(bsz, nc),
        in_specs=[pl.BlockSpec((q, SSD_CONV_DIM), lambda b, c: (b * nc + c, U16_XBC_BLK)),
                  pl.BlockSpec((q, SSD_INNER), lambda b, c: (b * nc + c, U16_Z_BLK)),
                  pl.BlockSpec((q, 128), lambda b, c: (b * nc + c, U32_DT_BLK128)),
                  pl.BlockSpec((1, SSD_HEADS, q), lambda b, c: (b, 0, c)),
                  full((CONV_WIDTH, SSD_CONV_DIM)), full((1, SSD_CONV_DIM)),
                  full((1, SSD_HEADS)), full((SSD_HEADS, 1)),
                  full((1, SSD_HEADS)), full((SSD_HEADS, 1)),
                  full((1, SSD_INNER)), full((1, SSD_INNER))],
        out_specs=pl.BlockSpec((q, SSD_INNER), lambda b, c: (b * nc + c, 0)),
        out_shape=jax.ShapeDtypeStruct((bsz * seq, SSD_INNER), BF16),
        scratch_shapes=[pltpu.VMEM((2 * q, SSD_CONV_DIM), BF16),
                        pltpu.VMEM((q, SSD_CONV_DIM), F32),
                        pltpu.VMEM((q, SSD_INNER), F32),
                        pltpu.VMEM((SSD_HEADS // 2, SSD_STATE, LANES), F32)],
        compiler_params=_cparams(("parallel", "arbitrary")),
        name="ssd",
    )(u16, u16, u32, dtt, cw, cb, dtb.reshape(1, -1), dtb.reshape(-1, 1), al.reshape(1, -1), al.reshape(-1, 1),
      d_full, ng.reshape(1, -1))


def _merge_kernel(x_ref, ya_ref, yb_ref, yc_ref, mg_ref, wa_ref, wb_ref, wc_ref, wo_ref, o_ref):
    d = D_MODEL
    mg = mg_ref[...].astype(F32)
    mix = jax.nn.sigmoid(mg[:, 0:d]) * _dot(ya_ref[...], wa_ref[...])
    mix = mix + jax.nn.sigmoid(mg[:, d:2 * d]) * _dot(yb_ref[...], wb_ref[...])
    mix = mix + jax.nn.sigmoid(mg[:, 2 * d:3 * d]) * _dot(yc_ref[...], wc_ref[...])
    o_ref[...] = x_ref[...] + _dot(mix.astype(BF16), wo_ref[...])


def _merge(x, ya, yb, yc, u16, wa, wb, wc, wo, tm):
    m, d = x.shape
    full = lambda a: pl.BlockSpec(a.shape, lambda i: (0, 0))
    return pl.pallas_call(
        _merge_kernel,
        grid=(m // tm,),
        in_specs=[pl.BlockSpec((tm, d), lambda i: (i, 0)),
                  pl.BlockSpec((tm, ya.shape[1]), lambda i: (i, 0)),
                  pl.BlockSpec((tm, yb.shape[1]), lambda i: (i, 0)),
                  pl.BlockSpec((tm, yc.shape[1]), lambda i: (i, 0)),
                  pl.BlockSpec((tm, 3 * d), lambda i: (i, U16_MG_BLK)),
                  full(wa), full(wb), full(wc), full(wo)],
        out_specs=pl.BlockSpec((tm, d), lambda i: (i, 0)),
        out_shape=jax.ShapeDtypeStruct((m, d), F32),
        compiler_params=_cparams(("parallel",)),
        name="merge",
    )(x, ya, yb, yc, u16, wa, wb, wc, wo)


def _ffn_kernel(x_ref, g_ref, w1_ref, w2_ref, o_ref, h_ref, acc_ref):
    j = pl.program_id(1)

    @pl.when(j == 0)
    def _():
        h_ref[...] = _rms(x_ref[...], g_ref[...]).astype(BF16)
        acc_ref[...] = x_ref[...]

    a = jnp.maximum(_dot(h_ref[...], w1_ref[...]), 0.0)
    acc_ref[...] += _dot((a * a).astype(BF16), w2_ref[...])

    @pl.when(j == pl.num_programs(1) - 1)
    def _():
        o_ref[...] = acc_ref[...]


def _ffn(x, g, w1, w2, tm, tf):
    m, d = x.shape
    f = w1.shape[1]
    return pl.pallas_call(
        _ffn_kernel,
        grid=(m // tm, f // tf),
        in_specs=[pl.BlockSpec((tm, d), lambda i, j: (i, 0)),
                  pl.BlockSpec((1, d), lambda i, j: (0, 0)),
                  pl.BlockSpec((d, tf), lambda i, j: (0, j)),
                  pl.BlockSpec((tf, d), lambda i, j: (j, 0))],
        out_specs=pl.BlockSpec((tm, d), lambda i, j: (i, 0)),
        out_shape=jax.ShapeDtypeStruct((m, d), F32),
        scratch_shapes=[pltpu.VMEM((tm, d), BF16), pltpu.VMEM((tm, d), F32)],
        compiler_params=_cparams(("parallel", "arbitrary")),
        name="ffn",
    )(x, g, w1, w2)


def _prep_w_in(w):
    offs = [0] + [int(o) for o in np.cumsum(IN_SPLITS)]
    seg = lambda k: w[:, offs[k]:offs[k + 1]]
    dq, dk, dv, iq, ik, iw, nq, nkv, ng, sz, sxbc, sdt, mg = [seg(k) for k in range(13)]
    zeros = lambda n: jnp.zeros((w.shape[0], n), w.dtype)
    w16 = jnp.concatenate([sxbc, mg, sz, dq, nq], axis=1).astype(BF16)
    w32 = jnp.concatenate([nkv[:, 256:768], nkv[:, 0:256], zeros(256),
                           dk, dv, iq, ik, iw, zeros(512 - 424),
                           ng, zeros(128 - ng.shape[1]),
                           sdt, zeros(128 - sdt.shape[1])], axis=1).astype(BF16)
    return w16, w32


def _pick_tile(n, pref):
    t = min(n, pref)
    while n % t:
        t //= 2
    return t


def kernel(x, norm1_g, w_in, dsa_q_norm, dsa_k_norm, nsa_q_norm, nsa_k_norm, nsa_cmp_pos, nsa_cmp_w,
           ssd_conv_w, ssd_conv_b, ssd_dt_bias, ssd_a_log, ssd_d, ssd_norm_g,
           w_br_dsa, w_br_nsa, w_br_ssd, w_out, norm2_g, w_ff1, w_ff2):
    bsz, seq, d = x.shape
    m = bsz * seq
    depth = w_in.shape[0]
    nch = seq // CMP_STRIDE
    n_blk = seq // SEL_BLOCK
    half = CMP_BLOCK // 2

    cmp_start = np.arange(nch) * CMP_STRIDE
    blk_start = np.arange(n_blk) * SEL_BLOCK
    cover = ((cmp_start[:, None] < blk_start[None, :] + SEL_BLOCK)
             & (cmp_start[:, None] + CMP_BLOCK > blk_start[None, :])
             & (np.arange(nch)[:, None] < (seq - CMP_BLOCK) // CMP_STRIDE + 1)).astype(np.float32)
    expand = (np.arange(seq)[None, :] // SEL_BLOCK == np.arange(n_blk)[:, None]).astype(np.float32)
    gexp = np.zeros((3, LANES, NSA_HEADS * HEAD_DIM), np.float32)
    for hh in range(NSA_HEADS):
        for br in range(3):
            gexp[br, 3 * hh + br, hh * HEAD_DIM:(hh + 1) * HEAD_DIM] = 1.0
    gexp = jnp.asarray(gexp, dtype=BF16)
    cover = jnp.asarray(cover)
    expand = jnp.asarray(expand, dtype=BF16)

    tm = _pick_tile(m, 1024)
    xf = x.reshape(m, d)
    for l in range(depth):
        w16, w32 = _prep_w_in(w_in[l])
        g1 = norm1_g[l].reshape(1, d)
        u16 = _norm_matmul(xf, g1, w16, BF16, tm, 2048)
        u32 = _norm_matmul(xf, g1, w32, F32, tm, U32_WIDTH)

        iw0 = U32_DS_BLK512 * 512 + DS_IW
        iwt = u32[:, iw0:iw0 + IDX_HEADS].reshape(bsz, seq, IDX_HEADS).transpose(0, 2, 1)
        ya = _dsa(u16, u32, iwt, dsa_q_norm[l].reshape(1, -1), dsa_k_norm[l].reshape(1, -1), bsz, seq)

        wl = nsa_cmp_w[l]
        zblk = jnp.zeros_like(wl)
        cw = jnp.concatenate([jnp.concatenate([wl, zblk], axis=3),
                              jnp.concatenate([zblk, wl], axis=3)], axis=2).astype(BF16)
        cpos = jnp.concatenate([nsa_cmp_pos[l]] * NSA_KV_HEADS, axis=-1)
        kc, vc = _nsa_compress(u32, cpos, cw, nsa_k_norm[l][0:1], bsz, seq)
        yb = _nsa(u16, u32, kc, vc, nsa_q_norm[l].reshape(1, -1), nsa_k_norm[l], cover, expand, gexp, bsz, seq)

        dt0 = U32_DT_BLK128 * 128
        dtt = u32[:, dt0:dt0 + SSD_HEADS].reshape(bsz, seq, SSD_HEADS).transpose(0, 2, 1)
        yc = _ssd(u16, u32, dtt, ssd_conv_w[l], ssd_conv_b[l].reshape(1, -1), ssd_dt_bias[l], ssd_a_log[l],
                  ssd_d[l], ssd_norm_g[l], bsz, seq)

        xf = _merge(xf, ya, yb, yc, u16,
                    w_br_dsa[l].astype(BF16), w_br_nsa[l].astype(BF16), w_br_ssd[l].astype(BF16),
                    w_out[l].astype(BF16), _pick_tile(m, 256))
        xf = _ffn(xf, norm2_g[l].reshape(1, d), w_ff1[l].astype(BF16), w_ff2[l].astype(BF16),
                  _pick_tile(m, 1024), 1024)
    return xf.reshape(bsz, seq, d)
```

```python
import functools
import math

import numpy as np
import jax
import jax.numpy as jnp
from jax import lax
from jax.experimental import pallas as pl
from jax.experimental.pallas import tpu as pltpu

F32 = jnp.float32
BF16 = jnp.bfloat16

D_MODEL = 1024
HEAD_DIM = 64
DSA_HEADS = 16
DSA_KV_DIM = 64
IDX_HEADS = 8
IDX_DIM = 32
DSA_TOPK = 256
NSA_HEADS = 16
NSA_KV_HEADS = 2
NSA_GROUP = NSA_HEADS // NSA_KV_HEADS
CMP_BLOCK = 32
CMP_STRIDE = 16
SEL_BLOCK = 64
SEL_TOPN = 4
WINDOW = 512
SSD_INNER = 2 * D_MODEL
SSD_HEAD_DIM = 64
SSD_HEADS = SSD_INNER // SSD_HEAD_DIM
SSD_GROUPS = 4
SSD_HPG = SSD_HEADS // SSD_GROUPS
SSD_STATE = 128
SSD_CONV_DIM = SSD_INNER + 2 * SSD_GROUPS * SSD_STATE
CONV_WIDTH = 4
SSD_CHUNK = 128
D_FF = 4 * D_MODEL
Q_BLOCK = 128
EPS = 1e-6
NEG = -1e30
IN_SPLITS = (DSA_HEADS * HEAD_DIM, DSA_KV_DIM, DSA_KV_DIM, IDX_HEADS * IDX_DIM, IDX_DIM, IDX_HEADS,
             NSA_HEADS * HEAD_DIM, 6 * NSA_KV_HEADS * HEAD_DIM, 3 * NSA_HEADS,
             SSD_INNER, SSD_CONV_DIM, SSD_HEADS, 3 * D_MODEL)

V7X_VMEM_LIMIT_BYTES = 56 * 1024 * 1024
LANES = 128

U16_WIDTH = 10240
U16_XBC_BLK = 0
U16_MG_BLK = 1
U16_Z_BLK = 3
U16_DQ_BLK = 8
U16_NQ_BLK = 9
U32_WIDTH = 1792
U32_NKV_BLK512 = 0
U32_KC_BLK128 = 4
U32_DS_BLK512 = 2
U32_NG_BLK128 = 12
U32_DT_BLK128 = 13
DS_DK, DS_DV, DS_IQ, DS_IK, DS_IW = 0, 64, 128, 384, 416

KEY_TILE = 256
BISECT_ITERS = 24
PAIR_STEP = 2
FLASH_UNROLL = 4
LOG2E = 1.4426950408889634
BISECT_UNROLL = 4


def _cparams(sem):
    return pltpu.CompilerParams(dimension_semantics=sem, vmem_limit_bytes=V7X_VMEM_LIMIT_BYTES)


def _rms(x, g):
    return x * lax.rsqrt(jnp.mean(x * x, axis=-1, keepdims=True) + EPS) * g


def _dot_nt(a, b):
    return lax.dot_general(a, b, (((1,), (1,)), ((), ())), preferred_element_type=F32)


def _dot(a, b):
    return jnp.dot(a, b, preferred_element_type=F32)


def _dot_hi(a, b):
    return jnp.dot(a, b, preferred_element_type=F32, precision=lax.Precision.HIGHEST)


def _norm_matmul_kernel(x_ref, g_ref, w_ref, o_ref, h_ref):
    @pl.when(pl.program_id(1) == 0)
    def _():
        h_ref[...] = _rms(x_ref[...], g_ref[...]).astype(BF16)

    o_ref[...] = _dot(h_ref[...], w_ref[...]).astype(o_ref.dtype)


def _norm_matmul(x, g, w, out_dtype, tm, tn):
    m, k = x.shape
    n = w.shape[1]
    return pl.pallas_call(
        _norm_matmul_kernel,
        grid=(m // tm, n // tn),
        in_specs=[pl.BlockSpec((tm, k), lambda i, j: (i, 0)),
                  pl.BlockSpec((1, k), lambda i, j: (0, 0)),
                  pl.BlockSpec((k, tn), lambda i, j: (0, j))],
        out_specs=pl.BlockSpec((tm, tn), lambda i, j: (i, j)),
        out_shape=jax.ShapeDtypeStruct((m, n), out_dtype),
        scratch_shapes=[pltpu.VMEM((tm, k), BF16)],
        compiler_params=_cparams(("parallel", "arbitrary")),
        name="norm_matmul",
    )(x, g, w)


def _flash_pairs(qp_ref, kbd_ref, wext_ref, bias_ref, nkt, n_pairs, pairs_per_kv, bufs, m_ref, acc_ref):
    (s0, s1), (p0, p1), (a0, a1) = bufs
    n_pg = n_pairs // PAIR_STEP
    rows = PAIR_STEP * Q_BLOCK
    n = nkt * n_pg
    lo_half = lax.broadcasted_iota(jnp.int32, (1, LANES), 1) < HEAD_DIM
    m_ref[...] = jnp.full(m_ref.shape, -jnp.inf, F32)
    acc_ref[...] = jnp.zeros(acc_ref.shape, F32)
    p1[...] = jnp.zeros(p1.shape, BF16)
    a1[...] = jnp.ones(a1.shape, F32)

    def where(j):
        kt = j // n_pg
        pg = j % n_pg
        g = (pg * PAIR_STEP) // pairs_per_kv
        return kt, pg, g, pl.multiple_of(kt * 2 * KEY_TILE, 2 * KEY_TILE), pl.multiple_of(pg * rows, rows)

    def qk(j, s_ref):
        kt, _, g, koff, r = where(j)
        s = _dot_nt(qp_ref[pl.ds(r, rows), :], kbd_ref[g, pl.ds(koff, 2 * KEY_TILE), :])
        s_ref[...] = s + jnp.tile(bias_ref[g, kt], (PAIR_STEP, 2))

    def softmax(j, s_ref, p_ref, a_ref):
        _, pg, _, _, _ = where(j)
        for u in range(PAIR_STEP):
            us = slice(u * Q_BLOCK, (u + 1) * Q_BLOCK)
            alpha = []
            for par in range(2):
                cs = slice(par * KEY_TILE, (par + 1) * KEY_TILE)
                hrow = pl.multiple_of(((pg * PAIR_STEP + u) * 2 + par) * Q_BLOCK, Q_BLOCK)
                m_old = m_ref[pl.ds(hrow, Q_BLOCK), :]
                m_new = jnp.maximum(m_old, jnp.max(s_ref[us, cs], axis=-1, keepdims=True))
                alpha.append(jnp.exp2(m_old - m_new))
                m_ref[pl.ds(hrow, Q_BLOCK), :] = m_new
                p_ref[us, cs] = jnp.exp2(
                    s_ref[us, cs] - jnp.concatenate([m_new] * (KEY_TILE // LANES), axis=1)).astype(BF16)
            a_ref[us, :] = jnp.where(lo_half, alpha[0], alpha[1])

    def pv(j, p_ref, a_ref):
        _, _, g, koff, r = where(j)
        a = a_ref[...]
        acc_ref[pl.ds(r, rows), :] = (jnp.concatenate([a, a], axis=1) * acc_ref[pl.ds(r, rows), :]
                                      + _dot(p_ref[...], wext_ref[g, pl.ds(koff, 2 * KEY_TILE), :]))

    qk(0, s0)

    def body(jj, c):
        for j in (FLASH_UNROLL * jj, FLASH_UNROLL * jj + 2):
            qk(j + 1, s1)
            softmax(j, s0, p0, a0)
            pv(jnp.maximum(j - 1, 0), p1, a1)
            qk(jnp.minimum(j + 2, n - 1), s0)
            softmax(j + 1, s1, p1, a1)
            pv(j, p0, a0)
        return c

    lax.fori_loop(0, n // FLASH_UNROLL, body, 0)
    pv(n - 1, p1, a1)


def _flash_scratch(n_pairs):
    step = PAIR_STEP * Q_BLOCK
    return ([pltpu.VMEM((step, 2 * KEY_TILE), F32)] * 2 + [pltpu.VMEM((step, 2 * KEY_TILE), BF16)] * 2
            + [pltpu.VMEM((step, LANES), F32)] * 2
            + [pltpu.VMEM((2 * n_pairs * Q_BLOCK, LANES), F32), pltpu.VMEM((n_pairs * Q_BLOCK, 2 * LANES), F32)])


def _norm_pairs(q_ref, g_ref, qp_ref, n_pairs):
    mult = HEAD_DIM ** -0.5 * LOG2E
    lo_half = lax.broadcasted_iota(jnp.int32, (1, LANES), 1) < HEAD_DIM
    g2 = jnp.concatenate([g_ref[...], g_ref[...]], axis=-1) * mult
    for j in range(n_pairs):
        x = q_ref[:, j * LANES:(j + 1) * LANES].astype(F32)
        x2 = x * x
        s_lo = jnp.sum(jnp.where(lo_half, x2, 0.0), axis=-1, keepdims=True)
        s_hi = jnp.sum(jnp.where(lo_half, 0.0, x2), axis=-1, keepdims=True)
        r = jnp.where(lo_half, lax.rsqrt(s_lo * (1.0 / HEAD_DIM) + EPS), lax.rsqrt(s_hi * (1.0 / HEAD_DIM) + EPS))
        qp_ref[j * Q_BLOCK:(j + 1) * Q_BLOCK, :] = (x * r * g2).astype(BF16)


def _store_pair_kv(kbd_ref, wext_ref, g, row0, n, k, v):
    z = jnp.zeros((n, HEAD_DIM), F32)
    one = jnp.ones((n, HEAD_DIM), F32)
    kbd_ref[g, pl.ds(row0, n), :] = jnp.concatenate([k, z], axis=-1).astype(BF16)
    kbd_ref[g, pl.ds(row0 + n, n), :] = jnp.concatenate([z, k], axis=-1).astype(BF16)
    wext_ref[g, pl.ds(row0, n), :] = jnp.concatenate([v, z, one, z], axis=-1).astype(BF16)
    wext_ref[g, pl.ds(row0 + n, n), :] = jnp.concatenate([z, v, z, one], axis=-1).astype(BF16)


def _dsa_kernel(q_ref, sq_ref, iwt_ref, sk_ref, gq_ref, gk_ref, o_ref,
                kbd_ref, wext_ref, ik_ref, idx_ref, bias_ref, iqs_ref,
                qp_ref, s0_ref, s1_ref, p0_ref, p1_ref, a0_ref, a1_ref, m_ref, acc_ref, *, seq, n_sel):
    i = pl.program_id(1)
    nq = Q_BLOCK
    n_pairs = DSA_HEADS // 2
    kt_shape = (KEY_TILE, nq)

    @pl.when(i == 0)
    def _():
        def rows(r, c):
            off = pl.multiple_of(r * KEY_TILE, KEY_TILE)
            sk = sk_ref[pl.ds(off, KEY_TILE), :]
            _store_pair_kv(kbd_ref, wext_ref, 0, pl.multiple_of(2 * off, 2 * KEY_TILE), KEY_TILE,
                           _rms(sk[:, DS_DK:DS_DK + HEAD_DIM], gk_ref[...]), sk[:, DS_DV:DS_DV + HEAD_DIM])
            ik_ref[pl.ds(off, KEY_TILE), :] = sk[:, DS_IK:DS_IK + IDX_DIM].astype(BF16)
            return c

        lax.fori_loop(0, seq // KEY_TILE, rows, 0)

    start = i * nq
    nkt = (start + nq + KEY_TILE - 1) // KEY_TILE
    t_row = (start + lax.broadcasted_iota(jnp.int32, (1, nq), 1)).astype(F32)
    sub_pos = lax.broadcasted_iota(jnp.int32, (KEY_TILE, 1), 0).astype(F32)

    sq = sq_ref[...]
    for h in range(IDX_HEADS):
        iqs_ref[h * nq:(h + 1) * nq, :] = sq[:, DS_IQ + h * IDX_DIM:DS_IQ + (h + 1) * IDX_DIM].astype(BF16)
    iwt = iwt_ref[0]

    def idx_tile(kt, carry):
        rmin, rmax = carry
        off = pl.multiple_of(kt * KEY_TILE, KEY_TILE)
        r = _dot_nt(ik_ref[pl.ds(off, KEY_TILE), :], iqs_ref[...])
        acc = jnp.zeros(kt_shape, F32)
        for h in range(IDX_HEADS):
            acc = acc + iwt[h:h + 1, :] * jnp.maximum(r[:, h * nq:(h + 1) * nq], 0.0)
        valid = (sub_pos + (kt * KEY_TILE).astype(F32)) <= t_row
        idx_ref[kt] = jnp.where(valid, acc, -jnp.inf)
        rmin = jnp.minimum(rmin, jnp.min(jnp.where(valid, acc, jnp.inf), axis=0, keepdims=True))
        rmax = jnp.maximum(rmax, jnp.max(jnp.where(valid, acc, -jnp.inf), axis=0, keepdims=True))
        return rmin, rmax

    npair = (nkt + 1) // 2
    rmin, rmax = lax.fori_loop(0, npair, lambda jj, c: idx_tile(2 * jj + 1, idx_tile(2 * jj, c)),
                               (jnp.full((1, nq), jnp.inf, F32), jnp.full((1, nq), -jnp.inf, F32)))

    kf = float(n_sel)

    def col_sum(x):
        parts = [x[r * 8:(r + 1) * 8] for r in range(KEY_TILE // 8)]
        while len(parts) > 1:
            parts = [parts[k] + parts[k + 1] for k in range(0, len(parts), 2)]
        return jnp.sum(parts[0], axis=0, keepdims=True)

    def count_gt(thr):
        def body(kt, acc):
            return acc + jnp.where(idx_ref[kt] > thr, 1.0, 0.0)

        return col_sum(lax.fori_loop(0, nkt, body, jnp.zeros(kt_shape, F32)))

    n_valid = t_row + 1.0
    lo0 = rmin - (jnp.abs(rmin) * 0.01 + 1.0)

    def bis_cond(c):
        it, _, _, clo, _ = c
        return jnp.logical_and(it < BISECT_ITERS, jnp.max(clo) > kf)

    def bis_body(c):
        it, lo, hi, clo, chi = c
        for _ in range(BISECT_UNROLL):
            mid = 0.5 * (lo + hi)
            cnt = count_gt(mid)
            ge = cnt >= kf
            lo, hi = jnp.where(ge, mid, lo), jnp.where(ge, hi, mid)
            clo, chi = jnp.where(ge, cnt, clo), jnp.where(ge, chi, cnt)
        return it + BISECT_UNROLL, lo, hi, clo, chi

    _, lo, hi, clo, chi = lax.while_loop(
        bis_cond, bis_body, (jnp.int32(0), lo0, rmax, n_valid, jnp.zeros((1, nq), F32)))

    eye_f = jnp.where(lax.broadcasted_iota(jnp.int32, (nq, nq), 0) == lax.broadcasted_iota(jnp.int32, (nq, nq), 1),
                      1.0, 0.0)
    eye = eye_f.astype(BF16)
    tri = jnp.where(lax.broadcasted_iota(jnp.int32, (KEY_TILE, KEY_TILE), 0)
                    <= lax.broadcasted_iota(jnp.int32, (KEY_TILE, KEY_TILE), 1), 1.0, 0.0).astype(BF16)
    quota_row = jnp.where(clo > kf, kf - chi, float(2 * seq))
    quota = jnp.sum(eye_f * quota_row, axis=1, keepdims=True)

    def mask_tile(kt, carry):
        v = idx_ref[kt]
        code = jnp.where(v > hi, 2.0, jnp.where(v > lo, 1.0, 0.0)).astype(BF16)
        code_t = _dot_nt(eye, code)
        tie_t = jnp.where((code_t > 0.5) & (code_t < 1.5), 1.0, 0.0)
        rank = _dot(tie_t.astype(BF16), tri) + carry
        keep = (code_t > 1.5) | ((tie_t > 0.5) & (rank <= quota))
        bias_ref[0, kt] = jnp.where(keep, 0.0, NEG)
        return carry + jnp.sum(tie_t, axis=1, keepdims=True)

    lax.fori_loop(0, npair, lambda jj, c: mask_tile(2 * jj + 1, mask_tile(2 * jj, c)), jnp.zeros((nq, 1), F32))

    _norm_pairs(q_ref, gq_ref, qp_ref, n_pairs)
    _flash_pairs(qp_ref, kbd_ref, wext_ref, bias_ref, nkt, n_pairs, n_pairs,
                 ((s0_ref, s1_ref), (p0_ref, p1_ref), (a0_ref, a1_ref)), m_ref, acc_ref)
    for j in range(n_pairs):
        acc = acc_ref[j * nq:(j + 1) * nq, :]
        o_ref[:, j * LANES:(j + 1) * LANES] = (acc[:, :LANES] / acc[:, LANES:]).astype(o_ref.dtype)


def _dsa(u16, u32, iwt, gq, gk, bsz, seq):
    n_sel = min(DSA_TOPK, seq // 4)
    nqb = seq // Q_BLOCK
    n_pairs = DSA_HEADS // 2
    kern = functools.partial(_dsa_kernel, seq=seq, n_sel=n_sel)
    return pl.pallas_call(
        kern,
        grid=(bsz, nqb),
        in_specs=[pl.BlockSpec((Q_BLOCK, 1024), lambda b, i: (b * nqb + i, U16_DQ_BLK)),
                  pl.BlockSpec((Q_BLOCK, 512), lambda b, i: (b * nqb + i, U32_DS_BLK512)),
                  pl.BlockSpec((1, IDX_HEADS, Q_BLOCK), lambda b, i: (b, 0, i)),
                  pl.BlockSpec((seq, 512), lambda b, i: (b, U32_DS_BLK512)),
                  pl.BlockSpec((1, HEAD_DIM), lambda b, i: (0, 0)),
                  pl.BlockSpec((1, DSA_KV_DIM), lambda b, i: (0, 0))],
        out_specs=pl.BlockSpec((Q_BLOCK, DSA_HEADS * HEAD_DIM), lambda b, i: (b * nqb + i, 0)),
        out_shape=jax.ShapeDtypeStruct((bsz * seq, DSA_HEADS * HEAD_DIM), BF16),
        scratch_shapes=[pltpu.VMEM((1, 2 * seq, LANES), BF16),
                        pltpu.VMEM((1, 2 * seq, 2 * LANES), BF16),
                        pltpu.VMEM((seq, IDX_DIM), BF16),
                        pltpu.VMEM((seq // KEY_TILE, KEY_TILE, Q_BLOCK), F32),
                        pltpu.VMEM((1, seq // KEY_TILE, Q_BLOCK, KEY_TILE), F32),
                        pltpu.VMEM((IDX_HEADS * Q_BLOCK, IDX_DIM), BF16),
                        pltpu.VMEM((n_pairs * Q_BLOCK, LANES), BF16)]
                       + _flash_scratch(n_pairs),
        compiler_params=_cparams(("parallel", "arbitrary")),
        name="dsa",
    )(u16, u32, iwt, u32, gq, gk)


def _nsa_compress_kernel(xk_ref, xv_ref, pos_ref, w_ref, g_ref, kc_ref, vc_ref):
    nch = xk_ref.shape[0] // CMP_STRIDE
    for jj, (x_ref, o_ref) in enumerate(((xk_ref, kc_ref), (xv_ref, vc_ref))):
        acc_a = jnp.zeros((nch, LANES), F32)
        acc_b = jnp.zeros((nch, LANES), F32)
        for l in range(CMP_STRIDE):
            x = x_ref[pl.ds(l, nch, stride=CMP_STRIDE), :]
            acc_a = acc_a + _dot((x + pos_ref[jj, l:l + 1, :]).astype(BF16), w_ref[jj, l])
            hi = CMP_STRIDE + l
            acc_b = acc_b + _dot((x + pos_ref[jj, hi:hi + 1, :]).astype(BF16), w_ref[jj, hi])
        out = acc_a + pltpu.roll(acc_b, nch - 1, axis=0)
        for g in range(NSA_KV_HEADS):
            og = out[:, g * HEAD_DIM:(g + 1) * HEAD_DIM]
            o_ref[0, g] = (_rms(og, g_ref[...]) if jj == 0 else og).astype(BF16)


def _nsa_compress(u32, pos, w, g, bsz, seq):
    nch = seq // CMP_STRIDE
    return pl.pallas_call(
        _nsa_compress_kernel,
        grid=(bsz,),
        in_specs=[pl.BlockSpec((seq, LANES), lambda b: (b, U32_KC_BLK128)),
                  pl.BlockSpec((seq, LANES), lambda b: (b, U32_KC_BLK128 + 1)),
                  pl.BlockSpec((2, CMP_BLOCK, LANES), lambda b: (0, 0, 0)),
                  pl.BlockSpec((2, CMP_BLOCK, LANES, LANES), lambda b: (0, 0, 0, 0)),
                  pl.BlockSpec((1, HEAD_DIM), lambda b: (0, 0))],
        out_specs=[pl.BlockSpec((1, NSA_KV_HEADS, nch, HEAD_DIM), lambda b: (b, 0, 0, 0)),
                   pl.BlockSpec((1, NSA_KV_HEADS, nch, HEAD_DIM), lambda b: (b, 0, 0, 0))],
        out_shape=[jax.ShapeDtypeStruct((bsz, NSA_KV_HEADS, nch, HEAD_DIM), BF16),
                   jax.ShapeDtypeStruct((bsz, NSA_KV_HEADS, nch, HEAD_DIM), BF16)],
        compiler_params=_cparams(("parallel",)),
        name="nsa_compress",
    )(u32, u32, pos, w, g)


def _nsa_kernel(q_ref, ng_ref, kv_ref, kc_ref, vc_ref, gq_ref, gk_ref, cover_ref, expand_ref, gexp_ref, o_ref,
                ksbd_ref, wsext_ref, kwbd_ref, wwext_ref, kcbd_ref, vcext_ref, bias_ref, wbias_ref, psum_ref,
                ocmp_ref, accw_ref, qp_ref, sw_ref, pw_ref, sc_ref, pc_ref,
                s0_ref, s1_ref, p0_ref, p1_ref, a0_ref, a1_ref, m_ref, acc_ref, *, seq):
    i = pl.program_id(1)
    nq = Q_BLOCK
    n_pairs = NSA_HEADS // 2
    gpairs = NSA_GROUP // 2
    grows = gpairs * nq
    n_cmp_pad = kc_ref.shape[2]
    n_blk = seq // SEL_BLOCK
    n_sel = min(SEL_TOPN, n_blk)
    win_keys = min(WINDOW + Q_BLOCK, seq)
    win_tiles = win_keys // Q_BLOCK
    wt = Q_BLOCK

    @pl.when(i == 0)
    def _():
        def rows(r, c):
            off = pl.multiple_of(r * KEY_TILE, KEY_TILE)
            kv = kv_ref[pl.ds(off, KEY_TILE), :]
            for g in range(NSA_KV_HEADS):
                c0 = g * HEAD_DIM
                _store_pair_kv(ksbd_ref, wsext_ref, g, pl.multiple_of(2 * off, 2 * KEY_TILE), KEY_TILE,
                               _rms(kv[:, c0:c0 + 64], gk_ref[1:2, :]), kv[:, 128 + c0:128 + c0 + 64])
                kw = _rms(kv[:, 256 + c0:256 + c0 + 64], gk_ref[2:3, :])
                vw = kv[:, 384 + c0:384 + c0 + 64]
                for hf in range(KEY_TILE // wt):
                    _store_pair_kv(kwbd_ref, wwext_ref, g, pl.multiple_of(2 * off + hf * 2 * wt, 2 * wt), wt,
                                   kw[hf * wt:(hf + 1) * wt], vw[hf * wt:(hf + 1) * wt])
            return c

        lax.fori_loop(0, seq // KEY_TILE, rows, 0)
        zc = jnp.zeros((n_cmp_pad, HEAD_DIM), BF16)
        for g in range(NSA_KV_HEADS):
            kcbd_ref[g, 0:n_cmp_pad, :] = jnp.concatenate([kc_ref[0, g], zc], axis=-1)
            kcbd_ref[g, n_cmp_pad:2 * n_cmp_pad, :] = jnp.concatenate([zc, kc_ref[0, g]], axis=-1)
            vcext_ref[g, 0:n_cmp_pad, :] = jnp.concatenate([vc_ref[0, g], zc], axis=-1)
            vcext_ref[g, n_cmp_pad:2 * n_cmp_pad, :] = jnp.concatenate([zc, vc_ref[0, g]], axis=-1)

    start = i * nq
    nkt = (start + nq + KEY_TILE - 1) // KEY_TILE
    t_i = start + lax.broadcasted_iota(jnp.int32, (nq, 1), 0)
    t_col = t_i.astype(F32)
    lane_pos = lax.broadcasted_iota(jnp.int32, (1, KEY_TILE), 1).astype(F32)

    _norm_pairs(q_ref, gq_ref, qp_ref, n_pairs)

    cmp_end = (lax.broadcasted_iota(jnp.int32, (1, n_cmp_pad), 1) * CMP_STRIDE + (CMP_BLOCK - 1)).astype(F32)
    vis = cmp_end <= t_col
    wbase = pl.multiple_of(jnp.maximum(start + nq - win_keys, 0), Q_BLOCK)
    wb2 = pl.multiple_of(2 * wbase, 2 * Q_BLOCK)
    col = lax.broadcasted_iota(jnp.int32, (1, 2 * win_keys), 1)
    wpos = (wbase + jnp.right_shift(col, int(math.log2(2 * wt))) * wt + jnp.bitwise_and(col, wt - 1)).astype(F32)
    wbias_ref[...] = jnp.where((wpos <= t_col) & (wpos > t_col - float(WINDOW)), 0.0, NEG)
    for g in range(NSA_KV_HEADS):
        gs = slice(g * grows, (g + 1) * grows)
        sc_ref[gs, :] = _dot_nt(qp_ref[gs, :], kcbd_ref[g])
        sw_ref[gs, :] = _dot_nt(qp_ref[gs, :], kwbd_ref[g, pl.ds(wb2, 2 * win_keys), :])

    for g in range(NSA_KV_HEADS):
        gs = slice(g * grows, (g + 1) * grows)
        tot = jnp.zeros((nq, n_cmp_pad), F32)
        for j in range(g * gpairs, (g + 1) * gpairs):
            rs = slice(j * nq, (j + 1) * nq)
            for par in range(2):
                cs = slice(par * n_cmp_pad, (par + 1) * n_cmp_pad)
                s = jnp.where(vis, sc_ref[rs, cs], NEG)
                e = jnp.exp2(s - jnp.max(s, axis=-1, keepdims=True))
                p = jnp.where(vis, e / jnp.sum(e, axis=-1, keepdims=True), 0.0)
                pc_ref[rs, cs] = p.astype(BF16)
                tot = tot + p
        psum_ref[g] = tot
        ocmp_ref[gs, :] = _dot(pc_ref[gs, :], vcext_ref[g])

    blk_j = lax.broadcasted_iota(jnp.int32, (1, n_blk), 1)
    cur1 = jnp.right_shift(t_i, int(math.log2(SEL_BLOCK)))
    cur = jnp.concatenate([cur1] * NSA_KV_HEADS, axis=0)
    imp = jnp.concatenate([_dot_hi(psum_ref[g], cover_ref[...]) for g in range(NSA_KV_HEADS)], axis=0)
    forced = (blk_j == cur) | (blk_j == 0)
    imp = jnp.where(forced, jnp.inf, jnp.where(blk_j > cur, -jnp.inf, imp))
    selb = jnp.zeros(imp.shape, jnp.bool_)
    for _ in range(n_sel):
        mx = jnp.max(imp, axis=-1, keepdims=True)
        first = jnp.min(jnp.where(imp == mx, blk_j, n_blk), axis=-1, keepdims=True)
        pick = blk_j == first
        selb = selb | pick
        imp = jnp.where(pick, -jnp.inf, imp)
    selb_bf = jnp.where(selb, 1.0, 0.0).astype(BF16)

    for g in range(NSA_KV_HEADS):
        for j in range(g * gpairs, (g + 1) * gpairs):
            rs = slice(j * nq, (j + 1) * nq)
            for par in range(2):
                cols = [slice(t * 2 * wt + par * wt, t * 2 * wt + (par + 1) * wt) for t in range(win_tiles)]
                s = [sw_ref[rs, cs] + wbias_ref[:, cs] for cs in cols]
                mx = s[0]
                for st in s[1:]:
                    mx = jnp.maximum(mx, st)
                m = jnp.max(mx, axis=-1, keepdims=True)
                for cs, st in zip(cols, s):
                    pw_ref[rs, cs] = jnp.exp2(st - m).astype(BF16)
        accw_ref[g * grows:(g + 1) * grows, :] = _dot(
            pw_ref[g * grows:(g + 1) * grows, :], wwext_ref[g, pl.ds(wb2, 2 * win_keys), :])

    def mask_tile(kt, c):
        off = pl.multiple_of(kt * KEY_TILE, KEY_TILE)
        hit = _dot(selb_bf, expand_ref[:, pl.ds(off, KEY_TILE)])
        ok = (lane_pos + (kt * KEY_TILE).astype(F32)) <= t_col
        for g in range(NSA_KV_HEADS):
            bias_ref[g, kt] = jnp.where((hit[g * nq:(g + 1) * nq] > 0.5) & ok, 0.0, NEG)
        return c

    lax.fori_loop(0, nkt, mask_tile, 0)
    _flash_pairs(qp_ref, ksbd_ref, wsext_ref, bias_ref, nkt, n_pairs, gpairs,
                 ((s0_ref, s1_ref), (p0_ref, p1_ref), (a0_ref, a1_ref)), m_ref, acc_ref)

    gates = jax.nn.sigmoid(ng_ref[...])
    g_hi = gates.astype(BF16)
    g_lo = (gates - g_hi.astype(F32)).astype(BF16)
    gb = [_dot(g_hi, gexp_ref[b]) + _dot(g_lo, gexp_ref[b]) for b in range(3)]
    for j in range(n_pairs):
        rs = slice(j * nq, (j + 1) * nq)
        ls = slice(j * LANES, (j + 1) * LANES)
        acc = acc_ref[rs, :]
        accw = accw_ref[rs, :]
        out = (gb[0][:, ls] * ocmp_ref[rs, :] + gb[1][:, ls] * (acc[:, :LANES] / acc[:, LANES:])
               + gb[2][:, ls] * (accw[:, :LANES] / accw[:, LANES:]))
        o_ref[:, ls] = out.astype(o_ref.dtype)


def _nsa(u16, u32, kc, vc, gq, gk, cover, expand, gexp, bsz, seq):
    nqb = seq // Q_BLOCK
    nch = kc.shape[2]
    n_blk = seq // SEL_BLOCK
    n_pairs = NSA_HEADS // 2
    prow = n_pairs * Q_BLOCK
    win_keys = min(WINDOW + Q_BLOCK, seq)
    kern = functools.partial(_nsa_kernel, seq=seq)
    return pl.pallas_call(
        kern,
        grid=(bsz, nqb),
        in_specs=[pl.BlockSpec((Q_BLOCK, 1024), lambda b, i: (b * nqb + i, U16_NQ_BLK)),
                  pl.BlockSpec((Q_BLOCK, 128), lambda b, i: (b * nqb + i, U32_NG_BLK128)),
                  pl.BlockSpec((seq, 512), lambda b, i: (b, U32_NKV_BLK512)),
                  pl.BlockSpec((1, NSA_KV_HEADS, nch, HEAD_DIM), lambda b, i: (b, 0, 0, 0)),
                  pl.BlockSpec((1, NSA_KV_HEADS, nch, HEAD_DIM), lambda b, i: (b, 0, 0, 0)),
                  pl.BlockSpec((1, HEAD_DIM), lambda b, i: (0, 0)),
                  pl.BlockSpec((3, HEAD_DIM), lambda b, i: (0, 0)),
                  pl.BlockSpec((nch, n_blk), lambda b, i: (0, 0)),
                  pl.BlockSpec((n_blk, seq), lambda b, i: (0, 0)),
                  pl.BlockSpec((3, LANES, NSA_HEADS * HEAD_DIM), lambda b, i: (0, 0, 0))],
        out_specs=pl.BlockSpec((Q_BLOCK, NSA_HEADS * HEAD_DIM), lambda b, i: (b * nqb + i, 0)),
        out_shape=jax.ShapeDtypeStruct((bsz * seq, NSA_HEADS * HEAD_DIM), BF16),
        scratch_shapes=[pltpu.VMEM((NSA_KV_HEADS, 2 * seq, LANES), BF16),
                        pltpu.VMEM((NSA_KV_HEADS, 2 * seq, 2 * LANES), BF16),
                        pltpu.VMEM((NSA_KV_HEADS, 2 * seq, LANES), BF16),
                        pltpu.VMEM((NSA_KV_HEADS, 2 * seq, 2 * LANES), BF16),
                        pltpu.VMEM((NSA_KV_HEADS, 2 * nch, LANES), BF16),
                        pltpu.VMEM((NSA_KV_HEADS, 2 * nch, LANES), BF16),
                        pltpu.VMEM((NSA_KV_HEADS, seq // KEY_TILE, Q_BLOCK, KEY_TILE), F32),
                        pltpu.VMEM((Q_BLOCK, 2 * win_keys), F32),
                        pltpu.VMEM((NSA_KV_HEADS, Q_BLOCK, nch), F32),
                        pltpu.VMEM((prow, LANES), F32),
                        pltpu.VMEM((prow, 2 * LANES), F32),
                        pltpu.VMEM((prow, LANES), BF16),
                        pltpu.VMEM((prow, 2 * win_keys), F32),
                        pltpu.VMEM((prow, 2 * win_keys), BF16),
                        pltpu.VMEM((prow, 2 * nch), F32),
                        pltpu.VMEM((prow, 2 * nch), BF16)]
                       + _flash_scratch(n_pairs),
        compiler_params=_cparams(("parallel", "arbitrary")),
        name="nsa",
    )(u16, u32, u32, kc, vc, gq, gk, cover, expand, gexp)


def _softplus(x):
    return jnp.maximum(x, 0.0) + jnp.log1p(jnp.exp(-jnp.abs(x)))


def _ssd_kernel(xbc_ref, z_ref, dt_ref, dtt_ref, cw_ref, cb_ref, dtb_ref, dtbt_ref, al_ref, alt_ref,
                d_ref, ng_ref, o_ref, xcat_ref, xa_ref, y_ref, st_ref):
    c = pl.program_id(1)
    q = SSD_CHUNK
    gn = SSD_GROUPS * SSD_STATE
    tail = 16
    lo_half = lax.broadcasted_iota(jnp.int32, (1, LANES), 1) < SSD_HEAD_DIM

    @pl.when(c == 0)
    def _():
        st_ref[...] = jnp.zeros_like(st_ref)
        xcat_ref[0:q, :] = jnp.zeros((q, SSD_CONV_DIM), BF16)

    @pl.when(c > 0)
    def _():
        xcat_ref[q - tail:q, :] = xcat_ref[2 * q - tail:2 * q, :]

    xcat_ref[q:2 * q, :] = xbc_ref[...]
    ri3 = lax.broadcasted_iota(jnp.int32, ((CONV_WIDTH - 1) * q, 2 * q), 0)
    ci3 = lax.broadcasted_iota(jnp.int32, ((CONV_WIDTH - 1) * q, 2 * q), 1)
    qbits = int(math.log2(q))
    src = q + jnp.bitwise_and(ri3, q - 1) - (jnp.right_shift(ri3, qbits) + 1)
    shifts = jnp.where(ci3 == src, 1.0, 0.0).astype(BF16)
    for cc in range(SSD_CONV_DIM // 512):
        cs = slice(cc * 512, (cc + 1) * 512)
        sh = _dot(shifts, xcat_ref[:, cs])
        acc = cb_ref[:, cs] + cw_ref[CONV_WIDTH - 1:CONV_WIDTH, cs] * xcat_ref[q:2 * q, cs].astype(F32)
        for s in range(1, CONV_WIDTH):
            acc = acc + cw_ref[CONV_WIDTH - 1 - s:CONV_WIDTH - s, cs] * sh[(s - 1) * q:s * q]
        xa_ref[:, cs] = acc * jax.nn.sigmoid(acc)

    dt = _softplus(dt_ref[:, :SSD_HEADS] + dtb_ref[...])
    dtt = _softplus(dtt_ref[0] + dtbt_ref[...])
    a = -jnp.exp(al_ref[...]) * LOG2E
    at = -jnp.exp(alt_ref[...]) * LOG2E
    ri = lax.broadcasted_iota(jnp.int32, (q, q), 0)
    ci = lax.broadcasted_iota(jnp.int32, (q, q), 1)
    tril = ri >= ci
    acum = _dot_hi(jnp.where(tril, 1.0, 0.0), dt * a)
    acumt = _dot_hi(dtt * at, jnp.where(ri <= ci, 1.0, 0.0))
    wrow = jnp.exp2(acumt[:, q - 1:q] - acumt) * dtt

    for g in range(SSD_GROUPS):
        bmat = xa_ref[:, SSD_INNER + g * SSD_STATE:SSD_INNER + (g + 1) * SSD_STATE]
        cmat = xa_ref[:, SSD_INNER + gn + g * SSD_STATE:SSD_INNER + gn + (g + 1) * SSD_STATE].astype(BF16)
        cb = _dot_nt(cmat, bmat.astype(BF16))
        bt = bmat.T
        for pp in range(SSD_HPG // 2):
            j = g * (SSD_HPG // 2) + pp
            ls = slice(j * LANES, (j + 1) * LANES)
            xs = xa_ref[:, ls]
            xbd = jnp.concatenate([jnp.where(lo_half, xs, 0.0), jnp.where(lo_half, 0.0, xs)], axis=0).astype(BF16)
            mm, dec, btw = [], [], []
            for par in range(2):
                h = 2 * j + par
                abc = jnp.broadcast_to(acum[:, h:h + 1], (q, q))
                lmat = jnp.exp2(jnp.where(tril, abc - acumt[h:h + 1, :], -jnp.inf))
                mm.append((cb * lmat * dtt[h:h + 1, :]).astype(BF16))
                dec.append(jnp.exp2(abc))
                btw.append((bt * wrow[h:h + 1, :]).astype(BF16))
            y = _dot(jnp.concatenate(mm, axis=1), xbd)
            y = y + _dot(cmat, st_ref[j].astype(BF16)) * jnp.where(lo_half, dec[0], dec[1])
            y_ref[:, ls] = y + xs * d_ref[:, ls]
            cdec = jnp.where(lo_half, dec[0][q - 1:q, :], dec[1][q - 1:q, :])
            st_ref[j] = st_ref[j] * cdec + _dot(jnp.concatenate(btw, axis=1), xbd)

    z = z_ref[...].astype(F32)
    yz = y_ref[...] * (z * jax.nn.sigmoid(z))
    o_ref[...] = _rms(yz, ng_ref[...]).astype(o_ref.dtype)


def _ssd(u16, u32, dtt, cw, cb, dtb, al, d, ng, bsz, seq):
    nc = seq // SSD_CHUNK
    q = SSD_CHUNK
    full = lambda shape: pl.BlockSpec(shape, lambda b, c: (0,) * len(shape))
    d_full = jnp.repeat(d, SSD_HEAD_DIM).reshape(1, SSD_INNER)
    return pl.pallas_call(
        _ssd_kernel,
        grid=(bsz, nc),
        in_specs=[pl.BlockSpec((q, SSD_CONV_DIM), lambda b, c: (b * nc + c, U16_XBC_BLK)),
                  pl.BlockSpec((q, SSD_INNER), lambda b, c: (b * nc + c, U16_Z_BLK)),
                  pl.BlockSpec((q, 128), lambda b, c: (b * nc + c, U32_DT_BLK128)),
                  pl.BlockSpec((1, SSD_HEADS, q), lambda b, c: (b, 0, c)),
                  full((CONV_WIDTH, SSD_CONV_DIM)), full((1, SSD_CONV_DIM)),
                  full((1, SSD_HEADS)), full((SSD_HEADS, 1)),
                  full((1, SSD_HEADS)), full((SSD_HEADS, 1)),
                  full((1, SSD_INNER)), full((1, SSD_INNER))],
        out_specs=pl.BlockSpec((q, SSD_INNER), lambda b, c: (b * nc + c, 0)),
        out_shape=jax.ShapeDtypeStruct((bsz * seq, SSD_INNER), BF16),
        scratch_shapes=[pltpu.VMEM((2 * q, SSD_CONV_DIM), BF16),
                        pltpu.VMEM((q, SSD_CONV_DIM), F32),
                        pltpu.VMEM((q, SSD_INNER), F32),
                        pltpu.VMEM((SSD_HEADS // 2, SSD_STATE, LANES), F32)],
        compiler_params=_cparams(("parallel", "arbitrary")),
        name="ssd",
    )(u16, u16, u32, dtt, cw, cb, dtb.reshape(1, -1), dtb.reshape(-1, 1), al.reshape(1, -1), al.reshape(-1, 1),
      d_full, ng.reshape(1, -1))


def _merge_kernel(x_ref, ya_ref, yb_ref, yc_ref, mg_ref, wa_ref, wb_ref, wc_ref, wo_ref, o_ref):
    d = D_MODEL
    mg = mg_ref[...].astype(F32)
    mix = jax.nn.sigmoid(mg[:, 0:d]) * _dot(ya_ref[...], wa_ref[...])
    mix = mix + jax.nn.sigmoid(mg[:, d:2 * d]) * _dot(yb_ref[...], wb_ref[...])
    mix = mix + jax.nn.sigmoid(mg[:, 2 * d:3 * d]) * _dot(yc_ref[...], wc_ref[...])
    o_ref[...] = x_ref[...] + _dot(mix.astype(BF16), wo_ref[...])


def _merge(x, ya, yb, yc, u16, wa, wb, wc, wo, tm):
    m, d = x.shape
    full = lambda a: pl.BlockSpec(a.shape, lambda i: (0, 0))
    return pl.pallas_call(
        _merge_kernel,
        grid=(m // tm,),
        in_specs=[pl.BlockSpec((tm, d), lambda i: (i, 0)),
                  pl.BlockSpec((tm, ya.shape[1]), lambda i: (i, 0)),
                  pl.BlockSpec((tm, yb.shape[1]), lambda i: (i, 0)),
                  pl.BlockSpec((tm, yc.shape[1]), lambda i: (i, 0)),
                  pl.BlockSpec((tm, 3 * d), lambda i: (i, U16_MG_BLK)),
                  full(wa), full(wb), full(wc), full(wo)],
        out_specs=pl.BlockSpec((tm, d), lambda i: (i, 0)),
        out_shape=jax.ShapeDtypeStruct((m, d), F32),
        compiler_params=_cparams(("parallel",)),
        name="merge",
    )(x, ya, yb, yc, u16, wa, wb, wc, wo)


def _ffn_kernel(x_ref, g_ref, w1_ref, w2_ref, o_ref, h_ref, acc_ref):
    j = pl.program_id(1)

    @pl.when(j == 0)
    def _():
        h_ref[...] = _rms(x_ref[...], g_ref[...]).astype(BF16)
        acc_ref[...] = x_ref[...]

    a = jnp.maximum(_dot(h_ref[...], w1_ref[...]), 0.0)
    acc_ref[...] += _dot((a * a).astype(BF16), w2_ref[...])

    @pl.when(j == pl.num_programs(1) - 1)
    def _():
        o_ref[...] = acc_ref[...]


def _ffn(x, g, w1, w2, tm, tf):
    m, d = x.shape
    f = w1.shape[1]
    return pl.pallas_call(
        _ffn_kernel,
        grid=(m // tm, f // tf),
        in_specs=[pl.BlockSpec((tm, d), lambda i, j: (i, 0)),
                  pl.BlockSpec((1, d), lambda i, j: (0, 0)),
                  pl.BlockSpec((d, tf), lambda i, j: (0, j)),
                  pl.BlockSpec((tf, d), lambda i, j: (j, 0))],
        out_specs=pl.BlockSpec((tm, d), lambda i, j: (i, 0)),
        out_shape=jax.ShapeDtypeStruct((m, d), F32),
        scratch_shapes=[pltpu.VMEM((tm, d), BF16), pltpu.VMEM((tm, d), F32)],
        compiler_params=_cparams(("parallel", "arbitrary")),
        name="ffn",
    )(x, g, w1, w2)


def _prep_w_in(w):
    offs = [0] + [int(o) for o in np.cumsum(IN_SPLITS)]
    seg = lambda k: w[:, offs[k]:offs[k + 1]]
    dq, dk, dv, iq, ik, iw, nq, nkv, ng, sz, sxbc, sdt, mg = [seg(k) for k in range(13)]
    zeros = lambda n: jnp.zeros((w.shape[0], n), w.dtype)
    w16 = jnp.concatenate([sxbc, mg, sz, dq, nq], axis=1).astype(BF16)
    w32 = jnp.concatenate([nkv[:, 256:768], nkv[:, 0:256], zeros(256),
                           dk, dv, iq, ik, iw, zeros(512 - 424),
                           ng, zeros(128 - ng.shape[1]),
                           sdt, zeros(128 - sdt.shape[1])], axis=1).astype(BF16)
    return w16, w32


def _pick_tile(n, pref):
    t = min(n, pref)
    while n % t:
        t //= 2
    return t


def kernel(x, norm1_g, w_in, dsa_q_norm, dsa_k_norm, nsa_q_norm, nsa_k_norm, nsa_cmp_pos, nsa_cmp_w,
           ssd_conv_w, ssd_conv_b, ssd_dt_bias, ssd_a_log, ssd_d, ssd_norm_g,
           w_br_dsa, w_br_nsa, w_br_ssd, w_out, norm2_g, w_ff1, w_ff2):
    bsz, seq, d = x.shape
    m = bsz * seq
    depth = w_in.shape[0]
    nch = seq // CMP_STRIDE
    n_blk = seq // SEL_BLOCK
    half = CMP_BLOCK // 2

    cmp_start = np.arange(nch) * CMP_STRIDE
    blk_start = np.arange(n_blk) * SEL_BLOCK
    cover = ((cmp_start[:, None] < blk_start[None, :] + SEL_BLOCK)
             & (cmp_start[:, None] + CMP_BLOCK > blk_start[None, :])
             & (np.arange(nch)[:, None] < (seq - CMP_BLOCK) // CMP_STRIDE + 1)).astype(np.float32)
    expand = (np.arange(seq)[None, :] // SEL_BLOCK == np.arange(n_blk)[:, None]).astype(np.float32)
    gexp = np.zeros((3, LANES, NSA_HEADS * HEAD_DIM), np.float32)
    for hh in range(NSA_HEADS):
        for br in range(3):
            gexp[br, 3 * hh + br, hh * HEAD_DIM:(hh + 1) * HEAD_DIM] = 1.0
    gexp = jnp.asarray(gexp, dtype=BF16)
    cover = jnp.asarray(cover)
    expand = jnp.asarray(expand, dtype=BF16)

    tm = _pick_tile(m, 1024)
    xf = x.reshape(m, d)
    for l in range(depth):
        w16, w32 = _prep_w_in(w_in[l])
        g1 = norm1_g[l].reshape(1, d)
        u16 = _norm_matmul(xf, g1, w16, BF16, tm, 2048)
        u32 = _norm_matmul(xf, g1, w32, F32, tm, U32_WIDTH)

        iw0 = U32_DS_BLK512 * 512 + DS_IW
        iwt = u32[:, iw0:iw0 + IDX_HEADS].reshape(bsz, seq, IDX_HEADS).transpose(0, 2, 1)
        ya = _dsa(u16, u32, iwt, dsa_q_norm[l].reshape(1, -1), dsa_k_norm[l].reshape(1, -1), bsz, seq)

        wl = nsa_cmp_w[l]
        zblk = jnp.zeros_like(wl)
        cw = jnp.concatenate([jnp.concatenate([wl, zblk], axis=3),
                              jnp.concatenate([zblk, wl], axis=3)], axis=2).astype(BF16)
        cpos = jnp.concatenate([nsa_cmp_pos[l]] * NSA_KV_HEADS, axis=-1)
        kc, vc = _nsa_compress(u32, cpos, cw, nsa_k_norm[l][0:1], bsz, seq)
        yb = _nsa(u16, u32, kc, vc, nsa_q_norm[l].reshape(1, -1), nsa_k_norm[l], cover, expand, gexp, bsz, seq)

        dt0 = U32_DT_BLK128 * 128
        dtt = u32[:, dt0:dt0 + SSD_HEADS].reshape(bsz, seq, SSD_HEADS).transpose(0, 2, 1)
        yc = _ssd(u16, u32, dtt, ssd_conv_w[l], ssd_conv_b[l].reshape(1, -1), ssd_dt_bias[l], ssd_a_log[l],
                  ssd_d[l], ssd_norm_g[l], bsz, seq)

        xf = _merge(xf, ya, yb, yc, u16,
                    w_br_dsa[l].astype(BF16), w_br_nsa[l].astype(BF16), w_br_ssd[l].astype(BF16),
                    w_out[l].astype(BF16), _pick_tile(m, 256))
        xf = _ffn(xf, norm2_g[l].reshape(1, d), w_ff1[l].astype(BF16), w_ff2[l].astype(BF16),
                  _pick_tile(m, 1024), 1024)
    return xf.reshape(bsz, seq, d)
```

```python
import functools
import math

import numpy as np
import jax
import jax.numpy as jnp
from jax import lax
from jax.experimental import pallas as pl
from jax.experimental.pallas import tpu as pltpu

F32 = jnp.float32
BF16 = jnp.bfloat16

D_MODEL = 1024
HEAD_DIM = 64
DSA_HEADS = 16
DSA_KV_DIM = 64
IDX_HEADS = 8
IDX_DIM = 32
DSA_TOPK = 256
NSA_HEADS = 16
NSA_KV_HEADS = 2
NSA_GROUP = NSA_HEADS // NSA_KV_HEADS
CMP_BLOCK = 32
CMP_STRIDE = 16
SEL_BLOCK = 64
SEL_TOPN = 4
WINDOW = 512
SSD_INNER = 2 * D_MODEL
SSD_HEAD_DIM = 64
SSD_HEADS = SSD_INNER // SSD_HEAD_DIM
SSD_GROUPS = 4
SSD_HPG = SSD_HEADS // SSD_GROUPS
SSD_STATE = 128
SSD_CONV_DIM = SSD_INNER + 2 * SSD_GROUPS * SSD_STATE
CONV_WIDTH = 4
SSD_CHUNK = 128
D_FF = 4 * D_MODEL
Q_BLOCK = 128
EPS = 1e-6
NEG = -1e30
IN_SPLITS = (DSA_HEADS * HEAD_DIM, DSA_KV_DIM, DSA_KV_DIM, IDX_HEADS * IDX_DIM, IDX_DIM, IDX_HEADS,
             NSA_HEADS * HEAD_DIM, 6 * NSA_KV_HEADS * HEAD_DIM, 3 * NSA_HEADS,
             SSD_INNER, SSD_CONV_DIM, SSD_HEADS, 3 * D_MODEL)

V7X_VMEM_LIMIT_BYTES = 56 * 1024 * 1024
LANES = 128

U16_WIDTH = 10240
U16_XBC_BLK = 0
U16_MG_BLK = 1
U16_Z_BLK = 3
U16_DQ_BLK = 8
U16_NQ_BLK = 9
U32_WIDTH = 1792
U32_NKV_BLK512 = 0
U32_KC_BLK128 = 4
U32_DS_BLK512 = 2
U32_NG_BLK128 = 12
U32_DT_BLK128 = 13
DS_DK, DS_DV, DS_IQ, DS_IK, DS_IW = 0, 64, 128, 384, 416

KEY_TILE = 256
BISECT_ITERS = 24
PAIR_STEP = 2
FLASH_UNROLL = 4
LOG2E = 1.4426950408889634
BISECT_UNROLL = 4
CONV_CHUNK = 256


def _cparams(sem):
    return pltpu.CompilerParams(dimension_semantics=sem, vmem_limit_bytes=V7X_VMEM_LIMIT_BYTES)


def _rms(x, g):
    return x * lax.rsqrt(jnp.mean(x * x, axis=-1, keepdims=True) + EPS) * g


def _dot_nt(a, b):
    return lax.dot_general(a, b, (((1,), (1,)), ((), ())), preferred_element_type=F32)


def _dot(a, b):
    return jnp.dot(a, b, preferred_element_type=F32)


def _dot_hi(a, b):
    return jnp.dot(a, b, preferred_element_type=F32, precision=lax.Precision.HIGHEST)


def _norm_matmul_kernel(x_ref, g_ref, w_ref, o_ref, h_ref):
    @pl.when(pl.program_id(1) == 0)
    def _():
        h_ref[...] = _rms(x_ref[...], g_ref[...]).astype(BF16)

    o_ref[...] = _dot(h_ref[...], w_ref[...]).astype(o_ref.dtype)


def _norm_matmul(x, g, w, out_dtype, tm, tn):
    m, k = x.shape
    n = w.shape[1]
    return pl.pallas_call(
        _norm_matmul_kernel,
        grid=(m // tm, n // tn),
        in_specs=[pl.BlockSpec((tm, k), lambda i, j: (i, 0)),
                  pl.BlockSpec((1, k), lambda i, j: (0, 0)),
                  pl.BlockSpec((k, tn), lambda i, j: (0, j))],
        out_specs=pl.BlockSpec((tm, tn), lambda i, j: (i, j)),
        out_shape=jax.ShapeDtypeStruct((m, n), out_dtype),
        scratch_shapes=[pltpu.VMEM((tm, k), BF16)],
        compiler_params=_cparams(("parallel", "arbitrary")),
        name="norm_matmul",
    )(x, g, w)


def _flash_pairs(qp_ref, kbd_ref, wext_ref, bias_ref, nkt, n_pairs, pairs_per_kv, bufs, m_ref, acc_ref):
    (s0, s1), (p0, p1), (a0, a1) = bufs
    n_pg = n_pairs // PAIR_STEP
    rows = PAIR_STEP * Q_BLOCK
    n = nkt * n_pg
    lo_half = lax.broadcasted_iota(jnp.int32, (1, LANES), 1) < HEAD_DIM
    m_ref[...] = jnp.full(m_ref.shape, -jnp.inf, F32)
    acc_ref[...] = jnp.zeros(acc_ref.shape, F32)
    p1[...] = jnp.zeros(p1.shape, BF16)
    a1[...] = jnp.ones(a1.shape, F32)

    def where(j):
        kt = j // n_pg
        pg = j % n_pg
        g = (pg * PAIR_STEP) // pairs_per_kv
        return kt, pg, g, pl.multiple_of(kt * 2 * KEY_TILE, 2 * KEY_TILE), pl.multiple_of(pg * rows, rows)

    def qk(j, s_ref):
        kt, _, g, koff, r = where(j)
        s = _dot_nt(qp_ref[pl.ds(r, rows), :], kbd_ref[g, pl.ds(koff, 2 * KEY_TILE), :])
        s_ref[...] = s + jnp.tile(bias_ref[g, kt], (PAIR_STEP, 2))

    def softmax(j, s_ref, p_ref, a_ref):
        _, pg, _, _, _ = where(j)
        for u in range(PAIR_STEP):
            us = slice(u * Q_BLOCK, (u + 1) * Q_BLOCK)
            alpha = []
            for par in range(2):
                cs = slice(par * KEY_TILE, (par + 1) * KEY_TILE)
                hrow = pl.multiple_of(((pg * PAIR_STEP + u) * 2 + par) * Q_BLOCK, Q_BLOCK)
                m_old = m_ref[pl.ds(hrow, Q_BLOCK), :]
                m_new = jnp.maximum(m_old, jnp.max(s_ref[us, cs], axis=-1, keepdims=True))
                alpha.append(jnp.exp2(m_old - m_new))
                m_ref[pl.ds(hrow, Q_BLOCK), :] = m_new
                p_ref[us, cs] = jnp.exp2(
                    s_ref[us, cs] - jnp.concatenate([m_new] * (KEY_TILE // LANES), axis=1)).astype(BF16)
            a_ref[us, :] = jnp.where(lo_half, alpha[0], alpha[1])

    def pv(j, p_ref, a_ref):
        _, _, g, koff, r = where(j)
        a = a_ref[...]
        acc_ref[pl.ds(r, rows), :] = (jnp.concatenate([a, a], axis=1) * acc_ref[pl.ds(r, rows), :]
                                      + _dot(p_ref[...], wext_ref[g, pl.ds(koff, 2 * KEY_TILE), :]))

    qk(0, s0)

    def body(jj, c):
        for j in (FLASH_UNROLL * jj, FLASH_UNROLL * jj + 2):
            qk(j + 1, s1)
            softmax(j, s0, p0, a0)
            pv(jnp.maximum(j - 1, 0), p1, a1)
            qk(jnp.minimum(j + 2, n - 1), s0)
            softmax(j + 1, s1, p1, a1)
            pv(j, p0, a0)
        return c

    lax.fori_loop(0, n // FLASH_UNROLL, body, 0)
    pv(n - 1, p1, a1)


def _flash_scratch(n_pairs):
    step = PAIR_STEP * Q_BLOCK
    return ([pltpu.VMEM((step, 2 * KEY_TILE), F32)] * 2 + [pltpu.VMEM((step, 2 * KEY_TILE), BF16)] * 2
            + [pltpu.VMEM((step, LANES), F32)] * 2
            + [pltpu.VMEM((2 * n_pairs * Q_BLOCK, LANES), F32), pltpu.VMEM((n_pairs * Q_BLOCK, 2 * LANES), F32)])


def _norm_pairs(q_ref, g_ref, qp_ref, n_pairs):
    mult = HEAD_DIM ** -0.5 * LOG2E
    lo_half = lax.broadcasted_iota(jnp.int32, (1, LANES), 1) < HEAD_DIM
    g2 = jnp.concatenate([g_ref[...], g_ref[...]], axis=-1) * mult
    for j in range(n_pairs):
        x = q_ref[:, j * LANES:(j + 1) * LANES].astype(F32)
        x2 = x * x
        s_lo = jnp.sum(jnp.where(lo_half, x2, 0.0), axis=-1, keepdims=True)
        s_hi = jnp.sum(jnp.where(lo_half, 0.0, x2), axis=-1, keepdims=True)
        r = jnp.where(lo_half, lax.rsqrt(s_lo * (1.0 / HEAD_DIM) + EPS), lax.rsqrt(s_hi * (1.0 / HEAD_DIM) + EPS))
        qp_ref[j * Q_BLOCK:(j + 1) * Q_BLOCK, :] = (x * r * g2).astype(BF16)


def _store_pair_kv(kbd_ref, wext_ref, g, row0, n, k, v):
    z = jnp.zeros((n, HEAD_DIM), F32)
    one = jnp.ones((n, HEAD_DIM), F32)
    kbd_ref[g, pl.ds(row0, n), :] = jnp.concatenate([k, z], axis=-1).astype(BF16)
    kbd_ref[g, pl.ds(row0 + n, n), :] = jnp.concatenate([z, k], axis=-1).astype(BF16)
    wext_ref[g, pl.ds(row0, n), :] = jnp.concatenate([v, z, one, z], axis=-1).astype(BF16)
    wext_ref[g, pl.ds(row0 + n, n), :] = jnp.concatenate([z, v, z, one], axis=-1).astype(BF16)


def _dsa_kernel(q_ref, sq_ref, iwt_ref, sk_ref, gq_ref, gk_ref, o_ref,
                kbd_ref, wext_ref, ik_ref, idx_ref, bias_ref, iqs_ref,
                qp_ref, s0_ref, s1_ref, p0_ref, p1_ref, a0_ref, a1_ref, m_ref, acc_ref, *, seq, n_sel):
    i = pl.program_id(1)
    nq = Q_BLOCK
    n_pairs = DSA_HEADS // 2
    kt_shape = (KEY_TILE, nq)

    @pl.when(i == 0)
    def _():
        def rows(r, c):
            off = pl.multiple_of(r * KEY_TILE, KEY_TILE)
            sk = sk_ref[pl.ds(off, KEY_TILE), :]
            _store_pair_kv(kbd_ref, wext_ref, 0, pl.multiple_of(2 * off, 2 * KEY_TILE), KEY_TILE,
                           _rms(sk[:, DS_DK:DS_DK + HEAD_DIM], gk_ref[...]), sk[:, DS_DV:DS_DV + HEAD_DIM])
            ik_ref[pl.ds(off, KEY_TILE), :] = sk[:, DS_IK:DS_IK + IDX_DIM].astype(BF16)
            return c

        lax.fori_loop(0, seq // KEY_TILE, rows, 0)

    start = i * nq
    nkt = (start + nq + KEY_TILE - 1) // KEY_TILE
    t_row = (start + lax.broadcasted_iota(jnp.int32, (1, nq), 1)).astype(F32)
    sub_pos = lax.broadcasted_iota(jnp.int32, (KEY_TILE, 1), 0).astype(F32)

    sq = sq_ref[...]
    for h in range(IDX_HEADS):
        iqs_ref[h * nq:(h + 1) * nq, :] = sq[:, DS_IQ + h * IDX_DIM:DS_IQ + (h + 1) * IDX_DIM].astype(BF16)
    iwt = iwt_ref[0]

    def idx_tile(kt, carry):
        rmin, rmax = carry
        off = pl.multiple_of(kt * KEY_TILE, KEY_TILE)
        r = _dot_nt(ik_ref[pl.ds(off, KEY_TILE), :], iqs_ref[...])
        acc = jnp.zeros(kt_shape, F32)
        for h in range(IDX_HEADS):
            acc = acc + iwt[h:h + 1, :] * jnp.maximum(r[:, h * nq:(h + 1) * nq], 0.0)
        valid = (sub_pos + (kt * KEY_TILE).astype(F32)) <= t_row
        idx_ref[kt] = jnp.where(valid, acc, -jnp.inf)
        rmin = jnp.minimum(rmin, jnp.min(jnp.where(valid, acc, jnp.inf), axis=0, keepdims=True))
        rmax = jnp.maximum(rmax, jnp.max(jnp.where(valid, acc, -jnp.inf), axis=0, keepdims=True))
        return rmin, rmax

    npair = (nkt + 1) // 2
    rmin, rmax = lax.fori_loop(0, npair, lambda jj, c: idx_tile(2 * jj + 1, idx_tile(2 * jj, c)),
                               (jnp.full((1, nq), jnp.inf, F32), jnp.full((1, nq), -jnp.inf, F32)))

    kf = float(n_sel)

    def col_sum(x):
        parts = [x[r * 8:(r + 1) * 8] for r in range(x.shape[0] // 8)]
        while len(parts) > 1:
            parts = [parts[k] + parts[k + 1] for k in range(0, len(parts), 2)]
        return jnp.sum(parts[0], axis=0, keepdims=True)

    fold = KEY_TILE // 4

    def count_gt(thr):
        def body(kt, acc):
            hit = jnp.where(idx_ref[kt] > thr, 1.0, 0.0)
            return acc + ((hit[0:fold] + hit[fold:2 * fold]) + (hit[2 * fold:3 * fold] + hit[3 * fold:]))

        return col_sum(lax.fori_loop(0, nkt, body, jnp.zeros((fold, nq), F32)))

    n_valid = t_row + 1.0
    lo0 = rmin - (jnp.abs(rmin) * 0.01 + 1.0)

    def bis_cond(c):
        it, _, _, clo, _ = c
        return jnp.logical_and(it < BISECT_ITERS, jnp.max(clo) > kf)

    def bis_body(c):
        it, lo, hi, clo, chi = c
        for _ in range(BISECT_UNROLL):
            mid = 0.5 * (lo + hi)
            cnt = count_gt(mid)
            ge = cnt >= kf
            lo, hi = jnp.where(ge, mid, lo), jnp.where(ge, hi, mid)
            clo, chi = jnp.where(ge, cnt, clo), jnp.where(ge, chi, cnt)
        return it + BISECT_UNROLL, lo, hi, clo, chi

    _, lo, hi, clo, chi = lax.while_loop(
        bis_cond, bis_body, (jnp.int32(0), lo0, rmax, n_valid, jnp.zeros((1, nq), F32)))

    eye_f = jnp.where(lax.broadcasted_iota(jnp.int32, (nq, nq), 0) == lax.broadcasted_iota(jnp.int32, (nq, nq), 1),
                      1.0, 0.0)
    eye = eye_f.astype(BF16)
    tri = jnp.where(lax.broadcasted_iota(jnp.int32, (KEY_TILE, KEY_TILE), 0)
                    <= lax.broadcasted_iota(jnp.int32, (KEY_TILE, KEY_TILE), 1), 1.0, 0.0).astype(BF16)
    quota_row = jnp.where(clo > kf, kf - chi, float(2 * seq))
    quota = jnp.sum(eye_f * quota_row, axis=1, keepdims=True)

    def mask_tile(kt, carry):
        v = idx_ref[kt]
        code = jnp.where(v > hi, 2.0, jnp.where(v > lo, 1.0, 0.0)).astype(BF16)
        code_t = _dot_nt(eye, code)
        tie_t = jnp.where((code_t > 0.5) & (code_t < 1.5), 1.0, 0.0)
        rank = _dot(tie_t.astype(BF16), tri) + carry
        keep = (code_t > 1.5) | ((tie_t > 0.5) & (rank <= quota))
        bias_ref[0, kt] = jnp.where(keep, 0.0, NEG)
        return carry + jnp.sum(tie_t, axis=1, keepdims=True)

    lax.fori_loop(0, npair, lambda jj, c: mask_tile(2 * jj + 1, mask_tile(2 * jj, c)), jnp.zeros((nq, 1), F32))

    _norm_pairs(q_ref, gq_ref, qp_ref, n_pairs)
    _flash_pairs(qp_ref, kbd_ref, wext_ref, bias_ref, nkt, n_pairs, n_pairs,
                 ((s0_ref, s1_ref), (p0_ref, p1_ref), (a0_ref, a1_ref)), m_ref, acc_ref)
    for j in range(n_pairs):
        acc = acc_ref[j * nq:(j + 1) * nq, :]
        o_ref[:, j * LANES:(j + 1) * LANES] = (acc[:, :LANES] / acc[:, LANES:]).astype(o_ref.dtype)


def _dsa(u16, u32, iwt, gq, gk, bsz, seq):
    n_sel = min(DSA_TOPK, seq // 4)
    nqb = seq // Q_BLOCK
    n_pairs = DSA_HEADS // 2
    kern = functools.partial(_dsa_kernel, seq=seq, n_sel=n_sel)
    return pl.pallas_call(
        kern,
        grid=(bsz, nqb),
        in_specs=[pl.BlockSpec((Q_BLOCK, 1024), lambda b, i: (b * nqb + i, U16_DQ_BLK)),
                  pl.BlockSpec((Q_BLOCK, 512), lambda b, i: (b * nqb + i, U32_DS_BLK512)),
                  pl.BlockSpec((1, IDX_HEADS, Q_BLOCK), lambda b, i: (b, 0, i)),
                  pl.BlockSpec((seq, 512), lambda b, i: (b, U32_DS_BLK512)),
                  pl.BlockSpec((1, HEAD_DIM), lambda b, i: (0, 0)),
                  pl.BlockSpec((1, DSA_KV_DIM), lambda b, i: (0, 0))],
        out_specs=pl.BlockSpec((Q_BLOCK, DSA_HEADS * HEAD_DIM), lambda b, i: (b * nqb + i, 0)),
        out_shape=jax.ShapeDtypeStruct((bsz * seq, DSA_HEADS * HEAD_DIM), BF16),
        scratch_shapes=[pltpu.VMEM((1, 2 * seq, LANES), BF16),
                        pltpu.VMEM((1, 2 * seq, 2 * LANES), BF16),
                        pltpu.VMEM((seq, IDX_DIM), BF16),
                        pltpu.VMEM((seq // KEY_TILE, KEY_TILE, Q_BLOCK), F32),
                        pltpu.VMEM((1, seq // KEY_TILE, Q_BLOCK, KEY_TILE), F32),
                        pltpu.VMEM((IDX_HEADS * Q_BLOCK, IDX_DIM), BF16),
                        pltpu.VMEM((n_pairs * Q_BLOCK, LANES), BF16)]
                       + _flash_scratch(n_pairs),
        compiler_params=_cparams(("parallel", "arbitrary")),
        name="dsa",
    )(u16, u32, iwt, u32, gq, gk)


def _nsa_compress_kernel(xk_ref, xv_ref, pos_ref, w_ref, g_ref, kc_ref, vc_ref):
    nch = xk_ref.shape[0] // CMP_STRIDE
    for jj, (x_ref, o_ref) in enumerate(((xk_ref, kc_ref), (xv_ref, vc_ref))):
        acc_a = jnp.zeros((nch, LANES), F32)
        acc_b = jnp.zeros((nch, LANES), F32)
        for l in range(CMP_STRIDE):
            x = x_ref[pl.ds(l, nch, stride=CMP_STRIDE), :]
            acc_a = acc_a + _dot((x + pos_ref[jj, l:l + 1, :]).astype(BF16), w_ref[jj, l])
            hi = CMP_STRIDE + l
            acc_b = acc_b + _dot((x + pos_ref[jj, hi:hi + 1, :]).astype(BF16), w_ref[jj, hi])
        out = acc_a + pltpu.roll(acc_b, nch - 1, axis=0)
        for g in range(NSA_KV_HEADS):
            og = out[:, g * HEAD_DIM:(g + 1) * HEAD_DIM]
            o_ref[0, g] = (_rms(og, g_ref[...]) if jj == 0 else og).astype(BF16)


def _nsa_compress(u32, pos, w, g, bsz, seq):
    nch = seq // CMP_STRIDE
    return pl.pallas_call(
        _nsa_compress_kernel,
        grid=(bsz,),
        in_specs=[pl.BlockSpec((seq, LANES), lambda b: (b, U32_KC_BLK128)),
                  pl.BlockSpec((seq, LANES), lambda b: (b, U32_KC_BLK128 + 1)),
                  pl.BlockSpec((2, CMP_BLOCK, LANES), lambda b: (0, 0, 0)),
                  pl.BlockSpec((2, CMP_BLOCK, LANES, LANES), lambda b: (0, 0, 0, 0)),
                  pl.BlockSpec((1, HEAD_DIM), lambda b: (0, 0))],
        out_specs=[pl.BlockSpec((1, NSA_KV_HEADS, nch, HEAD_DIM), lambda b: (b, 0, 0, 0)),
                   pl.BlockSpec((1, NSA_KV_HEADS, nch, HEAD_DIM), lambda b: (b, 0, 0, 0))],
        out_shape=[jax.ShapeDtypeStruct((bsz, NSA_KV_HEADS, nch, HEAD_DIM), BF16),
                   jax.ShapeDtypeStruct((bsz, NSA_KV_HEADS, nch, HEAD_DIM), BF16)],
        compiler_params=_cparams(("parallel",)),
        name="nsa_compress",
    )(u32, u32, pos, w, g)


def _nsa_kernel(q_ref, ng_ref, kv_ref, kc_ref, vc_ref, gq_ref, gk_ref, cover_ref, expand_ref, gexp_ref, o_ref,
                ksbd_ref, wsext_ref, kwbd_ref, wwext_ref, kcbd_ref, vcext_ref, bias_ref, wbias_ref, psum_ref,
                ocmp_ref, accw_ref, qp_ref, sw_ref, pw_ref, sc_ref, pc_ref,
                s0_ref, s1_ref, p0_ref, p1_ref, a0_ref, a1_ref, m_ref, acc_ref, *, seq):
    i = pl.program_id(1)
    nq = Q_BLOCK
    n_pairs = NSA_HEADS // 2
    gpairs = NSA_GROUP // 2
    grows = gpairs * nq
    n_cmp_pad = kc_ref.shape[2]
    n_blk = seq // SEL_BLOCK
    n_sel = min(SEL_TOPN, n_blk)
    win_keys = min(WINDOW + Q_BLOCK, seq)
    win_tiles = win_keys // Q_BLOCK
    wt = Q_BLOCK

    @pl.when(i == 0)
    def _():
        def rows(r, c):
            off = pl.multiple_of(r * KEY_TILE, KEY_TILE)
            kv = kv_ref[pl.ds(off, KEY_TILE), :]
            for g in range(NSA_KV_HEADS):
                c0 = g * HEAD_DIM
                _store_pair_kv(ksbd_ref, wsext_ref, g, pl.multiple_of(2 * off, 2 * KEY_TILE), KEY_TILE,
                               _rms(kv[:, c0:c0 + 64], gk_ref[1:2, :]), kv[:, 128 + c0:128 + c0 + 64])
                kw = _rms(kv[:, 256 + c0:256 + c0 + 64], gk_ref[2:3, :])
                vw = kv[:, 384 + c0:384 + c0 + 64]
                for hf in range(KEY_TILE // wt):
                    _store_pair_kv(kwbd_ref, wwext_ref, g, pl.multiple_of(2 * off + hf * 2 * wt, 2 * wt), wt,
                                   kw[hf * wt:(hf + 1) * wt], vw[hf * wt:(hf + 1) * wt])
            return c

        lax.fori_loop(0, seq // KEY_TILE, rows, 0)
        zc = jnp.zeros((n_cmp_pad, HEAD_DIM), BF16)
        for g in range(NSA_KV_HEADS):
            kcbd_ref[g, 0:n_cmp_pad, :] = jnp.concatenate([kc_ref[0, g], zc], axis=-1)
            kcbd_ref[g, n_cmp_pad:2 * n_cmp_pad, :] = jnp.concatenate([zc, kc_ref[0, g]], axis=-1)
            vcext_ref[g, 0:n_cmp_pad, :] = jnp.concatenate([vc_ref[0, g], zc], axis=-1)
            vcext_ref[g, n_cmp_pad:2 * n_cmp_pad, :] = jnp.concatenate([zc, vc_ref[0, g]], axis=-1)

    start = i * nq
    nkt = (start + nq + KEY_TILE - 1) // KEY_TILE
    t_i = start + lax.broadcasted_iota(jnp.int32, (nq, 1), 0)
    t_col = t_i.astype(F32)
    lane_pos = lax.broadcasted_iota(jnp.int32, (1, KEY_TILE), 1).astype(F32)

    _norm_pairs(q_ref, gq_ref, qp_ref, n_pairs)

    cmp_end = (lax.broadcasted_iota(jnp.int32, (1, n_cmp_pad), 1) * CMP_STRIDE + (CMP_BLOCK - 1)).astype(F32)
    vis = cmp_end <= t_col
    wbase = pl.multiple_of(jnp.maximum(start + nq - win_keys, 0), Q_BLOCK)
    wb2 = pl.multiple_of(2 * wbase, 2 * Q_BLOCK)
    col = lax.broadcasted_iota(jnp.int32, (1, 2 * win_keys), 1)
    wpos = (wbase + jnp.right_shift(col, int(math.log2(2 * wt))) * wt + jnp.bitwise_and(col, wt - 1)).astype(F32)
    wbias_ref[...] = jnp.where((wpos <= t_col) & (wpos > t_col - float(WINDOW)), 0.0, NEG)
    for g in range(NSA_KV_HEADS):
        gs = slice(g * grows, (g + 1) * grows)
        sc_ref[gs, :] = _dot_nt(qp_ref[gs, :], kcbd_ref[g])
        sw_ref[gs, :] = _dot_nt(qp_ref[gs, :], kwbd_ref[g, pl.ds(wb2, 2 * win_keys), :])

    for g in range(NSA_KV_HEADS):
        gs = slice(g * grows, (g + 1) * grows)
        tot = jnp.zeros((nq, n_cmp_pad), F32)
        for j in range(g * gpairs, (g + 1) * gpairs):
            rs = slice(j * nq, (j + 1) * nq)
            for par in range(2):
                cs = slice(par * n_cmp_pad, (par + 1) * n_cmp_pad)
                s = jnp.where(vis, sc_ref[rs, cs], NEG)
                e = jnp.exp2(s - jnp.max(s, axis=-1, keepdims=True))
                p = jnp.where(vis, e / jnp.sum(e, axis=-1, keepdims=True), 0.0)
                pc_ref[rs, cs] = p.astype(BF16)
                tot = tot + p
        psum_ref[g] = tot
        ocmp_ref[gs, :] = _dot(pc_ref[gs, :], vcext_ref[g])

    blk_j = lax.broadcasted_iota(jnp.int32, (1, n_blk), 1)
    cur1 = jnp.right_shift(t_i, int(math.log2(SEL_BLOCK)))
    cur = jnp.concatenate([cur1] * NSA_KV_HEADS, axis=0)
    imp = jnp.concatenate([_dot_hi(psum_ref[g], cover_ref[...]) for g in range(NSA_KV_HEADS)], axis=0)
    forced = (blk_j == cur) | (blk_j == 0)
    imp = jnp.where(forced, jnp.inf, jnp.where(blk_j > cur, -jnp.inf, imp))
    selb = jnp.zeros(imp.shape, jnp.bool_)
    for _ in range(n_sel):
        mx = jnp.max(imp, axis=-1, keepdims=True)
        first = jnp.min(jnp.where(imp == mx, blk_j, n_blk), axis=-1, keepdims=True)
        pick = blk_j == first
        selb = selb | pick
        imp = jnp.where(pick, -jnp.inf, imp)
    selb_bf = jnp.where(selb, 1.0, 0.0).astype(BF16)

    for g in range(NSA_KV_HEADS):
        for j in range(g * gpairs, (g + 1) * gpairs):
            rs = slice(j * nq, (j + 1) * nq)
            for par in range(2):
                cols = [slice(t * 2 * wt + par * wt, t * 2 * wt + (par + 1) * wt) for t in range(win_tiles)]
                s = [sw_ref[rs, cs] + wbias_ref[:, cs] for cs in cols]
                mx = s[0]
                for st in s[1:]:
                    mx = jnp.maximum(mx, st)
                m = jnp.max(mx, axis=-1, keepdims=True)
                for cs, st in zip(cols, s):
                    pw_ref[rs, cs] = jnp.exp2(st - m).astype(BF16)
        accw_ref[g * grows:(g + 1) * grows, :] = _dot(
            pw_ref[g * grows:(g + 1) * grows, :], wwext_ref[g, pl.ds(wb2, 2 * win_keys), :])

    def mask_tile(kt, c):
        off = pl.multiple_of(kt * KEY_TILE, KEY_TILE)
        hit = _dot(selb_bf, expand_ref[:, pl.ds(off, KEY_TILE)])
        ok = (lane_pos + (kt * KEY_TILE).astype(F32)) <= t_col
        for g in range(NSA_KV_HEADS):
            bias_ref[g, kt] = jnp.where((hit[g * nq:(g + 1) * nq] > 0.5) & ok, 0.0, NEG)
        return c

    lax.fori_loop(0, (nkt + 1) // 2, lambda jj, c: mask_tile(2 * jj + 1, mask_tile(2 * jj, c)), 0)
    _flash_pairs(qp_ref, ksbd_ref, wsext_ref, bias_ref, nkt, n_pairs, gpairs,
                 ((s0_ref, s1_ref), (p0_ref, p1_ref), (a0_ref, a1_ref)), m_ref, acc_ref)

    gates = jax.nn.sigmoid(ng_ref[...])
    g_hi = gates.astype(BF16)
    g_lo = (gates - g_hi.astype(F32)).astype(BF16)
    gb = [_dot(g_hi, gexp_ref[b]) + _dot(g_lo, gexp_ref[b]) for b in range(3)]
    for j in range(n_pairs):
        rs = slice(j * nq, (j + 1) * nq)
        ls = slice(j * LANES, (j + 1) * LANES)
        acc = acc_ref[rs, :]
        accw = accw_ref[rs, :]
        out = (gb[0][:, ls] * ocmp_ref[rs, :] + gb[1][:, ls] * (acc[:, :LANES] / acc[:, LANES:])
               + gb[2][:, ls] * (accw[:, :LANES] / accw[:, LANES:]))
        o_ref[:, ls] = out.astype(o_ref.dtype)


def _nsa(u16, u32, kc, vc, gq, gk, cover, expand, gexp, bsz, seq):
    nqb = seq // Q_BLOCK
    nch = kc.shape[2]
    n_blk = seq // SEL_BLOCK
    n_pairs = NSA_HEADS // 2
    prow = n_pairs * Q_BLOCK
    win_keys = min(WINDOW + Q_BLOCK, seq)
    kern = functools.partial(_nsa_kernel, seq=seq)
    return pl.pallas_call(
        kern,
        grid=(bsz, nqb),
        in_specs=[pl.BlockSpec((Q_BLOCK, 1024), lambda b, i: (b * nqb + i, U16_NQ_BLK)),
                  pl.BlockSpec((Q_BLOCK, 128), lambda b, i: (b * nqb + i, U32_NG_BLK128)),
                  pl.BlockSpec((seq, 512), lambda b, i: (b, U32_NKV_BLK512)),
                  pl.BlockSpec((1, NSA_KV_HEADS, nch, HEAD_DIM), lambda b, i: (b, 0, 0, 0)),
                  pl.BlockSpec((1, NSA_KV_HEADS, nch, HEAD_DIM), lambda b, i: (b, 0, 0, 0)),
                  pl.BlockSpec((1, HEAD_DIM), lambda b, i: (0, 0)),
                  pl.BlockSpec((3, HEAD_DIM), lambda b, i: (0, 0)),
                  pl.BlockSpec((nch, n_blk), lambda b, i: (0, 0)),
                  pl.BlockSpec((n_blk, seq), lambda b, i: (0, 0)),
                  pl.BlockSpec((3, LANES, NSA_HEADS * HEAD_DIM), lambda b, i: (0, 0, 0))],
        out_specs=pl.BlockSpec((Q_BLOCK, NSA_HEADS * HEAD_DIM), lambda b, i: (b * nqb + i, 0)),
        out_shape=jax.ShapeDtypeStruct((bsz * seq, NSA_HEADS * HEAD_DIM), BF16),
        scratch_shapes=[pltpu.VMEM((NSA_KV_HEADS, 2 * seq, LANES), BF16),
                        pltpu.VMEM((NSA_KV_HEADS, 2 * seq, 2 * LANES), BF16),
                        pltpu.VMEM((NSA_KV_HEADS, 2 * seq, LANES), BF16),
                        pltpu.VMEM((NSA_KV_HEADS, 2 * seq, 2 * LANES), BF16),
                        pltpu.VMEM((NSA_KV_HEADS, 2 * nch, LANES), BF16),
                        pltpu.VMEM((NSA_KV_HEADS, 2 * nch, LANES), BF16),
                        pltpu.VMEM((NSA_KV_HEADS, seq // KEY_TILE, Q_BLOCK, KEY_TILE), F32),
                        pltpu.VMEM((Q_BLOCK, 2 * win_keys), F32),
                        pltpu.VMEM((NSA_KV_HEADS, Q_BLOCK, nch), F32),
                        pltpu.VMEM((prow, LANES), F32),
                        pltpu.VMEM((prow, 2 * LANES), F32),
                        pltpu.VMEM((prow, LANES), BF16),
                        pltpu.VMEM((prow, 2 * win_keys), F32),
                        pltpu.VMEM((prow, 2 * win_keys), BF16),
                        pltpu.VMEM((prow, 2 * nch), F32),
                        pltpu.VMEM((prow, 2 * nch), BF16)]
                       + _flash_scratch(n_pairs),
        compiler_params=_cparams(("parallel", "arbitrary")),
        name="nsa",
    )(u16, u32, u32, kc, vc, gq, gk, cover, expand, gexp)


def _softplus(x):
    return jnp.maximum(x, 0.0) + jnp.log1p(jnp.exp(-jnp.abs(x)))


def _ssd_kernel(xbc_ref, z_ref, dt_ref, dtt_ref, cw_ref, cb_ref, dtb_ref, dtbt_ref, al_ref, alt_ref,
                d_ref, ng_ref, o_ref, xcat_ref, xa_ref, y_ref, st_ref):
    c = pl.program_id(1)
    q = SSD_CHUNK
    gn = SSD_GROUPS * SSD_STATE
    tail = 16
    lo_half = lax.broadcasted_iota(jnp.int32, (1, LANES), 1) < SSD_HEAD_DIM

    @pl.when(c == 0)
    def _():
        st_ref[...] = jnp.zeros_like(st_ref)
        xcat_ref[0:q, :] = jnp.zeros((q, SSD_CONV_DIM), BF16)

    @pl.when(c > 0)
    def _():
        xcat_ref[q - tail:q, :] = xcat_ref[2 * q - tail:2 * q, :]

    xcat_ref[q:2 * q, :] = xbc_ref[...]
    ri3 = lax.broadcasted_iota(jnp.int32, ((CONV_WIDTH - 1) * q, 2 * q), 0)
    ci3 = lax.broadcasted_iota(jnp.int32, ((CONV_WIDTH - 1) * q, 2 * q), 1)
    qbits = int(math.log2(q))
    src = q + jnp.bitwise_and(ri3, q - 1) - (jnp.right_shift(ri3, qbits) + 1)
    shifts = jnp.where(ci3 == src, 1.0, 0.0).astype(BF16)
    for cc in range(SSD_CONV_DIM // CONV_CHUNK):
        cs = slice(cc * CONV_CHUNK, (cc + 1) * CONV_CHUNK)
        sh = _dot(shifts, xcat_ref[:, cs])
        acc = cb_ref[:, cs] + cw_ref[CONV_WIDTH - 1:CONV_WIDTH, cs] * xcat_ref[q:2 * q, cs].astype(F32)
        for s in range(1, CONV_WIDTH):
            acc = acc + cw_ref[CONV_WIDTH - 1 - s:CONV_WIDTH - s, cs] * sh[(s - 1) * q:s * q]
        xa_ref[:, cs] = acc * jax.nn.sigmoid(acc)

    dt = _softplus(dt_ref[:, :SSD_HEADS] + dtb_ref[...])
    dtt = _softplus(dtt_ref[0] + dtbt_ref[...])
    a = -jnp.exp(al_ref[...]) * LOG2E
    at = -jnp.exp(alt_ref[...]) * LOG2E
    ri = lax.broadcasted_iota(jnp.int32, (q, q), 0)
    ci = lax.broadcasted_iota(jnp.int32, (q, q), 1)
    tril = ri >= ci
    acum = _dot_hi(jnp.where(tril, 1.0, 0.0), dt * a)
    acumt = _dot_hi(dtt * at, jnp.where(ri <= ci, 1.0, 0.0))
    wrow = jnp.exp2(acumt[:, q - 1:q] - acumt) * dtt

    for g in range(SSD_GROUPS):
        bmat = xa_ref[:, SSD_INNER + g * SSD_STATE:SSD_INNER + (g + 1) * SSD_STATE]
        cmat = xa_ref[:, SSD_INNER + gn + g * SSD_STATE:SSD_INNER + gn + (g + 1) * SSD_STATE].astype(BF16)
        cb = _dot_nt(cmat, bmat.astype(BF16))
        bt = bmat.T
        for pp in range(SSD_HPG // 2):
            j = g * (SSD_HPG // 2) + pp
            ls = slice(j * LANES, (j + 1) * LANES)
            xs = xa_ref[:, ls]
            xbd = jnp.concatenate([jnp.where(lo_half, xs, 0.0), jnp.where(lo_half, 0.0, xs)], axis=0).astype(BF16)
            mm, dec, btw = [], [], []
            for par in range(2):
                h = 2 * j + par
                abc = jnp.broadcast_to(acum[:, h:h + 1], (q, q))
                lmat = jnp.exp2(jnp.where(tril, abc - acumt[h:h + 1, :], -jnp.inf))
                mm.append((cb * lmat * dtt[h:h + 1, :]).astype(BF16))
                dec.append(jnp.exp2(abc))
                btw.append((bt * wrow[h:h + 1, :]).astype(BF16))
            y = _dot(jnp.concatenate(mm, axis=1), xbd)
            y = y + _dot(cmat, st_ref[j].astype(BF16)) * jnp.where(lo_half, dec[0], dec[1])
            y_ref[:, ls] = y + xs * d_ref[:, ls]
            cdec = jnp.where(lo_half, dec[0][q - 1:q, :], dec[1][q - 1:q, :])
            st_ref[j] = st_ref[j] * cdec + _dot(jnp.concatenate(btw, axis=1), xbd)

    z = z_ref[...].astype(F32)
    yz = y_ref[...] * (z * jax.nn.sigmoid(z))
    o_ref[...] = _rms(yz, ng_ref[...]).astype(o_ref.dtype)


def _ssd(u16, u32, dtt, cw, cb, dtb, al, d, ng, bsz, seq):
    nc = seq // SSD_CHUNK
    q = SSD_CHUNK
    full = lambda shape: pl.BlockSpec(shape, lambda b, c: (0,) * len(shape))
    d_full = jnp.repeat(d, SSD_HEAD_DIM).reshape(1, SSD_INNER)
    return pl.pallas_call(
        _ssd_kernel,
        grid=(bsz, nc),
        in_specs=[pl.BlockSpec((q, SSD_CONV_DIM), lambda b, c: (b * nc + c, U16_XBC_BLK)),
                  pl.BlockSpec((q, SSD_INNER), lambda b, c: (b * nc + c, U16_Z_BLK)),
                  pl.BlockSpec((q, 128), lambda b, c: (b * nc + c, U32_DT_BLK128)),
                  pl.BlockSpec((1, SSD_HEADS, q), lambda b, c: (b, 0, c)),
                  full((CONV_WIDTH, SSD_CONV_DIM)), full((1, SSD_CONV_DIM)),
                  full((1, SSD_HEADS)), full((SSD_HEADS, 1)),
                  full((1, SSD_HEADS)), full((SSD_HEADS, 1)),
                  full((1, SSD_INNER)), full((1, SSD_INNER))],
        out_specs=pl.BlockSpec((q, SSD_INNER), lambda b, c: (b * nc + c, 0)),
        out_shape=jax.ShapeDtypeStruct((bsz * seq, SSD_INNER), BF16),
        scratch_shapes=[pltpu.VMEM((2 * q, SSD_CONV_DIM), BF16),
                        pltpu.VMEM((q, SSD_CONV_DIM), F32),
                        pltpu.VMEM((q, SSD_INNER), F32),
                        pltpu.VMEM((SSD_HEADS // 2, SSD_STATE, LANES), F32)],
        compiler_params=_cparams(("parallel", "arbitrary")),
        name="ssd",
    )(u16, u16, u32, dtt, cw, cb, dtb.reshape(1, -1), dtb.reshape(-1, 1), al.reshape(1, -1), al.reshape(-1, 1),
      d_full, ng.reshape(1, -1))


def _merge_kernel(x_ref, ya_ref, yb_ref, yc_ref, mg_ref, wa_ref, wb_ref, wc_ref, wo_ref, o_ref):
    d = D_MODEL
    mg = mg_ref[...].astype(F32)
    mix = jax.nn.sigmoid(mg[:, 0:d]) * _dot(ya_ref[...], wa_ref[...])
    mix = mix + jax.nn.sigmoid(mg[:, d:2 * d]) * _dot(yb_ref[...], wb_ref[...])
    mix = mix + jax.nn.sigmoid(mg[:, 2 * d:3 * d]) * _dot(yc_ref[...], wc_ref[...])
    o_ref[...] = x_ref[...] + _dot(mix.astype(BF16), wo_ref[...])


def _merge(x, ya, yb, yc, u16, wa, wb, wc, wo, tm):
    m, d = x.shape
    full = lambda a: pl.BlockSpec(a.shape, lambda i: (0, 0))
    return pl.pallas_call(
        _merge_kernel,
        grid=(m // tm,),
        in_specs=[pl.BlockSpec((tm, d), lambda i: (i, 0)),
                  pl.BlockSpec((tm, ya.shape[1]), lambda i: (i, 0)),
                  pl.BlockSpec((tm, yb.shape[1]), lambda i: (i, 0)),
                  pl.BlockSpec((tm, yc.shape[1]), lambda i: (i, 0)),
                  pl.BlockSpec((tm, 3 * d), lambda i: (i, U16_MG_BLK)),
                  full(wa), full(wb), full(wc), full(wo)],
        out_specs=pl.BlockSpec((tm, d), lambda i: (i, 0)),
        out_shape=jax.ShapeDtypeStruct((m, d), F32),
        compiler_params=_cparams(("parallel",)),
        name="merge",
    )(x, ya, yb, yc, u16, wa, wb, wc, wo)


def _ffn_kernel(x_ref, g_ref, w1_ref, w2_ref, o_ref, h_ref, acc_ref):
    j = pl.program_id(1)

    @pl.when(j == 0)
    def _():
        h_ref[...] = _rms(x_ref[...], g_ref[...]).astype(BF16)
        acc_ref[...] = x_ref[...]

    a = jnp.maximum(_dot(h_ref[...], w1_ref[...]), 0.0)
    acc_ref[...] += _dot((a * a).astype(BF16), w2_ref[...])

    @pl.when(j == pl.num_programs(1) - 1)
    def _():
        o_ref[...] = acc_ref[...]


def _ffn(x, g, w1, w2, tm, tf):
    m, d = x.shape
    f = w1.shape[1]
    return pl.pallas_call(
        _ffn_kernel,
        grid=(m // tm, f // tf),
        in_specs=[pl.BlockSpec((tm, d), lambda i, j: (i, 0)),
                  pl.BlockSpec((1, d), lambda i, j: (0, 0)),
                  pl.BlockSpec((d, tf), lambda i, j: (0, j)),
                  pl.BlockSpec((tf, d), lambda i, j: (j, 0))],
        out_specs=pl.BlockSpec((tm, d), lambda i, j: (i, 0)),
        out_shape=jax.ShapeDtypeStruct((m, d), F32),
        scratch_shapes=[pltpu.VMEM((tm, d), BF16), pltpu.VMEM((tm, d), F32)],
        compiler_params=_cparams(("parallel", "arbitrary")),
        name="ffn",
    )(x, g, w1, w2)


def _prep_w_in(w):
    offs = [0] + [int(o) for o in np.cumsum(IN_SPLITS)]
    seg = lambda k: w[:, offs[k]:offs[k + 1]]
    dq, dk, dv, iq, ik, iw, nq, nkv, ng, sz, sxbc, sdt, mg = [seg(k) for k in range(13)]
    zeros = lambda n: jnp.zeros((w.shape[0], n), w.dtype)
    w16 = jnp.concatenate([sxbc, mg, sz, dq, nq], axis=1).astype(BF16)
    w32 = jnp.concatenate([nkv[:, 256:768], nkv[:, 0:256], zeros(256),
                           dk, dv, iq, ik, iw, zeros(512 - 424),
                           ng, zeros(128 - ng.shape[1]),
                           sdt, zeros(128 - sdt.shape[1])], axis=1).astype(BF16)
    return w16, w32


def _pick_tile(n, pref):
    t = min(n, pref)
    while n % t:
        t //= 2
    return t


def kernel(x, norm1_g, w_in, dsa_q_norm, dsa_k_norm, nsa_q_norm, nsa_k_norm, nsa_cmp_pos, nsa_cmp_w,
           ssd_conv_w, ssd_conv_b, ssd_dt_bias, ssd_a_log, ssd_d, ssd_norm_g,
           w_br_dsa, w_br_nsa, w_br_ssd, w_out, norm2_g, w_ff1, w_ff2):
    bsz, seq, d = x.shape
    m = bsz * seq
    depth = w_in.shape[0]
    nch = seq // CMP_STRIDE
    n_blk = seq // SEL_BLOCK
    half = CMP_BLOCK // 2

    cmp_start = np.arange(nch) * CMP_STRIDE
    blk_start = np.arange(n_blk) * SEL_BLOCK
    cover = ((cmp_start[:, None] < blk_start[None, :] + SEL_BLOCK)
             & (cmp_start[:, None] + CMP_BLOCK > blk_start[None, :])
             & (np.arange(nch)[:, None] < (seq - CMP_BLOCK) // CMP_STRIDE + 1)).astype(np.float32)
    expand = (np.arange(seq)[None, :] // SEL_BLOCK == np.arange(n_blk)[:, None]).astype(np.float32)
    gexp = np.zeros((3, LANES, NSA_HEADS * HEAD_DIM), np.float32)
    for hh in range(NSA_HEADS):
        for br in range(3):
            gexp[br, 3 * hh + br, hh * HEAD_DIM:(hh + 1) * HEAD_DIM] = 1.0
    gexp = jnp.asarray(gexp, dtype=BF16)
    cover = jnp.asarray(cover)
    expand = jnp.asarray(expand, dtype=BF16)

    tm = _pick_tile(m, 1024)
    xf = x.reshape(m, d)
    for l in range(depth):
        w16, w32 = _prep_w_in(w_in[l])
        g1 = norm1_g[l].reshape(1, d)
        u16 = _norm_matmul(xf, g1, w16, BF16, tm, 2048)
        u32 = _norm_matmul(xf, g1, w32, F32, tm, U32_WIDTH)

        iw0 = U32_DS_BLK512 * 512 + DS_IW
        iwt = u32[:, iw0:iw0 + IDX_HEADS].reshape(bsz, seq, IDX_HEADS).transpose(0, 2, 1)
        ya = _dsa(u16, u32, iwt, dsa_q_norm[l].reshape(1, -1), dsa_k_norm[l].reshape(1, -1), bsz, seq)

        wl = nsa_cmp_w[l]
        zblk = jnp.zeros_like(wl)
        cw = jnp.concatenate([jnp.concatenate([wl, zblk], axis=3),
                              jnp.concatenate([zblk, wl], axis=3)], axis=2).astype(BF16)
        cpos = jnp.concatenate([nsa_cmp_pos[l]] * NSA_KV_HEADS, axis=-1)
        kc, vc = _nsa_compress(u32, cpos, cw, nsa_k_norm[l][0:1], bsz, seq)
        yb = _nsa(u16, u32, kc, vc, nsa_q_norm[l].reshape(1, -1), nsa_k_norm[l], cover, expand, gexp, bsz, seq)

        dt0 = U32_DT_BLK128 * 128
        dtt = u32[:, dt0:dt0 + SSD_HEADS].reshape(bsz, seq, SSD_HEADS).transpose(0, 2, 1)
        yc = _ssd(u16, u32, dtt, ssd_conv_w[l], ssd_conv_b[l].reshape(1, -1), ssd_dt_bias[l], ssd_a_log[l],
                  ssd_d[l], ssd_norm_g[l], bsz, seq)

        xf = _merge(xf, ya, yb, yc, u16,
                    w_br_dsa[l].astype(BF16), w_br_nsa[l].astype(BF16), w_br_ssd[l].astype(BF16),
                    w_out[l].astype(BF16), _pick_tile(m, 256))
        xf = _ffn(xf, norm2_g[l].reshape(1, d), w_ff1[l].astype(BF16), w_ff2[l].astype(BF16),
                  _pick_tile(m, 1024), 1024)
    return xf.reshape(bsz, seq, d)
```

```python
import functools
import math

import numpy as np
import jax
import jax.numpy as jnp
from jax import lax
from jax.experimental import pallas as pl
from jax.experimental.pallas import tpu as pltpu

F32 = jnp.float32
BF16 = jnp.bfloat16

D_MODEL = 1024
HEAD_DIM = 64
DSA_HEADS = 16
DSA_KV_DIM = 64
IDX_HEADS = 8
IDX_DIM = 32
DSA_TOPK = 256
NSA_HEADS = 16
NSA_KV_HEADS = 2
NSA_GROUP = NSA_HEADS // NSA_KV_HEADS
CMP_BLOCK = 32
CMP_STRIDE = 16
SEL_BLOCK = 64
SEL_TOPN = 4
WINDOW = 512
SSD_INNER = 2 * D_MODEL
SSD_HEAD_DIM = 64
SSD_HEADS = SSD_INNER // SSD_HEAD_DIM
SSD_GROUPS = 4
SSD_HPG = SSD_HEADS // SSD_GROUPS
SSD_STATE = 128
SSD_CONV_DIM = SSD_INNER + 2 * SSD_GROUPS * SSD_STATE
CONV_WIDTH = 4
SSD_CHUNK = 128
D_FF = 4 * D_MODEL
Q_BLOCK = 128
DSA_Q_BLOCK = 256
EPS = 1e-6
NEG = -1e30
IN_SPLITS = (DSA_HEADS * HEAD_DIM, DSA_KV_DIM, DSA_KV_DIM, IDX_HEADS * IDX_DIM, IDX_DIM, IDX_HEADS,
             NSA_HEADS * HEAD_DIM, 6 * NSA_KV_HEADS * HEAD_DIM, 3 * NSA_HEADS,
             SSD_INNER, SSD_CONV_DIM, SSD_HEADS, 3 * D_MODEL)

V7X_VMEM_LIMIT_BYTES = 56 * 1024 * 1024
LANES = 128

U16_WIDTH = 10240
U16_XBC_BLK = 0
U16_MG_BLK = 1
U16_Z_BLK = 3
U16_DQ_BLK = 8
U16_NQ_BLK = 9
U32_WIDTH = 1792
U32_NKV_BLK512 = 0
U32_KC_BLK128 = 4
U32_DS_BLK512 = 2
U32_NG_BLK128 = 12
U32_DT_BLK128 = 13
DS_DK, DS_DV, DS_IQ, DS_IK, DS_IW = 0, 64, 128, 384, 416

KEY_TILE = 256
BISECT_ITERS = 24
PAIR_STEP = 2
FLASH_UNROLL = 4
LOG2E = 1.4426950408889634
BISECT_UNROLL = 4
CONV_CHUNK = 256


def _cparams(sem):
    return pltpu.CompilerParams(dimension_semantics=sem, vmem_limit_bytes=V7X_VMEM_LIMIT_BYTES)


def _rms(x, g):
    return x * lax.rsqrt(jnp.mean(x * x, axis=-1, keepdims=True) + EPS) * g


def _dot_nt(a, b):
    return lax.dot_general(a, b, (((1,), (1,)), ((), ())), preferred_element_type=F32)


def _dot(a, b):
    return jnp.dot(a, b, preferred_element_type=F32)


def _dot_hi(a, b):
    return jnp.dot(a, b, preferred_element_type=F32, precision=lax.Precision.HIGHEST)


def _norm_matmul_kernel(x_ref, g_ref, w_ref, o_ref, h_ref):
    @pl.when(pl.program_id(1) == 0)
    def _():
        h_ref[...] = _rms(x_ref[...], g_ref[...]).astype(BF16)

    o_ref[...] = _dot(h_ref[...], w_ref[...]).astype(o_ref.dtype)


def _norm_matmul(x, g, w, out_dtype, tm, tn):
    m, k = x.shape
    n = w.shape[1]
    return pl.pallas_call(
        _norm_matmul_kernel,
        grid=(m // tm, n // tn),
        in_specs=[pl.BlockSpec((tm, k), lambda i, j: (i, 0)),
                  pl.BlockSpec((1, k), lambda i, j: (0, 0)),
                  pl.BlockSpec((k, tn), lambda i, j: (0, j))],
        out_specs=pl.BlockSpec((tm, tn), lambda i, j: (i, j)),
        out_shape=jax.ShapeDtypeStruct((m, n), out_dtype),
        scratch_shapes=[pltpu.VMEM((tm, k), BF16)],
        compiler_params=_cparams(("parallel", "arbitrary")),
        name="norm_matmul",
    )(x, g, w)


def _flash_pairs(qp_ref, kbd_ref, wext_ref, bias_ref, nkt, n_pairs, pairs_per_kv, bufs, m_ref, acc_ref):
    (s0, s1), (p0, p1), (a0, a1) = bufs
    qb = bias_ref.shape[2]
    n_pg = n_pairs // PAIR_STEP
    rows = PAIR_STEP * qb
    n = nkt * n_pg
    lo_half = lax.broadcasted_iota(jnp.int32, (1, LANES), 1) < HEAD_DIM
    m_ref[...] = jnp.full(m_ref.shape, -jnp.inf, F32)
    acc_ref[...] = jnp.zeros(acc_ref.shape, F32)
    p1[...] = jnp.zeros(p1.shape, BF16)
    a1[...] = jnp.ones(a1.shape, F32)

    def where(j):
        kt = j // n_pg
        pg = j % n_pg
        g = (pg * PAIR_STEP) // pairs_per_kv
        return kt, pg, g, pl.multiple_of(kt * 2 * KEY_TILE, 2 * KEY_TILE), pl.multiple_of(pg * rows, rows)

    def qk(j, s_ref):
        kt, _, g, koff, r = where(j)
        s = _dot_nt(qp_ref[pl.ds(r, rows), :], kbd_ref[g, pl.ds(koff, 2 * KEY_TILE), :])
        s_ref[...] = s + jnp.tile(bias_ref[g, kt], (PAIR_STEP, 2))

    def softmax(j, s_ref, p_ref, a_ref):
        _, pg, _, _, _ = where(j)
        for u in range(PAIR_STEP):
            us = slice(u * qb, (u + 1) * qb)
            alpha = []
            for par in range(2):
                cs = slice(par * KEY_TILE, (par + 1) * KEY_TILE)
                hrow = pl.multiple_of(((pg * PAIR_STEP + u) * 2 + par) * qb, qb)
                m_old = m_ref[pl.ds(hrow, qb), :]
                m_new = jnp.maximum(m_old, jnp.max(s_ref[us, cs], axis=-1, keepdims=True))
                alpha.append(jnp.exp2(m_old - m_new))
                m_ref[pl.ds(hrow, qb), :] = m_new
                p_ref[us, cs] = jnp.exp2(
                    s_ref[us, cs] - jnp.concatenate([m_new] * (KEY_TILE // LANES), axis=1)).astype(BF16)
            a_ref[us, :] = jnp.where(lo_half, alpha[0], alpha[1])

    def pv(j, p_ref, a_ref):
        _, _, g, koff, r = where(j)
        a = a_ref[...]
        acc_ref[pl.ds(r, rows), :] = (jnp.concatenate([a, a], axis=1) * acc_ref[pl.ds(r, rows), :]
                                      + _dot(p_ref[...], wext_ref[g, pl.ds(koff, 2 * KEY_TILE), :]))

    qk(0, s0)

    def body(jj, c):
        for j in (FLASH_UNROLL * jj, FLASH_UNROLL * jj + 2):
            qk(j + 1, s1)
            softmax(j, s0, p0, a0)
            pv(jnp.maximum(j - 1, 0), p1, a1)
            qk(jnp.minimum(j + 2, n - 1), s0)
            softmax(j + 1, s1, p1, a1)
            pv(j, p0, a0)
        return c

    lax.fori_loop(0, n // FLASH_UNROLL, body, 0)
    pv(n - 1, p1, a1)


def _flash_scratch(n_pairs, qb):
    step = PAIR_STEP * qb
    return ([pltpu.VMEM((step, 2 * KEY_TILE), F32)] * 2 + [pltpu.VMEM((step, 2 * KEY_TILE), BF16)] * 2
            + [pltpu.VMEM((step, LANES), F32)] * 2
            + [pltpu.VMEM((2 * n_pairs * qb, LANES), F32), pltpu.VMEM((n_pairs * qb, 2 * LANES), F32)])


def _norm_pairs(q_ref, g_ref, qp_ref, n_pairs):
    qb = q_ref.shape[0]
    mult = HEAD_DIM ** -0.5 * LOG2E
    lo_half = lax.broadcasted_iota(jnp.int32, (1, LANES), 1) < HEAD_DIM
    g2 = jnp.concatenate([g_ref[...], g_ref[...]], axis=-1) * mult
    for j in range(n_pairs):
        x = q_ref[:, j * LANES:(j + 1) * LANES].astype(F32)
        x2 = x * x
        s_lo = jnp.sum(jnp.where(lo_half, x2, 0.0), axis=-1, keepdims=True)
        s_hi = jnp.sum(jnp.where(lo_half, 0.0, x2), axis=-1, keepdims=True)
        r = jnp.where(lo_half, lax.rsqrt(s_lo * (1.0 / HEAD_DIM) + EPS), lax.rsqrt(s_hi * (1.0 / HEAD_DIM) + EPS))
        qp_ref[j * qb:(j + 1) * qb, :] = (x * r * g2).astype(BF16)


def _store_pair_kv(kbd_ref, wext_ref, g, row0, n, k, v):
    z = jnp.zeros((n, HEAD_DIM), F32)
    one = jnp.ones((n, HEAD_DIM), F32)
    kbd_ref[g, pl.ds(row0, n), :] = jnp.concatenate([k, z], axis=-1).astype(BF16)
    kbd_ref[g, pl.ds(row0 + n, n), :] = jnp.concatenate([z, k], axis=-1).astype(BF16)
    wext_ref[g, pl.ds(row0, n), :] = jnp.concatenate([v, z, one, z], axis=-1).astype(BF16)
    wext_ref[g, pl.ds(row0 + n, n), :] = jnp.concatenate([z, v, z, one], axis=-1).astype(BF16)


def _dsa_kernel(q_ref, sq_ref, iwt_ref, sk_ref, gq_ref, gk_ref, o_ref,
                kbd_ref, wext_ref, ik_ref, idx_ref, bias_ref, iqs_ref,
                qp_ref, s0_ref, s1_ref, p0_ref, p1_ref, a0_ref, a1_ref, m_ref, acc_ref, *, seq, n_sel):
    i = pl.program_id(1)
    nq = q_ref.shape[0]
    n_pairs = DSA_HEADS // 2
    kt_shape = (KEY_TILE, nq)

    @pl.when(i == 0)
    def _():
        def rows(r, c):
            off = pl.multiple_of(r * KEY_TILE, KEY_TILE)
            sk = sk_ref[pl.ds(off, KEY_TILE), :]
            _store_pair_kv(kbd_ref, wext_ref, 0, pl.multiple_of(2 * off, 2 * KEY_TILE), KEY_TILE,
                           _rms(sk[:, DS_DK:DS_DK + HEAD_DIM], gk_ref[...]), sk[:, DS_DV:DS_DV + HEAD_DIM])
            ik_ref[pl.ds(off, KEY_TILE), :] = sk[:, DS_IK:DS_IK + IDX_DIM].astype(BF16)
            return c

        lax.fori_loop(0, seq // KEY_TILE, rows, 0)

    start = i * nq
    nkt = (start + nq + KEY_TILE - 1) // KEY_TILE
    t_row = (start + lax.broadcasted_iota(jnp.int32, (1, nq), 1)).astype(F32)
    sub_pos = lax.broadcasted_iota(jnp.int32, (KEY_TILE, 1), 0).astype(F32)

    sq = sq_ref[...]
    for h in range(IDX_HEADS):
        iqs_ref[h * nq:(h + 1) * nq, :] = sq[:, DS_IQ + h * IDX_DIM:DS_IQ + (h + 1) * IDX_DIM].astype(BF16)
    iwt = iwt_ref[0]

    def idx_tile(kt, carry):
        rmin, rmax = carry
        off = pl.multiple_of(kt * KEY_TILE, KEY_TILE)
        r = _dot_nt(ik_ref[pl.ds(off, KEY_TILE), :], iqs_ref[...])
        acc = jnp.zeros(kt_shape, F32)
        for h in range(IDX_HEADS):
            acc = acc + iwt[h:h + 1, :] * jnp.maximum(r[:, h * nq:(h + 1) * nq], 0.0)
        valid = (sub_pos + (kt * KEY_TILE).astype(F32)) <= t_row
        idx_ref[kt] = jnp.where(valid, acc, -jnp.inf)
        rmin = jnp.minimum(rmin, jnp.min(jnp.where(valid, acc, jnp.inf), axis=0, keepdims=True))
        rmax = jnp.maximum(rmax, jnp.max(jnp.where(valid, acc, -jnp.inf), axis=0, keepdims=True))
        return rmin, rmax

    npair = (nkt + 1) // 2
    rmin, rmax = lax.fori_loop(0, npair, lambda jj, c: idx_tile(2 * jj + 1, idx_tile(2 * jj, c)),
                               (jnp.full((1, nq), jnp.inf, F32), jnp.full((1, nq), -jnp.inf, F32)))

    kf = float(n_sel)

    def col_sum(x):
        parts = [x[r * 8:(r + 1) * 8] for r in range(x.shape[0] // 8)]
        while len(parts) > 1:
            parts = [parts[k] + parts[k + 1] for k in range(0, len(parts), 2)]
        return jnp.sum(parts[0], axis=0, keepdims=True)

    fold = KEY_TILE // 4

    def count_gt(thr):
        def body(kt, acc):
            hit = jnp.where(idx_ref[kt] > thr, 1.0, 0.0)
            return acc + ((hit[0:fold] + hit[fold:2 * fold]) + (hit[2 * fold:3 * fold] + hit[3 * fold:]))

        return col_sum(lax.fori_loop(0, nkt, body, jnp.zeros((fold, nq), F32)))

    n_valid = t_row + 1.0
    lo0 = rmin - (jnp.abs(rmin) * 0.01 + 1.0)

    def bis_cond(c):
        it, _, _, clo, _ = c
        return jnp.logical_and(it < BISECT_ITERS, jnp.max(clo) > kf)

    def bis_body(c):
        it, lo, hi, clo, chi = c
        for _ in range(BISECT_UNROLL):
            mid = 0.5 * (lo + hi)
            cnt = count_gt(mid)
            ge = cnt >= kf
            lo, hi = jnp.where(ge, mid, lo), jnp.where(ge, hi, mid)
            clo, chi = jnp.where(ge, cnt, clo), jnp.where(ge, chi, cnt)
        return it + BISECT_UNROLL, lo, hi, clo, chi

    _, lo, hi, clo, chi = lax.while_loop(
        bis_cond, bis_body, (jnp.int32(0), lo0, rmax, n_valid, jnp.zeros((1, nq), F32)))

    eye_f = jnp.where(lax.broadcasted_iota(jnp.int32, (nq, nq), 0) == lax.broadcasted_iota(jnp.int32, (nq, nq), 1),
                      1.0, 0.0)
    eye = eye_f.astype(BF16)
    tri = jnp.where(lax.broadcasted_iota(jnp.int32, (KEY_TILE, KEY_TILE), 0)
                    <= lax.broadcasted_iota(jnp.int32, (KEY_TILE, KEY_TILE), 1), 1.0, 0.0).astype(BF16)
    quota_row = jnp.where(clo > kf, kf - chi, float(2 * seq))
    quota = jnp.sum(eye_f * quota_row, axis=1, keepdims=True)

    def mask_tile(kt, carry):
        v = idx_ref[kt]
        code = jnp.where(v > hi, 2.0, jnp.where(v > lo, 1.0, 0.0)).astype(BF16)
        code_t = _dot_nt(eye, code)
        tie_t = jnp.where((code_t > 0.5) & (code_t < 1.5), 1.0, 0.0)
        rank = _dot(tie_t.astype(BF16), tri) + carry
        keep = (code_t > 1.5) | ((tie_t > 0.5) & (rank <= quota))
        bias_ref[0, kt] = jnp.where(keep, 0.0, NEG)
        return carry + jnp.sum(tie_t, axis=1, keepdims=True)

    lax.fori_loop(0, npair, lambda jj, c: mask_tile(2 * jj + 1, mask_tile(2 * jj, c)), jnp.zeros((nq, 1), F32))

    _norm_pairs(q_ref, gq_ref, qp_ref, n_pairs)
    _flash_pairs(qp_ref, kbd_ref, wext_ref, bias_ref, nkt, n_pairs, n_pairs,
                 ((s0_ref, s1_ref), (p0_ref, p1_ref), (a0_ref, a1_ref)), m_ref, acc_ref)
    for j in range(n_pairs):
        acc = acc_ref[j * nq:(j + 1) * nq, :]
        o_ref[:, j * LANES:(j + 1) * LANES] = (acc[:, :LANES] / acc[:, LANES:]).astype(o_ref.dtype)


def _dsa(u16, u32, iwt, gq, gk, bsz, seq):
    n_sel = min(DSA_TOPK, seq // 4)
    nqb = seq // DSA_Q_BLOCK
    n_pairs = DSA_HEADS // 2
    kern = functools.partial(_dsa_kernel, seq=seq, n_sel=n_sel)
    return pl.pallas_call(
        kern,
        grid=(bsz, nqb),
        in_specs=[pl.BlockSpec((DSA_Q_BLOCK, 1024), lambda b, i: (b * nqb + i, U16_DQ_BLK)),
                  pl.BlockSpec((DSA_Q_BLOCK, 512), lambda b, i: (b * nqb + i, U32_DS_BLK512)),
                  pl.BlockSpec((1, IDX_HEADS, DSA_Q_BLOCK), lambda b, i: (b, 0, i)),
                  pl.BlockSpec((seq, 512), lambda b, i: (b, U32_DS_BLK512)),
                  pl.BlockSpec((1, HEAD_DIM), lambda b, i: (0, 0)),
                  pl.BlockSpec((1, DSA_KV_DIM), lambda b, i: (0, 0))],
        out_specs=pl.BlockSpec((DSA_Q_BLOCK, DSA_HEADS * HEAD_DIM), lambda b, i: (b * nqb + i, 0)),
        out_shape=jax.ShapeDtypeStruct((bsz * seq, DSA_HEADS * HEAD_DIM), BF16),
        scratch_shapes=[pltpu.VMEM((1, 2 * seq, LANES), BF16),
                        pltpu.VMEM((1, 2 * seq, 2 * LANES), BF16),
                        pltpu.VMEM((seq, IDX_DIM), BF16),
                        pltpu.VMEM((seq // KEY_TILE, KEY_TILE, DSA_Q_BLOCK), F32),
                        pltpu.VMEM((1, seq // KEY_TILE, DSA_Q_BLOCK, KEY_TILE), F32),
                        pltpu.VMEM((IDX_HEADS * DSA_Q_BLOCK, IDX_DIM), BF16),
                        pltpu.VMEM((n_pairs * DSA_Q_BLOCK, LANES), BF16)]
                       + _flash_scratch(n_pairs, DSA_Q_BLOCK),
        compiler_params=_cparams(("parallel", "arbitrary")),
        name="dsa",
    )(u16, u32, iwt, u32, gq, gk)


def _nsa_compress_kernel(xk_ref, xv_ref, pos_ref, w_ref, g_ref, kc_ref, vc_ref):
    nch = xk_ref.shape[0] // CMP_STRIDE
    for jj, (x_ref, o_ref) in enumerate(((xk_ref, kc_ref), (xv_ref, vc_ref))):
        acc_a = jnp.zeros((nch, LANES), F32)
        acc_b = jnp.zeros((nch, LANES), F32)
        for l in range(CMP_STRIDE):
            x = x_ref[pl.ds(l, nch, stride=CMP_STRIDE), :]
            acc_a = acc_a + _dot((x + pos_ref[jj, l:l + 1, :]).astype(BF16), w_ref[jj, l])
            hi = CMP_STRIDE + l
            acc_b = acc_b + _dot((x + pos_ref[jj, hi:hi + 1, :]).astype(BF16), w_ref[jj, hi])
        out = acc_a + pltpu.roll(acc_b, nch - 1, axis=0)
        for g in range(NSA_KV_HEADS):
            og = out[:, g * HEAD_DIM:(g + 1) * HEAD_DIM]
            o_ref[0, g] = (_rms(og, g_ref[...]) if jj == 0 else og).astype(BF16)


def _nsa_compress(u32, pos, w, g, bsz, seq):
    nch = seq // CMP_STRIDE
    return pl.pallas_call(
        _nsa_compress_kernel,
        grid=(bsz,),
        in_specs=[pl.BlockSpec((seq, LANES), lambda b: (b, U32_KC_BLK128)),
                  pl.BlockSpec((seq, LANES), lambda b: (b, U32_KC_BLK128 + 1)),
                  pl.BlockSpec((2, CMP_BLOCK, LANES), lambda b: (0, 0, 0)),
                  pl.BlockSpec((2, CMP_BLOCK, LANES, LANES), lambda b: (0, 0, 0, 0)),
                  pl.BlockSpec((1, HEAD_DIM), lambda b: (0, 0))],
        out_specs=[pl.BlockSpec((1, NSA_KV_HEADS, nch, HEAD_DIM), lambda b: (b, 0, 0, 0)),
                   pl.BlockSpec((1, NSA_KV_HEADS, nch, HEAD_DIM), lambda b: (b, 0, 0, 0))],
        out_shape=[jax.ShapeDtypeStruct((bsz, NSA_KV_HEADS, nch, HEAD_DIM), BF16),
                   jax.ShapeDtypeStruct((bsz, NSA_KV_HEADS, nch, HEAD_DIM), BF16)],
        compiler_params=_cparams(("parallel",)),
        name="nsa_compress",
    )(u32, u32, pos, w, g)


def _nsa_kernel(q_ref, ng_ref, kv_ref, kc_ref, vc_ref, gq_ref, gk_ref, cover_ref, expand_ref, gexp_ref, o_ref,
                ksbd_ref, wsext_ref, kwbd_ref, wwext_ref, kcbd_ref, vcext_ref, bias_ref, wbias_ref, psum_ref,
                ocmp_ref, accw_ref, qp_ref, sw_ref, pw_ref, sc_ref, pc_ref,
                s0_ref, s1_ref, p0_ref, p1_ref, a0_ref, a1_ref, m_ref, acc_ref, *, seq):
    i = pl.program_id(1)
    nq = Q_BLOCK
    n_pairs = NSA_HEADS // 2
    gpairs = NSA_GROUP // 2
    grows = gpairs * nq
    n_cmp_pad = kc_ref.shape[2]
    n_blk = seq // SEL_BLOCK
    n_sel = min(SEL_TOPN, n_blk)
    win_keys = min(WINDOW + Q_BLOCK, seq)
    win_tiles = win_keys // Q_BLOCK
    wt = Q_BLOCK

    @pl.when(i == 0)
    def _():
        def rows(r, c):
            off = pl.multiple_of(r * KEY_TILE, KEY_TILE)
            kv = kv_ref[pl.ds(off, KEY_TILE), :]
            for g in range(NSA_KV_HEADS):
                c0 = g * HEAD_DIM
                _store_pair_kv(ksbd_ref, wsext_ref, g, pl.multiple_of(2 * off, 2 * KEY_TILE), KEY_TILE,
                               _rms(kv[:, c0:c0 + 64], gk_ref[1:2, :]), kv[:, 128 + c0:128 + c0 + 64])
                kw = _rms(kv[:, 256 + c0:256 + c0 + 64], gk_ref[2:3, :])
                vw = kv[:, 384 + c0:384 + c0 + 64]
                for hf in range(KEY_TILE // wt):
                    _store_pair_kv(kwbd_ref, wwext_ref, g, pl.multiple_of(2 * off + hf * 2 * wt, 2 * wt), wt,
                                   kw[hf * wt:(hf + 1) * wt], vw[hf * wt:(hf + 1) * wt])
            return c

        lax.fori_loop(0, seq // KEY_TILE, rows, 0)
        zc = jnp.zeros((n_cmp_pad, HEAD_DIM), BF16)
        for g in range(NSA_KV_HEADS):
            kcbd_ref[g, 0:n_cmp_pad, :] = jnp.concatenate([kc_ref[0, g], zc], axis=-1)
            kcbd_ref[g, n_cmp_pad:2 * n_cmp_pad, :] = jnp.concatenate([zc, kc_ref[0, g]], axis=-1)
            vcext_ref[g, 0:n_cmp_pad, :] = jnp.concatenate([vc_ref[0, g], zc], axis=-1)
            vcext_ref[g, n_cmp_pad:2 * n_cmp_pad, :] = jnp.concatenate([zc, vc_ref[0, g]], axis=-1)

    start = i * nq
    nkt = (start + nq + KEY_TILE - 1) // KEY_TILE
    t_i = start + lax.broadcasted_iota(jnp.int32, (nq, 1), 0)
    t_col = t_i.astype(F32)
    lane_pos = lax.broadcasted_iota(jnp.int32, (1, KEY_TILE), 1).astype(F32)

    _norm_pairs(q_ref, gq_ref, qp_ref, n_pairs)

    cmp_end = (lax.broadcasted_iota(jnp.int32, (1, n_cmp_pad), 1) * CMP_STRIDE + (CMP_BLOCK - 1)).astype(F32)
    vis = cmp_end <= t_col
    wbase = pl.multiple_of(jnp.maximum(start + nq - win_keys, 0), Q_BLOCK)
    wb2 = pl.multiple_of(2 * wbase, 2 * Q_BLOCK)
    col = lax.broadcasted_iota(jnp.int32, (1, 2 * win_keys), 1)
    wpos = (wbase + jnp.right_shift(col, int(math.log2(2 * wt))) * wt + jnp.bitwise_and(col, wt - 1)).astype(F32)
    wbias_ref[...] = jnp.where((wpos <= t_col) & (wpos > t_col - float(WINDOW)), 0.0, NEG)
    for g in range(NSA_KV_HEADS):
        gs = slice(g * grows, (g + 1) * grows)
        sc_ref[gs, :] = _dot_nt(qp_ref[gs, :], kcbd_ref[g])
        sw_ref[gs, :] = _dot_nt(qp_ref[gs, :], kwbd_ref[g, pl.ds(wb2, 2 * win_keys), :])

    for g in range(NSA_KV_HEADS):
        gs = slice(g * grows, (g + 1) * grows)
        tot = jnp.zeros((nq, n_cmp_pad), F32)
        for j in range(g * gpairs, (g + 1) * gpairs):
            rs = slice(j * nq, (j + 1) * nq)
            for par in range(2):
                cs = slice(par * n_cmp_pad, (par + 1) * n_cmp_pad)
                s = jnp.where(vis, sc_ref[rs, cs], NEG)
                e = jnp.exp2(s - jnp.max(s, axis=-1, keepdims=True))
                p = jnp.where(vis, e / jnp.sum(e, axis=-1, keepdims=True), 0.0)
                pc_ref[rs, cs] = p.astype(BF16)
                tot = tot + p
        psum_ref[g] = tot
        ocmp_ref[gs, :] = _dot(pc_ref[gs, :], vcext_ref[g])

    blk_j = lax.broadcasted_iota(jnp.int32, (1, n_blk), 1)
    cur1 = jnp.right_shift(t_i, int(math.log2(SEL_BLOCK)))
    cur = jnp.concatenate([cur1] * NSA_KV_HEADS, axis=0)
    imp = jnp.concatenate([_dot_hi(psum_ref[g], cover_ref[...]) for g in range(NSA_KV_HEADS)], axis=0)
    forced = (blk_j == cur) | (blk_j == 0)
    imp = jnp.where(forced, jnp.inf, jnp.where(blk_j > cur, -jnp.inf, imp))
    selb = jnp.zeros(imp.shape, jnp.bool_)
    for _ in range(n_sel):
        mx = jnp.max(imp, axis=-1, keepdims=True)
        first = jnp.min(jnp.where(imp == mx, blk_j, n_blk), axis=-1, keepdims=True)
        pick = blk_j == first
        selb = selb | pick
        imp = jnp.where(pick, -jnp.inf, imp)
    selb_bf = jnp.where(selb, 1.0, 0.0).astype(BF16)

    for g in range(NSA_KV_HEADS):
        for j in range(g * gpairs, (g + 1) * gpairs):
            rs = slice(j * nq, (j + 1) * nq)
            for par in range(2):
                cols = [slice(t * 2 * wt + par * wt, t * 2 * wt + (par + 1) * wt) for t in range(win_tiles)]
                s = [sw_ref[rs, cs] + wbias_ref[:, cs] for cs in cols]
                mx = s[0]
                for st in s[1:]:
                    mx = jnp.maximum(mx, st)
                m = jnp.max(mx, axis=-1, keepdims=True)
                for cs, st in zip(cols, s):
                    pw_ref[rs, cs] = jnp.exp2(st - m).astype(BF16)
        accw_ref[g * grows:(g + 1) * grows, :] = _dot(
            pw_ref[g * grows:(g + 1) * grows, :], wwext_ref[g, pl.ds(wb2, 2 * win_keys), :])

    def mask_tile(kt, c):
        off = pl.multiple_of(kt * KEY_TILE, KEY_TILE)
        hit = _dot(selb_bf, expand_ref[:, pl.ds(off, KEY_TILE)])
        ok = (lane_pos + (kt * KEY_TILE).astype(F32)) <= t_col
        for g in range(NSA_KV_HEADS):
            bias_ref[g, kt] = jnp.where((hit[g * nq:(g + 1) * nq] > 0.5) & ok, 0.0, NEG)
        return c

    lax.fori_loop(0, (nkt + 1) // 2, lambda jj, c: mask_tile(2 * jj + 1, mask_tile(2 * jj, c)), 0)
    _flash_pairs(qp_ref, ksbd_ref, wsext_ref, bias_ref, nkt, n_pairs, gpairs,
                 ((s0_ref, s1_ref), (p0_ref, p1_ref), (a0_ref, a1_ref)), m_ref, acc_ref)

    gates = jax.nn.sigmoid(ng_ref[...])
    g_hi = gates.astype(BF16)
    g_lo = (gates - g_hi.astype(F32)).astype(BF16)
    gb = [_dot(g_hi, gexp_ref[b]) + _dot(g_lo, gexp_ref[b]) for b in range(3)]
    for j in range(n_pairs):
        rs = slice(j * nq, (j + 1) * nq)
        ls = slice(j * LANES, (j + 1) * LANES)
        acc = acc_ref[rs, :]
        accw = accw_ref[rs, :]
        out = (gb[0][:, ls] * ocmp_ref[rs, :] + gb[1][:, ls] * (acc[:, :LANES] / acc[:, LANES:])
               + gb[2][:, ls] * (accw[:, :LANES] / accw[:, LANES:]))
        o_ref[:, ls] = out.astype(o_ref.dtype)


def _nsa(u16, u32, kc, vc, gq, gk, cover, expand, gexp, bsz, seq):
    nqb = seq // Q_BLOCK
    nch = kc.shape[2]
    n_blk = seq // SEL_BLOCK
    n_pairs = NSA_HEADS // 2
    prow = n_pairs * Q_BLOCK
    win_keys = min(WINDOW + Q_BLOCK, seq)
    kern = functools.partial(_nsa_kernel, seq=seq)
    return pl.pallas_call(
        kern,
        grid=(bsz, nqb),
        in_specs=[pl.BlockSpec((Q_BLOCK, 1024), lambda b, i: (b * nqb + i, U16_NQ_BLK)),
                  pl.BlockSpec((Q_BLOCK, 128), lambda b, i: (b * nqb + i, U32_NG_BLK128)),
                  pl.BlockSpec((seq, 512), lambda b, i: (b, U32_NKV_BLK512)),
                  pl.BlockSpec((1, NSA_KV_HEADS, nch, HEAD_DIM), lambda b, i: (b, 0, 0, 0)),
                  pl.BlockSpec((1, NSA_KV_HEADS, nch, HEAD_DIM), lambda b, i: (b, 0, 0, 0)),
                  pl.BlockSpec((1, HEAD_DIM), lambda b, i: (0, 0)),
                  pl.BlockSpec((3, HEAD_DIM), lambda b, i: (0, 0)),
                  pl.BlockSpec((nch, n_blk), lambda b, i: (0, 0)),
                  pl.BlockSpec((n_blk, seq), lambda b, i: (0, 0)),
                  pl.BlockSpec((3, LANES, NSA_HEADS * HEAD_DIM), lambda b, i: (0, 0, 0))],
        out_specs=pl.BlockSpec((Q_BLOCK, NSA_HEADS * HEAD_DIM), lambda b, i: (b * nqb + i, 0)),
        out_shape=jax.ShapeDtypeStruct((bsz * seq, NSA_HEADS * HEAD_DIM), BF16),
        scratch_shapes=[pltpu.VMEM((NSA_KV_HEADS, 2 * seq, LANES), BF16),
                        pltpu.VMEM((NSA_KV_HEADS, 2 * seq, 2 * LANES), BF16),
                        pltpu.VMEM((NSA_KV_HEADS, 2 * seq, LANES), BF16),
                        pltpu.VMEM((NSA_KV_HEADS, 2 * seq, 2 * LANES), BF16),
                        pltpu.VMEM((NSA_KV_HEADS, 2 * nch, LANES), BF16),
                        pltpu.VMEM((NSA_KV_HEADS, 2 * nch, LANES), BF16),
                        pltpu.VMEM((NSA_KV_HEADS, seq // KEY_TILE, Q_BLOCK, KEY_TILE), F32),
                        pltpu.VMEM((Q_BLOCK, 2 * win_keys), F32),
                        pltpu.VMEM((NSA_KV_HEADS, Q_BLOCK, nch), F32),
                        pltpu.VMEM((prow, LANES), F32),
                        pltpu.VMEM((prow, 2 * LANES), F32),
                        pltpu.VMEM((prow, LANES), BF16),
                        pltpu.VMEM((prow, 2 * win_keys), F32),
                        pltpu.VMEM((prow, 2 * win_keys), BF16),
                        pltpu.VMEM((prow, 2 * nch), F32),
                        pltpu.VMEM((prow, 2 * nch), BF16)]
                       + _flash_scratch(n_pairs, Q_BLOCK),
        compiler_params=_cparams(("parallel", "arbitrary")),
        name="nsa",
    )(u16, u32, u32, kc, vc, gq, gk, cover, expand, gexp)


def _softplus(x):
    return jnp.maximum(x, 0.0) + jnp.log1p(jnp.exp(-jnp.abs(x)))


def _ssd_kernel(xbc_ref, z_ref, dt_ref, dtt_ref, cw_ref, cb_ref, dtb_ref, dtbt_ref, al_ref, alt_ref,
                d_ref, ng_ref, o_ref, xcat_ref, xa_ref, y_ref, st_ref):
    c = pl.program_id(1)
    q = SSD_CHUNK
    gn = SSD_GROUPS * SSD_STATE
    tail = 16
    lo_half = lax.broadcasted_iota(jnp.int32, (1, LANES), 1) < SSD_HEAD_DIM

    @pl.when(c == 0)
    def _():
        st_ref[...] = jnp.zeros_like(st_ref)
        xcat_ref[0:q, :] = jnp.zeros((q, SSD_CONV_DIM), BF16)

    @pl.when(c > 0)
    def _():
        xcat_ref[q - tail:q, :] = xcat_ref[2 * q - tail:2 * q, :]

    xcat_ref[q:2 * q, :] = xbc_ref[...]
    ri3 = lax.broadcasted_iota(jnp.int32, ((CONV_WIDTH - 1) * q, 2 * q), 0)
    ci3 = lax.broadcasted_iota(jnp.int32, ((CONV_WIDTH - 1) * q, 2 * q), 1)
    qbits = int(math.log2(q))
    src = q + jnp.bitwise_and(ri3, q - 1) - (jnp.right_shift(ri3, qbits) + 1)
    shifts = jnp.where(ci3 == src, 1.0, 0.0).astype(BF16)
    for cc in range(SSD_CONV_DIM // CONV_CHUNK):
        cs = slice(cc * CONV_CHUNK, (cc + 1) * CONV_CHUNK)
        sh = _dot(shifts, xcat_ref[:, cs])
        acc = cb_ref[:, cs] + cw_ref[CONV_WIDTH - 1:CONV_WIDTH, cs] * xcat_ref[q:2 * q, cs].astype(F32)
        for s in range(1, CONV_WIDTH):
            acc = acc + cw_ref[CONV_WIDTH - 1 - s:CONV_WIDTH - s, cs] * sh[(s - 1) * q:s * q]
        xa_ref[:, cs] = acc * jax.nn.sigmoid(acc)

    dt = _softplus(dt_ref[:, :SSD_HEADS] + dtb_ref[...])
    dtt = _softplus(dtt_ref[0] + dtbt_ref[...])
    a = -jnp.exp(al_ref[...]) * LOG2E
    at = -jnp.exp(alt_ref[...]) * LOG2E
    ri = lax.broadcasted_iota(jnp.int32, (q, q), 0)
    ci = lax.broadcasted_iota(jnp.int32, (q, q), 1)
    tril = ri >= ci
    acum = _dot_hi(jnp.where(tril, 1.0, 0.0), dt * a)
    acumt = _dot_hi(dtt * at, jnp.where(ri <= ci, 1.0, 0.0))
    wrow = jnp.exp2(acumt[:, q - 1:q] - acumt) * dtt

    for g in range(SSD_GROUPS):
        bmat = xa_ref[:, SSD_INNER + g * SSD_STATE:SSD_INNER + (g + 1) * SSD_STATE]
        cmat = xa_ref[:, SSD_INNER + gn + g * SSD_STATE:SSD_INNER + gn + (g + 1) * SSD_STATE].astype(BF16)
        cb = _dot_nt(cmat, bmat.astype(BF16))
        bt = bmat.T
        for pp in range(SSD_HPG // 2):
            j = g * (SSD_HPG // 2) + pp
            ls = slice(j * LANES, (j + 1) * LANES)
            xs = xa_ref[:, ls]
            xbd = jnp.concatenate([jnp.where(lo_half, xs, 0.0), jnp.where(lo_half, 0.0, xs)], axis=0).astype(BF16)
            mm, dec, btw = [], [], []
            for par in range(2):
                h = 2 * j + par
                abc = jnp.broadcast_to(acum[:, h:h + 1], (q, q))
                lmat = jnp.exp2(jnp.where(tril, abc - acumt[h:h + 1, :], -jnp.inf))
                mm.append((cb * lmat * dtt[h:h + 1, :]).astype(BF16))
                dec.append(jnp.exp2(abc))
                btw.append((bt * wrow[h:h + 1, :]).astype(BF16))
            y = _dot(jnp.concatenate(mm, axis=1), xbd)
            y = y + _dot(cmat, st_ref[j].astype(BF16)) * jnp.where(lo_half, dec[0], dec[1])
            y_ref[:, ls] = y + xs * d_ref[:, ls]
            cdec = jnp.where(lo_half, dec[0][q - 1:q, :], dec[1][q - 1:q, :])
            st_ref[j] = st_ref[j] * cdec + _dot(jnp.concatenate(btw, axis=1), xbd)

    z = z_ref[...].astype(F32)
    yz = y_ref[...] * (z * jax.nn.sigmoid(z))
    o_ref[...] = _rms(yz, ng_ref[...]).astype(o_ref.dtype)


def _ssd(u16, u32, dtt, cw, cb, dtb, al, d, ng, bsz, seq):
    nc = seq // SSD_CHUNK
    q = SSD_CHUNK
    full = lambda shape: pl.BlockSpec(shape, lambda b, c: (0,) * len(shape))
    d_full = jnp.repeat(d, SSD_HEAD_DIM).reshape(1, SSD_INNER)
    return pl.pallas_call(
        _ssd_kernel,
        grid=(bsz, nc),
        in_specs=[pl.BlockSpec((q, SSD_CONV_DIM), lambda b, c: (b * nc + c, U16_XBC_BLK)),
                  pl.BlockSpec((q, SSD_INNER), lambda b, c: (b * nc + c, U16_Z_BLK)),
                  pl.BlockSpec((q, 128), lambda b, c: (b * nc + c, U32_DT_BLK128)),
                  pl.BlockSpec((1, SSD_HEADS, q), lambda b, c: (b, 0, c)),
                  full((CONV_WIDTH, SSD_CONV_DIM)), full((1, SSD_CONV_DIM)),
                  full((1, SSD_HEADS)), full((SSD_HEADS, 1)),
                  full((1, SSD_HEADS)), full((SSD_HEADS, 1)),
                  full((1, SSD_INNER)), full((1, SSD_INNER))],
        out_specs=pl.BlockSpec((q, SSD_INNER), lambda b, c: (b * nc + c, 0)),
        out_shape=jax.ShapeDtypeStruct((bsz * seq, SSD_INNER), BF16),
        scratch_shapes=[pltpu.VMEM((2 * q, SSD_CONV_DIM), BF16),
                        pltpu.VMEM((q, SSD_CONV_DIM), F32),
                        pltpu.VMEM((q, SSD_INNER), F32),
                        pltpu.VMEM((SSD_HEADS // 2, SSD_STATE, LANES), F32)],
        compiler_params=_cparams(("parallel", "arbitrary")),
        name="ssd",
    )(u16, u16, u32, dtt, cw, cb, dtb.reshape(1, -1), dtb.reshape(-1, 1), al.reshape(1, -1), al.reshape(-1, 1),
      d_full, ng.reshape(1, -1))


def _merge_kernel(x_ref, ya_ref, yb_ref, yc_ref, mg_ref, wa_ref, wb_ref, wc_ref, wo_ref, o_ref):
    d = D_MODEL
    mg = mg_ref[...].astype(F32)
    mix = jax.nn.sigmoid(mg[:, 0:d]) * _dot(ya_ref[...], wa_ref[...])
    mix = mix + jax.nn.sigmoid(mg[:, d:2 * d]) * _dot(yb_ref[...], wb_ref[...])
    mix = mix + jax.nn.sigmoid(mg[:, 2 * d:3 * d]) * _dot(yc_ref[...], wc_ref[...])
    o_ref[...] = x_ref[...] + _dot(mix.astype(BF16), wo_ref[...])


def _merge(x, ya, yb, yc, u16, wa, wb, wc, wo, tm):
    m, d = x.shape
    full = lambda a: pl.BlockSpec(a.shape, lambda i: (0, 0))
    return pl.pallas_call(
        _merge_kernel,
        grid=(m // tm,),
        in_specs=[pl.BlockSpec((tm, d), lambda i: (i, 0)),
                  pl.BlockSpec((tm, ya.shape[1]), lambda i: (i, 0)),
                  pl.BlockSpec((tm, yb.shape[1]), lambda i: (i, 0)),
                  pl.BlockSpec((tm, yc.shape[1]), lambda i: (i, 0)),
                  pl.BlockSpec((tm, 3 * d), lambda i: (i, U16_MG_BLK)),
                  full(wa), full(wb), full(wc), full(wo)],
        out_specs=pl.BlockSpec((tm, d), lambda i: (i, 0)),
        out_shape=jax.ShapeDtypeStruct((m, d), F32),
        compiler_params=_cparams(("parallel",)),
        name="merge",
    )(x, ya, yb, yc, u16, wa, wb, wc, wo)


def _ffn_kernel(x_ref, g_ref, w1_ref, w2_ref, o_ref, h_ref, acc_ref):
    j = pl.program_id(1)

    @pl.when(j == 0)
    def _():
        h_ref[...] = _rms(x_ref[...], g_ref[...]).astype(BF16)
        acc_ref[...] = x_ref[...]

    a = jnp.maximum(_dot(h_ref[...], w1_ref[...]), 0.0)
    acc_ref[...] += _dot((a * a).astype(BF16), w2_ref[...])

    @pl.when(j == pl.num_programs(1) - 1)
    def _():
        o_ref[...] = acc_ref[...]


def _ffn(x, g, w1, w2, tm, tf):
    m, d = x.shape
    f = w1.shape[1]
    return pl.pallas_call(
        _ffn_kernel,
        grid=(m // tm, f // tf),
        in_specs=[pl.BlockSpec((tm, d), lambda i, j: (i, 0)),
                  pl.BlockSpec((1, d), lambda i, j: (0, 0)),
                  pl.BlockSpec((d, tf), lambda i, j: (0, j)),
                  pl.BlockSpec((tf, d), lambda i, j: (j, 0))],
        out_specs=pl.BlockSpec((tm, d), lambda i, j: (i, 0)),
        out_shape=jax.ShapeDtypeStruct((m, d), F32),
        scratch_shapes=[pltpu.VMEM((tm, d), BF16), pltpu.VMEM((tm, d), F32)],
        compiler_params=_cparams(("parallel", "arbitrary")),
        name="ffn",
    )(x, g, w1, w2)


def _prep_w_in(w):
    offs = [0] + [int(o) for o in np.cumsum(IN_SPLITS)]
    seg = lambda k: w[:, offs[k]:offs[k + 1]]
    dq, dk, dv, iq, ik, iw, nq, nkv, ng, sz, sxbc, sdt, mg = [seg(k) for k in range(13)]
    zeros = lambda n: jnp.zeros((w.shape[0], n), w.dtype)
    w16 = jnp.concatenate([sxbc, mg, sz, dq, nq], axis=1).astype(BF16)
    w32 = jnp.concatenate([nkv[:, 256:768], nkv[:, 0:256], zeros(256),
                           dk, dv, iq, ik, iw, zeros(512 - 424),
                           ng, zeros(128 - ng.shape[1]),
                           sdt, zeros(128 - sdt.shape[1])], axis=1).astype(BF16)
    return w16, w32


def _pick_tile(n, pref):
    t = min(n, pref)
    while n % t:
        t //= 2
    return t


def kernel(x, norm1_g, w_in, dsa_q_norm, dsa_k_norm, nsa_q_norm, nsa_k_norm, nsa_cmp_pos, nsa_cmp_w,
           ssd_conv_w, ssd_conv_b, ssd_dt_bias, ssd_a_log, ssd_d, ssd_norm_g,
           w_br_dsa, w_br_nsa, w_br_ssd, w_out, norm2_g, w_ff1, w_ff2):
    bsz, seq, d = x.shape
    m = bsz * seq
    depth = w_in.shape[0]
    nch = seq // CMP_STRIDE
    n_blk = seq // SEL_BLOCK
    half = CMP_BLOCK // 2

    cmp_start = np.arange(nch) * CMP_STRIDE
    blk_start = np.arange(n_blk) * SEL_BLOCK
    cover = ((cmp_start[:, None] < blk_start[None, :] + SEL_BLOCK)
             & (cmp_start[:, None] + CMP_BLOCK > blk_start[None, :])
             & (np.arange(nch)[:, None] < (seq - CMP_BLOCK) // CMP_STRIDE + 1)).astype(np.float32)
    expand = (np.arange(seq)[None, :] // SEL_BLOCK == np.arange(n_blk)[:, None]).astype(np.float32)
    gexp = np.zeros((3, LANES, NSA_HEADS * HEAD_DIM), np.float32)
    for hh in range(NSA_HEADS):
        for br in range(3):
            gexp[br, 3 * hh + br, hh * HEAD_DIM:(hh + 1) * HEAD_DIM] = 1.0
    gexp = jnp.asarray(gexp, dtype=BF16)
    cover = jnp.asarray(cover)
    expand = jnp.asarray(expand, dtype=BF16)

    tm = _pick_tile(m, 1024)
    xf = x.reshape(m, d)
    for l in range(depth):
        w16, w32 = _prep_w_in(w_in[l])
        g1 = norm1_g[l].reshape(1, d)
        u16 = _norm_matmul(xf, g1, w16, BF16, tm, 2048)
        u32 = _norm_matmul(xf, g1, w32, F32, tm, U32_WIDTH)

        iw0 = U32_DS_BLK512 * 512 + DS_IW
        iwt = u32[:, iw0:iw0 + IDX_HEADS].reshape(bsz, seq, IDX_HEADS).transpose(0, 2, 1)
        ya = _dsa(u16, u32, iwt, dsa_q_norm[l].reshape(1, -1), dsa_k_norm[l].reshape(1, -1), bsz, seq)

        wl = nsa_cmp_w[l]
        zblk = jnp.zeros_like(wl)
        cw = jnp.concatenate([jnp.concatenate([wl, zblk], axis=3),
                              jnp.concatenate([zblk, wl], axis=3)], axis=2).astype(BF16)
        cpos = jnp.concatenate([nsa_cmp_pos[l]] * NSA_KV_HEADS, axis=-1)
        kc, vc = _nsa_compress(u32, cpos, cw, nsa_k_norm[l][0:1], bsz, seq)
        yb = _nsa(u16, u32, kc, vc, nsa_q_norm[l].reshape(1, -1), nsa_k_norm[l], cover, expand, gexp, bsz, seq)

        dt0 = U32_DT_BLK128 * 128
        dtt = u32[:, dt0:dt0 + SSD_HEADS].reshape(bsz, seq, SSD_HEADS).transpose(0, 2, 1)
        yc = _ssd(u16, u32, dtt, ssd_conv_w[l], ssd_conv_b[l].reshape(1, -1), ssd_dt_bias[l], ssd_a_log[l],
                  ssd_d[l], ssd_norm_g[l], bsz, seq)

        xf = _merge(xf, ya, yb, yc, u16,
                    w_br_dsa[l].astype(BF16), w_br_nsa[l].astype(BF16), w_br_ssd[l].astype(BF16),
                    w_out[l].astype(BF16), _pick_tile(m, 256))
        xf = _ffn(xf, norm2_g[l].reshape(1, d), w_ff1[l].astype(BF16), w_ff2[l].astype(BF16),
                  _pick_tile(m, 1024), 1024)
    return xf.reshape(bsz, seq, d)
```

```python
import functools
import math

import numpy as np
import jax
import jax.numpy as jnp
from jax import lax
from jax.experimental import pallas as pl
from jax.experimental.pallas import tpu as pltpu

F32 = jnp.float32
BF16 = jnp.bfloat16

D_MODEL = 1024
HEAD_DIM = 64
DSA_HEADS = 16
DSA_KV_DIM = 64
IDX_HEADS = 8
IDX_DIM = 32
DSA_TOPK = 256
NSA_HEADS = 16
NSA_KV_HEADS = 2
NSA_GROUP = NSA_HEADS // NSA_KV_HEADS
CMP_BLOCK = 32
CMP_STRIDE = 16
SEL_BLOCK = 64
SEL_TOPN = 4
WINDOW = 512
SSD_INNER = 2 * D_MODEL
SSD_HEAD_DIM = 64
SSD_HEADS = SSD_INNER // SSD_HEAD_DIM
SSD_GROUPS = 4
SSD_HPG = SSD_HEADS // SSD_GROUPS
SSD_STATE = 128
SSD_CONV_DIM = SSD_INNER + 2 * SSD_GROUPS * SSD_STATE
CONV_WIDTH = 4
SSD_CHUNK = 128
D_FF = 4 * D_MODEL
Q_BLOCK = 128
DSA_Q_BLOCK = 256
NSA_Q_BLOCK = 128
EPS = 1e-6
NEG = -1e30
IN_SPLITS = (DSA_HEADS * HEAD_DIM, DSA_KV_DIM, DSA_KV_DIM, IDX_HEADS * IDX_DIM, IDX_DIM, IDX_HEADS,
             NSA_HEADS * HEAD_DIM, 6 * NSA_KV_HEADS * HEAD_DIM, 3 * NSA_HEADS,
             SSD_INNER, SSD_CONV_DIM, SSD_HEADS, 3 * D_MODEL)

V7X_VMEM_LIMIT_BYTES = 56 * 1024 * 1024
LANES = 128

U16_WIDTH = 10240
U16_XBC_BLK = 0
U16_MG_BLK = 1
U16_Z_BLK = 3
U16_DQ_BLK = 8
U16_NQ_BLK = 9
U32_WIDTH = 1792
U32_NKV_BLK512 = 0
U32_KC_BLK128 = 4
U32_DS_BLK512 = 2
U32_NG_BLK128 = 12
U32_DT_BLK128 = 13
DS_DK, DS_DV, DS_IQ, DS_IK, DS_IW = 0, 64, 128, 384, 416

KEY_TILE = 256
BISECT_ITERS = 24
PAIR_STEP = 2
FLASH_UNROLL = 4
LOG2E = 1.4426950408889634
BISECT_UNROLL = 4
CONV_CHUNK = 256


def _cparams(sem):
    return pltpu.CompilerParams(dimension_semantics=sem, vmem_limit_bytes=V7X_VMEM_LIMIT_BYTES)


def _rms(x, g):
    return x * lax.rsqrt(jnp.mean(x * x, axis=-1, keepdims=True) + EPS) * g


def _dot_nt(a, b):
    return lax.dot_general(a, b, (((1,), (1,)), ((), ())), preferred_element_type=F32)


def _dot(a, b):
    return jnp.dot(a, b, preferred_element_type=F32)


def _dot_hi(a, b):
    return jnp.dot(a, b, preferred_element_type=F32, precision=lax.Precision.HIGHEST)


def _norm_matmul_kernel(x_ref, g_ref, w_ref, o_ref, h_ref):
    @pl.when(pl.program_id(1) == 0)
    def _():
        h_ref[...] = _rms(x_ref[...], g_ref[...]).astype(BF16)

    o_ref[...] = _dot(h_ref[...], w_ref[...]).astype(o_ref.dtype)


def _norm_matmul(x, g, w, out_dtype, tm, tn):
    m, k = x.shape
    n = w.shape[1]
    return pl.pallas_call(
        _norm_matmul_kernel,
        grid=(m // tm, n // tn),
        in_specs=[pl.BlockSpec((tm, k), lambda i, j: (i, 0)),
                  pl.BlockSpec((1, k), lambda i, j: (0, 0)),
                  pl.BlockSpec((k, tn), lambda i, j: (0, j))],
        out_specs=pl.BlockSpec((tm, tn), lambda i, j: (i, j)),
        out_shape=jax.ShapeDtypeStruct((m, n), out_dtype),
        scratch_shapes=[pltpu.VMEM((tm, k), BF16)],
        compiler_params=_cparams(("parallel", "arbitrary")),
        name="norm_matmul",
    )(x, g, w)


def _flash_pairs(qp_ref, kbd_ref, wext_ref, bias_ref, nkt, n_pairs, pairs_per_kv, bufs, m_ref, acc_ref):
    (s0, s1), (p0, p1), (a0, a1) = bufs
    qb = bias_ref.shape[2]
    n_pg = n_pairs // PAIR_STEP
    rows = PAIR_STEP * qb
    n = nkt * n_pg
    lo_half = lax.broadcasted_iota(jnp.int32, (1, LANES), 1) < HEAD_DIM
    m_ref[...] = jnp.full(m_ref.shape, -jnp.inf, F32)
    acc_ref[...] = jnp.zeros(acc_ref.shape, F32)
    p1[...] = jnp.zeros(p1.shape, BF16)
    a1[...] = jnp.ones(a1.shape, F32)

    def where(j):
        kt = j // n_pg
        pg = j % n_pg
        g = (pg * PAIR_STEP) // pairs_per_kv
        return kt, pg, g, pl.multiple_of(kt * 2 * KEY_TILE, 2 * KEY_TILE), pl.multiple_of(pg * rows, rows)

    def qk(j, s_ref):
        kt, _, g, koff, r = where(j)
        s = _dot_nt(qp_ref[pl.ds(r, rows), :], kbd_ref[g, pl.ds(koff, 2 * KEY_TILE), :])
        s_ref[...] = s + jnp.tile(bias_ref[g, kt], (PAIR_STEP, 2))

    def softmax(j, s_ref, p_ref, a_ref):
        _, pg, _, _, _ = where(j)
        for u in range(PAIR_STEP):
            us = slice(u * qb, (u + 1) * qb)
            alpha = []
            for par in range(2):
                cs = slice(par * KEY_TILE, (par + 1) * KEY_TILE)
                hrow = pl.multiple_of(((pg * PAIR_STEP + u) * 2 + par) * qb, qb)
                m_old = m_ref[pl.ds(hrow, qb), :]
                m_new = jnp.maximum(m_old, jnp.max(s_ref[us, cs], axis=-1, keepdims=True))
                alpha.append(jnp.exp2(m_old - m_new))
                m_ref[pl.ds(hrow, qb), :] = m_new
                p_ref[us, cs] = jnp.exp2(
                    s_ref[us, cs] - jnp.concatenate([m_new] * (KEY_TILE // LANES), axis=1)).astype(BF16)
            a_ref[us, :] = jnp.where(lo_half, alpha[0], alpha[1])

    def pv(j, p_ref, a_ref):
        _, _, g, koff, r = where(j)
        a = a_ref[...]
        acc_ref[pl.ds(r, rows), :] = (jnp.concatenate([a, a], axis=1) * acc_ref[pl.ds(r, rows), :]
                                      + _dot(p_ref[...], wext_ref[g, pl.ds(koff, 2 * KEY_TILE), :]))

    qk(0, s0)

    def body(jj, c):
        for j in (FLASH_UNROLL * jj, FLASH_UNROLL * jj + 2):
            qk(j + 1, s1)
            softmax(j, s0, p0, a0)
            pv(jnp.maximum(j - 1, 0), p1, a1)
            qk(jnp.minimum(j + 2, n - 1), s0)
            softmax(j + 1, s1, p1, a1)
            pv(j, p0, a0)
        return c

    lax.fori_loop(0, n // FLASH_UNROLL, body, 0)
    pv(n - 1, p1, a1)


def _flash_scratch(n_pairs, qb):
    step = PAIR_STEP * qb
    return ([pltpu.VMEM((step, 2 * KEY_TILE), F32)] * 2 + [pltpu.VMEM((step, 2 * KEY_TILE), BF16)] * 2
            + [pltpu.VMEM((step, LANES), F32)] * 2
            + [pltpu.VMEM((2 * n_pairs * qb, LANES), F32), pltpu.VMEM((n_pairs * qb, 2 * LANES), F32)])


def _norm_pairs(q_ref, g_ref, qp_ref, n_pairs):
    qb = q_ref.shape[0]
    mult = HEAD_DIM ** -0.5 * LOG2E
    lo_half = lax.broadcasted_iota(jnp.int32, (1, LANES), 1) < HEAD_DIM
    g2 = jnp.concatenate([g_ref[...], g_ref[...]], axis=-1) * mult
    for j in range(n_pairs):
        x = q_ref[:, j * LANES:(j + 1) * LANES].astype(F32)
        x2 = x * x
        s_lo = jnp.sum(jnp.where(lo_half, x2, 0.0), axis=-1, keepdims=True)
        s_hi = jnp.sum(jnp.where(lo_half, 0.0, x2), axis=-1, keepdims=True)
        r = jnp.where(lo_half, lax.rsqrt(s_lo * (1.0 / HEAD_DIM) + EPS), lax.rsqrt(s_hi * (1.0 / HEAD_DIM) + EPS))
        qp_ref[j * qb:(j + 1) * qb, :] = (x * r * g2).astype(BF16)


def _store_pair_kv(kbd_ref, wext_ref, g, row0, n, k, v):
    z = jnp.zeros((n, HEAD_DIM), F32)
    one = jnp.ones((n, HEAD_DIM), F32)
    kbd_ref[g, pl.ds(row0, n), :] = jnp.concatenate([k, z], axis=-1).astype(BF16)
    kbd_ref[g, pl.ds(row0 + n, n), :] = jnp.concatenate([z, k], axis=-1).astype(BF16)
    wext_ref[g, pl.ds(row0, n), :] = jnp.concatenate([v, z, one, z], axis=-1).astype(BF16)
    wext_ref[g, pl.ds(row0 + n, n), :] = jnp.concatenate([z, v, z, one], axis=-1).astype(BF16)


def _dsa_kernel(q_ref, sq_ref, iwt_ref, sk_ref, gq_ref, gk_ref, o_ref,
                kbd_ref, wext_ref, ik_ref, idx_ref, bias_ref, iqs_ref,
                qp_ref, s0_ref, s1_ref, p0_ref, p1_ref, a0_ref, a1_ref, m_ref, acc_ref, *, seq, n_sel):
    i = pl.program_id(1)
    nq = q_ref.shape[0]
    n_pairs = DSA_HEADS // 2
    kt_shape = (KEY_TILE, nq)

    @pl.when(i == 0)
    def _():
        def rows(r, c):
            off = pl.multiple_of(r * KEY_TILE, KEY_TILE)
            sk = sk_ref[pl.ds(off, KEY_TILE), :]
            _store_pair_kv(kbd_ref, wext_ref, 0, pl.multiple_of(2 * off, 2 * KEY_TILE), KEY_TILE,
                           _rms(sk[:, DS_DK:DS_DK + HEAD_DIM], gk_ref[...]), sk[:, DS_DV:DS_DV + HEAD_DIM])
            ik_ref[pl.ds(off, KEY_TILE), :] = sk[:, DS_IK:DS_IK + IDX_DIM].astype(BF16)
            return c

        lax.fori_loop(0, seq // KEY_TILE, rows, 0)

    start = i * nq
    nkt = (start + nq + KEY_TILE - 1) // KEY_TILE
    t_row = (start + lax.broadcasted_iota(jnp.int32, (1, nq), 1)).astype(F32)
    sub_pos = lax.broadcasted_iota(jnp.int32, (KEY_TILE, 1), 0).astype(F32)

    sq = sq_ref[...]
    for h in range(IDX_HEADS):
        iqs_ref[h * nq:(h + 1) * nq, :] = sq[:, DS_IQ + h * IDX_DIM:DS_IQ + (h + 1) * IDX_DIM].astype(BF16)
    iwt = iwt_ref[0]

    def idx_tile(kt, carry):
        rmin, rmax = carry
        off = pl.multiple_of(kt * KEY_TILE, KEY_TILE)
        r = _dot_nt(ik_ref[pl.ds(off, KEY_TILE), :], iqs_ref[...])
        acc = jnp.zeros(kt_shape, F32)
        for h in range(IDX_HEADS):
            acc = acc + iwt[h:h + 1, :] * jnp.maximum(r[:, h * nq:(h + 1) * nq], 0.0)
        valid = (sub_pos + (kt * KEY_TILE).astype(F32)) <= t_row
        idx_ref[kt] = jnp.where(valid, acc, -jnp.inf)
        rmin = jnp.minimum(rmin, jnp.min(jnp.where(valid, acc, jnp.inf), axis=0, keepdims=True))
        rmax = jnp.maximum(rmax, jnp.max(jnp.where(valid, acc, -jnp.inf), axis=0, keepdims=True))
        return rmin, rmax

    npair = (nkt + 1) // 2
    rmin, rmax = lax.fori_loop(0, npair, lambda jj, c: idx_tile(2 * jj + 1, idx_tile(2 * jj, c)),
                               (jnp.full((1, nq), jnp.inf, F32), jnp.full((1, nq), -jnp.inf, F32)))

    kf = float(n_sel)

    def col_sum(x):
        parts = [x[r * 8:(r + 1) * 8] for r in range(x.shape[0] // 8)]
        while len(parts) > 1:
            parts = [parts[k] + parts[k + 1] for k in range(0, len(parts), 2)]
        return jnp.sum(parts[0], axis=0, keepdims=True)

    fold = KEY_TILE // 4

    def count_gt(thr):
        def body(kt, acc):
            hit = jnp.where(idx_ref[kt] > thr, 1.0, 0.0)
            return acc + ((hit[0:fold] + hit[fold:2 * fold]) + (hit[2 * fold:3 * fold] + hit[3 * fold:]))

        return col_sum(lax.fori_loop(0, nkt, body, jnp.zeros((fold, nq), F32)))

    n_valid = t_row + 1.0
    lo0 = rmin - (jnp.abs(rmin) * 0.01 + 1.0)

    def bis_cond(c):
        it, _, _, clo, _ = c
        return jnp.logical_and(it < BISECT_ITERS, jnp.max(clo) > kf)

    def bis_body(c):
        it, lo, hi, clo, chi = c
        for _ in range(BISECT_UNROLL):
            mid = 0.5 * (lo + hi)
            cnt = count_gt(mid)
            ge = cnt >= kf
            lo, hi = jnp.where(ge, mid, lo), jnp.where(ge, hi, mid)
            clo, chi = jnp.where(ge, cnt, clo), jnp.where(ge, chi, cnt)
        return it + BISECT_UNROLL, lo, hi, clo, chi

    _, lo, hi, clo, chi = lax.while_loop(
        bis_cond, bis_body, (jnp.int32(0), lo0, rmax, n_valid, jnp.zeros((1, nq), F32)))

    eye_f = jnp.where(lax.broadcasted_iota(jnp.int32, (nq, nq), 0) == lax.broadcasted_iota(jnp.int32, (nq, nq), 1),
                      1.0, 0.0)
    eye = eye_f.astype(BF16)
    tri = jnp.where(lax.broadcasted_iota(jnp.int32, (KEY_TILE, KEY_TILE), 0)
                    <= lax.broadcasted_iota(jnp.int32, (KEY_TILE, KEY_TILE), 1), 1.0, 0.0).astype(BF16)
    quota_row = jnp.where(clo > kf, kf - chi, float(2 * seq))
    quota = jnp.sum(eye_f * quota_row, axis=1, keepdims=True)

    def mask_tile(kt, carry):
        v = idx_ref[kt]
        code = jnp.where(v > hi, 2.0, jnp.where(v > lo, 1.0, 0.0)).astype(BF16)
        code_t = _dot_nt(eye, code)
        tie_t = jnp.where((code_t > 0.5) & (code_t < 1.5), 1.0, 0.0)
        rank = _dot(tie_t.astype(BF16), tri) + carry
        keep = (code_t > 1.5) | ((tie_t > 0.5) & (rank <= quota))
        bias_ref[0, kt] = jnp.where(keep, 0.0, NEG)
        return carry + jnp.sum(tie_t, axis=1, keepdims=True)

    lax.fori_loop(0, npair, lambda jj, c: mask_tile(2 * jj + 1, mask_tile(2 * jj, c)), jnp.zeros((nq, 1), F32))

    _norm_pairs(q_ref, gq_ref, qp_ref, n_pairs)
    _flash_pairs(qp_ref, kbd_ref, wext_ref, bias_ref, nkt, n_pairs, n_pairs,
                 ((s0_ref, s1_ref), (p0_ref, p1_ref), (a0_ref, a1_ref)), m_ref, acc_ref)
    for j in range(n_pairs):
        acc = acc_ref[j * nq:(j + 1) * nq, :]
        o_ref[:, j * LANES:(j + 1) * LANES] = (acc[:, :LANES] / acc[:, LANES:]).astype(o_ref.dtype)


def _dsa(u16, u32, iwt, gq, gk, bsz, seq):
    n_sel = min(DSA_TOPK, seq // 4)
    nqb = seq // DSA_Q_BLOCK
    n_pairs = DSA_HEADS // 2
    kern = functools.partial(_dsa_kernel, seq=seq, n_sel=n_sel)
    return pl.pallas_call(
        kern,
        grid=(bsz, nqb),
        in_specs=[pl.BlockSpec((DSA_Q_BLOCK, 1024), lambda b, i: (b * nqb + i, U16_DQ_BLK)),
                  pl.BlockSpec((DSA_Q_BLOCK, 512), lambda b, i: (b * nqb + i, U32_DS_BLK512)),
                  pl.BlockSpec((1, IDX_HEADS, DSA_Q_BLOCK), lambda b, i: (b, 0, i)),
                  pl.BlockSpec((seq, 512), lambda b, i: (b, U32_DS_BLK512)),
                  pl.BlockSpec((1, HEAD_DIM), lambda b, i: (0, 0)),
                  pl.BlockSpec((1, DSA_KV_DIM), lambda b, i: (0, 0))],
        out_specs=pl.BlockSpec((DSA_Q_BLOCK, DSA_HEADS * HEAD_DIM), lambda b, i: (b * nqb + i, 0)),
        out_shape=jax.ShapeDtypeStruct((bsz * seq, DSA_HEADS * HEAD_DIM), BF16),
        scratch_shapes=[pltpu.VMEM((1, 2 * seq, LANES), BF16),
                        pltpu.VMEM((1, 2 * seq, 2 * LANES), BF16),
                        pltpu.VMEM((seq, IDX_DIM), BF16),
                        pltpu.VMEM((seq // KEY_TILE, KEY_TILE, DSA_Q_BLOCK), F32),
                        pltpu.VMEM((1, seq // KEY_TILE, DSA_Q_BLOCK, KEY_TILE), F32),
                        pltpu.VMEM((IDX_HEADS * DSA_Q_BLOCK, IDX_DIM), BF16),
                        pltpu.VMEM((n_pairs * DSA_Q_BLOCK, LANES), BF16)]
                       + _flash_scratch(n_pairs, DSA_Q_BLOCK),
        compiler_params=_cparams(("parallel", "arbitrary")),
        name="dsa",
    )(u16, u32, iwt, u32, gq, gk)


def _nsa_compress_kernel(xk_ref, xv_ref, pos_ref, w_ref, g_ref, kc_ref, vc_ref):
    nch = xk_ref.shape[0] // CMP_STRIDE
    for jj, (x_ref, o_ref) in enumerate(((xk_ref, kc_ref), (xv_ref, vc_ref))):
        acc_a = jnp.zeros((nch, LANES), F32)
        acc_b = jnp.zeros((nch, LANES), F32)
        for l in range(CMP_STRIDE):
            x = x_ref[pl.ds(l, nch, stride=CMP_STRIDE), :]
            acc_a = acc_a + _dot((x + pos_ref[jj, l:l + 1, :]).astype(BF16), w_ref[jj, l])
            hi = CMP_STRIDE + l
            acc_b = acc_b + _dot((x + pos_ref[jj, hi:hi + 1, :]).astype(BF16), w_ref[jj, hi])
        out = acc_a + pltpu.roll(acc_b, nch - 1, axis=0)
        for g in range(NSA_KV_HEADS):
            og = out[:, g * HEAD_DIM:(g + 1) * HEAD_DIM]
            o_ref[0, g] = (_rms(og, g_ref[...]) if jj == 0 else og).astype(BF16)


def _nsa_compress(u32, pos, w, g, bsz, seq):
    nch = seq // CMP_STRIDE
    return pl.pallas_call(
        _nsa_compress_kernel,
        grid=(bsz,),
        in_specs=[pl.BlockSpec((seq, LANES), lambda b: (b, U32_KC_BLK128)),
                  pl.BlockSpec((seq, LANES), lambda b: (b, U32_KC_BLK128 + 1)),
                  pl.BlockSpec((2, CMP_BLOCK, LANES), lambda b: (0, 0, 0)),
                  pl.BlockSpec((2, CMP_BLOCK, LANES, LANES), lambda b: (0, 0, 0, 0)),
                  pl.BlockSpec((1, HEAD_DIM), lambda b: (0, 0))],
        out_specs=[pl.BlockSpec((1, NSA_KV_HEADS, nch, HEAD_DIM), lambda b: (b, 0, 0, 0)),
                   pl.BlockSpec((1, NSA_KV_HEADS, nch, HEAD_DIM), lambda b: (b, 0, 0, 0))],
        out_shape=[jax.ShapeDtypeStruct((bsz, NSA_KV_HEADS, nch, HEAD_DIM), BF16),
                   jax.ShapeDtypeStruct((bsz, NSA_KV_HEADS, nch, HEAD_DIM), BF16)],
        compiler_params=_cparams(("parallel",)),
        name="nsa_compress",
    )(u32, u32, pos, w, g)


def _nsa_kernel(q_ref, ng_ref, kv_ref, kc_ref, vc_ref, gq_ref, gk_ref, cover_ref, expand_ref, gexp_ref, o_ref,
                ksbd_ref, wsext_ref, kwbd_ref, wwext_ref, kcbd_ref, vcext_ref, bias_ref, wbias_ref, psum_ref,
                ocmp_ref, accw_ref, qp_ref, sw_ref, pw_ref, sc_ref, pc_ref,
                s0_ref, s1_ref, p0_ref, p1_ref, a0_ref, a1_ref, m_ref, acc_ref, *, seq):
    i = pl.program_id(1)
    nq = q_ref.shape[0]
    n_pairs = NSA_HEADS // 2
    gpairs = NSA_GROUP // 2
    grows = gpairs * nq
    n_cmp_pad = kc_ref.shape[2]
    n_blk = seq // SEL_BLOCK
    n_sel = min(SEL_TOPN, n_blk)
    win_keys = min(WINDOW + nq, seq)
    wt = LANES
    win_tiles = win_keys // wt

    @pl.when(i == 0)
    def _():
        def rows(r, c):
            off = pl.multiple_of(r * KEY_TILE, KEY_TILE)
            kv = kv_ref[pl.ds(off, KEY_TILE), :]
            for g in range(NSA_KV_HEADS):
                c0 = g * HEAD_DIM
                _store_pair_kv(ksbd_ref, wsext_ref, g, pl.multiple_of(2 * off, 2 * KEY_TILE), KEY_TILE,
                               _rms(kv[:, c0:c0 + 64], gk_ref[1:2, :]), kv[:, 128 + c0:128 + c0 + 64])
                kw = _rms(kv[:, 256 + c0:256 + c0 + 64], gk_ref[2:3, :])
                vw = kv[:, 384 + c0:384 + c0 + 64]
                for hf in range(KEY_TILE // wt):
                    _store_pair_kv(kwbd_ref, wwext_ref, g, pl.multiple_of(2 * off + hf * 2 * wt, 2 * wt), wt,
                                   kw[hf * wt:(hf + 1) * wt], vw[hf * wt:(hf + 1) * wt])
            return c

        lax.fori_loop(0, seq // KEY_TILE, rows, 0)
        zc = jnp.zeros((n_cmp_pad, HEAD_DIM), BF16)
        for g in range(NSA_KV_HEADS):
            kcbd_ref[g, 0:n_cmp_pad, :] = jnp.concatenate([kc_ref[0, g], zc], axis=-1)
            kcbd_ref[g, n_cmp_pad:2 * n_cmp_pad, :] = jnp.concatenate([zc, kc_ref[0, g]], axis=-1)
            vcext_ref[g, 0:n_cmp_pad, :] = jnp.concatenate([vc_ref[0, g], zc], axis=-1)
            vcext_ref[g, n_cmp_pad:2 * n_cmp_pad, :] = jnp.concatenate([zc, vc_ref[0, g]], axis=-1)

    start = i * nq
    nkt = (start + nq + KEY_TILE - 1) // KEY_TILE
    t_i = start + lax.broadcasted_iota(jnp.int32, (nq, 1), 0)
    t_col = t_i.astype(F32)
    lane_pos = lax.broadcasted_iota(jnp.int32, (1, KEY_TILE), 1).astype(F32)

    _norm_pairs(q_ref, gq_ref, qp_ref, n_pairs)

    cmp_end = (lax.broadcasted_iota(jnp.int32, (1, n_cmp_pad), 1) * CMP_STRIDE + (CMP_BLOCK - 1)).astype(F32)
    vis = cmp_end <= t_col
    wbase = pl.multiple_of(jnp.maximum(start + nq - win_keys, 0), wt)
    wb2 = pl.multiple_of(2 * wbase, 2 * wt)
    col = lax.broadcasted_iota(jnp.int32, (1, 2 * win_keys), 1)
    wpos = (wbase + jnp.right_shift(col, int(math.log2(2 * wt))) * wt + jnp.bitwise_and(col, wt - 1)).astype(F32)
    wbias_ref[...] = jnp.where((wpos <= t_col) & (wpos > t_col - float(WINDOW)), 0.0, NEG)
    for g in range(NSA_KV_HEADS):
        gs = slice(g * grows, (g + 1) * grows)
        sc_ref[gs, :] = _dot_nt(qp_ref[gs, :], kcbd_ref[g])
        sw_ref[gs, :] = _dot_nt(qp_ref[gs, :], kwbd_ref[g, pl.ds(wb2, 2 * win_keys), :])

    for g in range(NSA_KV_HEADS):
        gs = slice(g * grows, (g + 1) * grows)
        tot = jnp.zeros((nq, n_cmp_pad), F32)
        for j in range(g * gpairs, (g + 1) * gpairs):
            rs = slice(j * nq, (j + 1) * nq)
            for par in range(2):
                cs = slice(par * n_cmp_pad, (par + 1) * n_cmp_pad)
                s = jnp.where(vis, sc_ref[rs, cs], NEG)
                e = jnp.exp2(s - jnp.max(s, axis=-1, keepdims=True))
                p = jnp.where(vis, e / jnp.sum(e, axis=-1, keepdims=True), 0.0)
                pc_ref[rs, cs] = p.astype(BF16)
                tot = tot + p
        psum_ref[g] = tot
        ocmp_ref[gs, :] = _dot(pc_ref[gs, :], vcext_ref[g])

    blk_j = lax.broadcasted_iota(jnp.int32, (1, n_blk), 1)
    cur1 = jnp.right_shift(t_i, int(math.log2(SEL_BLOCK)))
    cur = jnp.concatenate([cur1] * NSA_KV_HEADS, axis=0)
    imp = jnp.concatenate([_dot_hi(psum_ref[g], cover_ref[...]) for g in range(NSA_KV_HEADS)], axis=0)
    forced = (blk_j == cur) | (blk_j == 0)
    imp = jnp.where(forced, jnp.inf, jnp.where(blk_j > cur, -jnp.inf, imp))
    selb = jnp.zeros(imp.shape, jnp.bool_)
    for _ in range(n_sel):
        mx = jnp.max(imp, axis=-1, keepdims=True)
        first = jnp.min(jnp.where(imp == mx, blk_j, n_blk), axis=-1, keepdims=True)
        pick = blk_j == first
        selb = selb | pick
        imp = jnp.where(pick, -jnp.inf, imp)
    selb_bf = jnp.where(selb, 1.0, 0.0).astype(BF16)

    for g in range(NSA_KV_HEADS):
        gs = slice(g * grows, (g + 1) * grows)
        for j in range(g * gpairs, (g + 1) * gpairs):
            rs = slice(j * nq, (j + 1) * nq)
            for par in range(2):
                cols = [slice(t * 2 * wt + par * wt, t * 2 * wt + (par + 1) * wt) for t in range(win_tiles)]
                s = [sw_ref[rs, cs] + wbias_ref[:, cs] for cs in cols]
                mx = s[0]
                for st in s[1:]:
                    mx = jnp.maximum(mx, st)
                m = jnp.max(mx, axis=-1, keepdims=True)
                for cs, st in zip(cols, s):
                    pw_ref[rs, cs] = jnp.exp2(st - m).astype(BF16)
        accw_ref[gs, :] = _dot(pw_ref[gs, :], wwext_ref[g, pl.ds(wb2, 2 * win_keys), :])

    def mask_tile(kt, c):
        off = pl.multiple_of(kt * KEY_TILE, KEY_TILE)
        hit = _dot(selb_bf, expand_ref[:, pl.ds(off, KEY_TILE)])
        ok = (lane_pos + (kt * KEY_TILE).astype(F32)) <= t_col
        for g in range(NSA_KV_HEADS):
            bias_ref[g, kt] = jnp.where((hit[g * nq:(g + 1) * nq] > 0.5) & ok, 0.0, NEG)
        return c

    lax.fori_loop(0, (nkt + 1) // 2, lambda jj, c: mask_tile(2 * jj + 1, mask_tile(2 * jj, c)), 0)
    _flash_pairs(qp_ref, ksbd_ref, wsext_ref, bias_ref, nkt, n_pairs, gpairs,
                 ((s0_ref, s1_ref), (p0_ref, p1_ref), (a0_ref, a1_ref)), m_ref, acc_ref)

    gates = jax.nn.sigmoid(ng_ref[...])
    g_hi = gates.astype(BF16)
    g_lo = (gates - g_hi.astype(F32)).astype(BF16)
    gb = [_dot(g_hi, gexp_ref[b]) + _dot(g_lo, gexp_ref[b]) for b in range(3)]
    for j in range(n_pairs):
        rs = slice(j * nq, (j + 1) * nq)
        ls = slice(j * LANES, (j + 1) * LANES)
        acc = acc_ref[rs, :]
        accw = accw_ref[rs, :]
        out = (gb[0][:, ls] * ocmp_ref[rs, :] + gb[1][:, ls] * (acc[:, :LANES] / acc[:, LANES:])
               + gb[2][:, ls] * (accw[:, :LANES] / accw[:, LANES:]))
        o_ref[:, ls] = out.astype(o_ref.dtype)


def _nsa(u16, u32, kc, vc, gq, gk, cover, expand, gexp, bsz, seq):
    nqb = seq // NSA_Q_BLOCK
    nch = kc.shape[2]
    n_blk = seq // SEL_BLOCK
    n_pairs = NSA_HEADS // 2
    prow = n_pairs * NSA_Q_BLOCK
    win_keys = min(WINDOW + NSA_Q_BLOCK, seq)
    kern = functools.partial(_nsa_kernel, seq=seq)
    return pl.pallas_call(
        kern,
        grid=(bsz, nqb),
        in_specs=[pl.BlockSpec((NSA_Q_BLOCK, 1024), lambda b, i: (b * nqb + i, U16_NQ_BLK)),
                  pl.BlockSpec((NSA_Q_BLOCK, 128), lambda b, i: (b * nqb + i, U32_NG_BLK128)),
                  pl.BlockSpec((seq, 512), lambda b, i: (b, U32_NKV_BLK512)),
                  pl.BlockSpec((1, NSA_KV_HEADS, nch, HEAD_DIM), lambda b, i: (b, 0, 0, 0)),
                  pl.BlockSpec((1, NSA_KV_HEADS, nch, HEAD_DIM), lambda b, i: (b, 0, 0, 0)),
                  pl.BlockSpec((1, HEAD_DIM), lambda b, i: (0, 0)),
                  pl.BlockSpec((3, HEAD_DIM), lambda b, i: (0, 0)),
                  pl.BlockSpec((nch, n_blk), lambda b, i: (0, 0)),
                  pl.BlockSpec((n_blk, seq), lambda b, i: (0, 0)),
                  pl.BlockSpec((3, LANES, NSA_HEADS * HEAD_DIM), lambda b, i: (0, 0, 0))],
        out_specs=pl.BlockSpec((NSA_Q_BLOCK, NSA_HEADS * HEAD_DIM), lambda b, i: (b * nqb + i, 0)),
        out_shape=jax.ShapeDtypeStruct((bsz * seq, NSA_HEADS * HEAD_DIM), BF16),
        scratch_shapes=[pltpu.VMEM((NSA_KV_HEADS, 2 * seq, LANES), BF16),
                        pltpu.VMEM((NSA_KV_HEADS, 2 * seq, 2 * LANES), BF16),
                        pltpu.VMEM((NSA_KV_HEADS, 2 * seq, LANES), BF16),
                        pltpu.VMEM((NSA_KV_HEADS, 2 * seq, 2 * LANES), BF16),
                        pltpu.VMEM((NSA_KV_HEADS, 2 * nch, LANES), BF16),
                        pltpu.VMEM((NSA_KV_HEADS, 2 * nch, LANES), BF16),
                        pltpu.VMEM((NSA_KV_HEADS, seq // KEY_TILE, NSA_Q_BLOCK, KEY_TILE), F32),
                        pltpu.VMEM((NSA_Q_BLOCK, 2 * win_keys), F32),
                        pltpu.VMEM((NSA_KV_HEADS, NSA_Q_BLOCK, nch), F32),
                        pltpu.VMEM((prow, LANES), F32),
                        pltpu.VMEM((prow, 2 * LANES), F32),
                        pltpu.VMEM((prow, LANES), BF16),
                        pltpu.VMEM((prow, 2 * win_keys), F32),
                        pltpu.VMEM((prow, 2 * win_keys), BF16),
                        pltpu.VMEM((prow, 2 * nch), F32),
                        pltpu.VMEM((prow, 2 * nch), BF16)]
                       + _flash_scratch(n_pairs, NSA_Q_BLOCK),
        compiler_params=_cparams(("parallel", "arbitrary")),
        name="nsa",
    )(u16, u32, u32, kc, vc, gq, gk, cover, expand, gexp)


def _softplus(x):
    return jnp.maximum(x, 0.0) + jnp.log1p(jnp.exp(-jnp.abs(x)))


def _ssd_kernel(xbc_ref, z_ref, dt_ref, dtt_ref, cw_ref, cb_ref, dtb_ref, dtbt_ref, al_ref, alt_ref,
                d_ref, ng_ref, o_ref, xcat_ref, xa_ref, y_ref, st_ref):
    c = pl.program_id(1)
    q = SSD_CHUNK
    gn = SSD_GROUPS * SSD_STATE
    tail = 16
    lo_half = lax.broadcasted_iota(jnp.int32, (1, LANES), 1) < SSD_HEAD_DIM

    @pl.when(c == 0)
    def _():
        st_ref[...] = jnp.zeros_like(st_ref)
        xcat_ref[0:q, :] = jnp.zeros((q, SSD_CONV_DIM), BF16)

    @pl.when(c > 0)
    def _():
        xcat_ref[q - tail:q, :] = xcat_ref[2 * q - tail:2 * q, :]

    xcat_ref[q:2 * q, :] = xbc_ref[...]
    ri3 = lax.broadcasted_iota(jnp.int32, ((CONV_WIDTH - 1) * q, 2 * q), 0)
    ci3 = lax.broadcasted_iota(jnp.int32, ((CONV_WIDTH - 1) * q, 2 * q), 1)
    qbits = int(math.log2(q))
    src = q + jnp.bitwise_and(ri3, q - 1) - (jnp.right_shift(ri3, qbits) + 1)
    shifts = jnp.where(ci3 == src, 1.0, 0.0).astype(BF16)
    for cc in range(SSD_CONV_DIM // CONV_CHUNK):
        cs = slice(cc * CONV_CHUNK, (cc + 1) * CONV_CHUNK)
        sh = _dot(shifts, xcat_ref[:, cs])
        acc = cb_ref[:, cs] + cw_ref[CONV_WIDTH - 1:CONV_WIDTH, cs] * xcat_ref[q:2 * q, cs].astype(F32)
        for s in range(1, CONV_WIDTH):
            acc = acc + cw_ref[CONV_WIDTH - 1 - s:CONV_WIDTH - s, cs] * sh[(s - 1) * q:s * q]
        xa_ref[:, cs] = acc * jax.nn.sigmoid(acc)

    dt = _softplus(dt_ref[:, :SSD_HEADS] + dtb_ref[...])
    dtt = _softplus(dtt_ref[0] + dtbt_ref[...])
    a = -jnp.exp(al_ref[...]) * LOG2E
    at = -jnp.exp(alt_ref[...]) * LOG2E
    ri = lax.broadcasted_iota(jnp.int32, (q, q), 0)
    ci = lax.broadcasted_iota(jnp.int32, (q, q), 1)
    tril = ri >= ci
    acum = _dot_hi(jnp.where(tril, 1.0, 0.0), dt * a)
    acumt = _dot_hi(dtt * at, jnp.where(ri <= ci, 1.0, 0.0))
    wrow = jnp.exp2(acumt[:, q - 1:q] - acumt) * dtt

    for g in range(SSD_GROUPS):
        bmat = xa_ref[:, SSD_INNER + g * SSD_STATE:SSD_INNER + (g + 1) * SSD_STATE]
        cmat = xa_ref[:, SSD_INNER + gn + g * SSD_STATE:SSD_INNER + gn + (g + 1) * SSD_STATE].astype(BF16)
        cb = _dot_nt(cmat, bmat.astype(BF16))
        bt = bmat.T
        for pp in range(SSD_HPG // 2):
            j = g * (SSD_HPG // 2) + pp
            ls = slice(j * LANES, (j + 1) * LANES)
            xs = xa_ref[:, ls]
            xbd = jnp.concatenate([jnp.where(lo_half, xs, 0.0), jnp.where(lo_half, 0.0, xs)], axis=0).astype(BF16)
            mm, dec, btw = [], [], []
            for par in range(2):
                h = 2 * j + par
                abc = jnp.broadcast_to(acum[:, h:h + 1], (q, q))
                lmat = jnp.exp2(jnp.where(tril, abc - acumt[h:h + 1, :], -jnp.inf))
                mm.append((cb * lmat * dtt[h:h + 1, :]).astype(BF16))
                dec.append(jnp.exp2(abc))
                btw.append((bt * wrow[h:h + 1, :]).astype(BF16))
            y = _dot(jnp.concatenate(mm, axis=1), xbd)
            y = y + _dot(cmat, st_ref[j].astype(BF16)) * jnp.where(lo_half, dec[0], dec[1])
            y_ref[:, ls] = y + xs * d_ref[:, ls]
            cdec = jnp.where(lo_half, dec[0][q - 1:q, :], dec[1][q - 1:q, :])
            st_ref[j] = st_ref[j] * cdec + _dot(jnp.concatenate(btw, axis=1), xbd)

    z = z_ref[...].astype(F32)
    yz = y_ref[...] * (z * jax.nn.sigmoid(z))
    o_ref[...] = _rms(yz, ng_ref[...]).astype(o_ref.dtype)


def _ssd(u16, u32, dtt, cw, cb, dtb, al, d, ng, bsz, seq):
    nc = seq // SSD_CHUNK
    q = SSD_CHUNK
    full = lambda shape: pl.BlockSpec(shape, lambda b, c: (0,) * len(shape))
    d_full = jnp.repeat(d, SSD_HEAD_DIM).reshape(1, SSD_INNER)
    return pl.pallas_call(
        _ssd_kernel,
        grid=(bsz, nc),
        in_specs=[pl.BlockSpec((q, SSD_CONV_DIM), lambda b, c: (b * nc + c, U16_XBC_BLK)),
                  pl.BlockSpec((q, SSD_INNER), lambda b, c: (b * nc + c, U16_Z_BLK)),
                  pl.BlockSpec((q, 128), lambda b, c: (b * nc + c, U32_DT_BLK128)),
                  pl.BlockSpec((1, SSD_HEADS, q), lambda b, c: (b, 0, c)),
                  full((CONV_WIDTH, SSD_CONV_DIM)), full((1, SSD_CONV_DIM)),
                  full((1, SSD_HEADS)), full((SSD_HEADS, 1)),
                  full((1, SSD_HEADS)), full((SSD_HEADS, 1)),
                  full((1, SSD_INNER)), full((1, SSD_INNER))],
        out_specs=pl.BlockSpec((q, SSD_INNER), lambda b, c: (b * nc + c, 0)),
        out_shape=jax.ShapeDtypeStruct((bsz * seq, SSD_INNER), BF16),
        scratch_shapes=[pltpu.VMEM((2 * q, SSD_CONV_DIM), BF16),
                        pltpu.VMEM((q, SSD_CONV_DIM), F32),
                        pltpu.VMEM((q, SSD_INNER), F32),
                        pltpu.VMEM((SSD_HEADS // 2, SSD_STATE, LANES), F32)],
        compiler_params=_cparams(("parallel", "arbitrary")),
        name="ssd",
    )(u16, u16, u32, dtt, cw, cb, dtb.reshape(1, -1), dtb.reshape(-1, 1), al.reshape(1, -1), al.reshape(-1, 1),
      d_full, ng.reshape(1, -1))


def _merge_kernel(x_ref, ya_ref, yb_ref, yc_ref, mg_ref, wa_ref, wb_ref, wc_ref, wo_ref, o_ref):
    d = D_MODEL
    mg = mg_ref[...].astype(F32)
    mix = jax.nn.sigmoid(mg[:, 0:d]) * _dot(ya_ref[...], wa_ref[...])
    mix = mix + jax.nn.sigmoid(mg[:, d:2 * d]) * _dot(yb_ref[...], wb_ref[...])
    mix = mix + jax.nn.sigmoid(mg[:, 2 * d:3 * d]) * _dot(yc_ref[...], wc_ref[...])
    o_ref[...] = x_ref[...] + _dot(mix.astype(BF16), wo_ref[...])


def _merge(x, ya, yb, yc, u16, wa, wb, wc, wo, tm):
    m, d = x.shape
    full = lambda a: pl.BlockSpec(a.shape, lambda i: (0, 0))
    return pl.pallas_call(
        _merge_kernel,
        grid=(m // tm,),
        in_specs=[pl.BlockSpec((tm, d), lambda i: (i, 0)),
                  pl.BlockSpec((tm, ya.shape[1]), lambda i: (i, 0)),
                  pl.BlockSpec((tm, yb.shape[1]), lambda i: (i, 0)),
                  pl.BlockSpec((tm, yc.shape[1]), lambda i: (i, 0)),
                  pl.BlockSpec((tm, 3 * d), lambda i: (i, U16_MG_BLK)),
                  full(wa), full(wb), full(wc), full(wo)],
        out_specs=pl.BlockSpec((tm, d), lambda i: (i, 0)),
        out_shape=jax.ShapeDtypeStruct((m, d), F32),
        compiler_params=_cparams(("parallel",)),
        name="merge",
    )(x, ya, yb, yc, u16, wa, wb, wc, wo)


def _ffn_kernel(x_ref, g_ref, w1_ref, w2_ref, o_ref, h_ref, acc_ref):
    j = pl.program_id(1)

    @pl.when(j == 0)
    def _():
        h_ref[...] = _rms(x_ref[...], g_ref[...]).astype(BF16)
        acc_ref[...] = x_ref[...]

    a = jnp.maximum(_dot(h_ref[...], w1_ref[...].astype(BF16)), 0.0)
    acc_ref[...] += _dot((a * a).astype(BF16), w2_ref[...].astype(BF16))

    @pl.when(j == pl.num_programs(1) - 1)
    def _():
        o_ref[...] = acc_ref[...]


def _ffn(x, g, w1, w2, tm, tf):
    m, d = x.shape
    f = w1.shape[1]
    return pl.pallas_call(
        _ffn_kernel,
        grid=(m // tm, f // tf),
        in_specs=[pl.BlockSpec((tm, d), lambda i, j: (i, 0)),
                  pl.BlockSpec((1, d), lambda i, j: (0, 0)),
                  pl.BlockSpec((d, tf), lambda i, j: (0, j)),
                  pl.BlockSpec((tf, d), lambda i, j: (j, 0))],
        out_specs=pl.BlockSpec((tm, d), lambda i, j: (i, 0)),
        out_shape=jax.ShapeDtypeStruct((m, d), F32),
        scratch_shapes=[pltpu.VMEM((tm, d), BF16), pltpu.VMEM((tm, d), F32)],
        compiler_params=_cparams(("parallel", "arbitrary")),
        name="ffn",
    )(x, g, w1, w2)


def _prep_w_in(w):
    offs = [0] + [int(o) for o in np.cumsum(IN_SPLITS)]
    seg = lambda k: w[:, offs[k]:offs[k + 1]]
    dq, dk, dv, iq, ik, iw, nq, nkv, ng, sz, sxbc, sdt, mg = [seg(k) for k in range(13)]
    zeros = lambda n: jnp.zeros((w.shape[0], n), w.dtype)
    w16 = jnp.concatenate([sxbc, mg, sz, dq, nq], axis=1).astype(BF16)
    w32 = jnp.concatenate([nkv[:, 256:768], nkv[:, 0:256], zeros(256),
                           dk, dv, iq, ik, iw, zeros(512 - 424),
                           ng, zeros(128 - ng.shape[1]),
                           sdt, zeros(128 - sdt.shape[1])], axis=1).astype(BF16)
    return w16, w32


def _pick_tile(n, pref):
    t = min(n, pref)
    while n % t:
        t //= 2
    return t


def kernel(x, norm1_g, w_in, dsa_q_norm, dsa_k_norm, nsa_q_norm, nsa_k_norm, nsa_cmp_pos, nsa_cmp_w,
           ssd_conv_w, ssd_conv_b, ssd_dt_bias, ssd_a_log, ssd_d, ssd_norm_g,
           w_br_dsa, w_br_nsa, w_br_ssd, w_out, norm2_g, w_ff1, w_ff2):
    bsz, seq, d = x.shape
    m = bsz * seq
    depth = w_in.shape[0]
    nch = seq // CMP_STRIDE
    n_blk = seq // SEL_BLOCK
    half = CMP_BLOCK // 2

    cmp_start = np.arange(nch) * CMP_STRIDE
    blk_start = np.arange(n_blk) * SEL_BLOCK
    cover = ((cmp_start[:, None] < blk_start[None, :] + SEL_BLOCK)
             & (cmp_start[:, None] + CMP_BLOCK > blk_start[None, :])
             & (np.arange(nch)[:, None] < (seq - CMP_BLOCK) // CMP_STRIDE + 1)).astype(np.float32)
    expand = (np.arange(seq)[None, :] // SEL_BLOCK == np.arange(n_blk)[:, None]).astype(np.float32)
    gexp = np.zeros((3, LANES, NSA_HEADS * HEAD_DIM), np.float32)
    for hh in range(NSA_HEADS):
        for br in range(3):
            gexp[br, 3 * hh + br, hh * HEAD_DIM:(hh + 1) * HEAD_DIM] = 1.0
    gexp = jnp.asarray(gexp, dtype=BF16)
    cover = jnp.asarray(cover)
    expand = jnp.asarray(expand, dtype=BF16)

    tm = _pick_tile(m, 1024)
    xf = x.reshape(m, d)
    for l in range(depth):
        w16, w32 = _prep_w_in(w_in[l])
        g1 = norm1_g[l].reshape(1, d)
        u16 = _norm_matmul(xf, g1, w16, BF16, tm, 2048)
        u32 = _norm_matmul(xf, g1, w32, F32, tm, U32_WIDTH)

        iw0 = U32_DS_BLK512 * 512 + DS_IW
        iwt = u32[:, iw0:iw0 + IDX_HEADS].reshape(bsz, seq, IDX_HEADS).transpose(0, 2, 1)
        ya = _dsa(u16, u32, iwt, dsa_q_norm[l].reshape(1, -1), dsa_k_norm[l].reshape(1, -1), bsz, seq)

        wl = nsa_cmp_w[l]
        zblk = jnp.zeros_like(wl)
        cw = jnp.concatenate([jnp.concatenate([wl, zblk], axis=3),
                              jnp.concatenate([zblk, wl], axis=3)], axis=2).astype(BF16)
        cpos = jnp.concatenate([nsa_cmp_pos[l]] * NSA_KV_HEADS, axis=-1)
        kc, vc = _nsa_compress(u32, cpos, cw, nsa_k_norm[l][0:1], bsz, seq)
        yb = _nsa(u16, u32, kc, vc, nsa_q_norm[l].reshape(1, -1), nsa_k_norm[l], cover, expand, gexp, bsz, seq)

        dt0 = U32_DT_BLK128 * 128
        dtt = u32[:, dt0:dt0 + SSD_HEADS].reshape(bsz, seq, SSD_HEADS).transpose(0, 2, 1)
        yc = _ssd(u16, u32, dtt, ssd_conv_w[l], ssd_conv_b[l].reshape(1, -1), ssd_dt_bias[l], ssd_a_log[l],
                  ssd_d[l], ssd_norm_g[l], bsz, seq)

        xf = _merge(xf, ya, yb, yc, u16,
                    w_br_dsa[l].astype(BF16), w_br_nsa[l].astype(BF16), w_br_ssd[l].astype(BF16),
                    w_out[l].astype(BF16), _pick_tile(m, 256))
        xf = _ffn(xf, norm2_g[l].reshape(1, d), w_ff1[l], w_ff2[l],
                  _pick_tile(m, 1024), 1024)
    return xf.reshape(bsz, seq, d)
```

```python
import functools
import math

import numpy as np
import jax
import jax.numpy as jnp
from jax import lax
from jax.experimental import pallas as pl
from jax.experimental.pallas import tpu as pltpu

F32 = jnp.float32
BF16 = jnp.bfloat16

D_MODEL = 1024
HEAD_DIM = 64
DSA_HEADS = 16
DSA_KV_DIM = 64
IDX_HEADS = 8
IDX_DIM = 32
DSA_TOPK = 256
NSA_HEADS = 16
NSA_KV_HEADS = 2
NSA_GROUP = NSA_HEADS // NSA_KV_HEADS
CMP_BLOCK = 32
CMP_STRIDE = 16
SEL_BLOCK = 64
SEL_TOPN = 4
WINDOW = 512
SSD_INNER = 2 * D_MODEL
SSD_HEAD_DIM = 64
SSD_HEADS = SSD_INNER // SSD_HEAD_DIM
SSD_GROUPS = 4
SSD_HPG = SSD_HEADS // SSD_GROUPS
SSD_STATE = 128
SSD_CONV_DIM = SSD_INNER + 2 * SSD_GROUPS * SSD_STATE
CONV_WIDTH = 4
SSD_CHUNK = 128
D_FF = 4 * D_MODEL
Q_BLOCK = 128
DSA_Q_BLOCK = 256
NSA_Q_BLOCK = 128
EPS = 1e-6
NEG = -1e30
IN_SPLITS = (DSA_HEADS * HEAD_DIM, DSA_KV_DIM, DSA_KV_DIM, IDX_HEADS * IDX_DIM, IDX_DIM, IDX_HEADS,
             NSA_HEADS * HEAD_DIM, 6 * NSA_KV_HEADS * HEAD_DIM, 3 * NSA_HEADS,
             SSD_INNER, SSD_CONV_DIM, SSD_HEADS, 3 * D_MODEL)

V7X_VMEM_LIMIT_BYTES = 56 * 1024 * 1024
LANES = 128

U16_WIDTH = 10240
U16_XBC_BLK = 0
U16_MG_BLK = 1
U16_Z_BLK = 3
U16_DQ_BLK = 8
U16_NQ_BLK = 9
U32_WIDTH = 1792
U32_NKV_BLK512 = 0
U32_KC_BLK128 = 4
U32_DS_BLK512 = 2
U32_NG_BLK128 = 12
U32_DT_BLK128 = 13
DS_DK, DS_DV, DS_IQ, DS_IK, DS_IW = 0, 64, 128, 384, 416

KEY_TILE = 256
BISECT_ITERS = 24
PAIR_STEP = 2
FLASH_UNROLL = 4
LOG2E = 1.4426950408889634
BISECT_UNROLL = 4
CONV_CHUNK = 256


def _cparams(sem):
    return pltpu.CompilerParams(dimension_semantics=sem, vmem_limit_bytes=V7X_VMEM_LIMIT_BYTES)


def _rms(x, g):
    return x * lax.rsqrt(jnp.mean(x * x, axis=-1, keepdims=True) + EPS) * g


def _dot_nt(a, b):
    return lax.dot_general(a, b, (((1,), (1,)), ((), ())), preferred_element_type=F32)


def _dot(a, b):
    return jnp.dot(a, b, preferred_element_type=F32)


def _dot_hi(a, b):
    return jnp.dot(a, b, preferred_element_type=F32, precision=lax.Precision.HIGHEST)


def _norm_matmul_kernel(x_ref, g_ref, w_ref, o_ref, h_ref):
    @pl.when(pl.program_id(1) == 0)
    def _():
        h_ref[...] = _rms(x_ref[...], g_ref[...]).astype(BF16)

    o_ref[...] = _dot(h_ref[...], w_ref[...]).astype(o_ref.dtype)


def _norm_matmul(x, g, w, out_dtype, tm, tn):
    m, k = x.shape
    n = w.shape[1]
    return pl.pallas_call(
        _norm_matmul_kernel,
        grid=(m // tm, n // tn),
        in_specs=[pl.BlockSpec((tm, k), lambda i, j: (i, 0)),
                  pl.BlockSpec((1, k), lambda i, j: (0, 0)),
                  pl.BlockSpec((k, tn), lambda i, j: (0, j))],
        out_specs=pl.BlockSpec((tm, tn), lambda i, j: (i, j)),
        out_shape=jax.ShapeDtypeStruct((m, n), out_dtype),
        scratch_shapes=[pltpu.VMEM((tm, k), BF16)],
        compiler_params=_cparams(("parallel", "arbitrary")),
        name="norm_matmul",
    )(x, g, w)


def _flash_pairs(qp_ref, kbd_ref, wext_ref, bias_ref, nkt, n_pairs, pairs_per_kv, bufs, m_ref, acc_ref):
    (s0, s1), (p0, p1), (a0, a1) = bufs
    qb = bias_ref.shape[2]
    n_pg = n_pairs // PAIR_STEP
    rows = PAIR_STEP * qb
    n = nkt * n_pg
    lo_half = lax.broadcasted_iota(jnp.int32, (1, LANES), 1) < HEAD_DIM
    m_ref[...] = jnp.full(m_ref.shape, -jnp.inf, F32)
    acc_ref[...] = jnp.zeros(acc_ref.shape, F32)
    p1[...] = jnp.zeros(p1.shape, BF16)
    a1[...] = jnp.ones(a1.shape, F32)

    def where(j):
        kt = j // n_pg
        pg = j % n_pg
        g = (pg * PAIR_STEP) // pairs_per_kv
        return kt, pg, g, pl.multiple_of(kt * 2 * KEY_TILE, 2 * KEY_TILE), pl.multiple_of(pg * rows, rows)

    def qk(j, s_ref):
        kt, _, g, koff, r = where(j)
        s = _dot_nt(qp_ref[pl.ds(r, rows), :], kbd_ref[g, pl.ds(koff, 2 * KEY_TILE), :])
        s_ref[...] = s + jnp.tile(bias_ref[g, kt], (PAIR_STEP, 2))

    def softmax(j, s_ref, p_ref, a_ref):
        _, pg, _, _, _ = where(j)
        for u in range(PAIR_STEP):
            us = slice(u * qb, (u + 1) * qb)
            alpha = []
            for par in range(2):
                cs = slice(par * KEY_TILE, (par + 1) * KEY_TILE)
                hrow = pl.multiple_of(((pg * PAIR_STEP + u) * 2 + par) * qb, qb)
                m_old = m_ref[pl.ds(hrow, qb), :]
                m_new = jnp.maximum(m_old, jnp.max(s_ref[us, cs], axis=-1, keepdims=True))
                alpha.append(jnp.exp2(m_old - m_new))
                m_ref[pl.ds(hrow, qb), :] = m_new
                p_ref[us, cs] = jnp.exp2(
                    s_ref[us, cs] - jnp.concatenate([m_new] * (KEY_TILE // LANES), axis=1)).astype(BF16)
            a_ref[us, :] = jnp.where(lo_half, alpha[0], alpha[1])

    def pv(j, p_ref, a_ref):
        _, _, g, koff, r = where(j)
        a = a_ref[...]
        acc_ref[pl.ds(r, rows), :] = (jnp.concatenate([a, a], axis=1) * acc_ref[pl.ds(r, rows), :]
                                      + _dot(p_ref[...], wext_ref[g, pl.ds(koff, 2 * KEY_TILE), :]))

    qk(0, s0)

    def body(jj, c):
        for j in (FLASH_UNROLL * jj, FLASH_UNROLL * jj + 2):
            qk(j + 1, s1)
            softmax(j, s0, p0, a0)
            pv(jnp.maximum(j - 1, 0), p1, a1)
            qk(jnp.minimum(j + 2, n - 1), s0)
            softmax(j + 1, s1, p1, a1)
            pv(j, p0, a0)
        return c

    lax.fori_loop(0, n // FLASH_UNROLL, body, 0)
    pv(n - 1, p1, a1)


def _flash_scratch(n_pairs, qb):
    step = PAIR_STEP * qb
    return ([pltpu.VMEM((step, 2 * KEY_TILE), F32)] * 2 + [pltpu.VMEM((step, 2 * KEY_TILE), BF16)] * 2
            + [pltpu.VMEM((step, LANES), F32)] * 2
            + [pltpu.VMEM((2 * n_pairs * qb, LANES), F32), pltpu.VMEM((n_pairs * qb, 2 * LANES), F32)])


def _norm_pairs(q_ref, g_ref, qp_ref, n_pairs):
    qb = q_ref.shape[0]
    mult = HEAD_DIM ** -0.5 * LOG2E
    lo_half = lax.broadcasted_iota(jnp.int32, (1, LANES), 1) < HEAD_DIM
    g2 = jnp.concatenate([g_ref[...], g_ref[...]], axis=-1) * mult
    for j in range(n_pairs):
        x = q_ref[:, j * LANES:(j + 1) * LANES].astype(F32)
        x2 = x * x
        s_lo = jnp.sum(jnp.where(lo_half, x2, 0.0), axis=-1, keepdims=True)
        s_hi = jnp.sum(jnp.where(lo_half, 0.0, x2), axis=-1, keepdims=True)
        r = jnp.where(lo_half, lax.rsqrt(s_lo * (1.0 / HEAD_DIM) + EPS), lax.rsqrt(s_hi * (1.0 / HEAD_DIM) + EPS))
        qp_ref[j * qb:(j + 1) * qb, :] = (x * r * g2).astype(BF16)


def _store_pair_kv(kbd_ref, wext_ref, g, row0, n, k, v):
    z = jnp.zeros((n, HEAD_DIM), F32)
    one = jnp.ones((n, HEAD_DIM), F32)
    kbd_ref[g, pl.ds(row0, n), :] = jnp.concatenate([k, z], axis=-1).astype(BF16)
    kbd_ref[g, pl.ds(row0 + n, n), :] = jnp.concatenate([z, k], axis=-1).astype(BF16)
    wext_ref[g, pl.ds(row0, n), :] = jnp.concatenate([v, z, one, z], axis=-1).astype(BF16)
    wext_ref[g, pl.ds(row0 + n, n), :] = jnp.concatenate([z, v, z, one], axis=-1).astype(BF16)


def _dsa_kernel(q_ref, sq_ref, iwt_ref, sk_ref, gq_ref, gk_ref, o_ref,
                kbd_ref, wext_ref, ik_ref, idx_ref, bias_ref, iqs_ref,
                qp_ref, s0_ref, s1_ref, p0_ref, p1_ref, a0_ref, a1_ref, m_ref, acc_ref, *, seq, n_sel):
    i = pl.program_id(1)
    nq = q_ref.shape[0]
    n_pairs = DSA_HEADS // 2
    kt_shape = (KEY_TILE, nq)

    @pl.when(i == 0)
    def _():
        def rows(r, c):
            off = pl.multiple_of(r * KEY_TILE, KEY_TILE)
            sk = sk_ref[pl.ds(off, KEY_TILE), :]
            _store_pair_kv(kbd_ref, wext_ref, 0, pl.multiple_of(2 * off, 2 * KEY_TILE), KEY_TILE,
                           _rms(sk[:, DS_DK:DS_DK + HEAD_DIM], gk_ref[...]), sk[:, DS_DV:DS_DV + HEAD_DIM])
            ik_ref[pl.ds(off, KEY_TILE), :] = sk[:, DS_IK:DS_IK + IDX_DIM].astype(BF16)
            return c

        lax.fori_loop(0, seq // KEY_TILE, rows, 0)

    start = i * nq
    nkt = (start + nq + KEY_TILE - 1) // KEY_TILE
    t_row = (start + lax.broadcasted_iota(jnp.int32, (1, nq), 1)).astype(F32)
    sub_pos = lax.broadcasted_iota(jnp.int32, (KEY_TILE, 1), 0).astype(F32)

    sq = sq_ref[...]
    for h in range(IDX_HEADS):
        iqs_ref[h * nq:(h + 1) * nq, :] = sq[:, DS_IQ + h * IDX_DIM:DS_IQ + (h + 1) * IDX_DIM].astype(BF16)
    iwt = iwt_ref[0]

    def idx_tile(kt, carry):
        rmin, rmax = carry
        off = pl.multiple_of(kt * KEY_TILE, KEY_TILE)
        r = _dot_nt(ik_ref[pl.ds(off, KEY_TILE), :], iqs_ref[...])
        acc = jnp.zeros(kt_shape, F32)
        for h in range(IDX_HEADS):
            acc = acc + iwt[h:h + 1, :] * jnp.maximum(r[:, h * nq:(h + 1) * nq], 0.0)
        valid = (sub_pos + (kt * KEY_TILE).astype(F32)) <= t_row
        idx_ref[kt] = jnp.where(valid, acc, -jnp.inf)
        rmin = jnp.minimum(rmin, jnp.min(jnp.where(valid, acc, jnp.inf), axis=0, keepdims=True))
        rmax = jnp.maximum(rmax, jnp.max(jnp.where(valid, acc, -jnp.inf), axis=0, keepdims=True))
        return rmin, rmax

    npair = (nkt + 1) // 2
    rmin, rmax = lax.fori_loop(0, npair, lambda jj, c: idx_tile(2 * jj + 1, idx_tile(2 * jj, c)),
                               (jnp.full((1, nq), jnp.inf, F32), jnp.full((1, nq), -jnp.inf, F32)))

    kf = float(n_sel)

    def col_sum(x):
        parts = [x[r * 8:(r + 1) * 8] for r in range(x.shape[0] // 8)]
        while len(parts) > 1:
            parts = [parts[k] + parts[k + 1] for k in range(0, len(parts), 2)]
        return jnp.sum(parts[0], axis=0, keepdims=True)

    fold = KEY_TILE // 4

    def count_gt(thr):
        def body(kt, acc):
            hit = jnp.where(idx_ref[kt] > thr, 1.0, 0.0)
            return acc + ((hit[0:fold] + hit[fold:2 * fold]) + (hit[2 * fold:3 * fold] + hit[3 * fold:]))

        return col_sum(lax.fori_loop(0, nkt, body, jnp.zeros((fold, nq), F32)))

    n_valid = t_row + 1.0
    lo0 = rmin - (jnp.abs(rmin) * 0.01 + 1.0)

    def bis_cond(c):
        it, _, _, clo, _ = c
        return jnp.logical_and(it < BISECT_ITERS, jnp.max(clo) > kf)

    def bis_body(c):
        it, lo, hi, clo, chi = c
        for _ in range(BISECT_UNROLL):
            mid = 0.5 * (lo + hi)
            cnt = count_gt(mid)
            ge = cnt >= kf
            lo, hi = jnp.where(ge, mid, lo), jnp.where(ge, hi, mid)
            clo, chi = jnp.where(ge, cnt, clo), jnp.where(ge, chi, cnt)
        return it + BISECT_UNROLL, lo, hi, clo, chi

    _, lo, hi, clo, chi = lax.while_loop(
        bis_cond, bis_body, (jnp.int32(0), lo0, rmax, n_valid, jnp.zeros((1, nq), F32)))

    eye_f = jnp.where(lax.broadcasted_iota(jnp.int32, (nq, nq), 0) == lax.broadcasted_iota(jnp.int32, (nq, nq), 1),
                      1.0, 0.0)
    eye = eye_f.astype(BF16)
    tri = jnp.where(lax.broadcasted_iota(jnp.int32, (KEY_TILE, KEY_TILE), 0)
                    <= lax.broadcasted_iota(jnp.int32, (KEY_TILE, KEY_TILE), 1), 1.0, 0.0).astype(BF16)
    quota_row = jnp.where(clo > kf, kf - chi, float(2 * seq))
    quota = jnp.sum(eye_f * quota_row, axis=1, keepdims=True)

    def mask_tile(kt, carry):
        v = idx_ref[kt]
        code = jnp.where(v > hi, 2.0, jnp.where(v > lo, 1.0, 0.0)).astype(BF16)
        code_t = _dot_nt(eye, code)
        tie_t = jnp.where((code_t > 0.5) & (code_t < 1.5), 1.0, 0.0)
        rank = _dot(tie_t.astype(BF16), tri) + carry
        keep = (code_t > 1.5) | ((tie_t > 0.5) & (rank <= quota))
        bias_ref[0, kt] = jnp.where(keep, 0.0, NEG)
        return carry + jnp.sum(tie_t, axis=1, keepdims=True)

    lax.fori_loop(0, npair, lambda jj, c: mask_tile(2 * jj + 1, mask_tile(2 * jj, c)), jnp.zeros((nq, 1), F32))

    _norm_pairs(q_ref, gq_ref, qp_ref, n_pairs)
    _flash_pairs(qp_ref, kbd_ref, wext_ref, bias_ref, nkt, n_pairs, n_pairs,
                 ((s0_ref, s1_ref), (p0_ref, p1_ref), (a0_ref, a1_ref)), m_ref, acc_ref)
    for j in range(n_pairs):
        acc = acc_ref[j * nq:(j + 1) * nq, :]
        o_ref[:, j * LANES:(j + 1) * LANES] = (acc[:, :LANES] / acc[:, LANES:]).astype(o_ref.dtype)


def _dsa(u16, u32, iwt, gq, gk, bsz, seq):
    n_sel = min(DSA_TOPK, seq // 4)
    nqb = seq // DSA_Q_BLOCK
    n_pairs = DSA_HEADS // 2
    kern = functools.partial(_dsa_kernel, seq=seq, n_sel=n_sel)
    return pl.pallas_call(
        kern,
        grid=(bsz, nqb),
        in_specs=[pl.BlockSpec((DSA_Q_BLOCK, 1024), lambda b, i: (b * nqb + i, U16_DQ_BLK)),
                  pl.BlockSpec((DSA_Q_BLOCK, 512), lambda b, i: (b * nqb + i, U32_DS_BLK512)),
                  pl.BlockSpec((1, IDX_HEADS, DSA_Q_BLOCK), lambda b, i: (b, 0, i)),
                  pl.BlockSpec((seq, 512), lambda b, i: (b, U32_DS_BLK512)),
                  pl.BlockSpec((1, HEAD_DIM), lambda b, i: (0, 0)),
                  pl.BlockSpec((1, DSA_KV_DIM), lambda b, i: (0, 0))],
        out_specs=pl.BlockSpec((DSA_Q_BLOCK, DSA_HEADS * HEAD_DIM), lambda b, i: (b * nqb + i, 0)),
        out_shape=jax.ShapeDtypeStruct((bsz * seq, DSA_HEADS * HEAD_DIM), BF16),
        scratch_shapes=[pltpu.VMEM((1, 2 * seq, LANES), BF16),
                        pltpu.VMEM((1, 2 * seq, 2 * LANES), BF16),
                        pltpu.VMEM((seq, IDX_DIM), BF16),
                        pltpu.VMEM((seq // KEY_TILE, KEY_TILE, DSA_Q_BLOCK), F32),
                        pltpu.VMEM((1, seq // KEY_TILE, DSA_Q_BLOCK, KEY_TILE), F32),
                        pltpu.VMEM((IDX_HEADS * DSA_Q_BLOCK, IDX_DIM), BF16),
                        pltpu.VMEM((n_pairs * DSA_Q_BLOCK, LANES), BF16)]
                       + _flash_scratch(n_pairs, DSA_Q_BLOCK),
        compiler_params=_cparams(("parallel", "arbitrary")),
        name="dsa",
    )(u16, u32, iwt, u32, gq, gk)


def _nsa_compress_kernel(xk_ref, xv_ref, pos_ref, w_ref, g_ref, kc_ref, vc_ref):
    nch = xk_ref.shape[0] // CMP_STRIDE
    for jj, (x_ref, o_ref) in enumerate(((xk_ref, kc_ref), (xv_ref, vc_ref))):
        acc_a = jnp.zeros((nch, LANES), F32)
        acc_b = jnp.zeros((nch, LANES), F32)
        for l in range(CMP_STRIDE):
            x = x_ref[pl.ds(l, nch, stride=CMP_STRIDE), :]
            acc_a = acc_a + _dot((x + pos_ref[jj, l:l + 1, :]).astype(BF16), w_ref[jj, l])
            hi = CMP_STRIDE + l
            acc_b = acc_b + _dot((x + pos_ref[jj, hi:hi + 1, :]).astype(BF16), w_ref[jj, hi])
        out = acc_a + pltpu.roll(acc_b, nch - 1, axis=0)
        for g in range(NSA_KV_HEADS):
            og = out[:, g * HEAD_DIM:(g + 1) * HEAD_DIM]
            o_ref[0, g] = (_rms(og, g_ref[...]) if jj == 0 else og).astype(BF16)


def _nsa_compress(u32, pos, w, g, bsz, seq):
    nch = seq // CMP_STRIDE
    return pl.pallas_call(
        _nsa_compress_kernel,
        grid=(bsz,),
        in_specs=[pl.BlockSpec((seq, LANES), lambda b: (b, U32_KC_BLK128)),
                  pl.BlockSpec((seq, LANES), lambda b: (b, U32_KC_BLK128 + 1)),
                  pl.BlockSpec((2, CMP_BLOCK, LANES), lambda b: (0, 0, 0)),
                  pl.BlockSpec((2, CMP_BLOCK, LANES, LANES), lambda b: (0, 0, 0, 0)),
                  pl.BlockSpec((1, HEAD_DIM), lambda b: (0, 0))],
        out_specs=[pl.BlockSpec((1, NSA_KV_HEADS, nch, HEAD_DIM), lambda b: (b, 0, 0, 0)),
                   pl.BlockSpec((1, NSA_KV_HEADS, nch, HEAD_DIM), lambda b: (b, 0, 0, 0))],
        out_shape=[jax.ShapeDtypeStruct((bsz, NSA_KV_HEADS, nch, HEAD_DIM), BF16),
                   jax.ShapeDtypeStruct((bsz, NSA_KV_HEADS, nch, HEAD_DIM), BF16)],
        compiler_params=_cparams(("parallel",)),
        name="nsa_compress",
    )(u32, u32, pos, w, g)


def _nsa_kernel(q_ref, ng_ref, kv_ref, kc_ref, vc_ref, gq_ref, gk_ref, cover_ref, expand_ref, gexp_ref, o_ref,
                ksbd_ref, wsext_ref, kwbd_ref, wwext_ref, kcbd_ref, vcext_ref, bias_ref, wbias_ref, psum_ref,
                ocmp_ref, accw_ref, qp_ref, sw_ref, pw_ref, sc_ref, pc_ref,
                s0_ref, s1_ref, p0_ref, p1_ref, a0_ref, a1_ref, m_ref, acc_ref, *, seq):
    i = pl.program_id(1)
    nq = q_ref.shape[0]
    n_pairs = NSA_HEADS // 2
    gpairs = NSA_GROUP // 2
    grows = gpairs * nq
    n_cmp_pad = kc_ref.shape[2]
    n_blk = seq // SEL_BLOCK
    n_sel = min(SEL_TOPN, n_blk)
    win_keys = min(WINDOW + nq, seq)
    wt = LANES
    win_tiles = win_keys // wt

    @pl.when(i == 0)
    def _():
        def rows(r, c):
            off = pl.multiple_of(r * KEY_TILE, KEY_TILE)
            kv = kv_ref[pl.ds(off, KEY_TILE), :]
            for g in range(NSA_KV_HEADS):
                c0 = g * HEAD_DIM
                _store_pair_kv(ksbd_ref, wsext_ref, g, pl.multiple_of(2 * off, 2 * KEY_TILE), KEY_TILE,
                               _rms(kv[:, c0:c0 + 64], gk_ref[1:2, :]), kv[:, 128 + c0:128 + c0 + 64])
                kw = _rms(kv[:, 256 + c0:256 + c0 + 64], gk_ref[2:3, :])
                vw = kv[:, 384 + c0:384 + c0 + 64]
                for hf in range(KEY_TILE // wt):
                    _store_pair_kv(kwbd_ref, wwext_ref, g, pl.multiple_of(2 * off + hf * 2 * wt, 2 * wt), wt,
                                   kw[hf * wt:(hf + 1) * wt], vw[hf * wt:(hf + 1) * wt])
            return c

        lax.fori_loop(0, seq // KEY_TILE, rows, 0)
        zc = jnp.zeros((n_cmp_pad, HEAD_DIM), BF16)
        for g in range(NSA_KV_HEADS):
            kcbd_ref[g, 0:n_cmp_pad, :] = jnp.concatenate([kc_ref[0, g], zc], axis=-1)
            kcbd_ref[g, n_cmp_pad:2 * n_cmp_pad, :] = jnp.concatenate([zc, kc_ref[0, g]], axis=-1)
            vcext_ref[g, 0:n_cmp_pad, :] = jnp.concatenate([vc_ref[0, g], zc], axis=-1)
            vcext_ref[g, n_cmp_pad:2 * n_cmp_pad, :] = jnp.concatenate([zc, vc_ref[0, g]], axis=-1)

    start = i * nq
    nkt = (start + nq + KEY_TILE - 1) // KEY_TILE
    t_i = start + lax.broadcasted_iota(jnp.int32, (nq, 1), 0)
    t_col = t_i.astype(F32)
    lane_pos = lax.broadcasted_iota(jnp.int32, (1, KEY_TILE), 1).astype(F32)

    _norm_pairs(q_ref, gq_ref, qp_ref, n_pairs)

    cmp_end = (lax.broadcasted_iota(jnp.int32, (1, n_cmp_pad), 1) * CMP_STRIDE + (CMP_BLOCK - 1)).astype(F32)
    vis = cmp_end <= t_col
    wbase = pl.multiple_of(jnp.maximum(start + nq - win_keys, 0), wt)
    wb2 = pl.multiple_of(2 * wbase, 2 * wt)
    col = lax.broadcasted_iota(jnp.int32, (1, 2 * win_keys), 1)
    wpos = (wbase + jnp.right_shift(col, int(math.log2(2 * wt))) * wt + jnp.bitwise_and(col, wt - 1)).astype(F32)
    wbias_ref[...] = jnp.where((wpos <= t_col) & (wpos > t_col - float(WINDOW)), 0.0, NEG)
    for g in range(NSA_KV_HEADS):
        gs = slice(g * grows, (g + 1) * grows)
        sc_ref[gs, :] = _dot_nt(qp_ref[gs, :], kcbd_ref[g])
        sw_ref[gs, :] = _dot_nt(qp_ref[gs, :], kwbd_ref[g, pl.ds(wb2, 2 * win_keys), :])

    for g in range(NSA_KV_HEADS):
        gs = slice(g * grows, (g + 1) * grows)
        tot = jnp.zeros((nq, n_cmp_pad), F32)
        for j in range(g * gpairs, (g + 1) * gpairs):
            rs = slice(j * nq, (j + 1) * nq)
            for par in range(2):
                cs = slice(par * n_cmp_pad, (par + 1) * n_cmp_pad)
                s = jnp.where(vis, sc_ref[rs, cs], NEG)
                e = jnp.exp2(s - jnp.max(s, axis=-1, keepdims=True))
                p = jnp.where(vis, e / jnp.sum(e, axis=-1, keepdims=True), 0.0)
                pc_ref[rs, cs] = p.astype(BF16)
                tot = tot + p
        psum_ref[g] = tot
        ocmp_ref[gs, :] = _dot(pc_ref[gs, :], vcext_ref[g])

    blk_j = lax.broadcasted_iota(jnp.int32, (1, n_blk), 1)
    cur1 = jnp.right_shift(t_i, int(math.log2(SEL_BLOCK)))
    cur = jnp.concatenate([cur1] * NSA_KV_HEADS, axis=0)
    imp = jnp.concatenate([_dot_hi(psum_ref[g], cover_ref[...]) for g in range(NSA_KV_HEADS)], axis=0)
    forced = (blk_j == cur) | (blk_j == 0)
    imp = jnp.where(forced, jnp.inf, jnp.where(blk_j > cur, -jnp.inf, imp))
    selb = jnp.zeros(imp.shape, jnp.bool_)
    for _ in range(n_sel):
        mx = jnp.max(imp, axis=-1, keepdims=True)
        first = jnp.min(jnp.where(imp == mx, blk_j, n_blk), axis=-1, keepdims=True)
        pick = blk_j == first
        selb = selb | pick
        imp = jnp.where(pick, -jnp.inf, imp)
    selb_bf = jnp.where(selb, 1.0, 0.0).astype(BF16)

    for g in range(NSA_KV_HEADS):
        gs = slice(g * grows, (g + 1) * grows)
        for j in range(g * gpairs, (g + 1) * gpairs):
            rs = slice(j * nq, (j + 1) * nq)
            for par in range(2):
                cols = [slice(t * 2 * wt + par * wt, t * 2 * wt + (par + 1) * wt) for t in range(win_tiles)]
                s = [sw_ref[rs, cs] + wbias_ref[:, cs] for cs in cols]
                mx = s[0]
                for st in s[1:]:
                    mx = jnp.maximum(mx, st)
                m = jnp.max(mx, axis=-1, keepdims=True)
                for cs, st in zip(cols, s):
                    pw_ref[rs, cs] = jnp.exp2(st - m).astype(BF16)
        accw_ref[gs, :] = _dot(pw_ref[gs, :], wwext_ref[g, pl.ds(wb2, 2 * win_keys), :])

    def mask_tile(kt, c):
        off = pl.multiple_of(kt * KEY_TILE, KEY_TILE)
        hit = _dot(selb_bf, expand_ref[:, pl.ds(off, KEY_TILE)])
        ok = (lane_pos + (kt * KEY_TILE).astype(F32)) <= t_col
        for g in range(NSA_KV_HEADS):
            bias_ref[g, kt] = jnp.where((hit[g * nq:(g + 1) * nq] > 0.5) & ok, 0.0, NEG)
        return c

    lax.fori_loop(0, (nkt + 1) // 2, lambda jj, c: mask_tile(2 * jj + 1, mask_tile(2 * jj, c)), 0)
    _flash_pairs(qp_ref, ksbd_ref, wsext_ref, bias_ref, nkt, n_pairs, gpairs,
                 ((s0_ref, s1_ref), (p0_ref, p1_ref), (a0_ref, a1_ref)), m_ref, acc_ref)

    gates = jax.nn.sigmoid(ng_ref[...])
    g_hi = gates.astype(BF16)
    g_lo = (gates - g_hi.astype(F32)).astype(BF16)
    gb = [_dot(g_hi, gexp_ref[b]) + _dot(g_lo, gexp_ref[b]) for b in range(3)]
    for j in range(n_pairs):
        rs = slice(j * nq, (j + 1) * nq)
        ls = slice(j * LANES, (j + 1) * LANES)
        acc = acc_ref[rs, :]
        accw = accw_ref[rs, :]
        out = (gb[0][:, ls] * ocmp_ref[rs, :] + gb[1][:, ls] * (acc[:, :LANES] / acc[:, LANES:])
               + gb[2][:, ls] * (accw[:, :LANES] / accw[:, LANES:]))
        o_ref[:, ls] = out.astype(o_ref.dtype)


def _nsa(u16, u32, kc, vc, gq, gk, cover, expand, gexp, bsz, seq):
    nqb = seq // NSA_Q_BLOCK
    nch = kc.shape[2]
    n_blk = seq // SEL_BLOCK
    n_pairs = NSA_HEADS // 2
    prow = n_pairs * NSA_Q_BLOCK
    win_keys = min(WINDOW + NSA_Q_BLOCK, seq)
    kern = functools.partial(_nsa_kernel, seq=seq)
    return pl.pallas_call(
        kern,
        grid=(bsz, nqb),
        in_specs=[pl.BlockSpec((NSA_Q_BLOCK, 1024), lambda b, i: (b * nqb + i, U16_NQ_BLK)),
                  pl.BlockSpec((NSA_Q_BLOCK, 128), lambda b, i: (b * nqb + i, U32_NG_BLK128)),
                  pl.BlockSpec((seq, 512), lambda b, i: (b, U32_NKV_BLK512)),
                  pl.BlockSpec((1, NSA_KV_HEADS, nch, HEAD_DIM), lambda b, i: (b, 0, 0, 0)),
                  pl.BlockSpec((1, NSA_KV_HEADS, nch, HEAD_DIM), lambda b, i: (b, 0, 0, 0)),
                  pl.BlockSpec((1, HEAD_DIM), lambda b, i: (0, 0)),
                  pl.BlockSpec((3, HEAD_DIM), lambda b, i: (0, 0)),
                  pl.BlockSpec((nch, n_blk), lambda b, i: (0, 0)),
                  pl.BlockSpec((n_blk, seq), lambda b, i: (0, 0)),
                  pl.BlockSpec((3, LANES, NSA_HEADS * HEAD_DIM), lambda b, i: (0, 0, 0))],
        out_specs=pl.BlockSpec((NSA_Q_BLOCK, NSA_HEADS * HEAD_DIM), lambda b, i: (b * nqb + i, 0)),
        out_shape=jax.ShapeDtypeStruct((bsz * seq, NSA_HEADS * HEAD_DIM), BF16),
        scratch_shapes=[pltpu.VMEM((NSA_KV_HEADS, 2 * seq, LANES), BF16),
                        pltpu.VMEM((NSA_KV_HEADS, 2 * seq, 2 * LANES), BF16),
                        pltpu.VMEM((NSA_KV_HEADS, 2 * seq, LANES), BF16),
                        pltpu.VMEM((NSA_KV_HEADS, 2 * seq, 2 * LANES), BF16),
                        pltpu.VMEM((NSA_KV_HEADS, 2 * nch, LANES), BF16),
                        pltpu.VMEM((NSA_KV_HEADS, 2 * nch, LANES), BF16),
                        pltpu.VMEM((NSA_KV_HEADS, seq // KEY_TILE, NSA_Q_BLOCK, KEY_TILE), F32),
                        pltpu.VMEM((NSA_Q_BLOCK, 2 * win_keys), F32),
                        pltpu.VMEM((NSA_KV_HEADS, NSA_Q_BLOCK, nch), F32),
                        pltpu.VMEM((prow, LANES), F32),
                        pltpu.VMEM((prow, 2 * LANES), F32),
                        pltpu.VMEM((prow, LANES), BF16),
                        pltpu.VMEM((prow, 2 * win_keys), F32),
                        pltpu.VMEM((prow, 2 * win_keys), BF16),
                        pltpu.VMEM((prow, 2 * nch), F32),
                        pltpu.VMEM((prow, 2 * nch), BF16)]
                       + _flash_scratch(n_pairs, NSA_Q_BLOCK),
        compiler_params=_cparams(("parallel", "arbitrary")),
        name="nsa",
    )(u16, u32, u32, kc, vc, gq, gk, cover, expand, gexp)


def _softplus(x):
    return jnp.maximum(x, 0.0) + jnp.log1p(jnp.exp(-jnp.abs(x)))


def _ssd_kernel(xbc_ref, z_ref, dt_ref, dtt_ref, cw_ref, cb_ref, dtb_ref, dtbt_ref, al_ref, alt_ref,
                d_ref, ng_ref, o_ref, xcat_ref, xa_ref, y_ref, st_ref):
    c = pl.program_id(1)
    q = SSD_CHUNK
    gn = SSD_GROUPS * SSD_STATE
    tail = 16
    lo_half = lax.broadcasted_iota(jnp.int32, (1, LANES), 1) < SSD_HEAD_DIM

    @pl.when(c == 0)
    def _():
        st_ref[...] = jnp.zeros_like(st_ref)
        xcat_ref[0:q, :] = jnp.zeros((q, SSD_CONV_DIM), BF16)

    @pl.when(c > 0)
    def _():
        xcat_ref[q - tail:q, :] = xcat_ref[2 * q - tail:2 * q, :]

    xcat_ref[q:2 * q, :] = xbc_ref[...]
    ri3 = lax.broadcasted_iota(jnp.int32, ((CONV_WIDTH - 1) * q, 2 * q), 0)
    ci3 = lax.broadcasted_iota(jnp.int32, ((CONV_WIDTH - 1) * q, 2 * q), 1)
    qbits = int(math.log2(q))
    src = q + jnp.bitwise_and(ri3, q - 1) - (jnp.right_shift(ri3, qbits) + 1)
    shifts = jnp.where(ci3 == src, 1.0, 0.0).astype(BF16)
    for cc in range(SSD_CONV_DIM // CONV_CHUNK):
        cs = slice(cc * CONV_CHUNK, (cc + 1) * CONV_CHUNK)
        sh = _dot(shifts, xcat_ref[:, cs])
        acc = cb_ref[:, cs] + cw_ref[CONV_WIDTH - 1:CONV_WIDTH, cs] * xcat_ref[q:2 * q, cs].astype(F32)
        for s in range(1, CONV_WIDTH):
            acc = acc + cw_ref[CONV_WIDTH - 1 - s:CONV_WIDTH - s, cs] * sh[(s - 1) * q:s * q]
        xa_ref[:, cs] = acc * jax.nn.sigmoid(acc)

    dt = _softplus(dt_ref[:, :SSD_HEADS] + dtb_ref[...])
    dtt = _softplus(dtt_ref[0] + dtbt_ref[...])
    a = -jnp.exp(al_ref[...]) * LOG2E
    at = -jnp.exp(alt_ref[...]) * LOG2E
    ri = lax.broadcasted_iota(jnp.int32, (q, q), 0)
    ci = lax.broadcasted_iota(jnp.int32, (q, q), 1)
    tril = ri >= ci
    acum = _dot_hi(jnp.where(tril, 1.0, 0.0), dt * a)
    acumt = _dot_hi(dtt * at, jnp.where(ri <= ci, 1.0, 0.0))
    wrow = jnp.exp2(acumt[:, q - 1:q] - acumt) * dtt

    for g in range(SSD_GROUPS):
        bmat = xa_ref[:, SSD_INNER + g * SSD_STATE:SSD_INNER + (g + 1) * SSD_STATE]
        cmat = xa_ref[:, SSD_INNER + gn + g * SSD_STATE:SSD_INNER + gn + (g + 1) * SSD_STATE].astype(BF16)
        cb = _dot_nt(cmat, bmat.astype(BF16))
        bt = bmat.T
        for pp in range(SSD_HPG // 2):
            j = g * (SSD_HPG // 2) + pp
            ls = slice(j * LANES, (j + 1) * LANES)
            xs = xa_ref[:, ls]
            xbd = jnp.concatenate([jnp.where(lo_half, xs, 0.0), jnp.where(lo_half, 0.0, xs)], axis=0).astype(BF16)
            mm, dec, btw = [], [], []
            for par in range(2):
                h = 2 * j + par
                abc = jnp.broadcast_to(acum[:, h:h + 1], (q, q))
                lmat = jnp.exp2(jnp.where(tril, abc - acumt[h:h + 1, :], -jnp.inf))
                mm.append((cb * lmat * dtt[h:h + 1, :]).astype(BF16))
                dec.append(jnp.exp2(abc))
                btw.append((bt * wrow[h:h + 1, :]).astype(BF16))
            y = _dot(jnp.concatenate(mm, axis=1), xbd)
            y = y + _dot(cmat, st_ref[j].astype(BF16)) * jnp.where(lo_half, dec[0], dec[1])
            y_ref[:, ls] = y + xs * d_ref[:, ls]
            cdec = jnp.where(lo_half, dec[0][q - 1:q, :], dec[1][q - 1:q, :])
            st_ref[j] = st_ref[j] * cdec + _dot(jnp.concatenate(btw, axis=1), xbd)

    z = z_ref[...].astype(F32)
    yz = y_ref[...] * (z * jax.nn.sigmoid(z))
    o_ref[...] = _rms(yz, ng_ref[...]).astype(o_ref.dtype)


def _ssd(u16, u32, dtt, cw, cb, dtb, al, d, ng, bsz, seq):
    nc = seq // SSD_CHUNK
    q = SSD_CHUNK
    full = lambda shape: pl.BlockSpec(shape, lambda b, c: (0,) * len(shape))
    d_full = jnp.repeat(d, SSD_HEAD_DIM).reshape(1, SSD_INNER)
    return pl.pallas_call(
        _ssd_kernel,
        grid=(bsz, nc),
        in_specs=[pl.BlockSpec((q, SSD_CONV_DIM), lambda b, c: (b * nc + c, U16_XBC_BLK)),
                  pl.BlockSpec((q, SSD_INNER), lambda b, c: (b * nc + c, U16_Z_BLK)),
                  pl.BlockSpec((q, 128), lambda b, c: (b * nc + c, U32_DT_BLK128)),
                  pl.BlockSpec((1, SSD_HEADS, q), lambda b, c: (b, 0, c)),
                  full((CONV_WIDTH, SSD_CONV_DIM)), full((1, SSD_CONV_DIM)),
                  full((1, SSD_HEADS)), full((SSD_HEADS, 1)),
                  full((1, SSD_HEADS)), full((SSD_HEADS, 1)),
                  full((1, SSD_INNER)), full((1, SSD_INNER))],
        out_specs=pl.BlockSpec((q, SSD_INNER), lambda b, c: (b * nc + c, 0)),
        out_shape=jax.ShapeDtypeStruct((bsz * seq, SSD_INNER), BF16),
        scratch_shapes=[pltpu.VMEM((2 * q, SSD_CONV_DIM), BF16),
                        pltpu.VMEM((q, SSD_CONV_DIM), F32),
                        pltpu.VMEM((q, SSD_INNER), F32),
                        pltpu.VMEM((SSD_HEADS // 2, SSD_STATE, LANES), F32)],
        compiler_params=_cparams(("parallel", "arbitrary")),
        name="ssd",
    )(u16, u16, u32, dtt, cw, cb, dtb.reshape(1, -1), dtb.reshape(-1, 1), al.reshape(1, -1), al.reshape(-1, 1),
      d_full, ng.reshape(1, -1))


def _merge_kernel(x_ref, ya_ref, yb_ref, yc_ref, mg_ref, wa_ref, wb_ref, wc_ref, wo_ref, o_ref):
    d = D_MODEL
    mg = mg_ref[...].astype(F32)
    mix = jax.nn.sigmoid(mg[:, 0:d]) * _dot(ya_ref[...], wa_ref[...])
    mix = mix + jax.nn.sigmoid(mg[:, d:2 * d]) * _dot(yb_ref[...], wb_ref[...])
    mix = mix + jax.nn.sigmoid(mg[:, 2 * d:3 * d]) * _dot(yc_ref[...], wc_ref[...])
    o_ref[...] = x_ref[...] + _dot(mix.astype(BF16), wo_ref[...])


def _merge(x, ya, yb, yc, u16, wa, wb, wc, wo, tm):
    m, d = x.shape
    full = lambda a: pl.BlockSpec(a.shape, lambda i: (0, 0))
    return pl.pallas_call(
        _merge_kernel,
        grid=(m // tm,),
        in_specs=[pl.BlockSpec((tm, d), lambda i: (i, 0)),
                  pl.BlockSpec((tm, ya.shape[1]), lambda i: (i, 0)),
                  pl.BlockSpec((tm, yb.shape[1]), lambda i: (i, 0)),
                  pl.BlockSpec((tm, yc.shape[1]), lambda i: (i, 0)),
                  pl.BlockSpec((tm, 3 * d), lambda i: (i, U16_MG_BLK)),
                  full(wa), full(wb), full(wc), full(wo)],
        out_specs=pl.BlockSpec((tm, d), lambda i: (i, 0)),
        out_shape=jax.ShapeDtypeStruct((m, d), F32),
        compiler_params=_cparams(("parallel",)),
        name="merge",
    )(x, ya, yb, yc, u16, wa, wb, wc, wo)


def _ffn_kernel(x_ref, g_ref, w1_ref, w2_ref, o_ref, h_ref, acc_ref):
    j = pl.program_id(1)

    @pl.when(j == 0)
    def _():
        h_ref[...] = _rms(x_ref[...], g_ref[...]).astype(BF16)
        acc_ref[...] = x_ref[...]

    a = jnp.maximum(_dot(h_ref[...], w1_ref[0].astype(BF16)), 0.0)
    acc_ref[...] += _dot((a * a).astype(BF16), w2_ref[0].astype(BF16))

    @pl.when(j == pl.num_programs(1) - 1)
    def _():
        o_ref[...] = acc_ref[...]


def _ffn(x, g, w1, w2, layer, tm, tf):
    m, d = x.shape
    f = w1.shape[2]
    return pl.pallas_call(
        _ffn_kernel,
        grid=(m // tm, f // tf),
        in_specs=[pl.BlockSpec((tm, d), lambda i, j: (i, 0)),
                  pl.BlockSpec((1, d), lambda i, j: (0, 0)),
                  pl.BlockSpec((1, d, tf), lambda i, j: (layer, 0, j)),
                  pl.BlockSpec((1, tf, d), lambda i, j: (layer, j, 0))],
        out_specs=pl.BlockSpec((tm, d), lambda i, j: (i, 0)),
        out_shape=jax.ShapeDtypeStruct((m, d), F32),
        scratch_shapes=[pltpu.VMEM((tm, d), BF16), pltpu.VMEM((tm, d), F32)],
        compiler_params=_cparams(("parallel", "arbitrary")),
        name="ffn",
    )(x, g, w1, w2)


def _prep_w_in(w):
    offs = [0] + [int(o) for o in np.cumsum(IN_SPLITS)]
    seg = lambda k: w[:, offs[k]:offs[k + 1]]
    dq, dk, dv, iq, ik, iw, nq, nkv, ng, sz, sxbc, sdt, mg = [seg(k) for k in range(13)]
    zeros = lambda n: jnp.zeros((w.shape[0], n), w.dtype)
    w16 = jnp.concatenate([sxbc, mg, sz, dq, nq], axis=1).astype(BF16)
    w32 = jnp.concatenate([nkv[:, 256:768], nkv[:, 0:256], zeros(256),
                           dk, dv, iq, ik, iw, zeros(512 - 424),
                           ng, zeros(128 - ng.shape[1]),
                           sdt, zeros(128 - sdt.shape[1])], axis=1).astype(BF16)
    return w16, w32


def _pick_tile(n, pref):
    t = min(n, pref)
    while n % t:
        t //= 2
    return t


def kernel(x, norm1_g, w_in, dsa_q_norm, dsa_k_norm, nsa_q_norm, nsa_k_norm, nsa_cmp_pos, nsa_cmp_w,
           ssd_conv_w, ssd_conv_b, ssd_dt_bias, ssd_a_log, ssd_d, ssd_norm_g,
           w_br_dsa, w_br_nsa, w_br_ssd, w_out, norm2_g, w_ff1, w_ff2):
    bsz, seq, d = x.shape
    m = bsz * seq
    depth = w_in.shape[0]
    nch = seq // CMP_STRIDE
    n_blk = seq // SEL_BLOCK
    half = CMP_BLOCK // 2

    cmp_start = np.arange(nch) * CMP_STRIDE
    blk_start = np.arange(n_blk) * SEL_BLOCK
    cover = ((cmp_start[:, None] < blk_start[None, :] + SEL_BLOCK)
             & (cmp_start[:, None] + CMP_BLOCK > blk_start[None, :])
             & (np.arange(nch)[:, None] < (seq - CMP_BLOCK) // CMP_STRIDE + 1)).astype(np.float32)
    expand = (np.arange(seq)[None, :] // SEL_BLOCK == np.arange(n_blk)[:, None]).astype(np.float32)
    gexp = np.zeros((3, LANES, NSA_HEADS * HEAD_DIM), np.float32)
    for hh in range(NSA_HEADS):
        for br in range(3):
            gexp[br, 3 * hh + br, hh * HEAD_DIM:(hh + 1) * HEAD_DIM] = 1.0
    gexp = jnp.asarray(gexp, dtype=BF16)
    cover = jnp.asarray(cover)
    expand = jnp.asarray(expand, dtype=BF16)

    tm = _pick_tile(m, 1024)
    xf = x.reshape(m, d)
    for l in range(depth):
        w16, w32 = _prep_w_in(w_in[l])
        g1 = norm1_g[l].reshape(1, d)
        u16 = _norm_matmul(xf, g1, w16, BF16, tm, 2048)
        u32 = _norm_matmul(xf, g1, w32, F32, tm, U32_WIDTH)

        iw0 = U32_DS_BLK512 * 512 + DS_IW
        iwt = u32[:, iw0:iw0 + IDX_HEADS].reshape(bsz, seq, IDX_HEADS).transpose(0, 2, 1)
        ya = _dsa(u16, u32, iwt, dsa_q_norm[l].reshape(1, -1), dsa_k_norm[l].reshape(1, -1), bsz, seq)

        wl = nsa_cmp_w[l]
        zblk = jnp.zeros_like(wl)
        cw = jnp.concatenate([jnp.concatenate([wl, zblk], axis=3),
                              jnp.concatenate([zblk, wl], axis=3)], axis=2).astype(BF16)
        cpos = jnp.concatenate([nsa_cmp_pos[l]] * NSA_KV_HEADS, axis=-1)
        kc, vc = _nsa_compress(u32, cpos, cw, nsa_k_norm[l][0:1], bsz, seq)
        yb = _nsa(u16, u32, kc, vc, nsa_q_norm[l].reshape(1, -1), nsa_k_norm[l], cover, expand, gexp, bsz, seq)

        dt0 = U32_DT_BLK128 * 128
        dtt = u32[:, dt0:dt0 + SSD_HEADS].reshape(bsz, seq, SSD_HEADS).transpose(0, 2, 1)
        yc = _ssd(u16, u32, dtt, ssd_conv_w[l], ssd_conv_b[l].reshape(1, -1), ssd_dt_bias[l], ssd_a_log[l],
                  ssd_d[l], ssd_norm_g[l], bsz, seq)

        xf = _merge(xf, ya, yb, yc, u16,
                    w_br_dsa[l].astype(BF16), w_br_nsa[l].astype(BF16), w_br_ssd[l].astype(BF16),
                    w_out[l].astype(BF16), _pick_tile(m, 256))
        xf = _ffn(xf, norm2_g[l].reshape(1, d), w_ff1, w_ff2, l,
                  _pick_tile(m, 1024), 1024)
    return xf.reshape(bsz, seq, d)
```

```python
import functools
import math

import numpy as np
import jax
import jax.numpy as jnp
from jax import lax
from jax.experimental import pallas as pl
from jax.experimental.pallas import tpu as pltpu

F32 = jnp.float32
BF16 = jnp.bfloat16

D_MODEL = 1024
HEAD_DIM = 64
DSA_HEADS = 16
DSA_KV_DIM = 64
IDX_HEADS = 8
IDX_DIM = 32
DSA_TOPK = 256
NSA_HEADS = 16
NSA_KV_HEADS = 2
NSA_GROUP = NSA_HEADS // NSA_KV_HEADS
CMP_BLOCK = 32
CMP_STRIDE = 16
SEL_BLOCK = 64
SEL_TOPN = 4
WINDOW = 512
SSD_INNER = 2 * D_MODEL
SSD_HEAD_DIM = 64
SSD_HEADS = SSD_INNER // SSD_HEAD_DIM
SSD_GROUPS = 4
SSD_HPG = SSD_HEADS // SSD_GROUPS
SSD_STATE = 128
SSD_CONV_DIM = SSD_INNER + 2 * SSD_GROUPS * SSD_STATE
CONV_WIDTH = 4
SSD_CHUNK = 128
Q_BLOCK = 128
DSA_Q_BLOCK = 256
NSA_Q_BLOCK = 128
EPS = 1e-6
NEG = -1e30
IN_SPLITS = (DSA_HEADS * HEAD_DIM, DSA_KV_DIM, DSA_KV_DIM, IDX_HEADS * IDX_DIM, IDX_DIM, IDX_HEADS,
             NSA_HEADS * HEAD_DIM, 6 * NSA_KV_HEADS * HEAD_DIM, 3 * NSA_HEADS,
             SSD_INNER, SSD_CONV_DIM, SSD_HEADS, 3 * D_MODEL)

V7X_VMEM_LIMIT_BYTES = 56 * 1024 * 1024
LANES = 128

U16_WIDTH = 10240
U16_XBC_BLK = 0
U16_MG_BLK = 1
U16_Z_BLK = 3
U16_DQ_BLK = 8
U16_NQ_BLK = 9
U32_WIDTH = 1792
U32_NKV_BLK512 = 0
U32_KC_BLK128 = 4
U32_DS_BLK512 = 2
U32_NG_BLK128 = 12
U32_DT_BLK128 = 13
DS_DK, DS_DV, DS_IQ, DS_IK, DS_IW = 0, 64, 128, 384, 416

KEY_TILE = 256
BISECT_ITERS = 24
PAIR_STEP = 2
FLASH_UNROLL = 4
LOG2E = 1.4426950408889634
BISECT_UNROLL = 4
CONV_CHUNK = 256


def _cparams(sem):
    return pltpu.CompilerParams(dimension_semantics=sem, vmem_limit_bytes=V7X_VMEM_LIMIT_BYTES)


def _rms(x, g):
    return x * lax.rsqrt(jnp.mean(x * x, axis=-1, keepdims=True) + EPS) * g


def _dot_nt(a, b):
    return lax.dot_general(a, b, (((1,), (1,)), ((), ())), preferred_element_type=F32)


def _dot(a, b):
    return jnp.dot(a, b, preferred_element_type=F32)


def _dot_hi(a, b):
    return jnp.dot(a, b, preferred_element_type=F32, precision=lax.Precision.HIGHEST)


def _norm_matmul_kernel(x_ref, g_ref, w_ref, o_ref, h_ref):
    @pl.when(pl.program_id(1) == 0)
    def _():
        h_ref[...] = _rms(x_ref[...], g_ref[...]).astype(BF16)

    o_ref[...] = _dot(h_ref[...], w_ref[...]).astype(o_ref.dtype)


def _norm_matmul(x, g, w, out_dtype, tm, tn):
    m, k = x.shape
    n = w.shape[1]
    return pl.pallas_call(
        _norm_matmul_kernel,
        grid=(m // tm, n // tn),
        in_specs=[pl.BlockSpec((tm, k), lambda i, j: (i, 0)),
                  pl.BlockSpec((1, k), lambda i, j: (0, 0)),
                  pl.BlockSpec((k, tn), lambda i, j: (0, j))],
        out_specs=pl.BlockSpec((tm, tn), lambda i, j: (i, j)),
        out_shape=jax.ShapeDtypeStruct((m, n), out_dtype),
        scratch_shapes=[pltpu.VMEM((tm, k), BF16)],
        compiler_params=_cparams(("parallel", "arbitrary")),
        name="norm_matmul",
    )(x, g, w)


def _flash_pairs(qp_ref, kbd_ref, wext_ref, bias_ref, nkt, n_pairs, pairs_per_kv, bufs, m_ref, acc_ref):
    (s0, s1), (p0, p1), (a0, a1) = bufs
    qb = bias_ref.shape[2]
    n_pg = n_pairs // PAIR_STEP
    rows = PAIR_STEP * qb
    n = nkt * n_pg
    lo_half = lax.broadcasted_iota(jnp.int32, (1, LANES), 1) < HEAD_DIM
    m_ref[...] = jnp.full(m_ref.shape, -jnp.inf, F32)
    acc_ref[...] = jnp.zeros(acc_ref.shape, F32)
    p1[...] = jnp.zeros(p1.shape, BF16)
    a1[...] = jnp.ones(a1.shape, F32)

    def where(j):
        kt = j // n_pg
        pg = j % n_pg
        g = (pg * PAIR_STEP) // pairs_per_kv
        return kt, pg, g, pl.multiple_of(kt * 2 * KEY_TILE, 2 * KEY_TILE), pl.multiple_of(pg * rows, rows)

    def qk(j, s_ref):
        kt, _, g, koff, r = where(j)
        s = _dot_nt(qp_ref[pl.ds(r, rows), :], kbd_ref[g, pl.ds(koff, 2 * KEY_TILE), :])
        s_ref[...] = s + jnp.tile(bias_ref[g, kt], (PAIR_STEP, 2))

    def softmax(j, s_ref, p_ref, a_ref):
        _, pg, _, _, _ = where(j)
        for u in range(PAIR_STEP):
            us = slice(u * qb, (u + 1) * qb)
            alpha = []
            for par in range(2):
                cs = slice(par * KEY_TILE, (par + 1) * KEY_TILE)
                hrow = pl.multiple_of(((pg * PAIR_STEP + u) * 2 + par) * qb, qb)
                m_old = m_ref[pl.ds(hrow, qb), :]
                m_new = jnp.maximum(m_old, jnp.max(s_ref[us, cs], axis=-1, keepdims=True))
                alpha.append(jnp.exp2(m_old - m_new))
                m_ref[pl.ds(hrow, qb), :] = m_new
                p_ref[us, cs] = jnp.exp2(
                    s_ref[us, cs] - jnp.concatenate([m_new] * (KEY_TILE // LANES), axis=1)).astype(BF16)
            a_ref[us, :] = jnp.where(lo_half, alpha[0], alpha[1])

    def pv(j, p_ref, a_ref):
        _, _, g, koff, r = where(j)
        a = a_ref[...]
        acc_ref[pl.ds(r, rows), :] = (jnp.concatenate([a, a], axis=1) * acc_ref[pl.ds(r, rows), :]
                                      + _dot(p_ref[...], wext_ref[g, pl.ds(koff, 2 * KEY_TILE), :]))

    qk(0, s0)

    def body(jj, c):
        for j in (FLASH_UNROLL * jj, FLASH_UNROLL * jj + 2):
            qk(j + 1, s1)
            softmax(j, s0, p0, a0)
            pv(jnp.maximum(j - 1, 0), p1, a1)
            qk(jnp.minimum(j + 2, n - 1), s0)
            softmax(j + 1, s1, p1, a1)
            pv(j, p0, a0)
        return c

    lax.fori_loop(0, n // FLASH_UNROLL, body, 0)
    pv(n - 1, p1, a1)


def _flash_scratch(n_pairs, qb):
    step = PAIR_STEP * qb
    return ([pltpu.VMEM((step, 2 * KEY_TILE), F32)] * 2 + [pltpu.VMEM((step, 2 * KEY_TILE), BF16)] * 2
            + [pltpu.VMEM((step, LANES), F32)] * 2
            + [pltpu.VMEM((2 * n_pairs * qb, LANES), F32), pltpu.VMEM((n_pairs * qb, 2 * LANES), F32)])


def _norm_pairs(q_ref, g_ref, qp_ref, n_pairs):
    qb = q_ref.shape[0]
    mult = HEAD_DIM ** -0.5 * LOG2E
    lo_half = lax.broadcasted_iota(jnp.int32, (1, LANES), 1) < HEAD_DIM
    g2 = jnp.concatenate([g_ref[...], g_ref[...]], axis=-1) * mult
    for j in range(n_pairs):
        x = q_ref[:, j * LANES:(j + 1) * LANES].astype(F32)
        x2 = x * x
        s_lo = jnp.sum(jnp.where(lo_half, x2, 0.0), axis=-1, keepdims=True)
        s_hi = jnp.sum(jnp.where(lo_half, 0.0, x2), axis=-1, keepdims=True)
        r = jnp.where(lo_half, lax.rsqrt(s_lo * (1.0 / HEAD_DIM) + EPS), lax.rsqrt(s_hi * (1.0 / HEAD_DIM) + EPS))
        qp_ref[j * qb:(j + 1) * qb, :] = (x * r * g2).astype(BF16)


def _store_pair_kv(kbd_ref, wext_ref, g, row0, n, k, v):
    z = jnp.zeros((n, HEAD_DIM), F32)
    one = jnp.ones((n, HEAD_DIM), F32)
    kbd_ref[g, pl.ds(row0, n), :] = jnp.concatenate([k, z], axis=-1).astype(BF16)
    kbd_ref[g, pl.ds(row0 + n, n), :] = jnp.concatenate([z, k], axis=-1).astype(BF16)
    wext_ref[g, pl.ds(row0, n), :] = jnp.concatenate([v, z, one, z], axis=-1).astype(BF16)
    wext_ref[g, pl.ds(row0 + n, n), :] = jnp.concatenate([z, v, z, one], axis=-1).astype(BF16)


def _dsa_kernel(q_ref, sq_ref, iwt_ref, sk_ref, gq_ref, gk_ref, o_ref,
                kbd_ref, wext_ref, ik_ref, idx_ref, bias_ref, iqs_ref,
                qp_ref, s0_ref, s1_ref, p0_ref, p1_ref, a0_ref, a1_ref, m_ref, acc_ref, *, seq, n_sel):
    i = pl.program_id(1)
    nq = q_ref.shape[0]
    n_pairs = DSA_HEADS // 2
    kt_shape = (KEY_TILE, nq)

    @pl.when(i == 0)
    def _():
        def rows(r, c):
            off = pl.multiple_of(r * KEY_TILE, KEY_TILE)
            sk = sk_ref[pl.ds(off, KEY_TILE), :]
            _store_pair_kv(kbd_ref, wext_ref, 0, pl.multiple_of(2 * off, 2 * KEY_TILE), KEY_TILE,
                           _rms(sk[:, DS_DK:DS_DK + HEAD_DIM], gk_ref[...]), sk[:, DS_DV:DS_DV + HEAD_DIM])
            ik_ref[pl.ds(off, KEY_TILE), :] = sk[:, DS_IK:DS_IK + IDX_DIM].astype(BF16)
            return c

        lax.fori_loop(0, seq // KEY_TILE, rows, 0)

    start = i * nq
    nkt = (start + nq + KEY_TILE - 1) // KEY_TILE
    t_row = (start + lax.broadcasted_iota(jnp.int32, (1, nq), 1)).astype(F32)
    sub_pos = lax.broadcasted_iota(jnp.int32, (KEY_TILE, 1), 0).astype(F32)

    sq = sq_ref[...]
    for h in range(IDX_HEADS):
        iqs_ref[h * nq:(h + 1) * nq, :] = sq[:, DS_IQ + h * IDX_DIM:DS_IQ + (h + 1) * IDX_DIM].astype(BF16)
    iwt = iwt_ref[0]

    def idx_tile(kt, carry):
        rmin, rmax = carry
        off = pl.multiple_of(kt * KEY_TILE, KEY_TILE)
        r = _dot_nt(ik_ref[pl.ds(off, KEY_TILE), :], iqs_ref[...])
        acc = jnp.zeros(kt_shape, F32)
        for h in range(IDX_HEADS):
            acc = acc + iwt[h:h + 1, :] * jnp.maximum(r[:, h * nq:(h + 1) * nq], 0.0)
        valid = (sub_pos + (kt * KEY_TILE).astype(F32)) <= t_row
        idx_ref[kt] = jnp.where(valid, acc, -jnp.inf)
        rmin = jnp.minimum(rmin, jnp.min(jnp.where(valid, acc, jnp.inf), axis=0, keepdims=True))
        rmax = jnp.maximum(rmax, jnp.max(jnp.where(valid, acc, -jnp.inf), axis=0, keepdims=True))
        return rmin, rmax

    npair = (nkt + 1) // 2
    rmin, rmax = lax.fori_loop(0, npair, lambda jj, c: idx_tile(2 * jj + 1, idx_tile(2 * jj, c)),
                               (jnp.full((1, nq), jnp.inf, F32), jnp.full((1, nq), -jnp.inf, F32)))

    kf = float(n_sel)

    def col_sum(x):
        parts = [x[r * 8:(r + 1) * 8] for r in range(x.shape[0] // 8)]
        while len(parts) > 1:
            parts = [parts[k] + parts[k + 1] for k in range(0, len(parts), 2)]
        return jnp.sum(parts[0], axis=0, keepdims=True)

    fold = KEY_TILE // 4

    def count_gt(thr):
        def body(kt, acc):
            hit = jnp.where(idx_ref[kt] > thr, 1.0, 0.0)
            return acc + ((hit[0:fold] + hit[fold:2 * fold]) + (hit[2 * fold:3 * fold] + hit[3 * fold:]))

        return col_sum(lax.fori_loop(0, nkt, body, jnp.zeros((fold, nq), F32)))

    n_valid = t_row + 1.0
    lo0 = rmin - (jnp.abs(rmin) * 0.01 + 1.0)

    def bis_cond(c):
        it, _, _, clo, _ = c
        return jnp.logical_and(it < BISECT_ITERS, jnp.max(clo) > kf)

    def bis_body(c):
        it, lo, hi, clo, chi = c
        for _ in range(BISECT_UNROLL):
            mid = 0.5 * (lo + hi)
            cnt = count_gt(mid)
            ge = cnt >= kf
            lo, hi = jnp.where(ge, mid, lo), jnp.where(ge, hi, mid)
            clo, chi = jnp.where(ge, cnt, clo), jnp.where(ge, chi, cnt)
        return it + BISECT_UNROLL, lo, hi, clo, chi

    _, lo, hi, clo, chi = lax.while_loop(
        bis_cond, bis_body, (jnp.int32(0), lo0, rmax, n_valid, jnp.zeros((1, nq), F32)))

    eye_f = jnp.where(lax.broadcasted_iota(jnp.int32, (nq, nq), 0) == lax.broadcasted_iota(jnp.int32, (nq, nq), 1),
                      1.0, 0.0)
    eye = eye_f.astype(BF16)
    tri = jnp.where(lax.broadcasted_iota(jnp.int32, (KEY_TILE, KEY_TILE), 0)
                    <= lax.broadcasted_iota(jnp.int32, (KEY_TILE, KEY_TILE), 1), 1.0, 0.0).astype(BF16)
    quota_row = jnp.where(clo > kf, kf - chi, float(2 * seq))
    quota = jnp.sum(eye_f * quota_row, axis=1, keepdims=True)

    def mask_tile(kt, carry):
        v = idx_ref[kt]
        code = jnp.where(v > hi, 2.0, jnp.where(v > lo, 1.0, 0.0)).astype(BF16)
        code_t = _dot_nt(eye, code)
        tie_t = jnp.where((code_t > 0.5) & (code_t < 1.5), 1.0, 0.0)
        rank = _dot(tie_t.astype(BF16), tri) + carry
        keep = (code_t > 1.5) | ((tie_t > 0.5) & (rank <= quota))
        bias_ref[0, kt] = jnp.where(keep, 0.0, NEG)
        return carry + jnp.sum(tie_t, axis=1, keepdims=True)

    lax.fori_loop(0, npair, lambda jj, c: mask_tile(2 * jj + 1, mask_tile(2 * jj, c)), jnp.zeros((nq, 1), F32))

    _norm_pairs(q_ref, gq_ref, qp_ref, n_pairs)
    _flash_pairs(qp_ref, kbd_ref, wext_ref, bias_ref, nkt, n_pairs, n_pairs,
                 ((s0_ref, s1_ref), (p0_ref, p1_ref), (a0_ref, a1_ref)), m_ref, acc_ref)
    for j in range(n_pairs):
        acc = acc_ref[j * nq:(j + 1) * nq, :]
        o_ref[:, j * LANES:(j + 1) * LANES] = (acc[:, :LANES] / acc[:, LANES:]).astype(o_ref.dtype)


def _dsa(u16, u32, iwt, gq, gk, bsz, seq):
    n_sel = min(DSA_TOPK, seq // 4)
    nqb = seq // DSA_Q_BLOCK
    n_pairs = DSA_HEADS // 2
    kern = functools.partial(_dsa_kernel, seq=seq, n_sel=n_sel)
    return pl.pallas_call(
        kern,
        grid=(bsz, nqb),
        in_specs=[pl.BlockSpec((DSA_Q_BLOCK, 1024), lambda b, i: (b * nqb + i, U16_DQ_BLK)),
                  pl.BlockSpec((DSA_Q_BLOCK, 512), lambda b, i: (b * nqb + i, U32_DS_BLK512)),
                  pl.BlockSpec((1, IDX_HEADS, DSA_Q_BLOCK), lambda b, i: (b, 0, i)),
                  pl.BlockSpec((seq, 512), lambda b, i: (b, U32_DS_BLK512)),
                  pl.BlockSpec((1, HEAD_DIM), lambda b, i: (0, 0)),
                  pl.BlockSpec((1, DSA_KV_DIM), lambda b, i: (0, 0))],
        out_specs=pl.BlockSpec((DSA_Q_BLOCK, DSA_HEADS * HEAD_DIM), lambda b, i: (b * nqb + i, 0)),
        out_shape=jax.ShapeDtypeStruct((bsz * seq, DSA_HEADS * HEAD_DIM), BF16),
        scratch_shapes=[pltpu.VMEM((1, 2 * seq, LANES), BF16),
                        pltpu.VMEM((1, 2 * seq, 2 * LANES), BF16),
                        pltpu.VMEM((seq, IDX_DIM), BF16),
                        pltpu.VMEM((seq // KEY_TILE, KEY_TILE, DSA_Q_BLOCK), F32),
                        pltpu.VMEM((1, seq // KEY_TILE, DSA_Q_BLOCK, KEY_TILE), F32),
                        pltpu.VMEM((IDX_HEADS * DSA_Q_BLOCK, IDX_DIM), BF16),
                        pltpu.VMEM((n_pairs * DSA_Q_BLOCK, LANES), BF16)]
                       + _flash_scratch(n_pairs, DSA_Q_BLOCK),
        compiler_params=_cparams(("parallel", "arbitrary")),
        name="dsa",
    )(u16, u32, iwt, u32, gq, gk)


def _nsa_compress_kernel(xk_ref, xv_ref, pos_ref, w_ref, g_ref, kc_ref, vc_ref):
    nch = xk_ref.shape[0] // CMP_STRIDE
    for jj, (x_ref, o_ref) in enumerate(((xk_ref, kc_ref), (xv_ref, vc_ref))):
        acc_a = jnp.zeros((nch, LANES), F32)
        acc_b = jnp.zeros((nch, LANES), F32)
        for l in range(CMP_STRIDE):
            x = x_ref[pl.ds(l, nch, stride=CMP_STRIDE), :]
            acc_a = acc_a + _dot((x + pos_ref[jj, l:l + 1, :]).astype(BF16), w_ref[jj, l])
            hi = CMP_STRIDE + l
            acc_b = acc_b + _dot((x + pos_ref[jj, hi:hi + 1, :]).astype(BF16), w_ref[jj, hi])
        out = acc_a + pltpu.roll(acc_b, nch - 1, axis=0)
        for g in range(NSA_KV_HEADS):
            og = out[:, g * HEAD_DIM:(g + 1) * HEAD_DIM]
            o_ref[0, g] = (_rms(og, g_ref[...]) if jj == 0 else og).astype(BF16)


def _nsa_compress(u32, pos, w, g, bsz, seq):
    nch = seq // CMP_STRIDE
    return pl.pallas_call(
        _nsa_compress_kernel,
        grid=(bsz,),
        in_specs=[pl.BlockSpec((seq, LANES), lambda b: (b, U32_KC_BLK128)),
                  pl.BlockSpec((seq, LANES), lambda b: (b, U32_KC_BLK128 + 1)),
                  pl.BlockSpec((2, CMP_BLOCK, LANES), lambda b: (0, 0, 0)),
                  pl.BlockSpec((2, CMP_BLOCK, LANES, LANES), lambda b: (0, 0, 0, 0)),
                  pl.BlockSpec((1, HEAD_DIM), lambda b: (0, 0))],
        out_specs=[pl.BlockSpec((1, NSA_KV_HEADS, nch, HEAD_DIM), lambda b: (b, 0, 0, 0)),
                   pl.BlockSpec((1, NSA_KV_HEADS, nch, HEAD_DIM), lambda b: (b, 0, 0, 0))],
        out_shape=[jax.ShapeDtypeStruct((bsz, NSA_KV_HEADS, nch, HEAD_DIM), BF16),
                   jax.ShapeDtypeStruct((bsz, NSA_KV_HEADS, nch, HEAD_DIM), BF16)],
        compiler_params=_cparams(("parallel",)),
        name="nsa_compress",
    )(u32, u32, pos, w, g)


def _nsa_kernel(q_ref, ng_ref, kv_ref, kc_ref, vc_ref, gq_ref, gk_ref, cover_ref, expand_ref, gexp_ref, o_ref,
                ksbd_ref, wsext_ref, kwbd_ref, wwext_ref, kcbd_ref, vcext_ref, bias_ref, wbias_ref, psum_ref,
                ocmp_ref, accw_ref, qp_ref, sw_ref, pw_ref, sc_ref, pc_ref,
                s0_ref, s1_ref, p0_ref, p1_ref, a0_ref, a1_ref, m_ref, acc_ref, *, seq):
    i = pl.program_id(1)
    nq = q_ref.shape[0]
    n_pairs = NSA_HEADS // 2
    gpairs = NSA_GROUP // 2
    grows = gpairs * nq
    n_cmp_pad = kc_ref.shape[2]
    n_blk = seq // SEL_BLOCK
    n_sel = min(SEL_TOPN, n_blk)
    win_keys = min(WINDOW + nq, seq)
    wt = LANES
    win_tiles = win_keys // wt

    @pl.when(i == 0)
    def _():
        def rows(r, c):
            off = pl.multiple_of(r * KEY_TILE, KEY_TILE)
            kv = kv_ref[pl.ds(off, KEY_TILE), :]
            for g in range(NSA_KV_HEADS):
                c0 = g * HEAD_DIM
                _store_pair_kv(ksbd_ref, wsext_ref, g, pl.multiple_of(2 * off, 2 * KEY_TILE), KEY_TILE,
                               _rms(kv[:, c0:c0 + 64], gk_ref[1:2, :]), kv[:, 128 + c0:128 + c0 + 64])
                kw = _rms(kv[:, 256 + c0:256 + c0 + 64], gk_ref[2:3, :])
                vw = kv[:, 384 + c0:384 + c0 + 64]
                for hf in range(KEY_TILE // wt):
                    _store_pair_kv(kwbd_ref, wwext_ref, g, pl.multiple_of(2 * off + hf * 2 * wt, 2 * wt), wt,
                                   kw[hf * wt:(hf + 1) * wt], vw[hf * wt:(hf + 1) * wt])
            return c

        lax.fori_loop(0, seq // KEY_TILE, rows, 0)
        zc = jnp.zeros((n_cmp_pad, HEAD_DIM), BF16)
        for g in range(NSA_KV_HEADS):
            kcbd_ref[g, 0:n_cmp_pad, :] = jnp.concatenate([kc_ref[0, g], zc], axis=-1)
            kcbd_ref[g, n_cmp_pad:2 * n_cmp_pad, :] = jnp.concatenate([zc, kc_ref[0, g]], axis=-1)
            vcext_ref[g, 0:n_cmp_pad, :] = jnp.concatenate([vc_ref[0, g], zc], axis=-1)
            vcext_ref[g, n_cmp_pad:2 * n_cmp_pad, :] = jnp.concatenate([zc, vc_ref[0, g]], axis=-1)

    start = i * nq
    nkt = (start + nq + KEY_TILE - 1) // KEY_TILE
    t_i = start + lax.broadcasted_iota(jnp.int32, (nq, 1), 0)
    t_col = t_i.astype(F32)
    lane_pos = lax.broadcasted_iota(jnp.int32, (1, KEY_TILE), 1).astype(F32)

    _norm_pairs(q_ref, gq_ref, qp_ref, n_pairs)

    cmp_end = (lax.broadcasted_iota(jnp.int32, (1, n_cmp_pad), 1) * CMP_STRIDE + (CMP_BLOCK - 1)).astype(F32)
    vis = cmp_end <= t_col
    wbase = pl.multiple_of(jnp.maximum(start + nq - win_keys, 0), wt)
    wb2 = pl.multiple_of(2 * wbase, 2 * wt)
    col = lax.broadcasted_iota(jnp.int32, (1, 2 * win_keys), 1)
    wpos = (wbase + jnp.right_shift(col, int(math.log2(2 * wt))) * wt + jnp.bitwise_and(col, wt - 1)).astype(F32)
    wbias_ref[...] = jnp.where((wpos <= t_col) & (wpos > t_col - float(WINDOW)), 0.0, NEG)
    for g in range(NSA_KV_HEADS):
        gs = slice(g * grows, (g + 1) * grows)
        sc_ref[gs, :] = _dot_nt(qp_ref[gs, :], kcbd_ref[g])
        sw_ref[gs, :] = _dot_nt(qp_ref[gs, :], kwbd_ref[g, pl.ds(wb2, 2 * win_keys), :])

    for g in range(NSA_KV_HEADS):
        gs = slice(g * grows, (g + 1) * grows)
        tot = jnp.zeros((nq, n_cmp_pad), F32)
        for j in range(g * gpairs, (g + 1) * gpairs):
            rs = slice(j * nq, (j + 1) * nq)
            for par in range(2):
                cs = slice(par * n_cmp_pad, (par + 1) * n_cmp_pad)
                s = jnp.where(vis, sc_ref[rs, cs], NEG)
                e = jnp.exp2(s - jnp.max(s, axis=-1, keepdims=True))
                p = jnp.where(vis, e / jnp.sum(e, axis=-1, keepdims=True), 0.0)
                pc_ref[rs, cs] = p.astype(BF16)
                tot = tot + p
        psum_ref[g] = tot
        ocmp_ref[gs, :] = _dot(pc_ref[gs, :], vcext_ref[g])

    blk_j = lax.broadcasted_iota(jnp.int32, (1, n_blk), 1)
    cur1 = jnp.right_shift(t_i, int(math.log2(SEL_BLOCK)))
    cur = jnp.concatenate([cur1] * NSA_KV_HEADS, axis=0)
    imp = jnp.concatenate([_dot_hi(psum_ref[g], cover_ref[...]) for g in range(NSA_KV_HEADS)], axis=0)
    forced = (blk_j == cur) | (blk_j == 0)
    imp = jnp.where(forced, jnp.inf, jnp.where(blk_j > cur, -jnp.inf, imp))
    selb = jnp.zeros(imp.shape, jnp.bool_)
    for _ in range(n_sel):
        mx = jnp.max(imp, axis=-1, keepdims=True)
        first = jnp.min(jnp.where(imp == mx, blk_j, n_blk), axis=-1, keepdims=True)
        pick = blk_j == first
        selb = selb | pick
        imp = jnp.where(pick, -jnp.inf, imp)
    selb_bf = jnp.where(selb, 1.0, 0.0).astype(BF16)

    for g in range(NSA_KV_HEADS):
        gs = slice(g * grows, (g + 1) * grows)
        for j in range(g * gpairs, (g + 1) * gpairs):
            rs = slice(j * nq, (j + 1) * nq)
            for par in range(2):
                cols = [slice(t * 2 * wt + par * wt, t * 2 * wt + (par + 1) * wt) for t in range(win_tiles)]
                s = [sw_ref[rs, cs] + wbias_ref[:, cs] for cs in cols]
                mx = s[0]
                for st in s[1:]:
                    mx = jnp.maximum(mx, st)
                m = jnp.max(mx, axis=-1, keepdims=True)
                for cs, st in zip(cols, s):
                    pw_ref[rs, cs] = jnp.exp2(st - m).astype(BF16)
        accw_ref[gs, :] = _dot(pw_ref[gs, :], wwext_ref[g, pl.ds(wb2, 2 * win_keys), :])

    def mask_tile(kt, c):
        off = pl.multiple_of(kt * KEY_TILE, KEY_TILE)
        hit = _dot(selb_bf, expand_ref[:, pl.ds(off, KEY_TILE)])
        ok = (lane_pos + (kt * KEY_TILE).astype(F32)) <= t_col
        for g in range(NSA_KV_HEADS):
            bias_ref[g, kt] = jnp.where((hit[g * nq:(g + 1) * nq] > 0.5) & ok, 0.0, NEG)
        return c

    lax.fori_loop(0, (nkt + 1) // 2, lambda jj, c: mask_tile(2 * jj + 1, mask_tile(2 * jj, c)), 0)
    _flash_pairs(qp_ref, ksbd_ref, wsext_ref, bias_ref, nkt, n_pairs, gpairs,
                 ((s0_ref, s1_ref), (p0_ref, p1_ref), (a0_ref, a1_ref)), m_ref, acc_ref)

    gates = jax.nn.sigmoid(ng_ref[...])
    g_hi = gates.astype(BF16)
    g_lo = (gates - g_hi.astype(F32)).astype(BF16)
    gb = [_dot(g_hi, gexp_ref[b]) + _dot(g_lo, gexp_ref[b]) for b in range(3)]
    for j in range(n_pairs):
        rs = slice(j * nq, (j + 1) * nq)
        ls = slice(j * LANES, (j + 1) * LANES)
        acc = acc_ref[rs, :]
        accw = accw_ref[rs, :]
        out = (gb[0][:, ls] * ocmp_ref[rs, :] + gb[1][:, ls] * (acc[:, :LANES] / acc[:, LANES:])
               + gb[2][:, ls] * (accw[:, :LANES] / accw[:, LANES:]))
        o_ref[:, ls] = out.astype(o_ref.dtype)


def _nsa(u16, u32, kc, vc, gq, gk, cover, expand, gexp, bsz, seq):
    nqb = seq // NSA_Q_BLOCK
    nch = kc.shape[2]
    n_blk = seq // SEL_BLOCK
    n_pairs = NSA_HEADS // 2
    prow = n_pairs * NSA_Q_BLOCK
    win_keys = min(WINDOW + NSA_Q_BLOCK, seq)
    kern = functools.partial(_nsa_kernel, seq=seq)
    return pl.pallas_call(
        kern,
        grid=(bsz, nqb),
        in_specs=[pl.BlockSpec((NSA_Q_BLOCK, 1024), lambda b, i: (b * nqb + i, U16_NQ_BLK)),
                  pl.BlockSpec((NSA_Q_BLOCK, 128), lambda b, i: (b * nqb + i, U32_NG_BLK128)),
                  pl.BlockSpec((seq, 512), lambda b, i: (b, U32_NKV_BLK512)),
                  pl.BlockSpec((1, NSA_KV_HEADS, nch, HEAD_DIM), lambda b, i: (b, 0, 0, 0)),
                  pl.BlockSpec((1, NSA_KV_HEADS, nch, HEAD_DIM), lambda b, i: (b, 0, 0, 0)),
                  pl.BlockSpec((1, HEAD_DIM), lambda b, i: (0, 0)),
                  pl.BlockSpec((3, HEAD_DIM), lambda b, i: (0, 0)),
                  pl.BlockSpec((nch, n_blk), lambda b, i: (0, 0)),
                  pl.BlockSpec((n_blk, seq), lambda b, i: (0, 0)),
                  pl.BlockSpec((3, LANES, NSA_HEADS * HEAD_DIM), lambda b, i: (0, 0, 0))],
        out_specs=pl.BlockSpec((NSA_Q_BLOCK, NSA_HEADS * HEAD_DIM), lambda b, i: (b * nqb + i, 0)),
        out_shape=jax.ShapeDtypeStruct((bsz * seq, NSA_HEADS * HEAD_DIM), BF16),
        scratch_shapes=[pltpu.VMEM((NSA_KV_HEADS, 2 * seq, LANES), BF16),
                        pltpu.VMEM((NSA_KV_HEADS, 2 * seq, 2 * LANES), BF16),
                        pltpu.VMEM((NSA_KV_HEADS, 2 * seq, LANES), BF16),
                        pltpu.VMEM((NSA_KV_HEADS, 2 * seq, 2 * LANES), BF16),
                        pltpu.VMEM((NSA_KV_HEADS, 2 * nch, LANES), BF16),
                        pltpu.VMEM((NSA_KV_HEADS, 2 * nch, LANES), BF16),
                        pltpu.VMEM((NSA_KV_HEADS, seq // KEY_TILE, NSA_Q_BLOCK, KEY_TILE), F32),
                        pltpu.VMEM((NSA_Q_BLOCK, 2 * win_keys), F32),
                        pltpu.VMEM((NSA_KV_HEADS, NSA_Q_BLOCK, nch), F32),
                        pltpu.VMEM((prow, LANES), F32),
                        pltpu.VMEM((prow, 2 * LANES), F32),
                        pltpu.VMEM((prow, LANES), BF16),
                        pltpu.VMEM((prow, 2 * win_keys), F32),
                        pltpu.VMEM((prow, 2 * win_keys), BF16),
                        pltpu.VMEM((prow, 2 * nch), F32),
                        pltpu.VMEM((prow, 2 * nch), BF16)]
                       + _flash_scratch(n_pairs, NSA_Q_BLOCK),
        compiler_params=_cparams(("parallel", "arbitrary")),
        name="nsa",
    )(u16, u32, u32, kc, vc, gq, gk, cover, expand, gexp)


def _softplus(x):
    return jnp.maximum(x, 0.0) + jnp.log1p(jnp.exp(-jnp.abs(x)))


def _ssd_kernel(xbc_ref, z_ref, dt_ref, dtt_ref, cw_ref, cb_ref, dtb_ref, dtbt_ref, al_ref, alt_ref,
                d_ref, ng_ref, o_ref, xcat_ref, xa_ref, y_ref, st_ref):
    c = pl.program_id(1)
    q = SSD_CHUNK
    gn = SSD_GROUPS * SSD_STATE
    tail = 16
    lo_half = lax.broadcasted_iota(jnp.int32, (1, LANES), 1) < SSD_HEAD_DIM

    @pl.when(c == 0)
    def _():
        st_ref[...] = jnp.zeros_like(st_ref)
        xcat_ref[0:q, :] = jnp.zeros((q, SSD_CONV_DIM), BF16)

    @pl.when(c > 0)
    def _():
        xcat_ref[q - tail:q, :] = xcat_ref[2 * q - tail:2 * q, :]

    xcat_ref[q:2 * q, :] = xbc_ref[...]
    ri3 = lax.broadcasted_iota(jnp.int32, ((CONV_WIDTH - 1) * q, 2 * q), 0)
    ci3 = lax.broadcasted_iota(jnp.int32, ((CONV_WIDTH - 1) * q, 2 * q), 1)
    qbits = int(math.log2(q))
    src = q + jnp.bitwise_and(ri3, q - 1) - (jnp.right_shift(ri3, qbits) + 1)
    shifts = jnp.where(ci3 == src, 1.0, 0.0).astype(BF16)
    for cc in range(SSD_CONV_DIM // CONV_CHUNK):
        cs = slice(cc * CONV_CHUNK, (cc + 1) * CONV_CHUNK)
        sh = _dot(shifts, xcat_ref[:, cs])
        acc = cb_ref[:, cs] + cw_ref[CONV_WIDTH - 1:CONV_WIDTH, cs] * xcat_ref[q:2 * q, cs].astype(F32)
        for s in range(1, CONV_WIDTH):
            acc = acc + cw_ref[CONV_WIDTH - 1 - s:CONV_WIDTH - s, cs] * sh[(s - 1) * q:s * q]
        xa_ref[:, cs] = acc * jax.nn.sigmoid(acc)

    dt = _softplus(dt_ref[:, :SSD_HEADS] + dtb_ref[...])
    dtt = _softplus(dtt_ref[0] + dtbt_ref[...])
    a = -jnp.exp(al_ref[...]) * LOG2E
    at = -jnp.exp(alt_ref[...]) * LOG2E
    ri = lax.broadcasted_iota(jnp.int32, (q, q), 0)
    ci = lax.broadcasted_iota(jnp.int32, (q, q), 1)
    tril = ri >= ci
    acum = _dot_hi(jnp.where(tril, 1.0, 0.0), dt * a)
    acumt = _dot_hi(dtt * at, jnp.where(ri <= ci, 1.0, 0.0))
    wrow = jnp.exp2(acumt[:, q - 1:q] - acumt) * dtt

    for g in range(SSD_GROUPS):
        bmat = xa_ref[:, SSD_INNER + g * SSD_STATE:SSD_INNER + (g + 1) * SSD_STATE]
        cmat = xa_ref[:, SSD_INNER + gn + g * SSD_STATE:SSD_INNER + gn + (g + 1) * SSD_STATE].astype(BF16)
        cb = _dot_nt(cmat, bmat.astype(BF16))
        bt = bmat.T
        for pp in range(SSD_HPG // 2):
            j = g * (SSD_HPG // 2) + pp
            ls = slice(j * LANES, (j + 1) * LANES)
            xs = xa_ref[:, ls]
            xbd = jnp.concatenate([jnp.where(lo_half, xs, 0.0), jnp.where(lo_half, 0.0, xs)], axis=0).astype(BF16)
            mm, dec, btw = [], [], []
            for par in range(2):
                h = 2 * j + par
                abc = jnp.broadcast_to(acum[:, h:h + 1], (q, q))
                lmat = jnp.exp2(jnp.where(tril, abc - acumt[h:h + 1, :], -jnp.inf))
                mm.append((cb * lmat * dtt[h:h + 1, :]).astype(BF16))
                dec.append(jnp.exp2(abc))
                btw.append((bt * wrow[h:h + 1, :]).astype(BF16))
            y = _dot(jnp.concatenate(mm, axis=1), xbd)
            y = y + _dot(cmat, st_ref[j].astype(BF16)) * jnp.where(lo_half, dec[0], dec[1])
            y_ref[:, ls] = y + xs * d_ref[:, ls]
            cdec = jnp.where(lo_half, dec[0][q - 1:q, :], dec[1][q - 1:q, :])
            st_ref[j] = st_ref[j] * cdec + _dot(jnp.concatenate(btw, axis=1), xbd)

    z = z_ref[...].astype(F32)
    yz = y_ref[...] * (z * jax.nn.sigmoid(z))
    o_ref[...] = _rms(yz, ng_ref[...]).astype(o_ref.dtype)


def _ssd(u16, u32, dtt, cw, cb, dtb, al, d, ng, bsz, seq):
    nc = seq // SSD_CHUNK
    q = SSD_CHUNK
    full = lambda shape: pl.BlockSpec(shape, lambda b, c: (0,) * len(shape))
    d_full = jnp.repeat(d, SSD_HEAD_DIM).reshape(1, SSD_INNER)
    return pl.pallas_call(
        _ssd_kernel,
        grid=(bsz, nc),
        in_specs=[pl.BlockSpec((q, SSD_CONV_DIM), lambda b, c: (b * nc + c, U16_XBC_BLK)),
                  pl.BlockSpec((q, SSD_INNER), lambda b, c: (b * nc + c, U16_Z_BLK)),
                  pl.BlockSpec((q, 128), lambda b, c: (b * nc + c, U32_DT_BLK128)),
                  pl.BlockSpec((1, SSD_HEADS, q), lambda b, c: (b, 0, c)),
                  full((CONV_WIDTH, SSD_CONV_DIM)), full((1, SSD_CONV_DIM)),
                  full((1, SSD_HEADS)), full((SSD_HEADS, 1)),
                  full((1, SSD_HEADS)), full((SSD_HEADS, 1)),
                  full((1, SSD_INNER)), full((1, SSD_INNER))],
        out_specs=pl.BlockSpec((q, SSD_INNER), lambda b, c: (b * nc + c, 0)),
        out_shape=jax.ShapeDtypeStruct((bsz * seq, SSD_INNER), BF16),
        scratch_shapes=[pltpu.VMEM((2 * q, SSD_CONV_DIM), BF16),
                        pltpu.VMEM((q, SSD_CONV_DIM), F32),
                        pltpu.VMEM((q, SSD_INNER), F32),
                        pltpu.VMEM((SSD_HEADS // 2, SSD_STATE, LANES), F32)],
        compiler_params=_cparams(("parallel", "arbitrary")),
        name="ssd",
    )(u16, u16, u32, dtt, cw, cb, dtb.reshape(1, -1), dtb.reshape(-1, 1), al.reshape(1, -1), al.reshape(-1, 1),
      d_full, ng.reshape(1, -1))


def _merge_kernel(x_ref, ya_ref, yb_ref, yc_ref, mg_ref, wa_ref, wb_ref, wc_ref, wo_ref, o_ref):
    d = D_MODEL
    mg = mg_ref[...].astype(F32)
    mix = jax.nn.sigmoid(mg[:, 0:d]) * _dot(ya_ref[...], wa_ref[...])
    mix = mix + jax.nn.sigmoid(mg[:, d:2 * d]) * _dot(yb_ref[...], wb_ref[...])
    mix = mix + jax.nn.sigmoid(mg[:, 2 * d:3 * d]) * _dot(yc_ref[...], wc_ref[...])
    o_ref[...] = x_ref[...] + _dot(mix.astype(BF16), wo_ref[...])


def _merge(x, ya, yb, yc, u16, wa, wb, wc, wo, tm):
    m, d = x.shape
    full = lambda a: pl.BlockSpec(a.shape, lambda i: (0, 0))
    return pl.pallas_call(
        _merge_kernel,
        grid=(m // tm,),
        in_specs=[pl.BlockSpec((tm, d), lambda i: (i, 0)),
                  pl.BlockSpec((tm, ya.shape[1]), lambda i: (i, 0)),
                  pl.BlockSpec((tm, yb.shape[1]), lambda i: (i, 0)),
                  pl.BlockSpec((tm, yc.shape[1]), lambda i: (i, 0)),
                  pl.BlockSpec((tm, 3 * d), lambda i: (i, U16_MG_BLK)),
                  full(wa), full(wb), full(wc), full(wo)],
        out_specs=pl.BlockSpec((tm, d), lambda i: (i, 0)),
        out_shape=jax.ShapeDtypeStruct((m, d), F32),
        compiler_params=_cparams(("parallel",)),
        name="merge",
    )(x, ya, yb, yc, u16, wa, wb, wc, wo)


def _ffn_kernel(x_ref, g_ref, w1_ref, w2_ref, o_ref, h_ref, acc_ref):
    j = pl.program_id(1)

    @pl.when(j == 0)
    def _():
        h_ref[...] = _rms(x_ref[...], g_ref[...]).astype(BF16)
        acc_ref[...] = x_ref[...]

    a = jnp.maximum(_dot(h_ref[...], w1_ref[0].astype(BF16)), 0.0)
    acc_ref[...] += _dot((a * a).astype(BF16), w2_ref[0].astype(BF16))

    @pl.when(j == pl.num_programs(1) - 1)
    def _():
        o_ref[...] = acc_ref[...]


def _ffn(x, g, w1, w2, layer, tm, tf):
    m, d = x.shape
    f = w1.shape[2]
    return pl.pallas_call(
        _ffn_kernel,
        grid=(m // tm, f // tf),
        in_specs=[pl.BlockSpec((tm, d), lambda i, j: (i, 0)),
                  pl.BlockSpec((1, d), lambda i, j: (0, 0)),
                  pl.BlockSpec((1, d, tf), lambda i, j: (layer, 0, j)),
                  pl.BlockSpec((1, tf, d), lambda i, j: (layer, j, 0))],
        out_specs=pl.BlockSpec((tm, d), lambda i, j: (i, 0)),
        out_shape=jax.ShapeDtypeStruct((m, d), F32),
        scratch_shapes=[pltpu.VMEM((tm, d), BF16), pltpu.VMEM((tm, d), F32)],
        compiler_params=_cparams(("parallel", "arbitrary")),
        name="ffn",
    )(x, g, w1, w2)


def _prep_w_in(w):
    offs = [0] + [int(o) for o in np.cumsum(IN_SPLITS)]
    seg = lambda k: w[:, offs[k]:offs[k + 1]]
    dq, dk, dv, iq, ik, iw, nq, nkv, ng, sz, sxbc, sdt, mg = [seg(k) for k in range(13)]
    zeros = lambda n: jnp.zeros((w.shape[0], n), w.dtype)
    w16 = jnp.concatenate([sxbc, mg, sz, dq, nq], axis=1).astype(BF16)
    w32 = jnp.concatenate([nkv[:, 256:768], nkv[:, 0:256], zeros(256),
                           dk, dv, iq, ik, iw, zeros(512 - 424),
                           ng, zeros(128 - ng.shape[1]),
                           sdt, zeros(128 - sdt.shape[1])], axis=1).astype(BF16)
    return w16, w32


def _pick_tile(n, pref):
    t = min(n, pref)
    while n % t:
        t //= 2
    return t


def kernel(x, norm1_g, w_in, dsa_q_norm, dsa_k_norm, nsa_q_norm, nsa_k_norm, nsa_cmp_pos, nsa_cmp_w,
           ssd_conv_w, ssd_conv_b, ssd_dt_bias, ssd_a_log, ssd_d, ssd_norm_g,
           w_br_dsa, w_br_nsa, w_br_ssd, w_out, norm2_g, w_ff1, w_ff2):
    bsz, seq, d = x.shape
    m = bsz * seq
    depth = w_in.shape[0]
    nch = seq // CMP_STRIDE
    n_blk = seq // SEL_BLOCK
    half = CMP_BLOCK // 2

    cmp_start = np.arange(nch) * CMP_STRIDE
    blk_start = np.arange(n_blk) * SEL_BLOCK
    cover = ((cmp_start[:, None] < blk_start[None, :] + SEL_BLOCK)
             & (cmp_start[:, None] + CMP_BLOCK > blk_start[None, :])
             & (np.arange(nch)[:, None] < (seq - CMP_BLOCK) // CMP_STRIDE + 1)).astype(np.float32)
    expand = (np.arange(seq)[None, :] // SEL_BLOCK == np.arange(n_blk)[:, None]).astype(np.float32)
    gexp = np.zeros((3, LANES, NSA_HEADS * HEAD_DIM), np.float32)
    for hh in range(NSA_HEADS):
        for br in range(3):
            gexp[br, 3 * hh + br, hh * HEAD_DIM:(hh + 1) * HEAD_DIM] = 1.0
    gexp = jnp.asarray(gexp, dtype=BF16)
    cover = jnp.asarray(cover)
    expand = jnp.asarray(expand, dtype=BF16)

    tm = _pick_tile(m, 1024)
    xf = x.reshape(m, d)
    for l in range(depth):
        w16, w32 = _prep_w_in(w_in[l])
        g1 = norm1_g[l].reshape(1, d)
        u16 = _norm_matmul(xf, g1, w16, BF16, tm, 2048)
        u32 = _norm_matmul(xf, g1, w32, F32, tm, U32_WIDTH)

        iw0 = U32_DS_BLK512 * 512 + DS_IW
        iwt = u32[:, iw0:iw0 + IDX_HEADS].reshape(bsz, seq, IDX_HEADS).transpose(0, 2, 1)
        ya = _dsa(u16, u32, iwt, dsa_q_norm[l].reshape(1, -1), dsa_k_norm[l].reshape(1, -1), bsz, seq)

        wl = nsa_cmp_w[l]
        zblk = jnp.zeros_like(wl)
        cw = jnp.concatenate([jnp.concatenate([wl, zblk], axis=3),
                              jnp.concatenate([zblk, wl], axis=3)], axis=2).astype(BF16)
        cpos = jnp.concatenate([nsa_cmp_pos[l]] * NSA_KV_HEADS, axis=-1)
        kc, vc = _nsa_compress(u32, cpos, cw, nsa_k_norm[l][0:1], bsz, seq)
        yb = _nsa(u16, u32, kc, vc, nsa_q_norm[l].reshape(1, -1), nsa_k_norm[l], cover, expand, gexp, bsz, seq)

        dt0 = U32_DT_BLK128 * 128
        dtt = u32[:, dt0:dt0 + SSD_HEADS].reshape(bsz, seq, SSD_HEADS).transpose(0, 2, 1)
        yc = _ssd(u16, u32, dtt, ssd_conv_w[l], ssd_conv_b[l].reshape(1, -1), ssd_dt_bias[l], ssd_a_log[l],
                  ssd_d[l], ssd_norm_g[l], bsz, seq)

        xf = _merge(xf, ya, yb, yc, u16,
                    w_br_dsa[l].astype(BF16), w_br_nsa[l].astype(BF16), w_br_ssd[l].astype(BF16),
                    w_out[l].astype(BF16), _pick_tile(m, 256))
        xf = _ffn(xf, norm2_g[l].reshape(1, d), w_ff1, w_ff2, l,
                  _pick_tile(m, 1024), 1024)
    return xf.reshape(bsz, seq, d)
```

```python
import functools
import math

import numpy as np
import jax
import jax.numpy as jnp
from jax import lax
from jax.experimental import pallas as pl
from jax.experimental.pallas import tpu as pltpu

F32 = jnp.float32
BF16 = jnp.bfloat16

D_MODEL = 1024
HEAD_DIM = 64
DSA_HEADS = 16
DSA_KV_DIM = 64
IDX_HEADS = 8
IDX_DIM = 32
DSA_TOPK = 256
NSA_HEADS = 16
NSA_KV_HEADS = 2
NSA_GROUP = NSA_HEADS // NSA_KV_HEADS
CMP_BLOCK = 32
CMP_STRIDE = 16
SEL_BLOCK = 64
SEL_TOPN = 4
WINDOW = 512
SSD_INNER = 2 * D_MODEL
SSD_HEAD_DIM = 64
SSD_HEADS = SSD_INNER // SSD_HEAD_DIM
SSD_GROUPS = 4
SSD_HPG = SSD_HEADS // SSD_GROUPS
SSD_STATE = 128
SSD_CONV_DIM = SSD_INNER + 2 * SSD_GROUPS * SSD_STATE
CONV_WIDTH = 4
SSD_CHUNK = 128
Q_BLOCK = 128
DSA_Q_BLOCK = 256
NSA_Q_BLOCK = 128
EPS = 1e-6
NEG = -1e30
IN_SPLITS = (DSA_HEADS * HEAD_DIM, DSA_KV_DIM, DSA_KV_DIM, IDX_HEADS * IDX_DIM, IDX_DIM, IDX_HEADS,
             NSA_HEADS * HEAD_DIM, 6 * NSA_KV_HEADS * HEAD_DIM, 3 * NSA_HEADS,
             SSD_INNER, SSD_CONV_DIM, SSD_HEADS, 3 * D_MODEL)

V7X_VMEM_LIMIT_BYTES = 56 * 1024 * 1024
LANES = 128

U16_WIDTH = 10240
U16_XBC_BLK = 0
U16_MG_BLK = 1
U16_Z_BLK = 3
U16_DQ_BLK = 8
U16_NQ_BLK = 9
U32_WIDTH = 1792
U32_NKV_BLK512 = 0
U32_KC_BLK128 = 4
U32_DS_BLK512 = 2
U32_NG_BLK128 = 12
U32_DT_BLK128 = 13
DS_DK, DS_DV, DS_IQ, DS_IK, DS_IW = 0, 64, 128, 384, 416

KEY_TILE = 256
BISECT_ITERS = 24
PAIR_STEP = 2
FLASH_UNROLL = 4
LOG2E = 1.4426950408889634
BISECT_UNROLL = 4
CONV_CHUNK = 256


def _cparams(sem):
    return pltpu.CompilerParams(dimension_semantics=sem, vmem_limit_bytes=V7X_VMEM_LIMIT_BYTES)


def _rms(x, g):
    return x * lax.rsqrt(jnp.mean(x * x, axis=-1, keepdims=True) + EPS) * g


def _dot_nt(a, b):
    return lax.dot_general(a, b, (((1,), (1,)), ((), ())), preferred_element_type=F32)


def _dot(a, b):
    return jnp.dot(a, b, preferred_element_type=F32)


def _dot_hi(a, b):
    return jnp.dot(a, b, preferred_element_type=F32, precision=lax.Precision.HIGHEST)


def _norm_matmul_kernel(x_ref, g_ref, w_ref, o_ref, h_ref):
    @pl.when(pl.program_id(1) == 0)
    def _():
        h_ref[...] = _rms(x_ref[...], g_ref[...]).astype(BF16)

    o_ref[...] = _dot(h_ref[...], w_ref[...]).astype(o_ref.dtype)


def _norm_matmul(x, g, w, out_dtype, tm, tn):
    m, k = x.shape
    n = w.shape[1]
    return pl.pallas_call(
        _norm_matmul_kernel,
        grid=(m // tm, n // tn),
        in_specs=[pl.BlockSpec((tm, k), lambda i, j: (i, 0)),
                  pl.BlockSpec((1, k), lambda i, j: (0, 0)),
                  pl.BlockSpec((k, tn), lambda i, j: (0, j))],
        out_specs=pl.BlockSpec((tm, tn), lambda i, j: (i, j)),
        out_shape=jax.ShapeDtypeStruct((m, n), out_dtype),
        scratch_shapes=[pltpu.VMEM((tm, k), BF16)],
        compiler_params=_cparams(("parallel", "arbitrary")),
        name="norm_matmul",
    )(x, g, w)


def _flash_pairs(qp_ref, kbd_ref, wext_ref, bias_ref, nkt, n_pairs, pairs_per_kv, bufs, m_ref, acc_ref):
    (s0, s1), (p0, p1), (a0, a1) = bufs
    qb = bias_ref.shape[2]
    n_pg = n_pairs // PAIR_STEP
    rows = PAIR_STEP * qb
    n = nkt * n_pg
    lo_half = lax.broadcasted_iota(jnp.int32, (1, LANES), 1) < HEAD_DIM
    m_ref[...] = jnp.full(m_ref.shape, -jnp.inf, F32)
    acc_ref[...] = jnp.zeros(acc_ref.shape, F32)
    p1[...] = jnp.zeros(p1.shape, BF16)
    a1[...] = jnp.ones(a1.shape, F32)

    def where(j):
        kt = j // n_pg
        pg = j % n_pg
        g = (pg * PAIR_STEP) // pairs_per_kv
        return kt, pg, g, pl.multiple_of(kt * 2 * KEY_TILE, 2 * KEY_TILE), pl.multiple_of(pg * rows, rows)

    def qk(j, s_ref):
        kt, _, g, koff, r = where(j)
        s = _dot_nt(qp_ref[pl.ds(r, rows), :], kbd_ref[g, pl.ds(koff, 2 * KEY_TILE), :])
        s_ref[...] = s + jnp.tile(bias_ref[g, kt], (PAIR_STEP, 2))

    def softmax(j, s_ref, p_ref, a_ref):
        _, pg, _, _, _ = where(j)
        for u in range(PAIR_STEP):
            us = slice(u * qb, (u + 1) * qb)
            alpha = []
            for par in range(2):
                cs = slice(par * KEY_TILE, (par + 1) * KEY_TILE)
                hrow = pl.multiple_of(((pg * PAIR_STEP + u) * 2 + par) * qb, qb)
                m_old = m_ref[pl.ds(hrow, qb), :]
                m_new = jnp.maximum(m_old, jnp.max(s_ref[us, cs], axis=-1, keepdims=True))
                alpha.append(jnp.exp2(m_old - m_new))
                m_ref[pl.ds(hrow, qb), :] = m_new
                p_ref[us, cs] = jnp.exp2(
                    s_ref[us, cs] - jnp.concatenate([m_new] * (KEY_TILE // LANES), axis=1)).astype(BF16)
            a_ref[us, :] = jnp.where(lo_half, alpha[0], alpha[1])

    def pv(j, p_ref, a_ref):
        _, _, g, koff, r = where(j)
        a = a_ref[...]
        acc_ref[pl.ds(r, rows), :] = (jnp.concatenate([a, a], axis=1) * acc_ref[pl.ds(r, rows), :]
                                      + _dot(p_ref[...], wext_ref[g, pl.ds(koff, 2 * KEY_TILE), :]))

    qk(0, s0)

    def body(jj, c):
        for j in (FLASH_UNROLL * jj, FLASH_UNROLL * jj + 2):
            qk(j + 1, s1)
            softmax(j, s0, p0, a0)
            pv(jnp.maximum(j - 1, 0), p1, a1)
            qk(jnp.minimum(j + 2, n - 1), s0)
            softmax(j + 1, s1, p1, a1)
            pv(j, p0, a0)
        return c

    lax.fori_loop(0, n // FLASH_UNROLL, body, 0)
    pv(n - 1, p1, a1)


def _flash_scratch(n_pairs, qb):
    step = PAIR_STEP * qb
    return ([pltpu.VMEM((step, 2 * KEY_TILE), F32)] * 2 + [pltpu.VMEM((step, 2 * KEY_TILE), BF16)] * 2
            + [pltpu.VMEM((step, LANES), F32)] * 2
            + [pltpu.VMEM((2 * n_pairs * qb, LANES), F32), pltpu.VMEM((n_pairs * qb, 2 * LANES), F32)])


def _norm_pairs(q_ref, g_ref, qp_ref, n_pairs):
    qb = q_ref.shape[0]
    mult = HEAD_DIM ** -0.5 * LOG2E
    lo_half = lax.broadcasted_iota(jnp.int32, (1, LANES), 1) < HEAD_DIM
    g2 = jnp.concatenate([g_ref[...], g_ref[...]], axis=-1) * mult
    for j in range(n_pairs):
        x = q_ref[:, j * LANES:(j + 1) * LANES].astype(F32)
        x2 = x * x
        s_lo = jnp.sum(jnp.where(lo_half, x2, 0.0), axis=-1, keepdims=True)
        s_hi = jnp.sum(jnp.where(lo_half, 0.0, x2), axis=-1, keepdims=True)
        r = jnp.where(lo_half, lax.rsqrt(s_lo * (1.0 / HEAD_DIM) + EPS), lax.rsqrt(s_hi * (1.0 / HEAD_DIM) + EPS))
        qp_ref[j * qb:(j + 1) * qb, :] = (x * r * g2).astype(BF16)


def _store_pair_kv(kbd_ref, wext_ref, g, row0, n, k, v):
    z = jnp.zeros((n, HEAD_DIM), F32)
    one = jnp.ones((n, HEAD_DIM), F32)
    kbd_ref[g, pl.ds(row0, n), :] = jnp.concatenate([k, z], axis=-1).astype(BF16)
    kbd_ref[g, pl.ds(row0 + n, n), :] = jnp.concatenate([z, k], axis=-1).astype(BF16)
    wext_ref[g, pl.ds(row0, n), :] = jnp.concatenate([v, z, one, z], axis=-1).astype(BF16)
    wext_ref[g, pl.ds(row0 + n, n), :] = jnp.concatenate([z, v, z, one], axis=-1).astype(BF16)


def _dsa_kernel(q_ref, sq_ref, iwt_ref, sk_ref, gq_ref, gk_ref, o_ref,
                kbd_ref, wext_ref, ik_ref, idx_ref, bias_ref, iqs_ref,
                qp_ref, s0_ref, s1_ref, p0_ref, p1_ref, a0_ref, a1_ref, m_ref, acc_ref, *, seq, n_sel):
    i = pl.program_id(1)
    nq = q_ref.shape[0]
    n_pairs = DSA_HEADS // 2
    kt_shape = (KEY_TILE, nq)

    @pl.when(i == 0)
    def _():
        def rows(r, c):
            off = pl.multiple_of(r * KEY_TILE, KEY_TILE)
            sk = sk_ref[pl.ds(off, KEY_TILE), :]
            _store_pair_kv(kbd_ref, wext_ref, 0, pl.multiple_of(2 * off, 2 * KEY_TILE), KEY_TILE,
                           _rms(sk[:, DS_DK:DS_DK + HEAD_DIM], gk_ref[...]), sk[:, DS_DV:DS_DV + HEAD_DIM])
            ik_ref[pl.ds(off, KEY_TILE), :] = sk[:, DS_IK:DS_IK + IDX_DIM].astype(BF16)
            return c

        lax.fori_loop(0, seq // KEY_TILE, rows, 0)

    start = i * nq
    nkt = (start + nq + KEY_TILE - 1) // KEY_TILE
    t_row = (start + lax.broadcasted_iota(jnp.int32, (1, nq), 1)).astype(F32)
    sub_pos = lax.broadcasted_iota(jnp.int32, (KEY_TILE, 1), 0).astype(F32)

    sq = sq_ref[...]
    for h in range(IDX_HEADS):
        iqs_ref[h * nq:(h + 1) * nq, :] = sq[:, DS_IQ + h * IDX_DIM:DS_IQ + (h + 1) * IDX_DIM].astype(BF16)
    iwt = iwt_ref[0]

    def idx_tile(kt, carry):
        rmin, rmax = carry
        off = pl.multiple_of(kt * KEY_TILE, KEY_TILE)
        r = _dot_nt(ik_ref[pl.ds(off, KEY_TILE), :], iqs_ref[...])
        acc = jnp.zeros(kt_shape, F32)
        for h in range(IDX_HEADS):
            acc = acc + iwt[h:h + 1, :] * jnp.maximum(r[:, h * nq:(h + 1) * nq], 0.0)
        valid = (sub_pos + (kt * KEY_TILE).astype(F32)) <= t_row
        idx_ref[kt] = jnp.where(valid, acc, -jnp.inf)
        rmin = jnp.minimum(rmin, jnp.min(jnp.where(valid, acc, jnp.inf), axis=0, keepdims=True))
        rmax = jnp.maximum(rmax, jnp.max(jnp.where(valid, acc, -jnp.inf), axis=0, keepdims=True))
        return rmin, rmax

    npair = (nkt + 1) // 2
    rmin, rmax = lax.fori_loop(0, npair, lambda jj, c: idx_tile(2 * jj + 1, idx_tile(2 * jj, c)),
                               (jnp.full((1, nq), jnp.inf, F32), jnp.full((1, nq), -jnp.inf, F32)))

    kf = float(n_sel)

    def col_sum(x):
        parts = [x[r * 8:(r + 1) * 8] for r in range(x.shape[0] // 8)]
        while len(parts) > 1:
            parts = [parts[k] + parts[k + 1] for k in range(0, len(parts), 2)]
        return jnp.sum(parts[0], axis=0, keepdims=True)

    fold = KEY_TILE // 4

    def count_gt(thr):
        def body(kt, acc):
            hit = jnp.where(idx_ref[kt] > thr, 1.0, 0.0)
            return acc + ((hit[0:fold] + hit[fold:2 * fold]) + (hit[2 * fold:3 * fold] + hit[3 * fold:]))

        return col_sum(lax.fori_loop(0, nkt, body, jnp.zeros((fold, nq), F32)))

    n_valid = t_row + 1.0
    lo0 = rmin - (jnp.abs(rmin) * 0.01 + 1.0)

    def bis_cond(c):
        it, _, _, clo, _ = c
        return jnp.logical_and(it < BISECT_ITERS, jnp.max(clo) > kf)

    def bis_body(c):
        it, lo, hi, clo, chi = c
        for _ in range(BISECT_UNROLL):
            mid = 0.5 * (lo + hi)
            cnt = count_gt(mid)
            ge = cnt >= kf
            lo, hi = jnp.where(ge, mid, lo), jnp.where(ge, hi, mid)
            clo, chi = jnp.where(ge, cnt, clo), jnp.where(ge, chi, cnt)
        return it + BISECT_UNROLL, lo, hi, clo, chi

    _, lo, hi, clo, chi = lax.while_loop(
        bis_cond, bis_body, (jnp.int32(0), lo0, rmax, n_valid, jnp.zeros((1, nq), F32)))

    eye_f = jnp.where(lax.broadcasted_iota(jnp.int32, (nq, nq), 0) == lax.broadcasted_iota(jnp.int32, (nq, nq), 1),
                      1.0, 0.0)
    eye = eye_f.astype(BF16)
    tri = jnp.where(lax.broadcasted_iota(jnp.int32, (KEY_TILE, KEY_TILE), 0)
                    <= lax.broadcasted_iota(jnp.int32, (KEY_TILE, KEY_TILE), 1), 1.0, 0.0).astype(BF16)
    quota_row = jnp.where(clo > kf, kf - chi, float(2 * seq))
    quota = jnp.sum(eye_f * quota_row, axis=1, keepdims=True)

    def mask_tile(kt, carry):
        v = idx_ref[kt]
        code = jnp.where(v > hi, 2.0, jnp.where(v > lo, 1.0, 0.0)).astype(BF16)
        code_t = _dot_nt(eye, code)
        tie_t = jnp.where((code_t > 0.5) & (code_t < 1.5), 1.0, 0.0)
        rank = _dot(tie_t.astype(BF16), tri) + carry
        keep = (code_t > 1.5) | ((tie_t > 0.5) & (rank <= quota))
        bias_ref[0, kt] = jnp.where(keep, 0.0, NEG)
        return carry + jnp.sum(tie_t, axis=1, keepdims=True)

    lax.fori_loop(0, npair, lambda jj, c: mask_tile(2 * jj + 1, mask_tile(2 * jj, c)), jnp.zeros((nq, 1), F32))

    _norm_pairs(q_ref, gq_ref, qp_ref, n_pairs)
    _flash_pairs(qp_ref, kbd_ref, wext_ref, bias_ref, nkt, n_pairs, n_pairs,
                 ((s0_ref, s1_ref), (p0_ref, p1_ref), (a0_ref, a1_ref)), m_ref, acc_ref)
    for j in range(n_pairs):
        acc = acc_ref[j * nq:(j + 1) * nq, :]
        o_ref[:, j * LANES:(j + 1) * LANES] = (acc[:, :LANES] / acc[:, LANES:]).astype(o_ref.dtype)


def _dsa(u16, u32, iwt, gq, gk, bsz, seq):
    n_sel = min(DSA_TOPK, seq // 4)
    nqb = seq // DSA_Q_BLOCK
    n_pairs = DSA_HEADS // 2
    kern = functools.partial(_dsa_kernel, seq=seq, n_sel=n_sel)
    return pl.pallas_call(
        kern,
        grid=(bsz, nqb),
        in_specs=[pl.BlockSpec((DSA_Q_BLOCK, 1024), lambda b, i: (b * nqb + i, U16_DQ_BLK)),
                  pl.BlockSpec((DSA_Q_BLOCK, 512), lambda b, i: (b * nqb + i, U32_DS_BLK512)),
                  pl.BlockSpec((1, IDX_HEADS, DSA_Q_BLOCK), lambda b, i: (b, 0, i)),
                  pl.BlockSpec((seq, 512), lambda b, i: (b, U32_DS_BLK512)),
                  pl.BlockSpec((1, HEAD_DIM), lambda b, i: (0, 0)),
                  pl.BlockSpec((1, DSA_KV_DIM), lambda b, i: (0, 0))],
        out_specs=pl.BlockSpec((DSA_Q_BLOCK, DSA_HEADS * HEAD_DIM), lambda b, i: (b * nqb + i, 0)),
        out_shape=jax.ShapeDtypeStruct((bsz * seq, DSA_HEADS * HEAD_DIM), BF16),
        scratch_shapes=[pltpu.VMEM((1, 2 * seq, LANES), BF16),
                        pltpu.VMEM((1, 2 * seq, 2 * LANES), BF16),
                        pltpu.VMEM((seq, IDX_DIM), BF16),
                        pltpu.VMEM((seq // KEY_TILE, KEY_TILE, DSA_Q_BLOCK), F32),
                        pltpu.VMEM((1, seq // KEY_TILE, DSA_Q_BLOCK, KEY_TILE), F32),
                        pltpu.VMEM((IDX_HEADS * DSA_Q_BLOCK, IDX_DIM), BF16),
                        pltpu.VMEM((n_pairs * DSA_Q_BLOCK, LANES), BF16)]
                       + _flash_scratch(n_pairs, DSA_Q_BLOCK),
        compiler_params=_cparams(("parallel", "arbitrary")),
        name="dsa",
    )(u16, u32, iwt, u32, gq, gk)


def _nsa_compress_kernel(xk_ref, xv_ref, pos_ref, w_ref, g_ref, kc_ref, vc_ref):
    nch = xk_ref.shape[0] // CMP_STRIDE
    for jj, (x_ref, o_ref) in enumerate(((xk_ref, kc_ref), (xv_ref, vc_ref))):
        acc_a = jnp.zeros((nch, LANES), F32)
        acc_b = jnp.zeros((nch, LANES), F32)
        for l in range(CMP_STRIDE):
            x = x_ref[pl.ds(l, nch, stride=CMP_STRIDE), :]
            acc_a = acc_a + _dot((x + pos_ref[jj, l:l + 1, :]).astype(BF16), w_ref[jj, l])
            hi = CMP_STRIDE + l
            acc_b = acc_b + _dot((x + pos_ref[jj, hi:hi + 1, :]).astype(BF16), w_ref[jj, hi])
        out = acc_a + pltpu.roll(acc_b, nch - 1, axis=0)
        for g in range(NSA_KV_HEADS):
            og = out[:, g * HEAD_DIM:(g + 1) * HEAD_DIM]
            o_ref[0, g] = (_rms(og, g_ref[...]) if jj == 0 else og).astype(BF16)


def _nsa_compress(u32, pos, w, g, bsz, seq):
    nch = seq // CMP_STRIDE
    return pl.pallas_call(
        _nsa_compress_kernel,
        grid=(bsz,),
        in_specs=[pl.BlockSpec((seq, LANES), lambda b: (b, U32_KC_BLK128)),
                  pl.BlockSpec((seq, LANES), lambda b: (b, U32_KC_BLK128 + 1)),
                  pl.BlockSpec((2, CMP_BLOCK, LANES), lambda b: (0, 0, 0)),
                  pl.BlockSpec((2, CMP_BLOCK, LANES, LANES), lambda b: (0, 0, 0, 0)),
                  pl.BlockSpec((1, HEAD_DIM), lambda b: (0, 0))],
        out_specs=[pl.BlockSpec((1, NSA_KV_HEADS, nch, HEAD_DIM), lambda b: (b, 0, 0, 0)),
                   pl.BlockSpec((1, NSA_KV_HEADS, nch, HEAD_DIM), lambda b: (b, 0, 0, 0))],
        out_shape=[jax.ShapeDtypeStruct((bsz, NSA_KV_HEADS, nch, HEAD_DIM), BF16),
                   jax.ShapeDtypeStruct((bsz, NSA_KV_HEADS, nch, HEAD_DIM), BF16)],
        compiler_params=_cparams(("parallel",)),
        name="nsa_compress",
    )(u32, u32, pos, w, g)


def _nsa_kernel(q_ref, ng_ref, kv_ref, kc_ref, vc_ref, gq_ref, gk_ref, cover_ref, expand_ref, gexp_ref, o_ref,
                ksbd_ref, wsext_ref, kwbd_ref, wwext_ref, kcbd_ref, vcext_ref, bias_ref, wbias_ref, psum_ref,
                ocmp_ref, accw_ref, qp_ref, sw_ref, pw_ref, sc_ref, pc_ref,
                s0_ref, s1_ref, p0_ref, p1_ref, a0_ref, a1_ref, m_ref, acc_ref, *, seq):
    i = pl.program_id(1)
    nq = q_ref.shape[0]
    n_pairs = NSA_HEADS // 2
    gpairs = NSA_GROUP // 2
    grows = gpairs * nq
    n_cmp_pad = kc_ref.shape[2]
    n_blk = seq // SEL_BLOCK
    n_sel = min(SEL_TOPN, n_blk)
    win_keys = min(WINDOW + nq, seq)
    wt = LANES
    win_tiles = win_keys // wt

    @pl.when(i == 0)
    def _():
        def rows(r, c):
            off = pl.multiple_of(r * KEY_TILE, KEY_TILE)
            kv = kv_ref[pl.ds(off, KEY_TILE), :]
            for g in range(NSA_KV_HEADS):
                c0 = g * HEAD_DIM
                _store_pair_kv(ksbd_ref, wsext_ref, g, pl.multiple_of(2 * off, 2 * KEY_TILE), KEY_TILE,
                               _rms(kv[:, c0:c0 + 64], gk_ref[1:2, :]), kv[:, 128 + c0:128 + c0 + 64])
                kw = _rms(kv[:, 256 + c0:256 + c0 + 64], gk_ref[2:3, :])
                vw = kv[:, 384 + c0:384 + c0 + 64]
                for hf in range(KEY_TILE // wt):
                    _store_pair_kv(kwbd_ref, wwext_ref, g, pl.multiple_of(2 * off + hf * 2 * wt, 2 * wt), wt,
                                   kw[hf * wt:(hf + 1) * wt], vw[hf * wt:(hf + 1) * wt])
            return c

        lax.fori_loop(0, seq // KEY_TILE, rows, 0)
        zc = jnp.zeros((n_cmp_pad, HEAD_DIM), BF16)
        for g in range(NSA_KV_HEADS):
            kcbd_ref[g, 0:n_cmp_pad, :] = jnp.concatenate([kc_ref[0, g], zc], axis=-1)
            kcbd_ref[g, n_cmp_pad:2 * n_cmp_pad, :] = jnp.concatenate([zc, kc_ref[0, g]], axis=-1)
            vcext_ref[g, 0:n_cmp_pad, :] = jnp.concatenate([vc_ref[0, g], zc], axis=-1)
            vcext_ref[g, n_cmp_pad:2 * n_cmp_pad, :] = jnp.concatenate([zc, vc_ref[0, g]], axis=-1)

    start = i * nq
    nkt = (start + nq + KEY_TILE - 1) // KEY_TILE
    t_i = start + lax.broadcasted_iota(jnp.int32, (nq, 1), 0)
    t_col = t_i.astype(F32)
    lane_pos = lax.broadcasted_iota(jnp.int32, (1, KEY_TILE), 1).astype(F32)

    _norm_pairs(q_ref, gq_ref, qp_ref, n_pairs)

    cmp_end = (lax.broadcasted_iota(jnp.int32, (1, n_cmp_pad), 1) * CMP_STRIDE + (CMP_BLOCK - 1)).astype(F32)
    vis = cmp_end <= t_col
    wbase = pl.multiple_of(jnp.maximum(start + nq - win_keys, 0), wt)
    wb2 = pl.multiple_of(2 * wbase, 2 * wt)
    col = lax.broadcasted_iota(jnp.int32, (1, 2 * win_keys), 1)
    wpos = (wbase + jnp.right_shift(col, int(math.log2(2 * wt))) * wt + jnp.bitwise_and(col, wt - 1)).astype(F32)
    wbias_ref[...] = jnp.where((wpos <= t_col) & (wpos > t_col - float(WINDOW)), 0.0, NEG)
    for g in range(NSA_KV_HEADS):
        gs = slice(g * grows, (g + 1) * grows)
        sc_ref[gs, :] = _dot_nt(qp_ref[gs, :], kcbd_ref[g])
        sw_ref[gs, :] = _dot_nt(qp_ref[gs, :], kwbd_ref[g, pl.ds(wb2, 2 * win_keys), :])

    for g in range(NSA_KV_HEADS):
        gs = slice(g * grows, (g + 1) * grows)
        tot = jnp.zeros((nq, n_cmp_pad), F32)
        for j in range(g * gpairs, (g + 1) * gpairs):
            rs = slice(j * nq, (j + 1) * nq)
            for par in range(2):
                cs = slice(par * n_cmp_pad, (par + 1) * n_cmp_pad)
                s = jnp.where(vis, sc_ref[rs, cs], NEG)
                e = jnp.exp2(s - jnp.max(s, axis=-1, keepdims=True))
                p = jnp.where(vis, e / jnp.sum(e, axis=-1, keepdims=True), 0.0)
                pc_ref[rs, cs] = p.astype(BF16)
                tot = tot + p
        psum_ref[g] = tot
        ocmp_ref[gs, :] = _dot(pc_ref[gs, :], vcext_ref[g])

    blk_j = lax.broadcasted_iota(jnp.int32, (1, n_blk), 1)
    cur1 = jnp.right_shift(t_i, int(math.log2(SEL_BLOCK)))
    cur = jnp.concatenate([cur1] * NSA_KV_HEADS, axis=0)
    imp = jnp.concatenate([_dot_hi(psum_ref[g], cover_ref[...]) for g in range(NSA_KV_HEADS)], axis=0)
    forced = (blk_j == cur) | (blk_j == 0)
    imp = jnp.where(forced, jnp.inf, jnp.where(blk_j > cur, -jnp.inf, imp))
    selb = jnp.zeros(imp.shape, jnp.bool_)
    for _ in range(n_sel):
        mx = jnp.max(imp, axis=-1, keepdims=True)
        first = jnp.min(jnp.where(imp == mx, blk_j, n_blk), axis=-1, keepdims=True)
        pick = blk_j == first
        selb = selb | pick
        imp = jnp.where(pick, -jnp.inf, imp)
    selb_bf = jnp.where(selb, 1.0, 0.0).astype(BF16)

    for g in range(NSA_KV_HEADS):
        gs = slice(g * grows, (g + 1) * grows)
        for j in range(g * gpairs, (g + 1) * gpairs):
            rs = slice(j * nq, (j + 1) * nq)
            for par in range(2):
                cols = [slice(t * 2 * wt + par * wt, t * 2 * wt + (par + 1) * wt) for t in range(win_tiles)]
                s = [sw_ref[rs, cs] + wbias_ref[:, cs] for cs in cols]
                mx = s[0]
                for st in s[1:]:
                    mx = jnp.maximum(mx, st)
                m = jnp.max(mx, axis=-1, keepdims=True)
                for cs, st in zip(cols, s):
                    pw_ref[rs, cs] = jnp.exp2(st - m).astype(BF16)
        accw_ref[gs, :] = _dot(pw_ref[gs, :], wwext_ref[g, pl.ds(wb2, 2 * win_keys), :])

    def mask_tile(kt, c):
        off = pl.multiple_of(kt * KEY_TILE, KEY_TILE)
        hit = _dot(selb_bf, expand_ref[:, pl.ds(off, KEY_TILE)])
        ok = (lane_pos + (kt * KEY_TILE).astype(F32)) <= t_col
        for g in range(NSA_KV_HEADS):
            bias_ref[g, kt] = jnp.where((hit[g * nq:(g + 1) * nq] > 0.5) & ok, 0.0, NEG)
        return c

    lax.fori_loop(0, (nkt + 1) // 2, lambda jj, c: mask_tile(2 * jj + 1, mask_tile(2 * jj, c)), 0)
    _flash_pairs(qp_ref, ksbd_ref, wsext_ref, bias_ref, nkt, n_pairs, gpairs,
                 ((s0_ref, s1_ref), (p0_ref, p1_ref), (a0_ref, a1_ref)), m_ref, acc_ref)

    gates = jax.nn.sigmoid(ng_ref[...])
    g_hi = gates.astype(BF16)
    g_lo = (gates - g_hi.astype(F32)).astype(BF16)
    gb = [_dot(g_hi, gexp_ref[b]) + _dot(g_lo, gexp_ref[b]) for b in range(3)]
    for j in range(n_pairs):
        rs = slice(j * nq, (j + 1) * nq)
        ls = slice(j * LANES, (j + 1) * LANES)
        acc = acc_ref[rs, :]
        accw = accw_ref[rs, :]
        out = (gb[0][:, ls] * ocmp_ref[rs, :] + gb[1][:, ls] * (acc[:, :LANES] / acc[:, LANES:])
               + gb[2][:, ls] * (accw[:, :LANES] / accw[:, LANES:]))
        o_ref[:, ls] = out.astype(o_ref.dtype)


def _nsa(u16, u32, kc, vc, gq, gk, cover, expand, gexp, bsz, seq):
    nqb = seq // NSA_Q_BLOCK
    nch = kc.shape[2]
    n_blk = seq // SEL_BLOCK
    n_pairs = NSA_HEADS // 2
    prow = n_pairs * NSA_Q_BLOCK
    win_keys = min(WINDOW + NSA_Q_BLOCK, seq)
    kern = functools.partial(_nsa_kernel, seq=seq)
    return pl.pallas_call(
        kern,
        grid=(bsz, nqb),
        in_specs=[pl.BlockSpec((NSA_Q_BLOCK, 1024), lambda b, i: (b * nqb + i, U16_NQ_BLK)),
                  pl.BlockSpec((NSA_Q_BLOCK, 128), lambda b, i: (b * nqb + i, U32_NG_BLK128)),
                  pl.BlockSpec((seq, 512), lambda b, i: (b, U32_NKV_BLK512)),
                  pl.BlockSpec((1, NSA_KV_HEADS, nch, HEAD_DIM), lambda b, i: (b, 0, 0, 0)),
                  pl.BlockSpec((1, NSA_KV_HEADS, nch, HEAD_DIM), lambda b, i: (b, 0, 0, 0)),
                  pl.BlockSpec((1, HEAD_DIM), lambda b, i: (0, 0)),
                  pl.BlockSpec((3, HEAD_DIM), lambda b, i: (0, 0)),
                  pl.BlockSpec((nch, n_blk), lambda b, i: (0, 0)),
                  pl.BlockSpec((n_blk, seq), lambda b, i: (0, 0)),
                  pl.BlockSpec((3, LANES, NSA_HEADS * HEAD_DIM), lambda b, i: (0, 0, 0))],
        out_specs=pl.BlockSpec((NSA_Q_BLOCK, NSA_HEADS * HEAD_DIM), lambda b, i: (b * nqb + i, 0)),
        out_shape=jax.ShapeDtypeStruct((bsz * seq, NSA_HEADS * HEAD_DIM), BF16),
        scratch_shapes=[pltpu.VMEM((NSA_KV_HEADS, 2 * seq, LANES), BF16),
                        pltpu.VMEM((NSA_KV_HEADS, 2 * seq, 2 * LANES), BF16),
                        pltpu.VMEM((NSA_KV_HEADS, 2 * seq, LANES), BF16),
                        pltpu.VMEM((NSA_KV_HEADS, 2 * seq, 2 * LANES), BF16),
                        pltpu.VMEM((NSA_KV_HEADS, 2 * nch, LANES), BF16),
                        pltpu.VMEM((NSA_KV_HEADS, 2 * nch, LANES), BF16),
                        pltpu.VMEM((NSA_KV_HEADS, seq // KEY_TILE, NSA_Q_BLOCK, KEY_TILE), F32),
                        pltpu.VMEM((NSA_Q_BLOCK, 2 * win_keys), F32),
                        pltpu.VMEM((NSA_KV_HEADS, NSA_Q_BLOCK, nch), F32),
                        pltpu.VMEM((prow, LANES), F32),
                        pltpu.VMEM((prow, 2 * LANES), F32),
                        pltpu.VMEM((prow, LANES), BF16),
                        pltpu.VMEM((prow, 2 * win_keys), F32),
                        pltpu.VMEM((prow, 2 * win_keys), BF16),
                        pltpu.VMEM((prow, 2 * nch), F32),
                        pltpu.VMEM((prow, 2 * nch), BF16)]
                       + _flash_scratch(n_pairs, NSA_Q_BLOCK),
        compiler_params=_cparams(("parallel", "arbitrary")),
        name="nsa",
    )(u16, u32, u32, kc, vc, gq, gk, cover, expand, gexp)


def _softplus(x):
    return jnp.maximum(x, 0.0) + jnp.log1p(jnp.exp(-jnp.abs(x)))


def _ssd_kernel(xbc_ref, z_ref, dt_ref, dtt_ref, cw_ref, cb_ref, dtb_ref, dtbt_ref, al_ref, alt_ref,
                d_ref, ng_ref, o_ref, xcat_ref, xa_ref, y_ref, st_ref):
    c = pl.program_id(1)
    q = SSD_CHUNK
    gn = SSD_GROUPS * SSD_STATE
    tail = 16
    lo_half = lax.broadcasted_iota(jnp.int32, (1, LANES), 1) < SSD_HEAD_DIM

    @pl.when(c == 0)
    def _():
        st_ref[...] = jnp.zeros_like(st_ref)
        xcat_ref[0:q, :] = jnp.zeros((q, SSD_CONV_DIM), BF16)

    @pl.when(c > 0)
    def _():
        xcat_ref[q - tail:q, :] = xcat_ref[2 * q - tail:2 * q, :]

    xcat_ref[q:2 * q, :] = xbc_ref[...]
    ri3 = lax.broadcasted_iota(jnp.int32, ((CONV_WIDTH - 1) * q, 2 * q), 0)
    ci3 = lax.broadcasted_iota(jnp.int32, ((CONV_WIDTH - 1) * q, 2 * q), 1)
    qbits = int(math.log2(q))
    src = q + jnp.bitwise_and(ri3, q - 1) - (jnp.right_shift(ri3, qbits) + 1)
    shifts = jnp.where(ci3 == src, 1.0, 0.0).astype(BF16)
    for cc in range(SSD_CONV_DIM // CONV_CHUNK):
        cs = slice(cc * CONV_CHUNK, (cc + 1) * CONV_CHUNK)
        sh = _dot(shifts, xcat_ref[:, cs])
        acc = cb_ref[:, cs] + cw_ref[CONV_WIDTH - 1:CONV_WIDTH, cs] * xcat_ref[q:2 * q, cs].astype(F32)
        for s in range(1, CONV_WIDTH):
            acc = acc + cw_ref[CONV_WIDTH - 1 - s:CONV_WIDTH - s, cs] * sh[(s - 1) * q:s * q]
        xa_ref[:, cs] = acc * jax.nn.sigmoid(acc)

    dt = _softplus(dt_ref[:, :SSD_HEADS] + dtb_ref[...])
    dtt = _softplus(dtt_ref[0] + dtbt_ref[...])
    a = -jnp.exp(al_ref[...]) * LOG2E
    at = -jnp.exp(alt_ref[...]) * LOG2E
    ri = lax.broadcasted_iota(jnp.int32, (q, q), 0)
    ci = lax.broadcasted_iota(jnp.int32, (q, q), 1)
    tril = ri >= ci
    acum = _dot_hi(jnp.where(tril, 1.0, 0.0), dt * a)
    acumt = _dot_hi(dtt * at, jnp.where(ri <= ci, 1.0, 0.0))
    wrow = jnp.exp2(acumt[:, q - 1:q] - acumt) * dtt

    for g in range(SSD_GROUPS):
        bmat = xa_ref[:, SSD_INNER + g * SSD_STATE:SSD_INNER + (g + 1) * SSD_STATE]
        cmat = xa_ref[:, SSD_INNER + gn + g * SSD_STATE:SSD_INNER + gn + (g + 1) * SSD_STATE].astype(BF16)
        cb = _dot_nt(cmat, bmat.astype(BF16))
        bt = bmat.T
        for pp in range(SSD_HPG // 2):
            j = g * (SSD_HPG // 2) + pp
            ls = slice(j * LANES, (j + 1) * LANES)
            xs = xa_ref[:, ls]
            xbd = jnp.concatenate([jnp.where(lo_half, xs, 0.0), jnp.where(lo_half, 0.0, xs)], axis=0).astype(BF16)
            mm, dec, btw = [], [], []
            for par in range(2):
                h = 2 * j + par
                abc = jnp.broadcast_to(acum[:, h:h + 1], (q, q))
                lmat = jnp.exp2(jnp.where(tril, abc - acumt[h:h + 1, :], -jnp.inf))
                mm.append((cb * lmat * dtt[h:h + 1, :]).astype(BF16))
                dec.append(jnp.exp2(abc))
                btw.append((bt * wrow[h:h + 1, :]).astype(BF16))
            y = _dot(jnp.concatenate(mm, axis=1), xbd)
            y = y + _dot(cmat, st_ref[j].astype(BF16)) * jnp.where(lo_half, dec[0], dec[1])
            y_ref[:, ls] = y + xs * d_ref[:, ls]
            cdec = jnp.where(lo_half, dec[0][q - 1:q, :], dec[1][q - 1:q, :])
            st_ref[j] = st_ref[j] * cdec + _dot(jnp.concatenate(btw, axis=1), xbd)

    z = z_ref[...].astype(F32)
    yz = y_ref[...] * (z * jax.nn.sigmoid(z))
    o_ref[...] = _rms(yz, ng_ref[...]).astype(o_ref.dtype)


def _ssd(u16, u32, dtt, cw, cb, dtb, al, d, ng, bsz, seq):
    nc = seq // SSD_CHUNK
    q = SSD_CHUNK
    full = lambda shape: pl.BlockSpec(shape, lambda b, c: (0,) * len(shape))
    d_full = jnp.repeat(d, SSD_HEAD_DIM).reshape(1, SSD_INNER)
    return pl.pallas_call(
        _ssd_kernel,
        grid=(bsz, nc),
        in_specs=[pl.BlockSpec((q, SSD_CONV_DIM), lambda b, c: (b * nc + c, U16_XBC_BLK)),
                  pl.BlockSpec((q, SSD_INNER), lambda b, c: (b * nc + c, U16_Z_BLK)),
                  pl.BlockSpec((q, 128), lambda b, c: (b * nc + c, U32_DT_BLK128)),
                  pl.BlockSpec((1, SSD_HEADS, q), lambda b, c: (b, 0, c)),
                  full((CONV_WIDTH, SSD_CONV_DIM)), full((1, SSD_CONV_DIM)),
                  full((1, SSD_HEADS)), full((SSD_HEADS, 1)),
                  full((1, SSD_HEADS)), full((SSD_HEADS, 1)),
                  full((1, SSD_INNER)), full((1, SSD_INNER))],
        out_specs=pl.BlockSpec((q, SSD_INNER), lambda b, c: (b * nc + c, 0)),
        out_shape=jax.ShapeDtypeStruct((bsz * seq, SSD_INNER), BF16),
        scratch_shapes=[pltpu.VMEM((2 * q, SSD_CONV_DIM), BF16),
                        pltpu.VMEM((q, SSD_CONV_DIM), F32),
                        pltpu.VMEM((q, SSD_INNER), F32),
                        pltpu.VMEM((SSD_HEADS // 2, SSD_STATE, LANES), F32)],
        compiler_params=_cparams(("parallel", "arbitrary")),
        name="ssd",
    )(u16, u16, u32, dtt, cw, cb, dtb.reshape(1, -1), dtb.reshape(-1, 1), al.reshape(1, -1), al.reshape(-1, 1),
      d_full, ng.reshape(1, -1))


def _merge_kernel(x_ref, ya_ref, yb_ref, yc_ref, mg_ref, wa_ref, wb_ref, wc_ref, wo_ref, o_ref):
    d = D_MODEL
    mg = mg_ref[...].astype(F32)
    mix = jax.nn.sigmoid(mg[:, 0:d]) * _dot(ya_ref[...], wa_ref[...])
    mix = mix + jax.nn.sigmoid(mg[:, d:2 * d]) * _dot(yb_ref[...], wb_ref[...])
    mix = mix + jax.nn.sigmoid(mg[:, 2 * d:3 * d]) * _dot(yc_ref[...], wc_ref[...])
    o_ref[...] = x_ref[...] + _dot(mix.astype(BF16), wo_ref[...])


def _merge(x, ya, yb, yc, u16, wa, wb, wc, wo, tm):
    m, d = x.shape
    full = lambda a: pl.BlockSpec(a.shape, lambda i: (0, 0))
    return pl.pallas_call(
        _merge_kernel,
        grid=(m // tm,),
        in_specs=[pl.BlockSpec((tm, d), lambda i: (i, 0)),
                  pl.BlockSpec((tm, ya.shape[1]), lambda i: (i, 0)),
                  pl.BlockSpec((tm, yb.shape[1]), lambda i: (i, 0)),
                  pl.BlockSpec((tm, yc.shape[1]), lambda i: (i, 0)),
                  pl.BlockSpec((tm, 3 * d), lambda i: (i, U16_MG_BLK)),
                  full(wa), full(wb), full(wc), full(wo)],
        out_specs=pl.BlockSpec((tm, d), lambda i: (i, 0)),
        out_shape=jax.ShapeDtypeStruct((m, d), F32),
        compiler_params=_cparams(("parallel",)),
        name="merge",
    )(x, ya, yb, yc, u16, wa, wb, wc, wo)


def _ffn_kernel(x_ref, g_ref, w1_ref, w2_ref, o_ref, h_ref, acc_ref):
    j = pl.program_id(1)

    @pl.when(j == 0)
    def _():
        h_ref[...] = _rms(x_ref[...], g_ref[...]).astype(BF16)
        acc_ref[...] = x_ref[...]

    a = jnp.maximum(_dot(h_ref[...], w1_ref[0].astype(BF16)), 0.0)
    acc_ref[...] += _dot((a * a).astype(BF16), w2_ref[0].astype(BF16))

    @pl.when(j == pl.num_programs(1) - 1)
    def _():
        o_ref[...] = acc_ref[...]


def _ffn(x, g, w1, w2, layer, tm, tf):
    m, d = x.shape
    f = w1.shape[2]
    return pl.pallas_call(
        _ffn_kernel,
        grid=(m // tm, f // tf),
        in_specs=[pl.BlockSpec((tm, d), lambda i, j: (i, 0)),
                  pl.BlockSpec((1, d), lambda i, j: (0, 0)),
                  pl.BlockSpec((1, d, tf), lambda i, j: (layer, 0, j)),
                  pl.BlockSpec((1, tf, d), lambda i, j: (layer, j, 0))],
        out_specs=pl.BlockSpec((tm, d), lambda i, j: (i, 0)),
        out_shape=jax.ShapeDtypeStruct((m, d), F32),
        scratch_shapes=[pltpu.VMEM((tm, d), BF16), pltpu.VMEM((tm, d), F32)],
        compiler_params=_cparams(("parallel", "arbitrary")),
        name="ffn",
    )(x, g, w1, w2)


def _prep_w_in(w):
    offs = [0] + [int(o) for o in np.cumsum(IN_SPLITS)]
    seg = lambda k: w[:, offs[k]:offs[k + 1]]
    dq, dk, dv, iq, ik, iw, nq, nkv, ng, sz, sxbc, sdt, mg = [seg(k) for k in range(13)]
    zeros = lambda n: jnp.zeros((w.shape[0], n), w.dtype)
    w16 = jnp.concatenate([sxbc, mg, sz, dq, nq], axis=1).astype(BF16)
    w32 = jnp.concatenate([nkv[:, 256:768], nkv[:, 0:256], zeros(256),
                           dk, dv, iq, ik, iw, zeros(512 - 424),
                           ng, zeros(128 - ng.shape[1]),
                           sdt, zeros(128 - sdt.shape[1])], axis=1).astype(BF16)
    return w16, w32


def _pick_tile(n, pref):
    t = min(n, pref)
    while n % t:
        t //= 2
    return t


def kernel(x, norm1_g, w_in, dsa_q_norm, dsa_k_norm, nsa_q_norm, nsa_k_norm, nsa_cmp_pos, nsa_cmp_w,
           ssd_conv_w, ssd_conv_b, ssd_dt_bias, ssd_a_log, ssd_d, ssd_norm_g,
           w_br_dsa, w_br_nsa, w_br_ssd, w_out, norm2_g, w_ff1, w_ff2):
    bsz, seq, d = x.shape
    m = bsz * seq
    depth = w_in.shape[0]
    nch = seq // CMP_STRIDE
    n_blk = seq // SEL_BLOCK
    half = CMP_BLOCK // 2

    cmp_start = np.arange(nch) * CMP_STRIDE
    blk_start = np.arange(n_blk) * SEL_BLOCK
    cover = ((cmp_start[:, None] < blk_start[None, :] + SEL_BLOCK)
             & (cmp_start[:, None] + CMP_BLOCK > blk_start[None, :])
             & (np.arange(nch)[:, None] < (seq - CMP_BLOCK) // CMP_STRIDE + 1)).astype(np.float32)
    expand = (np.arange(seq)[None, :] // SEL_BLOCK == np.arange(n_blk)[:, None]).astype(np.float32)
    gexp = np.zeros((3, LANES, NSA_HEADS * HEAD_DIM), np.float32)
    for hh in range(NSA_HEADS):
        for br in range(3):
            gexp[br, 3 * hh + br, hh * HEAD_DIM:(hh + 1) * HEAD_DIM] = 1.0
    gexp = jnp.asarray(gexp, dtype=BF16)
    cover = jnp.asarray(cover)
    expand = jnp.asarray(expand, dtype=BF16)

    tm = _pick_tile(m, 1024)
    xf = x.reshape(m, d)
    for l in range(depth):
        w16, w32 = _prep_w_in(w_in[l])
        g1 = norm1_g[l].reshape(1, d)
        u16 = _norm_matmul(xf, g1, w16, BF16, tm, 2048)
        u32 = _norm_matmul(xf, g1, w32, F32, tm, U32_WIDTH)

        iw0 = U32_DS_BLK512 * 512 + DS_IW
        iwt = u32[:, iw0:iw0 + IDX_HEADS].reshape(bsz, seq, IDX_HEADS).transpose(0, 2, 1)
        ya = _dsa(u16, u32, iwt, dsa_q_norm[l].reshape(1, -1), dsa_k_norm[l].reshape(1, -1), bsz, seq)

        wl = nsa_cmp_w[l]
        zblk = jnp.zeros_like(wl)
        cw = jnp.concatenate([jnp.concatenate([wl, zblk], axis=3),
                              jnp.concatenate([zblk, wl], axis=3)], axis=2).astype(BF16)
        cpos = jnp.concatenate([nsa_cmp_pos[l]] * NSA_KV_HEADS, axis=-1)
        kc, vc = _nsa_compress(u32, cpos, cw, nsa_k_norm[l][0:1], bsz, seq)
        yb = _nsa(u16, u32, kc, vc, nsa_q_norm[l].reshape(1, -1), nsa_k_norm[l], cover, expand, gexp, bsz, seq)

        dt0 = U32_DT_BLK128 * 128
        dtt = u32[:, dt0:dt0 + SSD_HEADS].reshape(bsz, seq, SSD_HEADS).transpose(0, 2, 1)
        yc = _ssd(u16, u32, dtt, ssd_conv_w[l], ssd_conv_b[l].reshape(1, -1), ssd_dt_bias[l], ssd_a_log[l],
                  ssd_d[l], ssd_norm_g[l], bsz, seq)

        xf = _merge(xf, ya, yb, yc, u16,
                    w_br_dsa[l].astype(BF16), w_br_nsa[l].astype(BF16), w_br_ssd[l].astype(BF16),
                    w_out[l].astype(BF16), _pick_tile(m, 512))
        xf = _ffn(xf, norm2_g[l].reshape(1, d), w_ff1, w_ff2, l,
                  _pick_tile(m, 1024), 1024)
    return xf.reshape(bsz, seq, d)
```

```python
import functools
import math

import numpy as np
import jax
import jax.numpy as jnp
from jax import lax
from jax.experimental import pallas as pl
from jax.experimental.pallas import tpu as pltpu

F32 = jnp.float32
BF16 = jnp.bfloat16

D_MODEL = 1024
HEAD_DIM = 64
DSA_HEADS = 16
DSA_KV_DIM = 64
IDX_HEADS = 8
IDX_DIM = 32
DSA_TOPK = 256
NSA_HEADS = 16
NSA_KV_HEADS = 2
NSA_GROUP = NSA_HEADS // NSA_KV_HEADS
CMP_BLOCK = 32
CMP_STRIDE = 16
SEL_BLOCK = 64
SEL_TOPN = 4
WINDOW = 512
SSD_INNER = 2 * D_MODEL
SSD_HEAD_DIM = 64
SSD_HEADS = SSD_INNER // SSD_HEAD_DIM
SSD_GROUPS = 4
SSD_HPG = SSD_HEADS // SSD_GROUPS
SSD_STATE = 128
SSD_CONV_DIM = SSD_INNER + 2 * SSD_GROUPS * SSD_STATE
CONV_WIDTH = 4
SSD_CHUNK = 128
Q_BLOCK = 128
DSA_Q_BLOCK = 256
NSA_Q_BLOCK = 128
EPS = 1e-6
NEG = -1e30
IN_SPLITS = (DSA_HEADS * HEAD_DIM, DSA_KV_DIM, DSA_KV_DIM, IDX_HEADS * IDX_DIM, IDX_DIM, IDX_HEADS,
             NSA_HEADS * HEAD_DIM, 6 * NSA_KV_HEADS * HEAD_DIM, 3 * NSA_HEADS,
             SSD_INNER, SSD_CONV_DIM, SSD_HEADS, 3 * D_MODEL)

V7X_VMEM_LIMIT_BYTES = 56 * 1024 * 1024
LANES = 128

U16_WIDTH = 10240
U16_XBC_BLK = 0
U16_MG_BLK = 1
U16_Z_BLK = 3
U16_DQ_BLK = 8
U16_NQ_BLK = 9
U32_WIDTH = 1792
U32_NKV_BLK512 = 0
U32_KC_BLK128 = 4
U32_DS_BLK512 = 2
U32_NG_BLK128 = 12
U32_DT_BLK128 = 13
DS_DK, DS_DV, DS_IQ, DS_IK, DS_IW = 0, 64, 128, 384, 416

KEY_TILE = 256
BISECT_ITERS = 24
PAIR_STEP = 2
FLASH_UNROLL = 4
LOG2E = 1.4426950408889634
BISECT_UNROLL = 4
CONV_CHUNK = 256


def _cparams(sem):
    return pltpu.CompilerParams(dimension_semantics=sem, vmem_limit_bytes=V7X_VMEM_LIMIT_BYTES)


def _rms(x, g):
    return x * lax.rsqrt(jnp.mean(x * x, axis=-1, keepdims=True) + EPS) * g


def _dot_nt(a, b):
    return lax.dot_general(a, b, (((1,), (1,)), ((), ())), preferred_element_type=F32)


def _dot(a, b):
    return jnp.dot(a, b, preferred_element_type=F32)


def _dot_hi(a, b):
    return jnp.dot(a, b, preferred_element_type=F32, precision=lax.Precision.HIGHEST)


def _norm_matmul_kernel(x_ref, g_ref, w_ref, o_ref, h_ref):
    @pl.when(pl.program_id(1) == 0)
    def _():
        h_ref[...] = _rms(x_ref[...], g_ref[...]).astype(BF16)

    o_ref[...] = _dot(h_ref[...], w_ref[...]).astype(o_ref.dtype)


def _norm_matmul(x, g, w, out_dtype, tm, tn):
    m, k = x.shape
    n = w.shape[1]
    return pl.pallas_call(
        _norm_matmul_kernel,
        grid=(m // tm, n // tn),
        in_specs=[pl.BlockSpec((tm, k), lambda i, j: (i, 0)),
                  pl.BlockSpec((1, k), lambda i, j: (0, 0)),
                  pl.BlockSpec((k, tn), lambda i, j: (0, j))],
        out_specs=pl.BlockSpec((tm, tn), lambda i, j: (i, j)),
        out_shape=jax.ShapeDtypeStruct((m, n), out_dtype),
        scratch_shapes=[pltpu.VMEM((tm, k), BF16)],
        compiler_params=_cparams(("parallel", "arbitrary")),
        name="norm_matmul",
    )(x, g, w)


def _flash_pairs(qp_ref, kbd_ref, wext_ref, bias_ref, nkt, n_pairs, pairs_per_kv, bufs, m_ref, acc_ref):
    (s0, s1), (p0, p1), (a0, a1) = bufs
    qb = bias_ref.shape[2]
    n_pg = n_pairs // PAIR_STEP
    rows = PAIR_STEP * qb
    n = nkt * n_pg
    lo_half = lax.broadcasted_iota(jnp.int32, (1, LANES), 1) < HEAD_DIM
    m_ref[...] = jnp.full(m_ref.shape, -jnp.inf, F32)
    acc_ref[...] = jnp.zeros(acc_ref.shape, F32)
    p1[...] = jnp.zeros(p1.shape, BF16)
    a1[...] = jnp.ones(a1.shape, F32)

    def where(j):
        kt = j // n_pg
        pg = j % n_pg
        g = (pg * PAIR_STEP) // pairs_per_kv
        return kt, pg, g, pl.multiple_of(kt * 2 * KEY_TILE, 2 * KEY_TILE), pl.multiple_of(pg * rows, rows)

    def qk(j, s_ref):
        kt, _, g, koff, r = where(j)
        s = _dot_nt(qp_ref[pl.ds(r, rows), :], kbd_ref[g, pl.ds(koff, 2 * KEY_TILE), :])
        s_ref[...] = s + jnp.tile(bias_ref[g, kt], (PAIR_STEP, 2))

    def softmax(j, s_ref, p_ref, a_ref):
        _, pg, _, _, _ = where(j)
        for u in range(PAIR_STEP):
            us = slice(u * qb, (u + 1) * qb)
            alpha = []
            for par in range(2):
                cs = slice(par * KEY_TILE, (par + 1) * KEY_TILE)
                hrow = pl.multiple_of(((pg * PAIR_STEP + u) * 2 + par) * qb, qb)
                m_old = m_ref[pl.ds(hrow, qb), :]
                m_new = jnp.maximum(m_old, jnp.max(s_ref[us, cs], axis=-1, keepdims=True))
                alpha.append(jnp.exp2(m_old - m_new))
                m_ref[pl.ds(hrow, qb), :] = m_new
                p_ref[us, cs] = jnp.exp2(
                    s_ref[us, cs] - jnp.concatenate([m_new] * (KEY_TILE // LANES), axis=1)).astype(BF16)
            a_ref[us, :] = jnp.where(lo_half, alpha[0], alpha[1])

    def pv(j, p_ref, a_ref):
        _, _, g, koff, r = where(j)
        a = a_ref[...]
        acc_ref[pl.ds(r, rows), :] = (jnp.concatenate([a, a], axis=1) * acc_ref[pl.ds(r, rows), :]
                                      + _dot(p_ref[...], wext_ref[g, pl.ds(koff, 2 * KEY_TILE), :]))

    qk(0, s0)

    def body(jj, c):
        for j in (FLASH_UNROLL * jj, FLASH_UNROLL * jj + 2):
            qk(j + 1, s1)
            softmax(j, s0, p0, a0)
            pv(jnp.maximum(j - 1, 0), p1, a1)
            qk(jnp.minimum(j + 2, n - 1), s0)
            softmax(j + 1, s1, p1, a1)
            pv(j, p0, a0)
        return c

    lax.fori_loop(0, n // FLASH_UNROLL, body, 0)
    pv(n - 1, p1, a1)


def _flash_scratch(n_pairs, qb):
    step = PAIR_STEP * qb
    return ([pltpu.VMEM((step, 2 * KEY_TILE), F32)] * 2 + [pltpu.VMEM((step, 2 * KEY_TILE), BF16)] * 2
            + [pltpu.VMEM((step, LANES), F32)] * 2
            + [pltpu.VMEM((2 * n_pairs * qb, LANES), F32), pltpu.VMEM((n_pairs * qb, 2 * LANES), F32)])


def _norm_pairs(q_ref, g_ref, qp_ref, n_pairs):
    qb = q_ref.shape[0]
    mult = HEAD_DIM ** -0.5 * LOG2E
    lo_half = lax.broadcasted_iota(jnp.int32, (1, LANES), 1) < HEAD_DIM
    g2 = jnp.concatenate([g_ref[...], g_ref[...]], axis=-1) * mult
    for j in range(n_pairs):
        x = q_ref[:, j * LANES:(j + 1) * LANES].astype(F32)
        x2 = x * x
        s_lo = jnp.sum(jnp.where(lo_half, x2, 0.0), axis=-1, keepdims=True)
        s_hi = jnp.sum(jnp.where(lo_half, 0.0, x2), axis=-1, keepdims=True)
        r = jnp.where(lo_half, lax.rsqrt(s_lo * (1.0 / HEAD_DIM) + EPS), lax.rsqrt(s_hi * (1.0 / HEAD_DIM) + EPS))
        qp_ref[j * qb:(j + 1) * qb, :] = (x * r * g2).astype(BF16)


def _store_pair_kv(kbd_ref, wext_ref, g, row0, n, k, v):
    z = jnp.zeros((n, HEAD_DIM), F32)
    one = jnp.ones((n, HEAD_DIM), F32)
    kbd_ref[g, pl.ds(row0, n), :] = jnp.concatenate([k, z], axis=-1).astype(BF16)
    kbd_ref[g, pl.ds(row0 + n, n), :] = jnp.concatenate([z, k], axis=-1).astype(BF16)
    wext_ref[g, pl.ds(row0, n), :] = jnp.concatenate([v, z, one, z], axis=-1).astype(BF16)
    wext_ref[g, pl.ds(row0 + n, n), :] = jnp.concatenate([z, v, z, one], axis=-1).astype(BF16)


def _dsa_kernel(q_ref, sq_ref, iwt_ref, sk_ref, gq_ref, gk_ref, o_ref,
                kbd_ref, wext_ref, ik_ref, idx_ref, bias_ref, iqs_ref,
                qp_ref, s0_ref, s1_ref, p0_ref, p1_ref, a0_ref, a1_ref, m_ref, acc_ref, *, seq, n_sel):
    i = pl.program_id(1)
    nq = q_ref.shape[0]
    n_pairs = DSA_HEADS // 2
    kt_shape = (KEY_TILE, nq)

    @pl.when(i == 0)
    def _():
        def rows(r, c):
            off = pl.multiple_of(r * KEY_TILE, KEY_TILE)
            sk = sk_ref[pl.ds(off, KEY_TILE), :]
            _store_pair_kv(kbd_ref, wext_ref, 0, pl.multiple_of(2 * off, 2 * KEY_TILE), KEY_TILE,
                           _rms(sk[:, DS_DK:DS_DK + HEAD_DIM], gk_ref[...]), sk[:, DS_DV:DS_DV + HEAD_DIM])
            ik_ref[pl.ds(off, KEY_TILE), :] = sk[:, DS_IK:DS_IK + IDX_DIM].astype(BF16)
            return c

        lax.fori_loop(0, seq // KEY_TILE, rows, 0)

    start = i * nq
    nkt = (start + nq + KEY_TILE - 1) // KEY_TILE
    t_row = (start + lax.broadcasted_iota(jnp.int32, (1, nq), 1)).astype(F32)
    sub_pos = lax.broadcasted_iota(jnp.int32, (KEY_TILE, 1), 0).astype(F32)

    sq = sq_ref[...]
    for h in range(IDX_HEADS):
        iqs_ref[h * nq:(h + 1) * nq, :] = sq[:, DS_IQ + h * IDX_DIM:DS_IQ + (h + 1) * IDX_DIM].astype(BF16)
    iwt = iwt_ref[0]

    def idx_tile(kt, carry):
        rmin, rmax = carry
        off = pl.multiple_of(kt * KEY_TILE, KEY_TILE)
        r = _dot_nt(ik_ref[pl.ds(off, KEY_TILE), :], iqs_ref[...])
        acc = jnp.zeros(kt_shape, F32)
        for h in range(IDX_HEADS):
            acc = acc + iwt[h:h + 1, :] * jnp.maximum(r[:, h * nq:(h + 1) * nq], 0.0)
        valid = (sub_pos + (kt * KEY_TILE).astype(F32)) <= t_row
        idx_ref[kt] = jnp.where(valid, acc, -jnp.inf)
        rmin = jnp.minimum(rmin, jnp.min(jnp.where(valid, acc, jnp.inf), axis=0, keepdims=True))
        rmax = jnp.maximum(rmax, jnp.max(jnp.where(valid, acc, -jnp.inf), axis=0, keepdims=True))
        return rmin, rmax

    npair = (nkt + 1) // 2
    rmin, rmax = lax.fori_loop(0, npair, lambda jj, c: idx_tile(2 * jj + 1, idx_tile(2 * jj, c)),
                               (jnp.full((1, nq), jnp.inf, F32), jnp.full((1, nq), -jnp.inf, F32)))

    kf = float(n_sel)

    def col_sum(x):
        parts = [x[r * 8:(r + 1) * 8] for r in range(x.shape[0] // 8)]
        while len(parts) > 1:
            parts = [parts[k] + parts[k + 1] for k in range(0, len(parts), 2)]
        return jnp.sum(parts[0], axis=0, keepdims=True)

    fold = KEY_TILE // 4

    def count_gt(thr):
        def body(kt, acc):
            hit = jnp.where(idx_ref[kt] > thr, 1.0, 0.0)
            return acc + ((hit[0:fold] + hit[fold:2 * fold]) + (hit[2 * fold:3 * fold] + hit[3 * fold:]))

        return col_sum(lax.fori_loop(0, nkt, body, jnp.zeros((fold, nq), F32)))

    n_valid = t_row + 1.0
    lo0 = rmin - (jnp.abs(rmin) * 0.01 + 1.0)

    def bis_cond(c):
        it, _, _, clo, _ = c
        return jnp.logical_and(it < BISECT_ITERS, jnp.max(clo) > kf)

    def bis_body(c):
        it, lo, hi, clo, chi = c
        for _ in range(BISECT_UNROLL):
            mid = 0.5 * (lo + hi)
            cnt = count_gt(mid)
            ge = cnt >= kf
            lo, hi = jnp.where(ge, mid, lo), jnp.where(ge, hi, mid)
            clo, chi = jnp.where(ge, cnt, clo), jnp.where(ge, chi, cnt)
        return it + BISECT_UNROLL, lo, hi, clo, chi

    _, lo, hi, clo, chi = lax.while_loop(
        bis_cond, bis_body, (jnp.int32(0), lo0, rmax, n_valid, jnp.zeros((1, nq), F32)))

    eye_f = jnp.where(lax.broadcasted_iota(jnp.int32, (nq, nq), 0) == lax.broadcasted_iota(jnp.int32, (nq, nq), 1),
                      1.0, 0.0)
    eye = eye_f.astype(BF16)
    tri = jnp.where(lax.broadcasted_iota(jnp.int32, (KEY_TILE, KEY_TILE), 0)
                    <= lax.broadcasted_iota(jnp.int32, (KEY_TILE, KEY_TILE), 1), 1.0, 0.0).astype(BF16)
    quota_row = jnp.where(clo > kf, kf - chi, float(2 * seq))
    quota = jnp.sum(eye_f * quota_row, axis=1, keepdims=True)

    def mask_tile(kt, carry):
        v = idx_ref[kt]
        code = jnp.where(v > hi, 2.0, jnp.where(v > lo, 1.0, 0.0)).astype(BF16)
        code_t = _dot_nt(eye, code)
        tie_t = jnp.where((code_t > 0.5) & (code_t < 1.5), 1.0, 0.0)
        rank = _dot(tie_t.astype(BF16), tri) + carry
        keep = (code_t > 1.5) | ((tie_t > 0.5) & (rank <= quota))
        bias_ref[0, kt] = jnp.where(keep, 0.0, NEG)
        return carry + jnp.sum(tie_t, axis=1, keepdims=True)

    lax.fori_loop(0, npair, lambda jj, c: mask_tile(2 * jj + 1, mask_tile(2 * jj, c)), jnp.zeros((nq, 1), F32))

    _norm_pairs(q_ref, gq_ref, qp_ref, n_pairs)
    _flash_pairs(qp_ref, kbd_ref, wext_ref, bias_ref, nkt, n_pairs, n_pairs,
                 ((s0_ref, s1_ref), (p0_ref, p1_ref), (a0_ref, a1_ref)), m_ref, acc_ref)
    for j in range(n_pairs):
        acc = acc_ref[j * nq:(j + 1) * nq, :]
        o_ref[:, j * LANES:(j + 1) * LANES] = (acc[:, :LANES] / acc[:, LANES:]).astype(o_ref.dtype)


def _dsa(u16, u32, iwt, gq, gk, bsz, seq):
    n_sel = min(DSA_TOPK, seq // 4)
    nqb = seq // DSA_Q_BLOCK
    n_pairs = DSA_HEADS // 2
    kern = functools.partial(_dsa_kernel, seq=seq, n_sel=n_sel)
    return pl.pallas_call(
        kern,
        grid=(bsz, nqb),
        in_specs=[pl.BlockSpec((DSA_Q_BLOCK, 1024), lambda b, i: (b * nqb + i, U16_DQ_BLK)),
                  pl.BlockSpec((DSA_Q_BLOCK, 512), lambda b, i: (b * nqb + i, U32_DS_BLK512)),
                  pl.BlockSpec((1, IDX_HEADS, DSA_Q_BLOCK), lambda b, i: (b, 0, i)),
                  pl.BlockSpec((seq, 512), lambda b, i: (b, U32_DS_BLK512)),
                  pl.BlockSpec((1, HEAD_DIM), lambda b, i: (0, 0)),
                  pl.BlockSpec((1, DSA_KV_DIM), lambda b, i: (0, 0))],
        out_specs=pl.BlockSpec((DSA_Q_BLOCK, DSA_HEADS * HEAD_DIM), lambda b, i: (b * nqb + i, 0)),
        out_shape=jax.ShapeDtypeStruct((bsz * seq, DSA_HEADS * HEAD_DIM), BF16),
        scratch_shapes=[pltpu.VMEM((1, 2 * seq, LANES), BF16),
                        pltpu.VMEM((1, 2 * seq, 2 * LANES), BF16),
                        pltpu.VMEM((seq, IDX_DIM), BF16),
                        pltpu.VMEM((seq // KEY_TILE, KEY_TILE, DSA_Q_BLOCK), F32),
                        pltpu.VMEM((1, seq // KEY_TILE, DSA_Q_BLOCK, KEY_TILE), F32),
                        pltpu.VMEM((IDX_HEADS * DSA_Q_BLOCK, IDX_DIM), BF16),
                        pltpu.VMEM((n_pairs * DSA_Q_BLOCK, LANES), BF16)]
                       + _flash_scratch(n_pairs, DSA_Q_BLOCK),
        compiler_params=_cparams(("parallel", "arbitrary")),
        name="dsa",
    )(u16, u32, iwt, u32, gq, gk)


def _nsa_compress_kernel(xk_ref, xv_ref, pos_ref, w_ref, g_ref, kc_ref, vc_ref):
    nch = xk_ref.shape[0] // CMP_STRIDE
    for jj, (x_ref, o_ref) in enumerate(((xk_ref, kc_ref), (xv_ref, vc_ref))):
        acc_a = jnp.zeros((nch, LANES), F32)
        acc_b = jnp.zeros((nch, LANES), F32)
        for l in range(CMP_STRIDE):
            x = x_ref[pl.ds(l, nch, stride=CMP_STRIDE), :]
            acc_a = acc_a + _dot((x + pos_ref[jj, l:l + 1, :]).astype(BF16), w_ref[jj, l])
            hi = CMP_STRIDE + l
            acc_b = acc_b + _dot((x + pos_ref[jj, hi:hi + 1, :]).astype(BF16), w_ref[jj, hi])
        out = acc_a + pltpu.roll(acc_b, nch - 1, axis=0)
        for g in range(NSA_KV_HEADS):
            og = out[:, g * HEAD_DIM:(g + 1) * HEAD_DIM]
            o_ref[0, g] = (_rms(og, g_ref[...]) if jj == 0 else og).astype(BF16)


def _nsa_compress(u32, pos, w, g, bsz, seq):
    nch = seq // CMP_STRIDE
    return pl.pallas_call(
        _nsa_compress_kernel,
        grid=(bsz,),
        in_specs=[pl.BlockSpec((seq, LANES), lambda b: (b, U32_KC_BLK128)),
                  pl.BlockSpec((seq, LANES), lambda b: (b, U32_KC_BLK128 + 1)),
                  pl.BlockSpec((2, CMP_BLOCK, LANES), lambda b: (0, 0, 0)),
                  pl.BlockSpec((2, CMP_BLOCK, LANES, LANES), lambda b: (0, 0, 0, 0)),
                  pl.BlockSpec((1, HEAD_DIM), lambda b: (0, 0))],
        out_specs=[pl.BlockSpec((1, NSA_KV_HEADS, nch, HEAD_DIM), lambda b: (b, 0, 0, 0)),
                   pl.BlockSpec((1, NSA_KV_HEADS, nch, HEAD_DIM), lambda b: (b, 0, 0, 0))],
        out_shape=[jax.ShapeDtypeStruct((bsz, NSA_KV_HEADS, nch, HEAD_DIM), BF16),
                   jax.ShapeDtypeStruct((bsz, NSA_KV_HEADS, nch, HEAD_DIM), BF16)],
        compiler_params=_cparams(("parallel",)),
        name="nsa_compress",
    )(u32, u32, pos, w, g)


def _nsa_kernel(q_ref, ng_ref, kv_ref, kc_ref, vc_ref, gq_ref, gk_ref, cover_ref, expand_ref, gexp_ref, o_ref,
                ksbd_ref, wsext_ref, kwbd_ref, wwext_ref, kcbd_ref, vcext_ref, bias_ref, wbias_ref, psum_ref,
                ocmp_ref, accw_ref, qp_ref, sw_ref, pw_ref, sc_ref, pc_ref,
                s0_ref, s1_ref, p0_ref, p1_ref, a0_ref, a1_ref, m_ref, acc_ref, *, seq):
    i = pl.program_id(1)
    nq = q_ref.shape[0]
    n_pairs = NSA_HEADS // 2
    gpairs = NSA_GROUP // 2
    grows = gpairs * nq
    n_cmp_pad = kc_ref.shape[2]
    n_blk = seq // SEL_BLOCK
    n_sel = min(SEL_TOPN, n_blk)
    win_keys = min(WINDOW + nq, seq)
    wt = LANES
    win_tiles = win_keys // wt

    @pl.when(i == 0)
    def _():
        def rows(r, c):
            off = pl.multiple_of(r * KEY_TILE, KEY_TILE)
            kv = kv_ref[pl.ds(off, KEY_TILE), :]
            for g in range(NSA_KV_HEADS):
                c0 = g * HEAD_DIM
                _store_pair_kv(ksbd_ref, wsext_ref, g, pl.multiple_of(2 * off, 2 * KEY_TILE), KEY_TILE,
                               _rms(kv[:, c0:c0 + 64], gk_ref[1:2, :]), kv[:, 128 + c0:128 + c0 + 64])
                kw = _rms(kv[:, 256 + c0:256 + c0 + 64], gk_ref[2:3, :])
                vw = kv[:, 384 + c0:384 + c0 + 64]
                for hf in range(KEY_TILE // wt):
                    _store_pair_kv(kwbd_ref, wwext_ref, g, pl.multiple_of(2 * off + hf * 2 * wt, 2 * wt), wt,
                                   kw[hf * wt:(hf + 1) * wt], vw[hf * wt:(hf + 1) * wt])
            return c

        lax.fori_loop(0, seq // KEY_TILE, rows, 0)
        zc = jnp.zeros((n_cmp_pad, HEAD_DIM), BF16)
        for g in range(NSA_KV_HEADS):
            kcbd_ref[g, 0:n_cmp_pad, :] = jnp.concatenate([kc_ref[0, g], zc], axis=-1)
            kcbd_ref[g, n_cmp_pad:2 * n_cmp_pad, :] = jnp.concatenate([zc, kc_ref[0, g]], axis=-1)
            vcext_ref[g, 0:n_cmp_pad, :] = jnp.concatenate([vc_ref[0, g], zc], axis=-1)
            vcext_ref[g, n_cmp_pad:2 * n_cmp_pad, :] = jnp.concatenate([zc, vc_ref[0, g]], axis=-1)

    start = i * nq
    nkt = (start + nq + KEY_TILE - 1) // KEY_TILE
    t_i = start + lax.broadcasted_iota(jnp.int32, (nq, 1), 0)
    t_col = t_i.astype(F32)
    lane_pos = lax.broadcasted_iota(jnp.int32, (1, KEY_TILE), 1).astype(F32)

    _norm_pairs(q_ref, gq_ref, qp_ref, n_pairs)

    cmp_end = (lax.broadcasted_iota(jnp.int32, (1, n_cmp_pad), 1) * CMP_STRIDE + (CMP_BLOCK - 1)).astype(F32)
    vis = cmp_end <= t_col
    wbase = pl.multiple_of(jnp.maximum(start + nq - win_keys, 0), wt)
    wb2 = pl.multiple_of(2 * wbase, 2 * wt)
    col = lax.broadcasted_iota(jnp.int32, (1, 2 * win_keys), 1)
    wpos = (wbase + jnp.right_shift(col, int(math.log2(2 * wt))) * wt + jnp.bitwise_and(col, wt - 1)).astype(F32)
    wbias_ref[...] = jnp.where((wpos <= t_col) & (wpos > t_col - float(WINDOW)), 0.0, NEG)
    for g in range(NSA_KV_HEADS):
        gs = slice(g * grows, (g + 1) * grows)
        sc_ref[gs, :] = _dot_nt(qp_ref[gs, :], kcbd_ref[g])
        sw_ref[gs, :] = _dot_nt(qp_ref[gs, :], kwbd_ref[g, pl.ds(wb2, 2 * win_keys), :])

    for g in range(NSA_KV_HEADS):
        gs = slice(g * grows, (g + 1) * grows)
        tot = jnp.zeros((nq, n_cmp_pad), F32)
        for j in range(g * gpairs, (g + 1) * gpairs):
            rs = slice(j * nq, (j + 1) * nq)
            for par in range(2):
                cs = slice(par * n_cmp_pad, (par + 1) * n_cmp_pad)
                s = jnp.where(vis, sc_ref[rs, cs], NEG)
                e = jnp.exp2(s - jnp.max(s, axis=-1, keepdims=True))
                p = jnp.where(vis, e / jnp.sum(e, axis=-1, keepdims=True), 0.0)
                pc_ref[rs, cs] = p.astype(BF16)
                tot = tot + p
        psum_ref[g] = tot
        ocmp_ref[gs, :] = _dot(pc_ref[gs, :], vcext_ref[g])

    blk_j = lax.broadcasted_iota(jnp.int32, (1, n_blk), 1)
    cur1 = jnp.right_shift(t_i, int(math.log2(SEL_BLOCK)))
    cur = jnp.concatenate([cur1] * NSA_KV_HEADS, axis=0)
    imp = jnp.concatenate([_dot_hi(psum_ref[g], cover_ref[...]) for g in range(NSA_KV_HEADS)], axis=0)
    forced = (blk_j == cur) | (blk_j == 0)
    imp = jnp.where(forced, jnp.inf, jnp.where(blk_j > cur, -jnp.inf, imp))
    selb = jnp.zeros(imp.shape, jnp.bool_)
    for _ in range(n_sel):
        mx = jnp.max(imp, axis=-1, keepdims=True)
        first = jnp.min(jnp.where(imp == mx, blk_j, n_blk), axis=-1, keepdims=True)
        pick = blk_j == first
        selb = selb | pick
        imp = jnp.where(pick, -jnp.inf, imp)
    selb_bf = jnp.where(selb, 1.0, 0.0).astype(BF16)

    for g in range(NSA_KV_HEADS):
        gs = slice(g * grows, (g + 1) * grows)
        for j in range(g * gpairs, (g + 1) * gpairs):
            rs = slice(j * nq, (j + 1) * nq)
            for par in range(2):
                cols = [slice(t * 2 * wt + par * wt, t * 2 * wt + (par + 1) * wt) for t in range(win_tiles)]
                s = [sw_ref[rs, cs] + wbias_ref[:, cs] for cs in cols]
                mx = s[0]
                for st in s[1:]:
                    mx = jnp.maximum(mx, st)
                m = jnp.max(mx, axis=-1, keepdims=True)
                for cs, st in zip(cols, s):
                    pw_ref[rs, cs] = jnp.exp2(st - m).astype(BF16)
        accw_ref[gs, :] = _dot(pw_ref[gs, :], wwext_ref[g, pl.ds(wb2, 2 * win_keys), :])

    def mask_tile(kt, c):
        off = pl.multiple_of(kt * KEY_TILE, KEY_TILE)
        hit = _dot(selb_bf, expand_ref[:, pl.ds(off, KEY_TILE)])
        ok = (lane_pos + (kt * KEY_TILE).astype(F32)) <= t_col
        for g in range(NSA_KV_HEADS):
            bias_ref[g, kt] = jnp.where((hit[g * nq:(g + 1) * nq] > 0.5) & ok, 0.0, NEG)
        return c

    lax.fori_loop(0, (nkt + 1) // 2, lambda jj, c: mask_tile(2 * jj + 1, mask_tile(2 * jj, c)), 0)
    _flash_pairs(qp_ref, ksbd_ref, wsext_ref, bias_ref, nkt, n_pairs, gpairs,
                 ((s0_ref, s1_ref), (p0_ref, p1_ref), (a0_ref, a1_ref)), m_ref, acc_ref)

    gates = jax.nn.sigmoid(ng_ref[...])
    g_hi = gates.astype(BF16)
    g_lo = (gates - g_hi.astype(F32)).astype(BF16)
    gb = [_dot(g_hi, gexp_ref[b]) + _dot(g_lo, gexp_ref[b]) for b in range(3)]
    for j in range(n_pairs):
        rs = slice(j * nq, (j + 1) * nq)
        ls = slice(j * LANES, (j + 1) * LANES)
        acc = acc_ref[rs, :]
        accw = accw_ref[rs, :]
        out = (gb[0][:, ls] * ocmp_ref[rs, :] + gb[1][:, ls] * (acc[:, :LANES] / acc[:, LANES:])
               + gb[2][:, ls] * (accw[:, :LANES] / accw[:, LANES:]))
        o_ref[:, ls] = out.astype(o_ref.dtype)


def _nsa(u16, u32, kc, vc, gq, gk, cover, expand, gexp, bsz, seq):
    nqb = seq // NSA_Q_BLOCK
    nch = kc.shape[2]
    n_blk = seq // SEL_BLOCK
    n_pairs = NSA_HEADS // 2
    prow = n_pairs * NSA_Q_BLOCK
    win_keys = min(WINDOW + NSA_Q_BLOCK, seq)
    kern = functools.partial(_nsa_kernel, seq=seq)
    return pl.pallas_call(
        kern,
        grid=(bsz, nqb),
        in_specs=[pl.BlockSpec((NSA_Q_BLOCK, 1024), lambda b, i: (b * nqb + i, U16_NQ_BLK)),
                  pl.BlockSpec((NSA_Q_BLOCK, 128), lambda b, i: (b * nqb + i, U32_NG_BLK128)),
                  pl.BlockSpec((seq, 512), lambda b, i: (b, U32_NKV_BLK512)),
                  pl.BlockSpec((1, NSA_KV_HEADS, nch, HEAD_DIM), lambda b, i: (b, 0, 0, 0)),
                  pl.BlockSpec((1, NSA_KV_HEADS, nch, HEAD_DIM), lambda b, i: (b, 0, 0, 0)),
                  pl.BlockSpec((1, HEAD_DIM), lambda b, i: (0, 0)),
                  pl.BlockSpec((3, HEAD_DIM), lambda b, i: (0, 0)),
                  pl.BlockSpec((nch, n_blk), lambda b, i: (0, 0)),
                  pl.BlockSpec((n_blk, seq), lambda b, i: (0, 0)),
                  pl.BlockSpec((3, LANES, NSA_HEADS * HEAD_DIM), lambda b, i: (0, 0, 0))],
        out_specs=pl.BlockSpec((NSA_Q_BLOCK, NSA_HEADS * HEAD_DIM), lambda b, i: (b * nqb + i, 0)),
        out_shape=jax.ShapeDtypeStruct((bsz * seq, NSA_HEADS * HEAD_DIM), BF16),
        scratch_shapes=[pltpu.VMEM((NSA_KV_HEADS, 2 * seq, LANES), BF16),
                        pltpu.VMEM((NSA_KV_HEADS, 2 * seq, 2 * LANES), BF16),
                        pltpu.VMEM((NSA_KV_HEADS, 2 * seq, LANES), BF16),
                        pltpu.VMEM((NSA_KV_HEADS, 2 * seq, 2 * LANES), BF16),
                        pltpu.VMEM((NSA_KV_HEADS, 2 * nch, LANES), BF16),
                        pltpu.VMEM((NSA_KV_HEADS, 2 * nch, LANES), BF16),
                        pltpu.VMEM((NSA_KV_HEADS, seq // KEY_TILE, NSA_Q_BLOCK, KEY_TILE), F32),
                        pltpu.VMEM((NSA_Q_BLOCK, 2 * win_keys), F32),
                        pltpu.VMEM((NSA_KV_HEADS, NSA_Q_BLOCK, nch), F32),
                        pltpu.VMEM((prow, LANES), F32),
                        pltpu.VMEM((prow, 2 * LANES), F32),
                        pltpu.VMEM((prow, LANES), BF16),
                        pltpu.VMEM((prow, 2 * win_keys), F32),
                        pltpu.VMEM((prow, 2 * win_keys), BF16),
                        pltpu.VMEM((prow, 2 * nch), F32),
                        pltpu.VMEM((prow, 2 * nch), BF16)]
                       + _flash_scratch(n_pairs, NSA_Q_BLOCK),
        compiler_params=_cparams(("parallel", "arbitrary")),
        name="nsa",
    )(u16, u32, u32, kc, vc, gq, gk, cover, expand, gexp)


def _softplus(x):
    return jnp.maximum(x, 0.0) + jnp.log1p(jnp.exp(-jnp.abs(x)))


def _ssd_kernel(xbc_ref, z_ref, dt_ref, dtt_ref, cw_ref, cb_ref, dtb_ref, dtbt_ref, al_ref, alt_ref,
                d_ref, ng_ref, o_ref, xcat_ref, xa_ref, y_ref, st_ref):
    c = pl.program_id(1)
    q = SSD_CHUNK
    gn = SSD_GROUPS * SSD_STATE
    tail = 16
    lo_half = lax.broadcasted_iota(jnp.int32, (1, LANES), 1) < SSD_HEAD_DIM

    @pl.when(c == 0)
    def _():
        st_ref[...] = jnp.zeros_like(st_ref)
        xcat_ref[0:q, :] = jnp.zeros((q, SSD_CONV_DIM), BF16)

    @pl.when(c > 0)
    def _():
        xcat_ref[q - tail:q, :] = xcat_ref[2 * q - tail:2 * q, :]

    xcat_ref[q:2 * q, :] = xbc_ref[...]
    ri3 = lax.broadcasted_iota(jnp.int32, ((CONV_WIDTH - 1) * q, 2 * q), 0)
    ci3 = lax.broadcasted_iota(jnp.int32, ((CONV_WIDTH - 1) * q, 2 * q), 1)
    qbits = int(math.log2(q))
    src = q + jnp.bitwise_and(ri3, q - 1) - (jnp.right_shift(ri3, qbits) + 1)
    shifts = jnp.where(ci3 == src, 1.0, 0.0).astype(BF16)
    for cc in range(SSD_CONV_DIM // CONV_CHUNK):
        cs = slice(cc * CONV_CHUNK, (cc + 1) * CONV_CHUNK)
        sh = _dot(shifts, xcat_ref[:, cs])
        acc = cb_ref[:, cs] + cw_ref[CONV_WIDTH - 1:CONV_WIDTH, cs] * xcat_ref[q:2 * q, cs].astype(F32)
        for s in range(1, CONV_WIDTH):
            acc = acc + cw_ref[CONV_WIDTH - 1 - s:CONV_WIDTH - s, cs] * sh[(s - 1) * q:s * q]
        xa_ref[:, cs] = acc * jax.nn.sigmoid(acc)

    dt = _softplus(dt_ref[:, :SSD_HEADS] + dtb_ref[...])
    dtt = _softplus(dtt_ref[0] + dtbt_ref[...])
    a = -jnp.exp(al_ref[...]) * LOG2E
    at = -jnp.exp(alt_ref[...]) * LOG2E
    ri = lax.broadcasted_iota(jnp.int32, (q, q), 0)
    ci = lax.broadcasted_iota(jnp.int32, (q, q), 1)
    tril = ri >= ci
    acum = _dot_hi(jnp.where(tril, 1.0, 0.0), dt * a)
    acumt = _dot_hi(dtt * at, jnp.where(ri <= ci, 1.0, 0.0))
    wrow = jnp.exp2(acumt[:, q - 1:q] - acumt) * dtt

    for g in range(SSD_GROUPS):
        bmat = xa_ref[:, SSD_INNER + g * SSD_STATE:SSD_INNER + (g + 1) * SSD_STATE]
        cmat = xa_ref[:, SSD_INNER + gn + g * SSD_STATE:SSD_INNER + gn + (g + 1) * SSD_STATE].astype(BF16)
        cb = _dot_nt(cmat, bmat.astype(BF16))
        bt = bmat.T
        for pp in range(SSD_HPG // 2):
            j = g * (SSD_HPG // 2) + pp
            ls = slice(j * LANES, (j + 1) * LANES)
            xs = xa_ref[:, ls]
            xbd = jnp.concatenate([jnp.where(lo_half, xs, 0.0), jnp.where(lo_half, 0.0, xs)], axis=0).astype(BF16)
            mm, dec, btw = [], [], []
            for par in range(2):
                h = 2 * j + par
                abc = jnp.broadcast_to(acum[:, h:h + 1], (q, q))
                lmat = jnp.exp2(jnp.where(tril, abc - acumt[h:h + 1, :], -jnp.inf))
                mm.append((cb * lmat * dtt[h:h + 1, :]).astype(BF16))
                dec.append(jnp.exp2(abc))
                btw.append((bt * wrow[h:h + 1, :]).astype(BF16))
            y = _dot(jnp.concatenate(mm, axis=1), xbd)
            y = y + _dot(cmat, st_ref[j].astype(BF16)) * jnp.where(lo_half, dec[0], dec[1])
            y_ref[:, ls] = y + xs * d_ref[:, ls]
            cdec = jnp.where(lo_half, dec[0][q - 1:q, :], dec[1][q - 1:q, :])
            st_ref[j] = st_ref[j] * cdec + _dot(jnp.concatenate(btw, axis=1), xbd)

    z = z_ref[...].astype(F32)
    yz = y_ref[...] * (z * jax.nn.sigmoid(z))
    o_ref[...] = _rms(yz, ng_ref[...]).astype(o_ref.dtype)


def _ssd(u16, u32, dtt, cw, cb, dtb, al, d, ng, bsz, seq):
    nc = seq // SSD_CHUNK
    q = SSD_CHUNK
    full = lambda shape: pl.BlockSpec(shape, lambda b, c: (0,) * len(shape))
    d_full = jnp.repeat(d, SSD_HEAD_DIM).reshape(1, SSD_INNER)
    return pl.pallas_call(
        _ssd_kernel,
        grid=(bsz, nc),
        in_specs=[pl.BlockSpec((q, SSD_CONV_DIM), lambda b, c: (b * nc + c, U16_XBC_BLK)),
                  pl.BlockSpec((q, SSD_INNER), lambda b, c: (b * nc + c, U16_Z_BLK)),
                  pl.BlockSpec((q, 128), lambda b, c: (b * nc + c, U32_DT_BLK128)),
                  pl.BlockSpec((1, SSD_HEADS, q), lambda b, c: (b, 0, c)),
                  full((CONV_WIDTH, SSD_CONV_DIM)), full((1, SSD_CONV_DIM)),
                  full((1, SSD_HEADS)), full((SSD_HEADS, 1)),
                  full((1, SSD_HEADS)), full((SSD_HEADS, 1)),
                  full((1, SSD_INNER)), full((1, SSD_INNER))],
        out_specs=pl.BlockSpec((q, SSD_INNER), lambda b, c: (b * nc + c, 0)),
        out_shape=jax.ShapeDtypeStruct((bsz * seq, SSD_INNER), BF16),
        scratch_shapes=[pltpu.VMEM((2 * q, SSD_CONV_DIM), BF16),
                        pltpu.VMEM((q, SSD_CONV_DIM), F32),
                        pltpu.VMEM((q, SSD_INNER), F32),
                        pltpu.VMEM((SSD_HEADS // 2, SSD_STATE, LANES), F32)],
        compiler_params=_cparams(("parallel", "arbitrary")),
        name="ssd",
    )(u16, u16, u32, dtt, cw, cb, dtb.reshape(1, -1), dtb.reshape(-1, 1), al.reshape(1, -1), al.reshape(-1, 1),
      d_full, ng.reshape(1, -1))


def _merge_kernel(x_ref, ya_ref, yb_ref, yc_ref, mg_ref, wa_ref, wb_ref, wc_ref, wo_ref, o_ref):
    d = D_MODEL
    mg = mg_ref[...].astype(F32)
    mix = jax.nn.sigmoid(mg[:, 0:d]) * _dot(ya_ref[...], wa_ref[...])
    mix = mix + jax.nn.sigmoid(mg[:, d:2 * d]) * _dot(yb_ref[...], wb_ref[...])
    mix = mix + jax.nn.sigmoid(mg[:, 2 * d:3 * d]) * _dot(yc_ref[...], wc_ref[...])
    o_ref[...] = x_ref[...] + _dot(mix.astype(BF16), wo_ref[...])


def _merge(x, ya, yb, yc, u16, wa, wb, wc, wo, tm):
    m, d = x.shape
    full = lambda a: pl.BlockSpec(a.shape, lambda i: (0, 0))
    return pl.pallas_call(
        _merge_kernel,
        grid=(m // tm,),
        in_specs=[pl.BlockSpec((tm, d), lambda i: (i, 0)),
                  pl.BlockSpec((tm, ya.shape[1]), lambda i: (i, 0)),
                  pl.BlockSpec((tm, yb.shape[1]), lambda i: (i, 0)),
                  pl.BlockSpec((tm, yc.shape[1]), lambda i: (i, 0)),
                  pl.BlockSpec((tm, 3 * d), lambda i: (i, U16_MG_BLK)),
                  full(wa), full(wb), full(wc), full(wo)],
        out_specs=pl.BlockSpec((tm, d), lambda i: (i, 0)),
        out_shape=jax.ShapeDtypeStruct((m, d), F32),
        compiler_params=_cparams(("parallel",)),
        name="merge",
    )(x, ya, yb, yc, u16, wa, wb, wc, wo)


def _ffn_kernel(x_ref, g_ref, w1_ref, w2_ref, o_ref, h_ref):
    @pl.when(pl.program_id(1) == 0)
    def _():
        h_ref[...] = _rms(x_ref[...], g_ref[...]).astype(BF16)
        o_ref[...] = x_ref[...]

    a = jnp.maximum(_dot(h_ref[...], w1_ref[0].astype(BF16)), 0.0)
    o_ref[...] += _dot((a * a).astype(BF16), w2_ref[0].astype(BF16))


def _ffn(x, g, w1, w2, layer, tm, tf):
    m, d = x.shape
    f = w1.shape[2]
    return pl.pallas_call(
        _ffn_kernel,
        grid=(m // tm, f // tf),
        in_specs=[pl.BlockSpec((tm, d), lambda i, j: (i, 0)),
                  pl.BlockSpec((1, d), lambda i, j: (0, 0)),
                  pl.BlockSpec((1, d, tf), lambda i, j: (layer, 0, j)),
                  pl.BlockSpec((1, tf, d), lambda i, j: (layer, j, 0))],
        out_specs=pl.BlockSpec((tm, d), lambda i, j: (i, 0)),
        out_shape=jax.ShapeDtypeStruct((m, d), F32),
        scratch_shapes=[pltpu.VMEM((tm, d), BF16)],
        compiler_params=_cparams(("parallel", "arbitrary")),
        name="ffn",
    )(x, g, w1, w2)


def _prep_w_in(w):
    offs = [0] + [int(o) for o in np.cumsum(IN_SPLITS)]
    seg = lambda k: w[:, offs[k]:offs[k + 1]]
    dq, dk, dv, iq, ik, iw, nq, nkv, ng, sz, sxbc, sdt, mg = [seg(k) for k in range(13)]
    zeros = lambda n: jnp.zeros((w.shape[0], n), w.dtype)
    w16 = jnp.concatenate([sxbc, mg, sz, dq, nq], axis=1).astype(BF16)
    w32 = jnp.concatenate([nkv[:, 256:768], nkv[:, 0:256], zeros(256),
                           dk, dv, iq, ik, iw, zeros(512 - 424),
                           ng, zeros(128 - ng.shape[1]),
                           sdt, zeros(128 - sdt.shape[1])], axis=1).astype(BF16)
    return w16, w32


def _pick_tile(n, pref):
    t = min(n, pref)
    while n % t:
        t //= 2
    return t


def kernel(x, norm1_g, w_in, dsa_q_norm, dsa_k_norm, nsa_q_norm, nsa_k_norm, nsa_cmp_pos, nsa_cmp_w,
           ssd_conv_w, ssd_conv_b, ssd_dt_bias, ssd_a_log, ssd_d, ssd_norm_g,
           w_br_dsa, w_br_nsa, w_br_ssd, w_out, norm2_g, w_ff1, w_ff2):
    bsz, seq, d = x.shape
    m = bsz * seq
    depth = w_in.shape[0]
    nch = seq // CMP_STRIDE
    n_blk = seq // SEL_BLOCK
    half = CMP_BLOCK // 2

    cmp_start = np.arange(nch) * CMP_STRIDE
    blk_start = np.arange(n_blk) * SEL_BLOCK
    cover = ((cmp_start[:, None] < blk_start[None, :] + SEL_BLOCK)
             & (cmp_start[:, None] + CMP_BLOCK > blk_start[None, :])
             & (np.arange(nch)[:, None] < (seq - CMP_BLOCK) // CMP_STRIDE + 1)).astype(np.float32)
    expand = (np.arange(seq)[None, :] // SEL_BLOCK == np.arange(n_blk)[:, None]).astype(np.float32)
    gexp = np.zeros((3, LANES, NSA_HEADS * HEAD_DIM), np.float32)
    for hh in range(NSA_HEADS):
        for br in range(3):
            gexp[br, 3 * hh + br, hh * HEAD_DIM:(hh + 1) * HEAD_DIM] = 1.0
    gexp = jnp.asarray(gexp, dtype=BF16)
    cover = jnp.asarray(cover)
    expand = jnp.asarray(expand, dtype=BF16)

    tm = _pick_tile(m, 1024)
    xf = x.reshape(m, d)
    for l in range(depth):
        w16, w32 = _prep_w_in(w_in[l])
        g1 = norm1_g[l].reshape(1, d)
        u16 = _norm_matmul(xf, g1, w16, BF16, tm, 2048)
        u32 = _norm_matmul(xf, g1, w32, F32, tm, U32_WIDTH)

        iw0 = U32_DS_BLK512 * 512 + DS_IW
        iwt = u32[:, iw0:iw0 + IDX_HEADS].reshape(bsz, seq, IDX_HEADS).transpose(0, 2, 1)
        ya = _dsa(u16, u32, iwt, dsa_q_norm[l].reshape(1, -1), dsa_k_norm[l].reshape(1, -1), bsz, seq)

        wl = nsa_cmp_w[l]
        zblk = jnp.zeros_like(wl)
        cw = jnp.concatenate([jnp.concatenate([wl, zblk], axis=3),
                              jnp.concatenate([zblk, wl], axis=3)], axis=2).astype(BF16)
        cpos = jnp.concatenate([nsa_cmp_pos[l]] * NSA_KV_HEADS, axis=-1)
        kc, vc = _nsa_compress(u32, cpos, cw, nsa_k_norm[l][0:1], bsz, seq)
        yb = _nsa(u16, u32, kc, vc, nsa_q_norm[l].reshape(1, -1), nsa_k_norm[l], cover, expand, gexp, bsz, seq)

        dt0 = U32_DT_BLK128 * 128
        dtt = u32[:, dt0:dt0 + SSD_HEADS].reshape(bsz, seq, SSD_HEADS).transpose(0, 2, 1)
        yc = _ssd(u16, u32, dtt, ssd_conv_w[l], ssd_conv_b[l].reshape(1, -1), ssd_dt_bias[l], ssd_a_log[l],
                  ssd_d[l], ssd_norm_g[l], bsz, seq)

        xf = _merge(xf, ya, yb, yc, u16,
                    w_br_dsa[l].astype(BF16), w_br_nsa[l].astype(BF16), w_br_ssd[l].astype(BF16),
                    w_out[l].astype(BF16), _pick_tile(m, 512))
        xf = _ffn(xf, norm2_g[l].reshape(1, d), w_ff1, w_ff2, l,
                  _pick_tile(m, 1024), 1024)
    return xf.reshape(bsz, seq, d)
```

```python
import functools
import math

import numpy as np
import jax
import jax.numpy as jnp
from jax import lax
from jax.experimental import pallas as pl
from jax.experimental.pallas import tpu as pltpu

F32 = jnp.float32
BF16 = jnp.bfloat16

D_MODEL = 1024
HEAD_DIM = 64
DSA_HEADS = 16
DSA_KV_DIM = 64
IDX_HEADS = 8
IDX_DIM = 32
DSA_TOPK = 256
NSA_HEADS = 16
NSA_KV_HEADS = 2
NSA_GROUP = NSA_HEADS // NSA_KV_HEADS
CMP_BLOCK = 32
CMP_STRIDE = 16
SEL_BLOCK = 64
SEL_TOPN = 4
WINDOW = 512
SSD_INNER = 2 * D_MODEL
SSD_HEAD_DIM = 64
SSD_HEADS = SSD_INNER // SSD_HEAD_DIM
SSD_GROUPS = 4
SSD_HPG = SSD_HEADS // SSD_GROUPS
SSD_STATE = 128
SSD_CONV_DIM = SSD_INNER + 2 * SSD_GROUPS * SSD_STATE
CONV_WIDTH = 4
SSD_CHUNK = 128
Q_BLOCK = 128
DSA_Q_BLOCK = 256
NSA_Q_BLOCK = 128
EPS = 1e-6
NEG = -1e30
IN_SPLITS = (DSA_HEADS * HEAD_DIM, DSA_KV_DIM, DSA_KV_DIM, IDX_HEADS * IDX_DIM, IDX_DIM, IDX_HEADS,
             NSA_HEADS * HEAD_DIM, 6 * NSA_KV_HEADS * HEAD_DIM, 3 * NSA_HEADS,
             SSD_INNER, SSD_CONV_DIM, SSD_HEADS, 3 * D_MODEL)

V7X_VMEM_LIMIT_BYTES = 56 * 1024 * 1024
LANES = 128

U16_WIDTH = 10240
U16_XBC_BLK = 0
U16_MG_BLK = 1
U16_Z_BLK = 3
U16_DQ_BLK = 8
U16_NQ_BLK = 9
U32_WIDTH = 1792
U32_NKV_BLK512 = 0
U32_KC_BLK128 = 4
U32_DS_BLK512 = 2
U32_NG_BLK128 = 12
U32_DT_BLK128 = 13
DS_DK, DS_DV, DS_IQ, DS_IK, DS_IW = 0, 64, 128, 384, 416

KEY_TILE = 256
BISECT_ITERS = 24
PAIR_STEP = 2
FLASH_UNROLL = 4
LOG2E = 1.4426950408889634
BISECT_UNROLL = 4
CONV_CHUNK = 256


def _cparams(sem):
    return pltpu.CompilerParams(dimension_semantics=sem, vmem_limit_bytes=V7X_VMEM_LIMIT_BYTES)


def _rms(x, g):
    return x * lax.rsqrt(jnp.mean(x * x, axis=-1, keepdims=True) + EPS) * g


def _dot_nt(a, b):
    return lax.dot_general(a, b, (((1,), (1,)), ((), ())), preferred_element_type=F32)


def _dot(a, b):
    return jnp.dot(a, b, preferred_element_type=F32)


def _dot_hi(a, b):
    return jnp.dot(a, b, preferred_element_type=F32, precision=lax.Precision.HIGHEST)


def _norm_matmul_kernel(x_ref, g_ref, w_ref, o_ref, h_ref):
    @pl.when(pl.program_id(1) == 0)
    def _():
        h_ref[...] = _rms(x_ref[...], g_ref[...]).astype(BF16)

    o_ref[...] = _dot(h_ref[...], w_ref[...]).astype(o_ref.dtype)


def _norm_matmul(x, g, w, out_dtype, tm, tn):
    m, k = x.shape
    n = w.shape[1]
    return pl.pallas_call(
        _norm_matmul_kernel,
        grid=(m // tm, n // tn),
        in_specs=[pl.BlockSpec((tm, k), lambda i, j: (i, 0)),
                  pl.BlockSpec((1, k), lambda i, j: (0, 0)),
                  pl.BlockSpec((k, tn), lambda i, j: (0, j))],
        out_specs=pl.BlockSpec((tm, tn), lambda i, j: (i, j)),
        out_shape=jax.ShapeDtypeStruct((m, n), out_dtype),
        scratch_shapes=[pltpu.VMEM((tm, k), BF16)],
        compiler_params=_cparams(("parallel", "arbitrary")),
        name="norm_matmul",
    )(x, g, w)


def _flash_pairs(qp_ref, kbd_ref, wext_ref, bias_ref, nkt, n_pairs, pairs_per_kv, bufs, m_ref, acc_ref):
    (s0, s1), (p0, p1), (a0, a1) = bufs
    qb = bias_ref.shape[2]
    n_pg = n_pairs // PAIR_STEP
    rows = PAIR_STEP * qb
    n = nkt * n_pg
    lo_half = lax.broadcasted_iota(jnp.int32, (1, LANES), 1) < HEAD_DIM
    m_ref[...] = jnp.full(m_ref.shape, -jnp.inf, F32)
    acc_ref[...] = jnp.zeros(acc_ref.shape, F32)
    p1[...] = jnp.zeros(p1.shape, BF16)
    a1[...] = jnp.ones(a1.shape, F32)

    def where(j):
        kt = j // n_pg
        pg = j % n_pg
        g = (pg * PAIR_STEP) // pairs_per_kv
        return kt, pg, g, pl.multiple_of(kt * 2 * KEY_TILE, 2 * KEY_TILE), pl.multiple_of(pg * rows, rows)

    def qk(j, s_ref):
        kt, _, g, koff, r = where(j)
        s = _dot_nt(qp_ref[pl.ds(r, rows), :], kbd_ref[g, pl.ds(koff, 2 * KEY_TILE), :])
        s_ref[...] = s + jnp.tile(bias_ref[g, kt], (PAIR_STEP, 2))

    def softmax(j, s_ref, p_ref, a_ref):
        _, pg, _, _, _ = where(j)
        for u in range(PAIR_STEP):
            us = slice(u * qb, (u + 1) * qb)
            alpha = []
            for par in range(2):
                cs = slice(par * KEY_TILE, (par + 1) * KEY_TILE)
                hrow = pl.multiple_of(((pg * PAIR_STEP + u) * 2 + par) * qb, qb)
                m_old = m_ref[pl.ds(hrow, qb), :]
                m_new = jnp.maximum(m_old, jnp.max(s_ref[us, cs], axis=-1, keepdims=True))
                alpha.append(jnp.exp2(m_old - m_new))
                m_ref[pl.ds(hrow, qb), :] = m_new
                p_ref[us, cs] = jnp.exp2(
                    s_ref[us, cs] - jnp.concatenate([m_new] * (KEY_TILE // LANES), axis=1)).astype(BF16)
            a_ref[us, :] = jnp.where(lo_half, alpha[0], alpha[1])

    def pv(j, p_ref, a_ref):
        _, _, g, koff, r = where(j)
        a = a_ref[...]
        acc_ref[pl.ds(r, rows), :] = (jnp.concatenate([a, a], axis=1) * acc_ref[pl.ds(r, rows), :]
                                      + _dot(p_ref[...], wext_ref[g, pl.ds(koff, 2 * KEY_TILE), :]))

    qk(0, s0)

    def body(jj, c):
        for j in (FLASH_UNROLL * jj, FLASH_UNROLL * jj + 2):
            qk(j + 1, s1)
            softmax(j, s0, p0, a0)
            pv(jnp.maximum(j - 1, 0), p1, a1)
            qk(jnp.minimum(j + 2, n - 1), s0)
            softmax(j + 1, s1, p1, a1)
            pv(j, p0, a0)
        return c

    lax.fori_loop(0, n // FLASH_UNROLL, body, 0)
    pv(n - 1, p1, a1)


def _flash_scratch(n_pairs, qb):
    step = PAIR_STEP * qb
    return ([pltpu.VMEM((step, 2 * KEY_TILE), F32)] * 2 + [pltpu.VMEM((step, 2 * KEY_TILE), BF16)] * 2
            + [pltpu.VMEM((step, LANES), F32)] * 2
            + [pltpu.VMEM((2 * n_pairs * qb, LANES), F32), pltpu.VMEM((n_pairs * qb, 2 * LANES), F32)])


def _norm_pairs(q_ref, g_ref, qp_ref, n_pairs):
    qb = q_ref.shape[0]
    mult = HEAD_DIM ** -0.5 * LOG2E
    lo_half = lax.broadcasted_iota(jnp.int32, (1, LANES), 1) < HEAD_DIM
    g2 = jnp.concatenate([g_ref[...], g_ref[...]], axis=-1) * mult
    for j in range(n_pairs):
        x = q_ref[:, j * LANES:(j + 1) * LANES].astype(F32)
        x2 = x * x
        s_lo = jnp.sum(jnp.where(lo_half, x2, 0.0), axis=-1, keepdims=True)
        s_hi = jnp.sum(jnp.where(lo_half, 0.0, x2), axis=-1, keepdims=True)
        r = jnp.where(lo_half, lax.rsqrt(s_lo * (1.0 / HEAD_DIM) + EPS), lax.rsqrt(s_hi * (1.0 / HEAD_DIM) + EPS))
        qp_ref[j * qb:(j + 1) * qb, :] = (x * r * g2).astype(BF16)


def _store_pair_kv(kbd_ref, wext_ref, g, row0, n, k, v):
    z = jnp.zeros((n, HEAD_DIM), F32)
    one = jnp.ones((n, HEAD_DIM), F32)
    kbd_ref[g, pl.ds(row0, n), :] = jnp.concatenate([k, z], axis=-1).astype(BF16)
    kbd_ref[g, pl.ds(row0 + n, n), :] = jnp.concatenate([z, k], axis=-1).astype(BF16)
    wext_ref[g, pl.ds(row0, n), :] = jnp.concatenate([v, z, one, z], axis=-1).astype(BF16)
    wext_ref[g, pl.ds(row0 + n, n), :] = jnp.concatenate([z, v, z, one], axis=-1).astype(BF16)


def _dsa_kernel(q_ref, sq_ref, iwt_ref, sk_ref, gq_ref, gk_ref, o_ref,
                kbd_ref, wext_ref, ik_ref, idx_ref, bias_ref, iqs_ref,
                qp_ref, s0_ref, s1_ref, p0_ref, p1_ref, a0_ref, a1_ref, m_ref, acc_ref, *, seq, n_sel):
    i = pl.program_id(1)
    nq = q_ref.shape[0]
    n_pairs = DSA_HEADS // 2
    kt_shape = (KEY_TILE, nq)

    @pl.when(i == 0)
    def _():
        def rows(r, c):
            off = pl.multiple_of(r * KEY_TILE, KEY_TILE)
            sk = sk_ref[pl.ds(off, KEY_TILE), :]
            _store_pair_kv(kbd_ref, wext_ref, 0, pl.multiple_of(2 * off, 2 * KEY_TILE), KEY_TILE,
                           _rms(sk[:, DS_DK:DS_DK + HEAD_DIM], gk_ref[...]), sk[:, DS_DV:DS_DV + HEAD_DIM])
            ik_ref[pl.ds(off, KEY_TILE), :] = sk[:, DS_IK:DS_IK + IDX_DIM].astype(BF16)
            return c

        lax.fori_loop(0, seq // KEY_TILE, rows, 0)

    start = i * nq
    nkt = (start + nq + KEY_TILE - 1) // KEY_TILE
    t_row = (start + lax.broadcasted_iota(jnp.int32, (1, nq), 1)).astype(F32)
    sub_pos = lax.broadcasted_iota(jnp.int32, (KEY_TILE, 1), 0).astype(F32)

    sq = sq_ref[...]
    for h in range(IDX_HEADS):
        iqs_ref[h * nq:(h + 1) * nq, :] = sq[:, DS_IQ + h * IDX_DIM:DS_IQ + (h + 1) * IDX_DIM].astype(BF16)
    iwt = iwt_ref[0]

    def idx_tile(kt, carry):
        rmin, rmax = carry
        off = pl.multiple_of(kt * KEY_TILE, KEY_TILE)
        r = _dot_nt(ik_ref[pl.ds(off, KEY_TILE), :], iqs_ref[...])
        acc = jnp.zeros(kt_shape, F32)
        for h in range(IDX_HEADS):
            acc = acc + iwt[h:h + 1, :] * jnp.maximum(r[:, h * nq:(h + 1) * nq], 0.0)
        valid = (sub_pos + (kt * KEY_TILE).astype(F32)) <= t_row
        idx_ref[kt] = jnp.where(valid, acc, -jnp.inf)
        rmin = jnp.minimum(rmin, jnp.min(jnp.where(valid, acc, jnp.inf), axis=0, keepdims=True))
        rmax = jnp.maximum(rmax, jnp.max(jnp.where(valid, acc, -jnp.inf), axis=0, keepdims=True))
        return rmin, rmax

    npair = (nkt + 1) // 2
    rmin, rmax = lax.fori_loop(0, npair, lambda jj, c: idx_tile(2 * jj + 1, idx_tile(2 * jj, c)),
                               (jnp.full((1, nq), jnp.inf, F32), jnp.full((1, nq), -jnp.inf, F32)))

    kf = float(n_sel)

    def col_sum(x):
        parts = [x[r * 8:(r + 1) * 8] for r in range(x.shape[0] // 8)]
        while len(parts) > 1:
            parts = [parts[k] + parts[k + 1] for k in range(0, len(parts), 2)]
        return jnp.sum(parts[0], axis=0, keepdims=True)

    fold = KEY_TILE // 4

    def count_gt(thr):
        def body(kt, acc):
            hit = jnp.where(idx_ref[kt] > thr, 1.0, 0.0)
            return acc + ((hit[0:fold] + hit[fold:2 * fold]) + (hit[2 * fold:3 * fold] + hit[3 * fold:]))

        return col_sum(lax.fori_loop(0, nkt, body, jnp.zeros((fold, nq), F32)))

    n_valid = t_row + 1.0
    lo0 = rmin - (jnp.abs(rmin) * 0.01 + 1.0)

    def bis_cond(c):
        it, _, _, clo, _ = c
        return jnp.logical_and(it < BISECT_ITERS, jnp.max(clo) > kf)

    def bis_body(c):
        it, lo, hi, clo, chi = c
        for _ in range(BISECT_UNROLL):
            mid = 0.5 * (lo + hi)
            cnt = count_gt(mid)
            ge = cnt >= kf
            lo, hi = jnp.where(ge, mid, lo), jnp.where(ge, hi, mid)
            clo, chi = jnp.where(ge, cnt, clo), jnp.where(ge, chi, cnt)
        return it + BISECT_UNROLL, lo, hi, clo, chi

    _, lo, hi, clo, chi = lax.while_loop(
        bis_cond, bis_body, (jnp.int32(0), lo0, rmax, n_valid, jnp.zeros((1, nq), F32)))

    eye_f = jnp.where(lax.broadcasted_iota(jnp.int32, (nq, nq), 0) == lax.broadcasted_iota(jnp.int32, (nq, nq), 1),
                      1.0, 0.0)
    eye = eye_f.astype(BF16)
    tri = jnp.where(lax.broadcasted_iota(jnp.int32, (KEY_TILE, KEY_TILE), 0)
                    <= lax.broadcasted_iota(jnp.int32, (KEY_TILE, KEY_TILE), 1), 1.0, 0.0).astype(BF16)
    quota_row = jnp.where(clo > kf, kf - chi, float(2 * seq))
    quota = jnp.sum(eye_f * quota_row, axis=1, keepdims=True)

    def mask_tile(kt, carry):
        v = idx_ref[kt]
        code = jnp.where(v > hi, 2.0, jnp.where(v > lo, 1.0, 0.0)).astype(BF16)
        code_t = _dot_nt(eye, code)
        tie_t = jnp.where((code_t > 0.5) & (code_t < 1.5), 1.0, 0.0)
        rank = _dot(tie_t.astype(BF16), tri) + carry
        keep = (code_t > 1.5) | ((tie_t > 0.5) & (rank <= quota))
        bias_ref[0, kt] = jnp.where(keep, 0.0, NEG)
        return carry + jnp.sum(tie_t, axis=1, keepdims=True)

    lax.fori_loop(0, npair, lambda jj, c: mask_tile(2 * jj + 1, mask_tile(2 * jj, c)), jnp.zeros((nq, 1), F32))

    _norm_pairs(q_ref, gq_ref, qp_ref, n_pairs)
    _flash_pairs(qp_ref, kbd_ref, wext_ref, bias_ref, nkt, n_pairs, n_pairs,
                 ((s0_ref, s1_ref), (p0_ref, p1_ref), (a0_ref, a1_ref)), m_ref, acc_ref)
    for j in range(n_pairs):
        acc = acc_ref[j * nq:(j + 1) * nq, :]
        o_ref[:, j * LANES:(j + 1) * LANES] = (acc[:, :LANES] / acc[:, LANES:]).astype(o_ref.dtype)


def _dsa(u16, u32, iwt, gq, gk, bsz, seq):
    n_sel = min(DSA_TOPK, seq // 4)
    nqb = seq // DSA_Q_BLOCK
    n_pairs = DSA_HEADS // 2
    kern = functools.partial(_dsa_kernel, seq=seq, n_sel=n_sel)
    return pl.pallas_call(
        kern,
        grid=(bsz, nqb),
        in_specs=[pl.BlockSpec((DSA_Q_BLOCK, 1024), lambda b, i: (b * nqb + i, U16_DQ_BLK)),
                  pl.BlockSpec((DSA_Q_BLOCK, 512), lambda b, i: (b * nqb + i, U32_DS_BLK512)),
                  pl.BlockSpec((1, IDX_HEADS, DSA_Q_BLOCK), lambda b, i: (b, 0, i)),
                  pl.BlockSpec((seq, 512), lambda b, i: (b, U32_DS_BLK512)),
                  pl.BlockSpec((1, HEAD_DIM), lambda b, i: (0, 0)),
                  pl.BlockSpec((1, DSA_KV_DIM), lambda b, i: (0, 0))],
        out_specs=pl.BlockSpec((DSA_Q_BLOCK, DSA_HEADS * HEAD_DIM), lambda b, i: (b * nqb + i, 0)),
        out_shape=jax.ShapeDtypeStruct((bsz * seq, DSA_HEADS * HEAD_DIM), BF16),
        scratch_shapes=[pltpu.VMEM((1, 2 * seq, LANES), BF16),
                        pltpu.VMEM((1, 2 * seq, 2 * LANES), BF16),
                        pltpu.VMEM((seq, IDX_DIM), BF16),
                        pltpu.VMEM((seq // KEY_TILE, KEY_TILE, DSA_Q_BLOCK), F32),
                        pltpu.VMEM((1, seq // KEY_TILE, DSA_Q_BLOCK, KEY_TILE), F32),
                        pltpu.VMEM((IDX_HEADS * DSA_Q_BLOCK, IDX_DIM), BF16),
                        pltpu.VMEM((n_pairs * DSA_Q_BLOCK, LANES), BF16)]
                       + _flash_scratch(n_pairs, DSA_Q_BLOCK),
        compiler_params=_cparams(("parallel", "arbitrary")),
        name="dsa",
    )(u16, u32, iwt, u32, gq, gk)


def _nsa_compress_kernel(xk_ref, xv_ref, pos_ref, w_ref, g_ref, kc_ref, vc_ref):
    nch = xk_ref.shape[0] // CMP_STRIDE
    for jj, (x_ref, o_ref) in enumerate(((xk_ref, kc_ref), (xv_ref, vc_ref))):
        acc_a = jnp.zeros((nch, LANES), F32)
        acc_b = jnp.zeros((nch, LANES), F32)
        for l in range(CMP_STRIDE):
            x = x_ref[pl.ds(l, nch, stride=CMP_STRIDE), :]
            acc_a = acc_a + _dot((x + pos_ref[jj, l:l + 1, :]).astype(BF16), w_ref[jj, l])
            hi = CMP_STRIDE + l
            acc_b = acc_b + _dot((x + pos_ref[jj, hi:hi + 1, :]).astype(BF16), w_ref[jj, hi])
        out = acc_a + pltpu.roll(acc_b, nch - 1, axis=0)
        for g in range(NSA_KV_HEADS):
            og = out[:, g * HEAD_DIM:(g + 1) * HEAD_DIM]
            o_ref[0, g] = (_rms(og, g_ref[...]) if jj == 0 else og).astype(BF16)


def _nsa_compress(u32, pos, w, g, bsz, seq):
    nch = seq // CMP_STRIDE
    return pl.pallas_call(
        _nsa_compress_kernel,
        grid=(bsz,),
        in_specs=[pl.BlockSpec((seq, LANES), lambda b: (b, U32_KC_BLK128)),
                  pl.BlockSpec((seq, LANES), lambda b: (b, U32_KC_BLK128 + 1)),
                  pl.BlockSpec((2, CMP_BLOCK, LANES), lambda b: (0, 0, 0)),
                  pl.BlockSpec((2, CMP_BLOCK, LANES, LANES), lambda b: (0, 0, 0, 0)),
                  pl.BlockSpec((1, HEAD_DIM), lambda b: (0, 0))],
        out_specs=[pl.BlockSpec((1, NSA_KV_HEADS, nch, HEAD_DIM), lambda b: (b, 0, 0, 0)),
                   pl.BlockSpec((1, NSA_KV_HEADS, nch, HEAD_DIM), lambda b: (b, 0, 0, 0))],
        out_shape=[jax.ShapeDtypeStruct((bsz, NSA_KV_HEADS, nch, HEAD_DIM), BF16),
                   jax.ShapeDtypeStruct((bsz, NSA_KV_HEADS, nch, HEAD_DIM), BF16)],
        compiler_params=_cparams(("parallel",)),
        name="nsa_compress",
    )(u32, u32, pos, w, g)


def _nsa_kernel(q_ref, ng_ref, kv_ref, kc_ref, vc_ref, gq_ref, gk_ref, cover_ref, expand_ref, gexp_ref, o_ref,
                ksbd_ref, wsext_ref, kwbd_ref, wwext_ref, kcbd_ref, vcext_ref, bias_ref, wbias_ref, psum_ref,
                ocmp_ref, accw_ref, qp_ref, sw_ref, pw_ref, sc_ref, pc_ref,
                s0_ref, s1_ref, p0_ref, p1_ref, a0_ref, a1_ref, m_ref, acc_ref, *, seq):
    i = pl.program_id(1)
    nq = q_ref.shape[0]
    n_pairs = NSA_HEADS // 2
    gpairs = NSA_GROUP // 2
    grows = gpairs * nq
    n_cmp_pad = kc_ref.shape[2]
    n_blk = seq // SEL_BLOCK
    n_sel = min(SEL_TOPN, n_blk)
    win_keys = min(WINDOW + nq, seq)
    wt = LANES
    win_tiles = win_keys // wt

    @pl.when(i == 0)
    def _():
        def rows(r, c):
            off = pl.multiple_of(r * KEY_TILE, KEY_TILE)
            kv = kv_ref[pl.ds(off, KEY_TILE), :]
            for g in range(NSA_KV_HEADS):
                c0 = g * HEAD_DIM
                _store_pair_kv(ksbd_ref, wsext_ref, g, pl.multiple_of(2 * off, 2 * KEY_TILE), KEY_TILE,
                               _rms(kv[:, c0:c0 + 64], gk_ref[1:2, :]), kv[:, 128 + c0:128 + c0 + 64])
                kw = _rms(kv[:, 256 + c0:256 + c0 + 64], gk_ref[2:3, :])
                vw = kv[:, 384 + c0:384 + c0 + 64]
                for hf in range(KEY_TILE // wt):
                    _store_pair_kv(kwbd_ref, wwext_ref, g, pl.multiple_of(2 * off + hf * 2 * wt, 2 * wt), wt,
                                   kw[hf * wt:(hf + 1) * wt], vw[hf * wt:(hf + 1) * wt])
            return c

        lax.fori_loop(0, seq // KEY_TILE, rows, 0)
        zc = jnp.zeros((n_cmp_pad, HEAD_DIM), BF16)
        for g in range(NSA_KV_HEADS):
            kcbd_ref[g, 0:n_cmp_pad, :] = jnp.concatenate([kc_ref[0, g], zc], axis=-1)
            kcbd_ref[g, n_cmp_pad:2 * n_cmp_pad, :] = jnp.concatenate([zc, kc_ref[0, g]], axis=-1)
            vcext_ref[g, 0:n_cmp_pad, :] = jnp.concatenate([vc_ref[0, g], zc], axis=-1)
            vcext_ref[g, n_cmp_pad:2 * n_cmp_pad, :] = jnp.concatenate([zc, vc_ref[0, g]], axis=-1)

    start = i * nq
    nkt = (start + nq + KEY_TILE - 1) // KEY_TILE
    t_i = start + lax.broadcasted_iota(jnp.int32, (nq, 1), 0)
    t_col = t_i.astype(F32)
    lane_pos = lax.broadcasted_iota(jnp.int32, (1, KEY_TILE), 1).astype(F32)

    _norm_pairs(q_ref, gq_ref, qp_ref, n_pairs)

    cmp_end = (lax.broadcasted_iota(jnp.int32, (1, n_cmp_pad), 1) * CMP_STRIDE + (CMP_BLOCK - 1)).astype(F32)
    vis = cmp_end <= t_col
    wbase = pl.multiple_of(jnp.maximum(start + nq - win_keys, 0), wt)
    wb2 = pl.multiple_of(2 * wbase, 2 * wt)
    col = lax.broadcasted_iota(jnp.int32, (1, 2 * win_keys), 1)
    wpos = (wbase + jnp.right_shift(col, int(math.log2(2 * wt))) * wt + jnp.bitwise_and(col, wt - 1)).astype(F32)
    wbias_ref[...] = jnp.where((wpos <= t_col) & (wpos > t_col - float(WINDOW)), 0.0, NEG)
    for g in range(NSA_KV_HEADS):
        gs = slice(g * grows, (g + 1) * grows)
        sc_ref[gs, :] = _dot_nt(qp_ref[gs, :], kcbd_ref[g])
        sw_ref[gs, :] = _dot_nt(qp_ref[gs, :], kwbd_ref[g, pl.ds(wb2, 2 * win_keys), :])

    for g in range(NSA_KV_HEADS):
        gs = slice(g * grows, (g + 1) * grows)
        tot = jnp.zeros((nq, n_cmp_pad), F32)
        for j in range(g * gpairs, (g + 1) * gpairs):
            rs = slice(j * nq, (j + 1) * nq)
            for par in range(2):
                cs = slice(par * n_cmp_pad, (par + 1) * n_cmp_pad)
                s = jnp.where(vis, sc_ref[rs, cs], NEG)
                e = jnp.exp2(s - jnp.max(s, axis=-1, keepdims=True))
                p = jnp.where(vis, e / jnp.sum(e, axis=-1, keepdims=True), 0.0)
                pc_ref[rs, cs] = p.astype(BF16)
                tot = tot + p
        psum_ref[g] = tot
        ocmp_ref[gs, :] = _dot(pc_ref[gs, :], vcext_ref[g])

    blk_j = lax.broadcasted_iota(jnp.int32, (1, n_blk), 1)
    cur1 = jnp.right_shift(t_i, int(math.log2(SEL_BLOCK)))
    cur = jnp.concatenate([cur1] * NSA_KV_HEADS, axis=0)
    imp = jnp.concatenate([_dot_hi(psum_ref[g], cover_ref[...]) for g in range(NSA_KV_HEADS)], axis=0)
    forced = (blk_j == cur) | (blk_j == 0)
    imp = jnp.where(forced, jnp.inf, jnp.where(blk_j > cur, -jnp.inf, imp))
    selb = jnp.zeros(imp.shape, jnp.bool_)
    for _ in range(n_sel):
        mx = jnp.max(imp, axis=-1, keepdims=True)
        first = jnp.min(jnp.where(imp == mx, blk_j, n_blk), axis=-1, keepdims=True)
        pick = blk_j == first
        selb = selb | pick
        imp = jnp.where(pick, -jnp.inf, imp)
    selb_bf = jnp.where(selb, 1.0, 0.0).astype(BF16)

    for g in range(NSA_KV_HEADS):
        gs = slice(g * grows, (g + 1) * grows)
        for j in range(g * gpairs, (g + 1) * gpairs):
            rs = slice(j * nq, (j + 1) * nq)
            for par in range(2):
                cols = [slice(t * 2 * wt + par * wt, t * 2 * wt + (par + 1) * wt) for t in range(win_tiles)]
                s = [sw_ref[rs, cs] + wbias_ref[:, cs] for cs in cols]
                mx = s[0]
                for st in s[1:]:
                    mx = jnp.maximum(mx, st)
                m = jnp.max(mx, axis=-1, keepdims=True)
                for cs, st in zip(cols, s):
                    pw_ref[rs, cs] = jnp.exp2(st - m).astype(BF16)
        accw_ref[gs, :] = _dot(pw_ref[gs, :], wwext_ref[g, pl.ds(wb2, 2 * win_keys), :])

    def mask_tile(kt, c):
        off = pl.multiple_of(kt * KEY_TILE, KEY_TILE)
        hit = _dot(selb_bf, expand_ref[:, pl.ds(off, KEY_TILE)])
        ok = (lane_pos + (kt * KEY_TILE).astype(F32)) <= t_col
        for g in range(NSA_KV_HEADS):
            bias_ref[g, kt] = jnp.where((hit[g * nq:(g + 1) * nq] > 0.5) & ok, 0.0, NEG)
        return c

    lax.fori_loop(0, (nkt + 1) // 2, lambda jj, c: mask_tile(2 * jj + 1, mask_tile(2 * jj, c)), 0)
    _flash_pairs(qp_ref, ksbd_ref, wsext_ref, bias_ref, nkt, n_pairs, gpairs,
                 ((s0_ref, s1_ref), (p0_ref, p1_ref), (a0_ref, a1_ref)), m_ref, acc_ref)

    gates = jax.nn.sigmoid(ng_ref[...])
    g_hi = gates.astype(BF16)
    g_lo = (gates - g_hi.astype(F32)).astype(BF16)
    gb = [_dot(g_hi, gexp_ref[b]) + _dot(g_lo, gexp_ref[b]) for b in range(3)]
    for j in range(n_pairs):
        rs = slice(j * nq, (j + 1) * nq)
        ls = slice(j * LANES, (j + 1) * LANES)
        acc = acc_ref[rs, :]
        accw = accw_ref[rs, :]
        out = (gb[0][:, ls] * ocmp_ref[rs, :] + gb[1][:, ls] * (acc[:, :LANES] / acc[:, LANES:])
               + gb[2][:, ls] * (accw[:, :LANES] / accw[:, LANES:]))
        o_ref[:, ls] = out.astype(o_ref.dtype)


def _nsa(u16, u32, kc, vc, gq, gk, cover, expand, gexp, bsz, seq):
    nqb = seq // NSA_Q_BLOCK
    nch = kc.shape[2]
    n_blk = seq // SEL_BLOCK
    n_pairs = NSA_HEADS // 2
    prow = n_pairs * NSA_Q_BLOCK
    win_keys = min(WINDOW + NSA_Q_BLOCK, seq)
    kern = functools.partial(_nsa_kernel, seq=seq)
    return pl.pallas_call(
        kern,
        grid=(bsz, nqb),
        in_specs=[pl.BlockSpec((NSA_Q_BLOCK, 1024), lambda b, i: (b * nqb + i, U16_NQ_BLK)),
                  pl.BlockSpec((NSA_Q_BLOCK, 128), lambda b, i: (b * nqb + i, U32_NG_BLK128)),
                  pl.BlockSpec((seq, 512), lambda b, i: (b, U32_NKV_BLK512)),
                  pl.BlockSpec((1, NSA_KV_HEADS, nch, HEAD_DIM), lambda b, i: (b, 0, 0, 0)),
                  pl.BlockSpec((1, NSA_KV_HEADS, nch, HEAD_DIM), lambda b, i: (b, 0, 0, 0)),
                  pl.BlockSpec((1, HEAD_DIM), lambda b, i: (0, 0)),
                  pl.BlockSpec((3, HEAD_DIM), lambda b, i: (0, 0)),
                  pl.BlockSpec((nch, n_blk), lambda b, i: (0, 0)),
                  pl.BlockSpec((n_blk, seq), lambda b, i: (0, 0)),
                  pl.BlockSpec((3, LANES, NSA_HEADS * HEAD_DIM), lambda b, i: (0, 0, 0))],
        out_specs=pl.BlockSpec((NSA_Q_BLOCK, NSA_HEADS * HEAD_DIM), lambda b, i: (b * nqb + i, 0)),
        out_shape=jax.ShapeDtypeStruct((bsz * seq, NSA_HEADS * HEAD_DIM), BF16),
        scratch_shapes=[pltpu.VMEM((NSA_KV_HEADS, 2 * seq, LANES), BF16),
                        pltpu.VMEM((NSA_KV_HEADS, 2 * seq, 2 * LANES), BF16),
                        pltpu.VMEM((NSA_KV_HEADS, 2 * seq, LANES), BF16),
                        pltpu.VMEM((NSA_KV_HEADS, 2 * seq, 2 * LANES), BF16),
                        pltpu.VMEM((NSA_KV_HEADS, 2 * nch, LANES), BF16),
                        pltpu.VMEM((NSA_KV_HEADS, 2 * nch, LANES), BF16),
                        pltpu.VMEM((NSA_KV_HEADS, seq // KEY_TILE, NSA_Q_BLOCK, KEY_TILE), F32),
                        pltpu.VMEM((NSA_Q_BLOCK, 2 * win_keys), F32),
                        pltpu.VMEM((NSA_KV_HEADS, NSA_Q_BLOCK, nch), F32),
                        pltpu.VMEM((prow, LANES), F32),
                        pltpu.VMEM((prow, 2 * LANES), F32),
                        pltpu.VMEM((prow, LANES), BF16),
                        pltpu.VMEM((prow, 2 * win_keys), F32),
                        pltpu.VMEM((prow, 2 * win_keys), BF16),
                        pltpu.VMEM((prow, 2 * nch), F32),
                        pltpu.VMEM((prow, 2 * nch), BF16)]
                       + _flash_scratch(n_pairs, NSA_Q_BLOCK),
        compiler_params=_cparams(("parallel", "arbitrary")),
        name="nsa",
    )(u16, u32, u32, kc, vc, gq, gk, cover, expand, gexp)


def _softplus(x):
    return jnp.maximum(x, 0.0) + jnp.log1p(jnp.exp(-jnp.abs(x)))


def _ssd_kernel(xbc_ref, z_ref, dt_ref, dtt_ref, cw_ref, cb_ref, dtb_ref, dtbt_ref, al_ref, alt_ref,
                d_ref, ng_ref, o_ref, xcat_ref, xa_ref, y_ref, st_ref):
    c = pl.program_id(1)
    q = SSD_CHUNK
    gn = SSD_GROUPS * SSD_STATE
    tail = 16
    lo_half = lax.broadcasted_iota(jnp.int32, (1, LANES), 1) < SSD_HEAD_DIM

    @pl.when(c == 0)
    def _():
        st_ref[...] = jnp.zeros_like(st_ref)
        xcat_ref[0:q, :] = jnp.zeros((q, SSD_CONV_DIM), BF16)

    @pl.when(c > 0)
    def _():
        xcat_ref[q - tail:q, :] = xcat_ref[2 * q - tail:2 * q, :]

    xcat_ref[q:2 * q, :] = xbc_ref[...]
    ri3 = lax.broadcasted_iota(jnp.int32, ((CONV_WIDTH - 1) * q, 2 * q), 0)
    ci3 = lax.broadcasted_iota(jnp.int32, ((CONV_WIDTH - 1) * q, 2 * q), 1)
    qbits = int(math.log2(q))
    src = q + jnp.bitwise_and(ri3, q - 1) - (jnp.right_shift(ri3, qbits) + 1)
    shifts = jnp.where(ci3 == src, 1.0, 0.0).astype(BF16)
    for cc in range(SSD_CONV_DIM // CONV_CHUNK):
        cs = slice(cc * CONV_CHUNK, (cc + 1) * CONV_CHUNK)
        sh = _dot(shifts, xcat_ref[:, cs])
        acc = cb_ref[:, cs] + cw_ref[CONV_WIDTH - 1:CONV_WIDTH, cs] * xcat_ref[q:2 * q, cs].astype(F32)
        for s in range(1, CONV_WIDTH):
            acc = acc + cw_ref[CONV_WIDTH - 1 - s:CONV_WIDTH - s, cs] * sh[(s - 1) * q:s * q]
        xa_ref[:, cs] = acc * jax.nn.sigmoid(acc)

    dt = _softplus(dt_ref[:, :SSD_HEADS] + dtb_ref[...])
    dtt = _softplus(dtt_ref[0] + dtbt_ref[...])
    a = -jnp.exp(al_ref[...]) * LOG2E
    at = -jnp.exp(alt_ref[...]) * LOG2E
    ri = lax.broadcasted_iota(jnp.int32, (q, q), 0)
    ci = lax.broadcasted_iota(jnp.int32, (q, q), 1)
    tril = ri >= ci
    acum = _dot_hi(jnp.where(tril, 1.0, 0.0), dt * a)
    acumt = _dot_hi(dtt * at, jnp.where(ri <= ci, 1.0, 0.0))
    wrow = jnp.exp2(acumt[:, q - 1:q] - acumt) * dtt

    for g in range(SSD_GROUPS):
        bmat = xa_ref[:, SSD_INNER + g * SSD_STATE:SSD_INNER + (g + 1) * SSD_STATE]
        cmat = xa_ref[:, SSD_INNER + gn + g * SSD_STATE:SSD_INNER + gn + (g + 1) * SSD_STATE].astype(BF16)
        cb = _dot_nt(cmat, bmat.astype(BF16))
        bt = bmat.T
        for pp in range(SSD_HPG // 2):
            j = g * (SSD_HPG // 2) + pp
            ls = slice(j * LANES, (j + 1) * LANES)
            xs = xa_ref[:, ls]
            xbd = jnp.concatenate([jnp.where(lo_half, xs, 0.0), jnp.where(lo_half, 0.0, xs)], axis=0).astype(BF16)
            mm, dec, btw = [], [], []
            for par in range(2):
                h = 2 * j + par
                abc = jnp.broadcast_to(acum[:, h:h + 1], (q, q))
                lmat = jnp.exp2(jnp.where(tril, abc - acumt[h:h + 1, :], -jnp.inf))
                mm.append((cb * lmat * dtt[h:h + 1, :]).astype(BF16))
                dec.append(jnp.exp2(abc))
                btw.append((bt * wrow[h:h + 1, :]).astype(BF16))
            y = _dot(jnp.concatenate(mm, axis=1), xbd)
            y = y + _dot(cmat, st_ref[j].astype(BF16)) * jnp.where(lo_half, dec[0], dec[1])
            y_ref[:, ls] = y + xs * d_ref[:, ls]
            cdec = jnp.where(lo_half, dec[0][q - 1:q, :], dec[1][q - 1:q, :])
            st_ref[j] = st_ref[j] * cdec + _dot(jnp.concatenate(btw, axis=1), xbd)

    z = z_ref[...].astype(F32)
    yz = y_ref[...] * (z * jax.nn.sigmoid(z))
    o_ref[...] = _rms(yz, ng_ref[...]).astype(o_ref.dtype)


def _ssd(u16, u32, dtt, cw, cb, dtb, al, d, ng, bsz, seq):
    nc = seq // SSD_CHUNK
    q = SSD_CHUNK
    full = lambda shape: pl.BlockSpec(shape, lambda b, c: (0,) * len(shape))
    d_full = jnp.repeat(d, SSD_HEAD_DIM).reshape(1, SSD_INNER)
    return pl.pallas_call(
        _ssd_kernel,
        grid=(bsz, nc),
        in_specs=[pl.BlockSpec((q, SSD_CONV_DIM), lambda b, c: (b * nc + c, U16_XBC_BLK)),
                  pl.BlockSpec((q, SSD_INNER), lambda b, c: (b * nc + c, U16_Z_BLK)),
                  pl.BlockSpec((q, 128), lambda b, c: (b * nc + c, U32_DT_BLK128)),
                  pl.BlockSpec((1, SSD_HEADS, q), lambda b, c: (b, 0, c)),
                  full((CONV_WIDTH, SSD_CONV_DIM)), full((1, SSD_CONV_DIM)),
                  full((1, SSD_HEADS)), full((SSD_HEADS, 1)),
                  full((1, SSD_HEADS)), full((SSD_HEADS, 1)),
                  full((1, SSD_INNER)), full((1, SSD_INNER))],
        out_specs=pl.BlockSpec((q, SSD_INNER), lambda b, c: (b * nc + c, 0)),
        out_shape=jax.ShapeDtypeStruct((bsz * seq, SSD_INNER), BF16),
        scratch_shapes=[pltpu.VMEM((2 * q, SSD_CONV_DIM), BF16),
                        pltpu.VMEM((q, SSD_CONV_DIM), F32),
                        pltpu.VMEM((q, SSD_INNER), F32),
                        pltpu.VMEM((SSD_HEADS // 2, SSD_STATE, LANES), F32)],
        compiler_params=_cparams(("parallel", "arbitrary")),
        name="ssd",
    )(u16, u16, u32, dtt, cw, cb, dtb.reshape(1, -1), dtb.reshape(-1, 1), al.reshape(1, -1), al.reshape(-1, 1),
      d_full, ng.reshape(1, -1))


def _merge_kernel(x_ref, ya_ref, yb_ref, yc_ref, mg_ref, wa_ref, wb_ref, wc_ref, wo_ref, o_ref):
    d = D_MODEL
    mg = mg_ref[...].astype(F32)
    mix = jax.nn.sigmoid(mg[:, 0:d]) * _dot(ya_ref[...], wa_ref[...])
    mix = mix + jax.nn.sigmoid(mg[:, d:2 * d]) * _dot(yb_ref[...], wb_ref[...])
    mix = mix + jax.nn.sigmoid(mg[:, 2 * d:3 * d]) * _dot(yc_ref[...], wc_ref[...])
    o_ref[...] = x_ref[...] + _dot(mix.astype(BF16), wo_ref[...])


def _merge(x, ya, yb, yc, u16, wa, wb, wc, wo, tm):
    m, d = x.shape
    full = lambda a: pl.BlockSpec(a.shape, lambda i: (0, 0))
    return pl.pallas_call(
        _merge_kernel,
        grid=(m // tm,),
        in_specs=[pl.BlockSpec((tm, d), lambda i: (i, 0)),
                  pl.BlockSpec((tm, ya.shape[1]), lambda i: (i, 0)),
                  pl.BlockSpec((tm, yb.shape[1]), lambda i: (i, 0)),
                  pl.BlockSpec((tm, yc.shape[1]), lambda i: (i, 0)),
                  pl.BlockSpec((tm, 3 * d), lambda i: (i, U16_MG_BLK)),
                  full(wa), full(wb), full(wc), full(wo)],
        out_specs=pl.BlockSpec((tm, d), lambda i: (i, 0)),
        out_shape=jax.ShapeDtypeStruct((m, d), F32),
        compiler_params=_cparams(("parallel",)),
        name="merge",
    )(x, ya, yb, yc, u16, wa, wb, wc, wo)


def _ffn_kernel(x_ref, g_ref, w1_ref, w2_ref, o_ref, h_ref):
    @pl.when(pl.program_id(1) == 0)
    def _():
        h_ref[...] = _rms(x_ref[...], g_ref[...]).astype(BF16)
        o_ref[...] = x_ref[...]

    a = jnp.maximum(_dot(h_ref[...], w1_ref[0].astype(BF16)), 0.0)
    o_ref[...] += _dot((a * a).astype(BF16), w2_ref[0].astype(BF16))


def _ffn(x, g, w1, w2, layer, tm, tf):
    m, d = x.shape
    f = w1.shape[2]
    return pl.pallas_call(
        _ffn_kernel,
        grid=(m // tm, f // tf),
        in_specs=[pl.BlockSpec((tm, d), lambda i, j: (i, 0)),
                  pl.BlockSpec((1, d), lambda i, j: (0, 0)),
                  pl.BlockSpec((1, d, tf), lambda i, j: (layer, 0, j)),
                  pl.BlockSpec((1, tf, d), lambda i, j: (layer, j, 0))],
        out_specs=pl.BlockSpec((tm, d), lambda i, j: (i, 0)),
        out_shape=jax.ShapeDtypeStruct((m, d), F32),
        scratch_shapes=[pltpu.VMEM((tm, d), BF16)],
        compiler_params=_cparams(("parallel", "arbitrary")),
        name="ffn",
    )(x, g, w1, w2)


def _prep_w_in(w):
    offs = [0] + [int(o) for o in np.cumsum(IN_SPLITS)]
    seg = lambda k: w[:, offs[k]:offs[k + 1]]
    dq, dk, dv, iq, ik, iw, nq, nkv, ng, sz, sxbc, sdt, mg = [seg(k) for k in range(13)]
    zeros = lambda n: jnp.zeros((w.shape[0], n), w.dtype)
    w16 = jnp.concatenate([sxbc, mg, sz, dq, nq], axis=1).astype(BF16)
    w32 = jnp.concatenate([nkv[:, 256:768], nkv[:, 0:256], zeros(256),
                           dk, dv, iq, ik, iw, zeros(512 - 424),
                           ng, zeros(128 - ng.shape[1]),
                           sdt, zeros(128 - sdt.shape[1])], axis=1).astype(BF16)
    return w16, w32


def _pick_tile(n, pref):
    t = min(n, pref)
    while n % t:
        t //= 2
    return t


def kernel(x, norm1_g, w_in, dsa_q_norm, dsa_k_norm, nsa_q_norm, nsa_k_norm, nsa_cmp_pos, nsa_cmp_w,
           ssd_conv_w, ssd_conv_b, ssd_dt_bias, ssd_a_log, ssd_d, ssd_norm_g,
           w_br_dsa, w_br_nsa, w_br_ssd, w_out, norm2_g, w_ff1, w_ff2):
    bsz, seq, d = x.shape
    m = bsz * seq
    depth = w_in.shape[0]
    nch = seq // CMP_STRIDE
    n_blk = seq // SEL_BLOCK
    half = CMP_BLOCK // 2

    cmp_start = np.arange(nch) * CMP_STRIDE
    blk_start = np.arange(n_blk) * SEL_BLOCK
    cover = ((cmp_start[:, None] < blk_start[None, :] + SEL_BLOCK)
             & (cmp_start[:, None] + CMP_BLOCK > blk_start[None, :])
             & (np.arange(nch)[:, None] < (seq - CMP_BLOCK) // CMP_STRIDE + 1)).astype(np.float32)
    expand = (np.arange(seq)[None, :] // SEL_BLOCK == np.arange(n_blk)[:, None]).astype(np.float32)
    gexp = np.zeros((3, LANES, NSA_HEADS * HEAD_DIM), np.float32)
    for hh in range(NSA_HEADS):
        for br in range(3):
            gexp[br, 3 * hh + br, hh * HEAD_DIM:(hh + 1) * HEAD_DIM] = 1.0
    gexp = jnp.asarray(gexp, dtype=BF16)
    cover = jnp.asarray(cover)
    expand = jnp.asarray(expand, dtype=BF16)

    tm = _pick_tile(m, 1024)
    xf = x.reshape(m, d)
    for l in range(depth):
        w16, w32 = _prep_w_in(w_in[l])
        g1 = norm1_g[l].reshape(1, d)
        u16 = _norm_matmul(xf, g1, w16, BF16, tm, 2560)
        u32 = _norm_matmul(xf, g1, w32, F32, tm, U32_WIDTH)

        iw0 = U32_DS_BLK512 * 512 + DS_IW
        iwt = u32[:, iw0:iw0 + IDX_HEADS].reshape(bsz, seq, IDX_HEADS).transpose(0, 2, 1)
        ya = _dsa(u16, u32, iwt, dsa_q_norm[l].reshape(1, -1), dsa_k_norm[l].reshape(1, -1), bsz, seq)

        wl = nsa_cmp_w[l]
        zblk = jnp.zeros_like(wl)
        cw = jnp.concatenate([jnp.concatenate([wl, zblk], axis=3),
                              jnp.concatenate([zblk, wl], axis=3)], axis=2).astype(BF16)
        cpos = jnp.concatenate([nsa_cmp_pos[l]] * NSA_KV_HEADS, axis=-1)
        kc, vc = _nsa_compress(u32, cpos, cw, nsa_k_norm[l][0:1], bsz, seq)
        yb = _nsa(u16, u32, kc, vc, nsa_q_norm[l].reshape(1, -1), nsa_k_norm[l], cover, expand, gexp, bsz, seq)

        dt0 = U32_DT_BLK128 * 128
        dtt = u32[:, dt0:dt0 + SSD_HEADS].reshape(bsz, seq, SSD_HEADS).transpose(0, 2, 1)
        yc = _ssd(u16, u32, dtt, ssd_conv_w[l], ssd_conv_b[l].reshape(1, -1), ssd_dt_bias[l], ssd_a_log[l],
                  ssd_d[l], ssd_norm_g[l], bsz, seq)

        xf = _merge(xf, ya, yb, yc, u16,
                    w_br_dsa[l].astype(BF16), w_br_nsa[l].astype(BF16), w_br_ssd[l].astype(BF16),
                    w_out[l].astype(BF16), _pick_tile(m, 512))
        xf = _ffn(xf, norm2_g[l].reshape(1, d), w_ff1, w_ff2, l,
                  _pick_tile(m, 1024), 1024)
    return xf.reshape(bsz, seq, d)
```
